```python
import jax, jax.numpy as jnp
from jax import lax
import numpy as np

D_MODEL = 1024
BATCH = 4
SEQ = 4096
DEPTH = 2
DEC_BATCH = 32
DEC_SEQ = 8
PAST_LEN = 8192
PAGE_SIZE = 128

A_GROUPS = ((128, 1), (512, 4), (2048, 16))
N_GROUPS = 3
A_HEADS = 4
A_HD = 128
A_WIDTH = A_HEADS * A_HD
A_BLOCK = 128
B_HEADS = 4
B_DK = 128
B_DV = 128
B_CONV = 4
B_QK = B_HEADS * B_DK
B_V = B_HEADS * B_DV
B_CONV_CH = 2 * B_QK + B_V
C_HEADS = 4
C_DK = 64
C_DV = 128
C_QK = C_HEADS * C_DK
C_V = C_HEADS * C_DV
CHUNK = 64
ROPE_THETA = 10000.0
EPS = 1e-6
D_FF = -(-(8 * D_MODEL) // (3 * 256)) * 256
IN_SIZES = (3 * N_GROUPS * A_WIDTH, B_CONV_CH, B_V, B_HEADS, B_HEADS, C_QK, C_QK, C_V, C_V, 3 * D_MODEL)
N_IN = sum(IN_SIZES)
F32 = jnp.float32

kernel_name = 'hybrid_dilated_delta_retention_decoder_step'


def _rmsnorm(x, g):
    xf = x.astype(F32)
    y = xf * lax.rsqrt(jnp.mean(xf * xf, axis=-1, keepdims=True) + EPS)
    return (y * g.astype(F32)).astype(x.dtype)


def _l2norm(x):
    return x * lax.rsqrt(jnp.sum(x * x, axis=-1, keepdims=True) + EPS)


def _rope(x, pos):
    hd = x.shape[-1]
    inv = ROPE_THETA ** (-jnp.arange(0, hd, 2, dtype=F32) / hd)
    ang = pos.astype(F32)[:, None] * inv[None, :]
    shape = (ang.shape[0],) + (1,) * (x.ndim - 3) + (hd // 2,)
    cos = jnp.cos(ang).reshape(shape)
    sin = jnp.sin(ang).reshape(shape)
    xf = x.astype(F32)
    x1, x2 = xf[..., :hd // 2], xf[..., hd // 2:]
    return jnp.concatenate([x1 * cos - x2 * sin, x2 * cos + x1 * sin], axis=-1).astype(x.dtype)


def _causal_conv(x, buf, w):
    T = x.shape[1]
    xp = jnp.concatenate([buf.astype(x.dtype), x], axis=1)
    out = xp[:, 0:T] * w[0]
    for i in range(1, B_CONV):
        out = out + xp[:, i:i + T] * w[i]
    return out, xp[:, -(B_CONV - 1):]


def _to_chunks(x, C):
    B, T = x.shape[:2]
    nc = -(-T // C)
    x = jnp.pad(x, [(0, 0), (0, nc * C - T)] + [(0, 0)] * (x.ndim - 2))
    x = x.reshape((B, nc, C) + x.shape[2:])
    return x.transpose((1, 0, 3, 2) + tuple(range(4, x.ndim)))


def _from_chunks(x, T):
    nc, B, H, C, d = x.shape
    return x.transpose(1, 0, 3, 2, 4).reshape(B, nc * C, H, d)[:, :T]


def _decay_matrix(gc, incl):
    C = gc.shape[-1]
    i = jnp.arange(C)[:, None]
    j = jnp.arange(C)[None, :]
    mask = (i >= j) if incl else (i > j)
    diff = gc[..., :, None] - gc[..., None, :]
    return jnp.where(mask, jnp.exp(jnp.where(mask, diff, 0.0)), 0.0)


def _gated_delta_rule(q, k, v, beta, g, S0):
    T = q.shape[1]
    dv = v.shape[-1]
    C = min(CHUNK, T)
    qc, kc, vc = (_to_chunks(t.astype(F32), C) for t in (q, k, v))
    bc = _to_chunks(beta.astype(F32), C)
    gc = jnp.cumsum(_to_chunks(g.astype(F32), C), axis=-1)
    kb = kc * bc[..., None]
    eye = jnp.eye(C, dtype=F32)
    lower = _decay_matrix(gc, False) * jnp.einsum('nbhid,nbhjd->nbhij', kb, kc)
    rhs = jnp.concatenate([vc * bc[..., None], kb * jnp.exp(gc)[..., None]], axis=-1)
    sol = lax.linalg.triangular_solve(eye + lower, rhs, left_side=True, lower=True, unit_diagonal=True)
    u, w = sol[..., :dv], sol[..., dv:]
    attn = _decay_matrix(gc, True) * jnp.einsum('nbhid,nbhjd->nbhij', qc, kc)

    def step(S, xs):
        qi, ki, gi, ui, wi, ai = xs
        v_new = ui - jnp.einsum('bhck,bhkv->bhcv', wi, S)
        o = jnp.einsum('bhck,bhkv->bhcv', qi * jnp.exp(gi)[..., None], S) + jnp.einsum('bhij,bhjv->bhiv', ai, v_new)
        gl = gi[..., -1:]
        S = S * jnp.exp(gl)[..., None] + jnp.einsum('bhck,bhcv->bhkv', ki * jnp.exp(gl - gi)[..., None], v_new)
        return S, o

    S, o = lax.scan(step, S0.astype(F32), (qc, kc, gc, u, w, attn))
    return _from_chunks(o, T), S


def _retention(q, k, v, g, R0):
    T = q.shape[1]
    C = min(CHUNK, T)
    qc, kc, vc = (_to_chunks(t.astype(F32), C) for t in (q, k, v))
    gc = jnp.cumsum(_to_chunks(g.astype(F32), C), axis=-1)
    attn = _decay_matrix(gc, True) * jnp.einsum('nbhid,nbhjd->nbhij', qc, kc)
    intra = jnp.einsum('nbhij,nbhjv->nbhiv', attn, vc)

    def step(R, xs):
        qi, ki, vi, gi, oi = xs
        o = oi + jnp.einsum('bhck,bhkv->bhcv', qi * jnp.exp(gi)[..., None], R)
        gl = gi[..., -1:]
        R = R * jnp.exp(gl)[..., None] + jnp.einsum('bhck,bhcv->bhkv', ki * jnp.exp(gl - gi)[..., None], vi)
        return R, o

    R, o = lax.scan(step, R0.astype(F32), (qc, kc, vc, gc, intra))
    return _from_chunks(o, T), R


def _dilated_window_prompt(q, k, v, window, dil):
    B, T, H, hd = q.shape
    Ls = T // dil
    w = window // dil
    bq = min(A_BLOCK, Ls)
    nb = -(-Ls // bq)
    Lp = nb * bq

    def sub(x, front):
        x = x.reshape(B, Ls, dil, H, hd).transpose(0, 2, 1, 3, 4).astype(F32)
        return jnp.pad(x, ((0, 0), (0, 0), (front, Lp - Ls), (0, 0), (0, 0)))

    qb = sub(q, 0).reshape(B, dil, nb, bq, H, hd)
    idx = jnp.arange(nb)[:, None] * bq + jnp.arange(w + bq)[None, :]
    kb = jnp.take(sub(k, w), idx, axis=2)
    vb = jnp.take(sub(v, w), idx, axis=2)
    s = jnp.einsum('brnqhd,brnkhd->brnhqk', qb, kb) * (hd ** -0.5)
    qi = jnp.arange(bq)[:, None]
    kj = jnp.arange(w + bq)[None, :]
    dist = qi - kj + w
    kpos = jnp.arange(nb)[:, None, None] * bq + kj[None] - w
    valid = (dist >= 0) & (dist <= w) & (kpos >= 0)
    s = jnp.where(valid[:, None], s, -jnp.inf)
    m = jnp.max(s, axis=-1, keepdims=True)
    p = jnp.exp(s - m)
    den = jnp.sum(p, axis=-1)
    o = jnp.einsum('brnhqk,brnkhd->brnqhd', p, vb) / jnp.swapaxes(den, -1, -2)[..., None]
    lse = jnp.swapaxes(m[..., 0] + jnp.log(den), -1, -2)

    def unsub(x):
        x = x.reshape((B, dil, Lp) + x.shape[4:])[:, :, :Ls]
        return jnp.swapaxes(x, 1, 2).reshape((B, T) + x.shape[3:])

    return unsub(o), unsub(lse)


def _dilated_window_sample(q, k, v, kv_buf, window, dil):
    DB, S, H, hd = q.shape
    L = kv_buf.shape[1]
    kk = jnp.concatenate([kv_buf[:, :, 0].astype(k.dtype), k], axis=1).astype(F32)
    vv = jnp.concatenate([kv_buf[:, :, 1].astype(v.dtype), v], axis=1).astype(F32)
    n = window // dil + 1
    idx = L + jnp.arange(S)[:, None] - dil * jnp.arange(n)[None, :]
    valid = idx >= 0
    idx = jnp.maximum(idx, 0)
    kg = kk[:, idx]
    vg = vv[:, idx]
    s = jnp.einsum('bqhd,bqkhd->bhqk', q.astype(F32), kg) * (hd ** -0.5)
    s = jnp.where(valid, s, -jnp.inf)
    m = jnp.max(s, axis=-1, keepdims=True)
    p = jnp.exp(s - m)
    den = jnp.sum(p, axis=-1)
    o = jnp.einsum('bhqk,bqkhd->bqhd', p, vg) / jnp.swapaxes(den, 1, 2)[..., None]
    lse = jnp.swapaxes(m[..., 0] + jnp.log(den), 1, 2)
    return o, lse


def _layer(x, pos, kv_bufs, conv_buf, S0, R0, prompt, norm1_g, w_in, a_q_norm_g, a_k_norm_g, b_conv_w,
           b_a_log, b_dt_bias, b_out_norm_g, c_out_norm_g, w_out_a, w_out_b, w_out_c, w_out, norm2_g,
           w_ffn_in, w_ffn_out):
    Bsz, T, _ = x.shape
    h = _rmsnorm(x, norm1_g)
    proj = h @ w_in
    a_qkv, b_qkv, b_z, b_beta, b_a, c_q, c_k, c_v, c_z, gates = jnp.split(
        proj, np.cumsum(IN_SIZES)[:-1].tolist(), axis=-1)

    a_qkv = a_qkv.reshape(Bsz, T, 3, N_GROUPS, A_HEADS, A_HD)
    a_q = _rope(_rmsnorm(a_qkv[:, :, 0], a_q_norm_g), pos)
    a_k = _rope(_rmsnorm(a_qkv[:, :, 1], a_k_norm_g), pos)
    a_v = a_qkv[:, :, 2]
    outs, lses, new_kv = [], [], []
    for gi, (win, dil) in enumerate(A_GROUPS):
        qg, kg, vg = a_q[:, :, gi], a_k[:, :, gi], a_v[:, :, gi]
        if prompt:
            o, lse = _dilated_window_prompt(qg, kg, vg, win, dil)
            keep = min(win, T)
            new_kv.append(jnp.stack([kg[:, T - keep:], vg[:, T - keep:]], axis=2))
        else:
            o, lse = _dilated_window_sample(qg, kg, vg, kv_bufs[gi], win, dil)
            new_kv.append(jnp.stack([kg, vg], axis=2))
        outs.append(o)
        lses.append(lse)
    wts = jax.nn.softmax(jnp.stack(lses), axis=0)
    o_a = jnp.sum(wts[..., None] * jnp.stack(outs), axis=0).reshape(Bsz, T, A_WIDTH).astype(x.dtype)

    b_act, conv_new = _causal_conv(b_qkv, conv_buf, b_conv_w)
    b_act = jax.nn.silu(b_act)
    bq, bk, bv = jnp.split(b_act, [B_QK, 2 * B_QK], axis=-1)
    bq = _l2norm(bq.reshape(Bsz, T, B_HEADS, B_DK).astype(F32)) * (B_DK ** -0.5)
    bk = _l2norm(bk.reshape(Bsz, T, B_HEADS, B_DK).astype(F32))
    bv = bv.reshape(Bsz, T, B_HEADS, B_DV)
    beta = jax.nn.sigmoid(b_beta.astype(F32))
    g = -jnp.exp(b_a_log.astype(F32)) * jax.nn.softplus(b_a.astype(F32) + b_dt_bias.astype(F32))
    o_b, S_new = _gated_delta_rule(bq, bk, bv, beta, g, S0)
    o_b = _rmsnorm(o_b, b_out_norm_g) * jax.nn.silu(b_z.reshape(Bsz, T, B_HEADS, B_DV).astype(F32))
    o_b = o_b.reshape(Bsz, T, B_V).astype(x.dtype)

    cq = _rope(c_q.reshape(Bsz, T, C_HEADS, C_DK), pos)
    ck = _rope(c_k.reshape(Bsz, T, C_HEADS, C_DK), pos) * (C_DK ** -0.5)
    cv = c_v.reshape(Bsz, T, C_HEADS, C_DV)
    log_gamma = jnp.log1p(-jnp.exp2(-5.0 - jnp.arange(C_HEADS, dtype=F32)))
    o_c, R_new = _retention(cq, ck, cv, jnp.broadcast_to(log_gamma, (Bsz, T, C_HEADS)), R0)
    o_c = _rmsnorm(o_c, c_out_norm_g) * jax.nn.silu(c_z.reshape(Bsz, T, C_HEADS, C_DV).astype(F32))
    o_c = o_c.reshape(Bsz, T, C_V).astype(x.dtype)

    gts = jax.nn.sigmoid(gates.reshape(Bsz, T, 3, D_MODEL))
    merged = gts[:, :, 0] * (o_a @ w_out_a) + gts[:, :, 1] * (o_b @ w_out_b) + gts[:, :, 2] * (o_c @ w_out_c)
    x = x + (merged @ w_out).astype(x.dtype)

    ff_g, ff_u = jnp.split(_rmsnorm(x, norm2_g) @ w_ffn_in, 2, axis=-1)
    x = x + ((jax.nn.silu(ff_g) * ff_u) @ w_ffn_out).astype(x.dtype)
    return x, new_kv, conv_new, S_new, R_new


def setup_inputs(seed: int = 0) -> dict:
    key = jax.random.key(seed)
    ks = jax.random.split(key, 32)

    def nrm(k, shape, scale):
        return jax.random.normal(k, shape, F32) * scale

    lens = [min(w, PAST_LEN) for w, _ in A_GROUPS]
    dt = jnp.exp(jax.random.uniform(ks[14], (DEPTH, B_HEADS), F32, np.log(1e-3), np.log(1e-1)))
    return {
        'x_prompt': nrm(ks[0], (BATCH, SEQ, D_MODEL), 1.0),
        'x_sample': nrm(ks[1], (DEC_BATCH, DEC_SEQ, D_MODEL), 1.0),
        'cache_a_kv0': nrm(ks[2], (DEPTH, DEC_BATCH, lens[0], 2, A_HEADS, A_HD), 1.0),
        'cache_a_kv1': nrm(ks[3], (DEPTH, DEC_BATCH, lens[1], 2, A_HEADS, A_HD), 1.0),
        'cache_a_kv2': nrm(ks[4], (DEPTH, DEC_BATCH, lens[2], 2, A_HEADS, A_HD), 1.0),
        'state_b_conv': nrm(ks[5], (DEPTH, DEC_BATCH, B_CONV - 1, B_CONV_CH), 1.0),
        'state_b_S': nrm(ks[6], (DEPTH, DEC_BATCH, B_HEADS, B_DK, B_DV), 0.1),
        'state_c_R': nrm(ks[7], (DEPTH, DEC_BATCH, C_HEADS, C_DK, C_DV), 0.3),
        'norm1_g': 1.0 + nrm(ks[8], (DEPTH, D_MODEL), 0.02),
        'w_in': nrm(ks[9], (DEPTH, D_MODEL, N_IN), D_MODEL ** -0.5),
        'a_q_norm_g': 1.0 + nrm(ks[10], (DEPTH, A_HD), 0.02),
        'a_k_norm_g': 1.0 + nrm(ks[11], (DEPTH, A_HD), 0.02),
        'b_conv_w': nrm(ks[12], (DEPTH, B_CONV, B_CONV_CH), 0.5),
        'b_a_log': jnp.log(jax.random.uniform(ks[13], (DEPTH, B_HEADS), F32, 1.0, 16.0)),
        'b_dt_bias': jnp.log(jnp.expm1(dt)),
        'b_out_norm_g': 1.0 + nrm(ks[15], (DEPTH, B_DV), 0.02),
        'c_out_norm_g': 1.0 + nrm(ks[16], (DEPTH, C_DV), 0.02),
        'w_out_a': nrm(ks[17], (DEPTH, A_WIDTH, D_MODEL), A_WIDTH ** -0.5),
        'w_out_b': nrm(ks[18], (DEPTH, B_V, D_MODEL), B_V ** -0.5),
        'w_out_c': nrm(ks[19], (DEPTH, C_V, D_MODEL), C_V ** -0.5),
        'w_out': nrm(ks[20], (DEPTH, D_MODEL, D_MODEL), D_MODEL ** -0.5),
        'norm2_g': 1.0 + nrm(ks[21], (DEPTH, D_MODEL), 0.02),
        'w_ffn_in': nrm(ks[22], (DEPTH, D_MODEL, 2 * D_FF), D_MODEL ** -0.5),
        'w_ffn_out': nrm(ks[23], (DEPTH, D_FF, D_MODEL), D_FF ** -0.5),
    }


def reference(x_prompt, x_sample, cache_a_kv0, cache_a_kv1, cache_a_kv2, state_b_conv, state_b_S, state_c_R,
              norm1_g, w_in, a_q_norm_g, a_k_norm_g, b_conv_w, b_a_log, b_dt_bias, b_out_norm_g, c_out_norm_g,
              w_out_a, w_out_b, w_out_c, w_out, norm2_g, w_ffn_in, w_ffn_out):
    Bp, T = x_prompt.shape[:2]
    S = x_sample.shape[1]
    pos_p = jnp.arange(T)
    pos_s = PAST_LEN + jnp.arange(S)
    yp, ys = x_prompt, x_sample
    pk, sk = ([], [], []), ([], [], [])
    pc, pS, pR, sc, sS, sR = [], [], [], [], [], []
    for l in range(DEPTH):
        lw = (norm1_g[l], w_in[l], a_q_norm_g[l], a_k_norm_g[l], b_conv_w[l], b_a_log[l], b_dt_bias[l],
              b_out_norm_g[l], c_out_norm_g[l], w_out_a[l], w_out_b[l], w_out_c[l], w_out[l], norm2_g[l],
              w_ffn_in[l], w_ffn_out[l])
        yp, kv, cv, Sn, Rn = _layer(
            yp, pos_p, None, jnp.zeros((Bp, B_CONV - 1, B_CONV_CH), x_prompt.dtype),
            jnp.zeros((Bp, B_HEADS, B_DK, B_DV), F32), jnp.zeros((Bp, C_HEADS, C_DK, C_DV), F32), True, *lw)
        for gi in range(N_GROUPS):
            pk[gi].append(kv[gi])
        pc.append(cv)
        pS.append(Sn)
        pR.append(Rn)
        ys, kv, cv, Sn, Rn = _layer(
            ys, pos_s, (cache_a_kv0[l], cache_a_kv1[l], cache_a_kv2[l]), state_b_conv[l], state_b_S[l],
            state_c_R[l], False, *lw)
        for gi in range(N_GROUPS):
            sk[gi].append(kv[gi])
        sc.append(cv)
        sS.append(Sn)
        sR.append(Rn)
    new_a_kv0_prompt = jnp.stack(pk[0])
    new_a_kv1_prompt = jnp.stack(pk[1])
    new_a_kv2_prompt = jnp.stack(pk[2])
    new_b_conv_prompt = jnp.stack(pc)
    new_b_S_prompt = jnp.stack(pS)
    new_c_R_prompt = jnp.stack(pR)
    new_a_kv0_sample = jnp.stack(sk[0])
    new_a_kv1_sample = jnp.stack(sk[1])
    new_a_kv2_sample = jnp.stack(sk[2])
    new_b_conv_sample = jnp.stack(sc)
    new_b_S_sample = jnp.stack(sS)
    new_c_R_sample = jnp.stack(sR)
    return (yp, ys, new_a_kv0_prompt, new_a_kv1_prompt, new_a_kv2_prompt, new_b_conv_prompt, new_b_S_prompt,
            new_c_R_prompt, new_a_kv0_sample, new_a_kv1_sample, new_a_kv2_sample, new_b_conv_sample,
            new_b_S_sample, new_c_R_sample)
```

```python
import functools
import math

import jax
import jax.numpy as jnp
import numpy as np
from jax import lax
from jax.experimental import pallas as pl
from jax.experimental.pallas import tpu as pltpu

F32 = jnp.float32
BF16 = jnp.bfloat16

D_MODEL = 1024
PAST_LEN = 8192
A_GROUPS = ((128, 1), (512, 4), (2048, 16))
N_GROUPS = 3
A_HEADS = 4
A_HD = 128
A_WIDTH = A_HEADS * A_HD
A_KEYS = 128
B_HEADS = 4
B_DK = 128
B_DV = 128
B_CONV = 4
B_QK = B_HEADS * B_DK
B_V = B_HEADS * B_DV
B_CONV_CH = 2 * B_QK + B_V
C_HEADS = 4
C_DK = 64
C_DV = 128
C_QK = C_HEADS * C_DK
C_V = C_HEADS * C_DV
CHUNK = 64
ROPE_THETA = 10000.0
EPS = 1e-6
D_FF = 2816
IN_SIZES = (3 * N_GROUPS * A_WIDTH, B_CONV_CH, B_V, B_HEADS, B_HEADS, C_QK, C_QK, C_V, C_V, 3 * D_MODEL)
IN_OFFS = tuple(int(v) for v in np.cumsum((0,) + IN_SIZES))

ROW_BLOCK = 128
SUBLANES = 8
LANES = 128
VMEM_LIMIT = 48 * 1024 * 1024


def _cparams(sem):
    return pltpu.CompilerParams(dimension_semantics=sem, vmem_limit_bytes=VMEM_LIMIT)


def _rms(x, g):
    return x * lax.rsqrt(jnp.mean(x * x, axis=-1, keepdims=True) + EPS) * g


def _silu(x):
    return x * jax.nn.sigmoid(x)


def _softplus(x):
    return jnp.maximum(x, 0.0) + jnp.log(1.0 + jnp.exp(-jnp.abs(x)))


def _dot(a, b):
    return jnp.dot(a.astype(BF16), b.astype(BF16), preferred_element_type=F32)


def _dot_nt(a, b):
    return lax.dot_general(a.astype(BF16), b.astype(BF16), (((1,), (1,)), ((), ())), preferred_element_type=F32)


def _dot_tn(a, b):
    return lax.dot_general(a.astype(BF16), b.astype(BF16), (((0,), (0,)), ((), ())), preferred_element_type=F32)


def _proj_a_kernel(x_ref, g1_ref, w_ref, qg_ref, kg_ref, cos_ref, sin_ref, q_ref, k_ref, v_ref):
    h = _rms(x_ref[...], g1_ref[...]).astype(BF16)
    cos = cos_ref[...]
    sin = sin_ref[...]

    def norm_rope(seg, g):
        y = _rms(seg, g)
        return y * cos + pltpu.roll(y, A_HD // 2, 1) * sin

    for j in range(3 * N_GROUPS):
        acc = jnp.dot(h, w_ref[:, j * A_WIDTH:(j + 1) * A_WIDTH], preferred_element_type=F32)
        if j < N_GROUPS:
            for hh in range(A_HEADS):
                sl = slice(hh * A_HD, (hh + 1) * A_HD)
                q_ref[:, j * A_WIDTH + hh * A_HD:j * A_WIDTH + (hh + 1) * A_HD] = (
                    norm_rope(acc[:, sl], qg_ref[...]) * (A_HD ** -0.5))
        elif j < 2 * N_GROUPS:
            jj = j - N_GROUPS
            for hh in range(A_HEADS):
                sl = slice(hh * A_HD, (hh + 1) * A_HD)
                k_ref[:, jj * A_WIDTH + hh * A_HD:jj * A_WIDTH + (hh + 1) * A_HD] = norm_rope(acc[:, sl], kg_ref[...])
        else:
            jj = j - 2 * N_GROUPS
            v_ref[:, jj * A_WIDTH:(jj + 1) * A_WIDTH] = acc


def _proj_a(x, g1, w, qg, kg, cos, sin, seq_len):
    n = x.shape[0]
    tm = 256
    nw = N_GROUPS * A_WIDTH
    tab_blocks = cos.shape[0] // tm
    row = lambda i: (i, 0)
    fixed = lambda i: (0, 0)
    tab = (lambda i: (i % tab_blocks, 0)) if tab_blocks > 1 else fixed
    return pl.pallas_call(
        _proj_a_kernel,
        grid=(n // tm,),
        in_specs=[
            pl.BlockSpec((tm, D_MODEL), row),
            pl.BlockSpec((1, D_MODEL), fixed),
            pl.BlockSpec((D_MODEL, 3 * nw), fixed),
            pl.BlockSpec((1, A_HD), fixed),
            pl.BlockSpec((1, A_HD), fixed),
            pl.BlockSpec((tm, A_HD), tab),
            pl.BlockSpec((tm, A_HD), tab),
        ],
        out_specs=[pl.BlockSpec((tm, nw), row)] * 3,
        out_shape=[jax.ShapeDtypeStruct((n, nw), F32)] * 3,
        compiler_params=_cparams(("parallel",)),
        name="proj_a",
    )(x, g1, w, qg, kg, cos, sin)


def _attn_prompt_kernel(q_ref, kc_ref, kp_ref, vc_ref, vp_ref, o_ref, lse_ref):
    n = pl.program_id(2)
    qi = lax.broadcasted_iota(jnp.int32, (ROW_BLOCK, ROW_BLOCK), 0)
    kj = lax.broadcasted_iota(jnp.int32, (ROW_BLOCK, ROW_BLOCK), 1)
    cur_ok = kj <= qi
    prev_ok = jnp.logical_and(kj >= qi, n > 0)
    lane = lax.broadcasted_iota(jnp.int32, (ROW_BLOCK, LANES), 1)
    lse_blk = jnp.zeros((ROW_BLOCK, LANES), F32)
    neg = -jnp.inf
    for hh in range(A_HEADS):
        sl = slice(hh * A_HD, (hh + 1) * A_HD)
        q = q_ref[:, sl]
        s_cur = jnp.where(cur_ok, _dot_nt(q, kc_ref[:, sl]), neg)
        s_prev = jnp.where(prev_ok, _dot_nt(q, kp_ref[:, sl]), neg)
        m = jnp.maximum(jnp.max(s_cur, axis=-1, keepdims=True), jnp.max(s_prev, axis=-1, keepdims=True))
        p_cur = jnp.exp(s_cur - m)
        p_prev = jnp.exp(s_prev - m)
        den = jnp.sum(p_cur, axis=-1, keepdims=True) + jnp.sum(p_prev, axis=-1, keepdims=True)
        o = (_dot(p_cur, vc_ref[:, sl]) + _dot(p_prev, vp_ref[:, sl])) / den
        o_ref[:, sl] = o.astype(o_ref.dtype)
        lse = m + jnp.log(den)
        lse_blk = jnp.where(lane // 32 == hh, lse, lse_blk)
    lse_ref[...] = lse_blk


def _attn_prompt(q, k, v, gi, batch, seq_len):
    _, dil = A_GROUPS[gi]
    ls = seq_len // dil
    nblk = ls // ROW_BLOCK
    width = N_GROUPS * A_WIDTH
    qv = q.reshape(batch, ls, dil * width)
    kv = k.reshape(batch, ls, dil * width)
    vv = v.reshape(batch, ls, dil * width)
    cur = lambda b, r, n: (b, n, r * N_GROUPS + gi)
    prev = lambda b, r, n: (b, jnp.maximum(n - 1, 0), r * N_GROUPS + gi)
    out = lambda b, r, n: (b, n, r)
    blk = (None, ROW_BLOCK, A_WIDTH)
    o, lse = pl.pallas_call(
        _attn_prompt_kernel,
        grid=(batch, dil, nblk),
        in_specs=[pl.BlockSpec(blk, cur), pl.BlockSpec(blk, cur), pl.BlockSpec(blk, prev),
                  pl.BlockSpec(blk, cur), pl.BlockSpec(blk, prev)],
        out_specs=[pl.BlockSpec(blk, out), pl.BlockSpec((None, ROW_BLOCK, LANES), out)],
        out_shape=[jax.ShapeDtypeStruct((batch, ls, dil * A_WIDTH), BF16),
                   jax.ShapeDtypeStruct((batch, ls, dil * LANES), F32)],
        compiler_params=_cparams(("parallel", "parallel", "arbitrary")),
        name=f"attn_prompt_g{gi}",
    )(qv, kv, kv, vv, vv)
    return o.reshape(batch * seq_len, A_WIDTH), lse.reshape(batch * seq_len, LANES)


def _attn_sample_kernel(q_ref, kn_ref, vn_ref, cache_ref, o_ref, lse_ref, *, dil, n_new):
    row = lax.broadcasted_iota(jnp.int32, (A_KEYS, 1), 0)
    trow = lax.broadcasted_iota(jnp.int32, (n_new, 1), 0)
    lane = lax.broadcasted_iota(jnp.int32, (1, LANES), 1)
    neg = -jnp.inf
    kv_width = 2 * A_WIDTH
    for s in range(n_new):
        res = s % dil
        first = s // dil
        new_ok = jnp.logical_and(trow <= s, (s - trow) % dil == 0)
        lse_row = jnp.zeros((1, LANES), F32)
        for hh in range(A_HEADS):
            sl = slice(hh * A_HD, (hh + 1) * A_HD)
            q = q_ref[s:s + 1, sl]
            kc = cache_ref[:, res * kv_width + hh * A_HD:res * kv_width + (hh + 1) * A_HD]
            vc = cache_ref[:, res * kv_width + A_WIDTH + hh * A_HD:res * kv_width + A_WIDTH + (hh + 1) * A_HD]
            sc = jnp.sum(kc * q, axis=-1, keepdims=True)
            if first > 0:
                sc = jnp.where(row >= first, sc, neg)
            sn = jnp.where(new_ok, jnp.sum(kn_ref[:, sl] * q, axis=-1, keepdims=True), neg)
            m = jnp.maximum(jnp.max(sc, axis=0, keepdims=True), jnp.max(sn, axis=0, keepdims=True))
            pc = jnp.exp(sc - m)
            pn = jnp.exp(sn - m)
            den = jnp.sum(pc, axis=0, keepdims=True) + jnp.sum(pn, axis=0, keepdims=True)
            o = (jnp.sum(pc * vc, axis=0, keepdims=True) + jnp.sum(pn * vn_ref[:, sl], axis=0, keepdims=True)) / den
            o_ref[s:s + 1, sl] = o.astype(o_ref.dtype)
            lse_row = jnp.where(lane // 32 == hh, m + jnp.log(den), lse_row)
        lse_ref[s:s + 1, :] = lse_row


def _attn_sample(q, k, v, cache, layer, gi, batch, n_new):
    win, dil = A_GROUPS[gi]
    depth = cache.shape[0]
    assert cache.shape[2] == win and win // dil == A_KEYS
    kv_width = 2 * A_WIDTH
    n_res = min(dil, n_new)
    cv = cache.reshape(depth, batch, A_KEYS, dil * kv_width)
    width = N_GROUPS * A_WIDTH
    qv = q.reshape(batch, n_new, width)
    kv = k.reshape(batch, n_new, width)
    vv = v.reshape(batch, n_new, width)
    grp = lambda b: (b, 0, gi)
    o, lse = pl.pallas_call(
        functools.partial(_attn_sample_kernel, dil=dil, n_new=n_new),
        grid=(batch,),
        in_specs=[pl.BlockSpec((None, n_new, A_WIDTH), grp)] * 3
        + [pl.BlockSpec((None, None, A_KEYS, n_res * kv_width), lambda b: (layer, b, 0, 0))],
        out_specs=[pl.BlockSpec((None, n_new, A_WIDTH), lambda b: (b, 0, 0)),
                   pl.BlockSpec((None, n_new, LANES), lambda b: (b, 0, 0))],
        out_shape=[jax.ShapeDtypeStruct((batch, n_new, A_WIDTH), BF16),
                   jax.ShapeDtypeStruct((batch, n_new, LANES), F32)],
        compiler_params=_cparams(("parallel",)),
        name=f"attn_sample_g{gi}",
    )(qv, kv, vv, cv)
    return o.reshape(batch * n_new, A_WIDTH), lse.reshape(batch * n_new, LANES)


def _proj_b_kernel(x_ref, g1_ref, wqkv_ref, wz_ref, wba_ref, wbat_ref, p_ref, z_ref, bac_ref, bar_ref):
    h = _rms(x_ref[...], g1_ref[...]).astype(BF16)
    for j in range(3):
        sl = slice(j * B_QK, (j + 1) * B_QK)
        p_ref[:, sl] = jnp.dot(h, wqkv_ref[:, sl], preferred_element_type=F32)
    z_ref[...] = jnp.dot(h, wz_ref[...], preferred_element_type=F32)
    bac_ref[...] = jnp.dot(h, wba_ref[...], preferred_element_type=F32)
    bar_ref[...] = lax.dot_general(wbat_ref[...], h, (((1,), (1,)), ((), ())), preferred_element_type=F32)


def _proj_b(x, g1, wqkv, wz, wba, wbat):
    n = x.shape[0]
    tm = 256
    row = lambda i: (i, 0)
    fixed = lambda i: (0, 0)
    return pl.pallas_call(
        _proj_b_kernel,
        grid=(n // tm,),
        in_specs=[
            pl.BlockSpec((tm, D_MODEL), row),
            pl.BlockSpec((1, D_MODEL), fixed),
            pl.BlockSpec((D_MODEL, B_CONV_CH), fixed),
            pl.BlockSpec((D_MODEL, B_V), fixed),
            pl.BlockSpec((D_MODEL, LANES), fixed),
            pl.BlockSpec((2 * SUBLANES, D_MODEL), fixed),
        ],
        out_specs=[pl.BlockSpec((tm, B_CONV_CH), row), pl.BlockSpec((tm, B_V), row),
                   pl.BlockSpec((tm, LANES), row), pl.BlockSpec((2 * SUBLANES, tm), lambda i: (0, i))],
        out_shape=[jax.ShapeDtypeStruct((n, B_CONV_CH), F32), jax.ShapeDtypeStruct((n, B_V), F32),
                   jax.ShapeDtypeStruct((n, LANES), F32), jax.ShapeDtypeStruct((2 * SUBLANES, n), F32)],
        compiler_params=_cparams(("parallel",)),
        name="proj_b",
    )(x, g1, wqkv, wz, wba, wbat)


def _b_prep_kernel(p_ref, halo_ref, cst_ref, bac_ref, bar_ref, cw_ref, alog_r_ref, dt_r_ref, alog_c_ref, dt_c_ref,
                   qg_ref, kd_ref, u_ref, w_ref, attn_ref, egl_ref, e_scr, *, blocks_per_seq, t_valid):
    i = pl.program_id(0)
    blk = i % blocks_per_seq
    rows = ROW_BLOCK

    before = jnp.where(blk == 0, cst_ref[...], halo_ref[...])
    e_scr[0:SUBLANES, :] = before
    e_scr[SUBLANES:SUBLANES + rows, :] = p_ref[...]
    xc = e_scr[SUBLANES:SUBLANES + rows, :] * cw_ref[B_CONV - 1:B_CONV, :]
    for kk in range(1, B_CONV):
        xc = xc + e_scr[SUBLANES - kk:SUBLANES - kk + rows, :] * cw_ref[B_CONV - 1 - kk:B_CONV - kk, :]
    act = _silu(xc)

    ri = lax.broadcasted_iota(jnp.int32, (rows, LANES), 0)
    li = lax.broadcasted_iota(jnp.int32, (rows, LANES), 1)
    li16 = lax.broadcasted_iota(jnp.int32, (2 * SUBLANES, LANES), 1)
    li1 = lax.broadcasted_iota(jnp.int32, (1, LANES), 1)
    masked = t_valid < blocks_per_seq * rows
    if masked:
        row_ok = (blk * rows + ri) < t_valid
        col_ok = (blk * rows + li16) < t_valid
        act = jnp.where(ri[:, 0:1] + blk * rows < t_valid, act, 0.0)

    head_lane = jnp.logical_and(li1 >= B_HEADS, li1 < 2 * B_HEADS)
    a_r = jnp.where(head_lane, -jnp.exp(alog_r_ref[...]), 0.0)
    g_col = a_r * _softplus(bac_ref[...] + dt_r_ref[...])
    si = lax.broadcasted_iota(jnp.int32, (2 * SUBLANES, 1), 0)
    head_sub = jnp.logical_and(si >= B_HEADS, si < 2 * B_HEADS)
    a_c = jnp.where(head_sub, -jnp.exp(alog_c_ref[...]), 0.0)
    g_row = a_c * _softplus(bar_ref[...] + dt_c_ref[...])
    if masked:
        g_col = jnp.where(row_ok, g_col, 0.0)
        g_row = jnp.where(col_ok, g_row, 0.0)

    rpos = ri % CHUNK
    lpos = li16 % CHUNK
    gc = g_col
    rev = g_col
    gcr = g_row
    step = 1
    while step < CHUNK:
        gc = gc + jnp.where(rpos >= step, pltpu.roll(gc, step, 0), 0.0)
        rev = rev + jnp.where(rpos < CHUNK - step, pltpu.roll(rev, rows - step, 0), 0.0)
        gcr = gcr + jnp.where(lpos >= step, pltpu.roll(gcr, step, 1), 0.0)
        step *= 2
    rev = rev - g_col
    egl_ref[...] = jnp.exp(gc + rev)

    same = (ri // CHUNK) == (li // CHUNK)
    incl = jnp.logical_and(same, ri >= li)
    strict = jnp.logical_and(same, ri > li)
    eye = (ri == li).astype(F32)

    for hh in range(B_HEADS):
        sl = slice(hh * B_DK, (hh + 1) * B_DK)
        gc_c = gc[:, B_HEADS + hh:B_HEADS + hh + 1]
        gc_r = gcr[B_HEADS + hh:B_HEADS + hh + 1, :]
        dec = jnp.where(incl, jnp.exp(jnp.where(incl, gc_c - gc_r, 0.0)), 0.0)
        q = act[:, sl]
        q = q * lax.rsqrt(jnp.sum(q * q, axis=-1, keepdims=True) + EPS) * (B_DK ** -0.5)
        k = act[:, B_QK + hh * B_DK:B_QK + (hh + 1) * B_DK]
        k = k * lax.rsqrt(jnp.sum(k * k, axis=-1, keepdims=True) + EPS)
        v = act[:, 2 * B_QK + hh * B_DV:2 * B_QK + (hh + 1) * B_DV]
        beta = jax.nn.sigmoid(bac_ref[:, hh:hh + 1])
        kb = k * beta
        kbf = k.astype(BF16)
        low = jnp.where(strict, dec * _dot_nt(kb, kbf), 0.0)
        attn_ref[:, sl] = dec * _dot_nt(q, kbf)
        tinv = eye - low
        pw = low
        sq = 2
        while sq < CHUNK:
            pw = _dot(pw, pw)
            tinv = tinv + _dot(tinv, pw)
            sq *= 2
        u_ref[:, sl] = _dot(tinv, v * beta)
        w_ref[:, sl] = _dot(tinv, kb * jnp.exp(gc_c))
        qg_ref[:, sl] = q * jnp.exp(gc_c)
        kd_ref[:, sl] = k * jnp.exp(rev[:, B_HEADS + hh:B_HEADS + hh + 1])


def _b_prep(p, cstate, bac, bar, cw, alog_r, dt_r, alog_c, dt_c, seq_len, t_valid):
    n = p.shape[0]
    bps = seq_len // ROW_BLOCK
    row = lambda i: (i, 0)
    fixed = lambda i: (0, 0)
    per_row = ROW_BLOCK // SUBLANES
    wide = jax.ShapeDtypeStruct((n, B_V), F32)
    return pl.pallas_call(
        functools.partial(_b_prep_kernel, blocks_per_seq=bps, t_valid=t_valid),
        grid=(n // ROW_BLOCK,),
        in_specs=[
            pl.BlockSpec((ROW_BLOCK, B_CONV_CH), row),
            pl.BlockSpec((SUBLANES, B_CONV_CH), lambda i: (jnp.maximum(i * per_row - 1, 0), 0)),
            pl.BlockSpec((None, SUBLANES, B_CONV_CH), lambda i: (i // bps, 0, 0)),
            pl.BlockSpec((ROW_BLOCK, LANES), row),
            pl.BlockSpec((2 * SUBLANES, ROW_BLOCK), lambda i: (0, i)),
            pl.BlockSpec((B_CONV, B_CONV_CH), fixed),
            pl.BlockSpec((1, LANES), fixed),
            pl.BlockSpec((1, LANES), fixed),
            pl.BlockSpec((2 * SUBLANES, 1), fixed),
            pl.BlockSpec((2 * SUBLANES, 1), fixed),
        ],
        out_specs=[pl.BlockSpec((ROW_BLOCK, B_V), row)] * 5 + [pl.BlockSpec((ROW_BLOCK, LANES), row)],
        out_shape=[wide] * 5 + [jax.ShapeDtypeStruct((n, LANES), F32)],
        scratch_shapes=[pltpu.VMEM((SUBLANES + ROW_BLOCK, B_CONV_CH), F32)],
        compiler_params=_cparams(("parallel",)),
        name="b_prep",
    )(p, p, cstate, bac, bar, cw, alog_r, dt_r, alog_c, dt_c)


def _b_scan_kernel(qg_ref, kd_ref, u_ref, w_ref, attn_ref, egl_ref, z_ref, s0_ref, gout_ref, o_ref, s_ref, *, nb):
    c = pl.program_id(1)

    @pl.when(c == 0)
    def _():
        s_ref[...] = s0_ref[...]

    half = c % (ROW_BLOCK // CHUNK)
    rgrp = lax.broadcasted_iota(jnp.int32, (ROW_BLOCK, B_DV), 0) // CHUNK
    here = rgrp == half
    for b in range(nb):
        for hh in range(B_HEADS):
            sl = slice(hh * B_DV, (hh + 1) * B_DV)
            s = s_ref[b, hh]
            v_new = u_ref[b, :, sl] - _dot(w_ref[b, :, sl], s)
            v_full = jnp.where(here, jnp.concatenate([v_new] * (ROW_BLOCK // CHUNK), axis=0), 0.0)
            o = _dot(qg_ref[b, :, sl], s) + _dot(attn_ref[b, :, sl], v_full)
            decay = egl_ref[b, 0:1, B_HEADS + hh:B_HEADS + hh + 1]
            s_ref[b, hh] = s * decay + _dot_tn(kd_ref[b, :, sl], v_new)
            o_ref[b, :, sl] = (_rms(o, gout_ref[...]) * _silu(z_ref[b, :, sl])).astype(o_ref.dtype)


def _b_scan(qg, kd, u, w, attn, egl, z, s0, gout, batch, seq_len):
    nb = 4
    nchunk = seq_len // CHUNK
    v3 = lambda a: a.reshape(batch, seq_len, a.shape[-1])
    rows = lambda bi, c: (bi, c, 0)
    state = lambda bi, c: (bi, 0, 0, 0)
    wide = pl.BlockSpec((nb, CHUNK, B_V), rows)
    o, s_new = pl.pallas_call(
        functools.partial(_b_scan_kernel, nb=nb),
        grid=(batch // nb, nchunk),
        in_specs=[wide] * 5 + [pl.BlockSpec((nb, CHUNK, LANES), rows), wide,
                               pl.BlockSpec((nb, B_HEADS, B_DK, B_DV), state),
                               pl.BlockSpec((1, B_DV), lambda bi, c: (0, 0))],
        out_specs=[wide, pl.BlockSpec((nb, B_HEADS, B_DK, B_DV), state)],
        out_shape=[jax.ShapeDtypeStruct((batch, seq_len, B_V), BF16),
                   jax.ShapeDtypeStruct((batch, B_HEADS, B_DK, B_DV), F32)],
        compiler_params=_cparams(("parallel", "arbitrary")),
        name="b_scan",
    )(v3(qg), v3(kd), v3(u), v3(w), v3(attn), v3(egl), v3(z), s0, gout)
    return o.reshape(batch * seq_len, B_V), s_new


def _proj_c_kernel(x_ref, g1_ref, w_ref, cos_ref, sin_ref, q_ref, k_ref, v_ref, z_ref):
    h = _rms(x_ref[...], g1_ref[...]).astype(BF16)
    cos = cos_ref[...]
    sin = sin_ref[...]
    lane = lax.broadcasted_iota(jnp.int32, cos.shape, 1)
    first_half = (lane % C_DK) < (C_DK // 2)

    def rope(seg):
        swapped = jnp.where(first_half, pltpu.roll(seg, LANES - C_DK // 2, 1), pltpu.roll(seg, C_DK // 2, 1))
        return seg * cos + swapped * sin

    qk = jnp.dot(h, w_ref[:, 0:2 * C_QK], preferred_element_type=F32)
    for j in range(2 * C_QK // LANES):
        seg = rope(qk[:, j * LANES:(j + 1) * LANES])
        if j < C_QK // LANES:
            q_ref[:, j * LANES:(j + 1) * LANES] = seg
        else:
            jj = j - C_QK // LANES
            k_ref[:, jj * LANES:(jj + 1) * LANES] = seg * (C_DK ** -0.5)
    v_ref[...] = jnp.dot(h, w_ref[:, 2 * C_QK:2 * C_QK + C_V], preferred_element_type=F32)
    z_ref[...] = jnp.dot(h, w_ref[:, 2 * C_QK + C_V:2 * C_QK + 2 * C_V], preferred_element_type=F32)


def _proj_c(x, g1, w, cos, sin):
    n = x.shape[0]
    tm = 256
    tab_blocks = cos.shape[0] // tm
    row = lambda i: (i, 0)
    fixed = lambda i: (0, 0)
    tab = (lambda i: (i % tab_blocks, 0)) if tab_blocks > 1 else fixed
    return pl.pallas_call(
        _proj_c_kernel,
        grid=(n // tm,),
        in_specs=[
            pl.BlockSpec((tm, D_MODEL), row),
            pl.BlockSpec((1, D_MODEL), fixed),
            pl.BlockSpec((D_MODEL, 2 * C_QK + 2 * C_V), fixed),
            pl.BlockSpec((tm, LANES), tab),
            pl.BlockSpec((tm, LANES), tab),
        ],
        out_specs=[pl.BlockSpec((tm, C_QK), row), pl.BlockSpec((tm, C_QK), row),
                   pl.BlockSpec((tm, C_V), row), pl.BlockSpec((tm, C_V), row)],
        out_shape=[jax.ShapeDtypeStruct((n, C_QK), F32), jax.ShapeDtypeStruct((n, C_QK), F32),
                   jax.ShapeDtypeStruct((n, C_V), F32), jax.ShapeDtypeStruct((n, C_V), F32)],
        compiler_params=_cparams(("parallel",)),
        name="proj_c",
    )(x, g1, w, cos, sin)


def _log_gamma(hh):
    return math.log1p(-(2.0 ** (-5.0 - hh)))


def _c_scan_kernel(q_ref, k_ref, v_ref, z_ref, r0_ref, gout_ref, o_ref, r_ref, *, nb, t_valid):
    c = pl.program_id(1)
    rows = ROW_BLOCK

    @pl.when(c == 0)
    def _():
        r_ref[...] = r0_ref[...]

    left = jnp.clip(t_valid - c * rows, 0, rows)
    ri = lax.broadcasted_iota(jnp.int32, (rows, rows), 0)
    ci = lax.broadcasted_iota(jnp.int32, (rows, rows), 1)
    cnt_i = jnp.minimum(ri + 1, left).astype(F32)
    cnt_j = jnp.minimum(ci + 1, left).astype(F32)
    incl = ri >= ci
    steps = jnp.where(incl, cnt_i - cnt_j, 0.0)
    cnt_col = cnt_i[:, 0:1]
    left_f = left.astype(F32)
    qk_lane = lax.broadcasted_iota(jnp.int32, (1, C_QK), 1) // C_DK
    lg_lane = jnp.zeros((1, C_QK), F32)
    for hh in range(C_HEADS):
        lg_lane = jnp.where(qk_lane == hh, _log_gamma(hh), lg_lane)
    qk_sub = lax.broadcasted_iota(jnp.int32, (C_QK, 1), 0) // C_DK
    lg_sub = jnp.zeros((C_QK, 1), F32)
    for hh in range(C_HEADS):
        lg_sub = jnp.where(qk_sub == hh, _log_gamma(hh), lg_sub)
    q_scale = jnp.exp(cnt_col * lg_lane)
    k_scale = jnp.exp((left_f - cnt_col) * lg_lane)
    r_scale = jnp.exp(left_f * lg_sub)
    row_ok = (lax.broadcasted_iota(jnp.int32, (rows, 1), 0) + c * rows) < t_valid
    diag = (lax.broadcasted_iota(jnp.int32, (C_QK, C_V), 0) // C_DK) == (
        lax.broadcasted_iota(jnp.int32, (C_QK, C_V), 1) // C_DV)

    for b in range(nb):
        q = q_ref[b]
        k = jnp.where(row_ok, k_ref[b], 0.0)
        v = v_ref[b]
        r = r_ref[b]
        inter = _dot(q * q_scale, r)
        r_ref[b] = r * r_scale + jnp.where(diag, _dot_tn(k * k_scale, v), 0.0)
        for hh in range(C_HEADS):
            sl = slice(hh * C_DV, (hh + 1) * C_DV)
            k_h = jnp.where(qk_lane == hh, k, 0.0)
            att = jnp.exp(steps * _log_gamma(hh)) * _dot_nt(q, k_h)
            att = jnp.where(incl, att, 0.0)
            o = inter[:, sl] + _dot(att, v[:, sl])
            o_ref[b, :, sl] = (_rms(o, gout_ref[...]) * _silu(z_ref[b, :, sl])).astype(o_ref.dtype)


def _c_scan(q, k, v, z, r0, gout, batch, seq_len, t_valid):
    nb = 4
    nblk = seq_len // ROW_BLOCK
    v3 = lambda a: a.reshape(batch, seq_len, a.shape[-1])
    rows = lambda bi, c: (bi, c, 0)
    state = lambda bi, c: (bi, 0, 0)
    o, r_new = pl.pallas_call(
        functools.partial(_c_scan_kernel, nb=nb, t_valid=t_valid),
        grid=(batch // nb, nblk),
        in_specs=[pl.BlockSpec((nb, ROW_BLOCK, C_QK), rows), pl.BlockSpec((nb, ROW_BLOCK, C_QK), rows),
                  pl.BlockSpec((nb, ROW_BLOCK, C_V), rows), pl.BlockSpec((nb, ROW_BLOCK, C_V), rows),
                  pl.BlockSpec((nb, C_QK, C_V), state), pl.BlockSpec((1, C_DV), lambda bi, c: (0, 0))],
        out_specs=[pl.BlockSpec((nb, ROW_BLOCK, C_V), rows), pl.BlockSpec((nb, C_QK, C_V), state)],
        out_shape=[jax.ShapeDtypeStruct((batch, seq_len, C_V), BF16),
                   jax.ShapeDtypeStruct((batch, C_QK, C_V), F32)],
        compiler_params=_cparams(("parallel", "arbitrary")),
        name="c_scan",
    )(v3(q), v3(k), v3(v), v3(z), r0, gout)
    return o.reshape(batch * seq_len, C_V), r_new


def _merge_kernel(x_ref, g1_ref, wg_ref, o0_ref, o1_ref, o2_ref, l0_ref, l1_ref, l2_ref, ob_ref, oc_ref,
                  wa_ref, wb_ref, wc_ref, wo_ref, y_ref):
    x = x_ref[...]
    h = _rms(x, g1_ref[...]).astype(BF16)
    heads = []
    for hh in range(A_HEADS):
        sl = slice(hh * A_HD, (hh + 1) * A_HD)
        ls = [r[:, 32 * hh:32 * hh + 1] for r in (l0_ref, l1_ref, l2_ref)]
        m = jnp.maximum(jnp.maximum(ls[0], ls[1]), ls[2])
        es = [jnp.exp(l - m) for l in ls]
        tot = es[0] + es[1] + es[2]
        acc = (es[0] / tot) * o0_ref[:, sl].astype(F32)
        acc = acc + (es[1] / tot) * o1_ref[:, sl].astype(F32)
        acc = acc + (es[2] / tot) * o2_ref[:, sl].astype(F32)
        heads.append(acc.astype(BF16))
    o_a = jnp.concatenate(heads, axis=1)
    merged = None
    for gi, (o_g, w_ref) in enumerate(((o_a, wa_ref), (ob_ref[...], wb_ref), (oc_ref[...], wc_ref))):
        gate = jax.nn.sigmoid(jnp.dot(h, wg_ref[:, gi * D_MODEL:(gi + 1) * D_MODEL], preferred_element_type=F32))
        term = gate * jnp.dot(o_g, w_ref[...], preferred_element_type=F32)
        merged = term if merged is None else merged + term
    y_ref[...] = x + jnp.dot(merged.astype(BF16), wo_ref[...], preferred_element_type=F32)


def _merge(x, g1, wg, o_groups, lses, o_b, o_c, wa, wb, wc, wo):
    n = x.shape[0]
    tm = 256
    row = lambda i: (i, 0)
    fixed = lambda i: (0, 0)
    half = pl.BlockSpec((tm, A_WIDTH), row)
    lse = pl.BlockSpec((tm, LANES), row)
    wbr = pl.BlockSpec((A_WIDTH, D_MODEL), fixed)
    return pl.pallas_call(
        _merge_kernel,
        grid=(n // tm,),
        in_specs=[pl.BlockSpec((tm, D_MODEL), row), pl.BlockSpec((1, D_MODEL), fixed),
                  pl.BlockSpec((D_MODEL, 3 * D_MODEL), fixed),
                  half, half, half, lse, lse, lse, half, half, wbr, wbr, wbr,
                  pl.BlockSpec((D_MODEL, D_MODEL), fixed)],
        out_specs=pl.BlockSpec((tm, D_MODEL), row),
        out_shape=jax.ShapeDtypeStruct((n, D_MODEL), F32),
        compiler_params=_cparams(("parallel",)),
        name="merge",
    )(x, g1, wg, *o_groups, *lses, o_b, o_c, wa, wb, wc, wo)


def _ffn_kernel(x_ref, g2_ref, wg_ref, wu_ref, wo_ref, y_ref, h_scr, acc_scr):
    j = pl.program_id(1)

    @pl.when(j == 0)
    def _():
        h_scr[...] = _rms(x_ref[...], g2_ref[...]).astype(BF16)
        acc_scr[...] = jnp.zeros_like(acc_scr)

    h = h_scr[...]
    gate = jnp.dot(h, wg_ref[...], preferred_element_type=F32)
    up = jnp.dot(h, wu_ref[...], preferred_element_type=F32)
    acc_scr[...] += jnp.dot((_silu(gate) * up).astype(BF16), wo_ref[...], preferred_element_type=F32)

    @pl.when(j == pl.num_programs(1) - 1)
    def _():
        y_ref[...] = x_ref[...] + acc_scr[...]


def _ffn(x, g2, w_in, w_out):
    n = x.shape[0]
    tm = min(n, 512)
    tf = 256
    nf = D_FF // tf
    row = lambda i, j: (i, 0)
    return pl.pallas_call(
        _ffn_kernel,
        grid=(n // tm, nf),
        in_specs=[pl.BlockSpec((tm, D_MODEL), row), pl.BlockSpec((1, D_MODEL), lambda i, j: (0, 0)),
                  pl.BlockSpec((D_MODEL, tf), lambda i, j: (0, j)),
                  pl.BlockSpec((D_MODEL, tf), lambda i, j: (0, nf + j)),
                  pl.BlockSpec((tf, D_MODEL), lambda i, j: (j, 0))],
        out_specs=pl.BlockSpec((tm, D_MODEL), row),
        out_shape=jax.ShapeDtypeStruct((n, D_MODEL), F32),
        scratch_shapes=[pltpu.VMEM((tm, D_MODEL), BF16), pltpu.VMEM((tm, D_MODEL), F32)],
        compiler_params=_cparams(("parallel", "arbitrary")),
        name="ffn",
    )(x, g2, w_in, w_in, w_out)


def _rope_tables(pos, hd, reps):
    inv = ROPE_THETA ** (-jnp.arange(0, hd, 2, dtype=F32) / hd)
    ang = pos.astype(F32)[:, None] * inv[None, :]
    cos = jnp.cos(ang)
    sin = jnp.sin(ang)
    cos2 = jnp.concatenate([cos, cos], axis=1)
    sin2 = jnp.concatenate([-sin, sin], axis=1)
    return jnp.tile(cos2, (1, reps)), jnp.tile(sin2, (1, reps))


def _pad_rows(a, batch, t, t_pad):
    if t == t_pad:
        return a
    a = a.reshape(batch, t, a.shape[-1])
    a = jnp.pad(a, ((0, 0), (0, t_pad - t), (0, 0)))
    return a.reshape(batch * t_pad, a.shape[-1])


def _unpad_rows(a, batch, t, t_pad):
    if t == t_pad:
        return a
    return a.reshape(batch, t_pad, a.shape[-1])[:, :t].reshape(batch * t, a.shape[-1])


def _layer(x, pos, batch, t, lw, caches, layer, conv_state, s0, r0):
    n = batch * t
    prompt = caches is None
    reps = max(1, 256 // t)
    cos_a, sin_a = _rope_tables(pos, A_HD, 1)
    cos_c, sin_c = _rope_tables(pos, C_DK, LANES // C_DK)
    if reps > 1:
        cos_a, sin_a, cos_c, sin_c = (jnp.tile(a, (reps, 1)) for a in (cos_a, sin_a, cos_c, sin_c))

    q, k, v = _proj_a(x, lw["g1"], lw["w_a"], lw["qn"], lw["kn"], cos_a, sin_a, t)
    outs, lses = [], []
    for gi in range(N_GROUPS):
        if prompt:
            o, lse = _attn_prompt(q, k, v, gi, batch, t)
        else:
            o, lse = _attn_sample(q, k, v, caches[gi], layer, gi, batch, t)
        outs.append(o)
        lses.append(lse)
    k5 = k.reshape(batch, t, N_GROUPS, A_HEADS, A_HD)
    v5 = v.reshape(batch, t, N_GROUPS, A_HEADS, A_HD)
    new_kv = []
    for gi, (win, _) in enumerate(A_GROUPS):
        keep = min(win, t) if prompt else t
        new_kv.append(jnp.stack([k5[:, t - keep:, gi], v5[:, t - keep:, gi]], axis=2))

    t_pad = -(-t // ROW_BLOCK) * ROW_BLOCK
    p, z_b, bac, bar = _proj_b(x, lw["g1"], lw["w_bqkv"], lw["w_bz"], lw["w_ba"], lw["w_bat"])
    conv_new = jnp.concatenate([conv_state, p.reshape(batch, t, B_CONV_CH)], axis=1)[:, -(B_CONV - 1):]
    cst = jnp.pad(conv_state, ((0, 0), (SUBLANES - (B_CONV - 1), 0), (0, 0)))
    bar_p = _pad_rows(bar.T, batch, t, t_pad).T if t_pad != t else bar
    qg, kd, u, w, attn, egl = _b_prep(
        _pad_rows(p, batch, t, t_pad), cst, _pad_rows(bac, batch, t, t_pad), bar_p, lw["conv_w"],
        lw["alog_r"], lw["dt_r"], lw["alog_c"], lw["dt_c"], t_pad, t)
    o_b, s_new = _b_scan(qg, kd, u, w, attn, egl, _pad_rows(z_b, batch, t, t_pad), s0, lw["gb"], batch, t_pad)
    o_b = _unpad_rows(o_b, batch, t, t_pad)

    cq, ck, cv, cz = _proj_c(x, lw["g1"], lw["w_c"], cos_c, sin_c)
    r_bd = jnp.zeros((batch, C_HEADS, C_DK, C_HEADS, C_DV), F32)
    for hh in range(C_HEADS):
        r_bd = r_bd.at[:, hh, :, hh, :].set(r0[:, hh])
    o_c, r_new = _c_scan(*(_pad_rows(a, batch, t, t_pad) for a in (cq, ck, cv, cz)),
                         r_bd.reshape(batch, C_QK, C_V), lw["gc"], batch, t_pad, t)
    o_c = _unpad_rows(o_c, batch, t, t_pad)
    r_new = r_new.reshape(batch, C_HEADS, C_DK, C_HEADS, C_DV)
    r_new = jnp.stack([r_new[:, hh, :, hh, :] for hh in range(C_HEADS)], axis=1)

    x = _merge(x, lw["g1"], lw["w_g"], outs, lses, o_b, o_c, lw["w_oa"], lw["w_ob"], lw["w_oc"], lw["w_o"])
    x = _ffn(x, lw["g2"], lw["w_fi"], lw["w_fo"])
    return x, new_kv, conv_new, s_new, r_new


def _layer_weights(l, norm1_g, w_in, a_q_norm_g, a_k_norm_g, b_conv_w, b_a_log, b_dt_bias, b_out_norm_g,
                   c_out_norm_g, w_out_a, w_out_b, w_out_c, w_out, norm2_g, w_ffn_in, w_ffn_out):
    o = IN_OFFS
    wl = w_in[l]
    w_ba = wl[:, o[3]:o[5]]
    pad_r = lambda a: jnp.pad(a.reshape(1, B_HEADS), ((0, 0), (B_HEADS, LANES - 2 * B_HEADS)))
    pad_c = lambda a: jnp.pad(a.reshape(B_HEADS, 1), ((B_HEADS, 2 * SUBLANES - 2 * B_HEADS), (0, 0)))
    return dict(
        g1=norm1_g[l].reshape(1, D_MODEL), g2=norm2_g[l].reshape(1, D_MODEL),
        w_a=wl[:, o[0]:o[1]].astype(BF16),
        w_bqkv=wl[:, o[1]:o[2]].astype(BF16), w_bz=wl[:, o[2]:o[3]].astype(BF16),
        w_ba=jnp.pad(w_ba, ((0, 0), (0, LANES - 2 * B_HEADS))).astype(BF16),
        w_bat=jnp.pad(w_ba.T, ((0, 2 * SUBLANES - 2 * B_HEADS), (0, 0))).astype(BF16),
        w_c=wl[:, o[5]:o[9]].astype(BF16), w_g=wl[:, o[9]:o[10]].astype(BF16),
        qn=a_q_norm_g[l].reshape(1, A_HD), kn=a_k_norm_g[l].reshape(1, A_HD),
        conv_w=b_conv_w[l],
        alog_r=pad_r(b_a_log[l]), dt_r=pad_r(b_dt_bias[l]), alog_c=pad_c(b_a_log[l]), dt_c=pad_c(b_dt_bias[l]),
        gb=b_out_norm_g[l].reshape(1, B_DV), gc=c_out_norm_g[l].reshape(1, C_DV),
        w_oa=w_out_a[l].astype(BF16), w_ob=w_out_b[l].astype(BF16), w_oc=w_out_c[l].astype(BF16),
        w_o=w_out[l].astype(BF16), w_fi=w_ffn_in[l].astype(BF16), w_fo=w_ffn_out[l].astype(BF16),
    )


def kernel(x_prompt, x_sample, cache_a_kv0, cache_a_kv1, cache_a_kv2, state_b_conv, state_b_S, state_c_R, norm1_g, w_in, a_q_norm_g, a_k_norm_g, b_conv_w, b_a_log, b_dt_bias, b_out_norm_g, c_out_norm_g, w_out_a, w_out_b, w_out_c, w_out, norm2_g, w_ffn_in, w_ffn_out):
    bp, t = x_prompt.shape[:2]
    bs, s = x_sample.shape[:2]
    depth = w_in.shape[0]
    pos_p = jnp.arange(t)
    pos_s = PAST_LEN + jnp.arange(s)
    yp = x_prompt.reshape(bp * t, D_MODEL)
    ys = x_sample.reshape(bs * s, D_MODEL)
    caches = (cache_a_kv0, cache_a_kv1, cache_a_kv2)
    zeros_conv = jnp.zeros((bp, B_CONV - 1, B_CONV_CH), F32)
    zeros_s = jnp.zeros((bp, B_HEADS, B_DK, B_DV), F32)
    zeros_r = jnp.zeros((bp, C_HEADS, C_DK, C_DV), F32)
    acc = [[] for _ in range(12)]
    for l in range(depth):
        lw = _layer_weights(l, norm1_g, w_in, a_q_norm_g, a_k_norm_g, b_conv_w, b_a_log, b_dt_bias,
                            b_out_norm_g, c_out_norm_g, w_out_a, w_out_b, w_out_c, w_out, norm2_g,
                            w_ffn_in, w_ffn_out)
        yp, kv, cv, sn, rn = _layer(yp, pos_p, bp, t, lw, None, l, zeros_conv, zeros_s, zeros_r)
        for i, a in enumerate((kv[0], kv[1], kv[2], cv, sn, rn)):
            acc[i].append(a)
        ys, kv, cv, sn, rn = _layer(ys, pos_s, bs, s, lw, caches, l, state_b_conv[l], state_b_S[l], state_c_R[l])
        for i, a in enumerate((kv[0], kv[1], kv[2], cv, sn, rn)):
            acc[6 + i].append(a)
    return (yp.reshape(bp, t, D_MODEL), ys.reshape(bs, s, D_MODEL)) + tuple(jnp.stack(a) for a in acc)
```

```python
import functools
import math

import jax
import jax.numpy as jnp
import numpy as np
from jax import lax
from jax.experimental import pallas as pl
from jax.experimental.pallas import tpu as pltpu

F32 = jnp.float32
BF16 = jnp.bfloat16

D_MODEL = 1024
PAST_LEN = 8192
A_GROUPS = ((128, 1), (512, 4), (2048, 16))
N_GROUPS = 3
A_HEADS = 4
A_HD = 128
A_WIDTH = A_HEADS * A_HD
A_KEYS = 128
B_HEADS = 4
B_DK = 128
B_DV = 128
B_CONV = 4
B_QK = B_HEADS * B_DK
B_V = B_HEADS * B_DV
B_CONV_CH = 2 * B_QK + B_V
C_HEADS = 4
C_DK = 64
C_DV = 128
C_QK = C_HEADS * C_DK
C_V = C_HEADS * C_DV
CHUNK = 64
ROPE_THETA = 10000.0
EPS = 1e-6
D_FF = 2816
IN_SIZES = (3 * N_GROUPS * A_WIDTH, B_CONV_CH, B_V, B_HEADS, B_HEADS, C_QK, C_QK, C_V, C_V, 3 * D_MODEL)
IN_OFFS = tuple(int(v) for v in np.cumsum((0,) + IN_SIZES))

ROW_BLOCK = 128
RES = 16
SUBLANES = 8
LANES = 128
VMEM_LIMIT = 48 * 1024 * 1024


def _cparams(sem):
    return pltpu.CompilerParams(dimension_semantics=sem, vmem_limit_bytes=VMEM_LIMIT)


def _rms(x, g):
    return x * lax.rsqrt(jnp.mean(x * x, axis=-1, keepdims=True) + EPS) * g


def _silu(x):
    return x * jax.nn.sigmoid(x)


def _softplus(x):
    return jnp.maximum(x, 0.0) + jnp.log(1.0 + jnp.exp(-jnp.abs(x)))


def _dot(a, b):
    return jnp.dot(a.astype(BF16), b.astype(BF16), preferred_element_type=F32)


def _dot_nt(a, b):
    return lax.dot_general(a.astype(BF16), b.astype(BF16), (((1,), (1,)), ((), ())), preferred_element_type=F32)


def _dot_tn(a, b):
    return lax.dot_general(a.astype(BF16), b.astype(BF16), (((0,), (0,)), ((), ())), preferred_element_type=F32)


def _swap_row_grid(scr, val):
    slabs = val.shape[1] // LANES
    for c in range(slabs):
        scr[c] = val[:, c * LANES:(c + 1) * LANES]
    cols = [jnp.concatenate([scr[c, pl.ds(r, RES, stride=RES), :] for r in range(RES)], axis=0)
            for c in range(slabs)]
    return jnp.concatenate(cols, axis=1)


def _proj_a_kernel(x_ref, g1_ref, w_ref, qg_ref, kg_ref, cos_ref, sin_ref, q_ref, k_ref, v_ref, *scr, residue_major):
    x = x_ref[...]
    tm = x.shape[0]
    if residue_major:
        x = _swap_row_grid(scr[0], x)
    h = _rms(x, g1_ref[...]).astype(BF16)
    cos = cos_ref[...].reshape(tm, A_HD)
    sin = sin_ref[...].reshape(tm, A_HD)

    def norm_rope(seg, g):
        y = _rms(seg, g)
        return y * cos + pltpu.roll(y, A_HD // 2, 1) * sin

    def put(ref, col, val):
        if residue_major:
            ref[:, :, col:col + val.shape[1]] = val.reshape(RES, tm // RES, val.shape[1])
        else:
            ref[:, col:col + val.shape[1]] = val

    for j in range(3 * N_GROUPS):
        acc = jnp.dot(h, w_ref[:, j * A_WIDTH:(j + 1) * A_WIDTH], preferred_element_type=F32)
        if j < N_GROUPS:
            for hh in range(A_HEADS):
                sl = slice(hh * A_HD, (hh + 1) * A_HD)
                put(q_ref, j * A_WIDTH + hh * A_HD, norm_rope(acc[:, sl], qg_ref[...]) * (A_HD ** -0.5))
        elif j < 2 * N_GROUPS:
            jj = j - N_GROUPS
            for hh in range(A_HEADS):
                sl = slice(hh * A_HD, (hh + 1) * A_HD)
                put(k_ref, jj * A_WIDTH + hh * A_HD, norm_rope(acc[:, sl], kg_ref[...]))
        else:
            jj = j - 2 * N_GROUPS
            put(v_ref, jj * A_WIDTH, acc)


def _proj_a(x, g1, w, qg, kg, cos, sin, batch, seq_len, residue_major):
    n = x.shape[0]
    tm = RES * RES
    nw = N_GROUPS * A_WIDTH
    fixed = lambda i: (0, 0)
    common = [pl.BlockSpec((1, D_MODEL), fixed), pl.BlockSpec((D_MODEL, 3 * nw), fixed),
              pl.BlockSpec((1, A_HD), fixed), pl.BlockSpec((1, A_HD), fixed)]
    if residue_major:
        tiles = seq_len // tm
        tab = pl.BlockSpec((RES, tm // RES, A_HD), lambda i: (0, i % tiles, 0))
        out_spec = pl.BlockSpec((None, RES, tm // RES, nw), lambda i: (i // tiles, 0, i % tiles, 0))
        out_shape = jax.ShapeDtypeStruct((batch, RES, seq_len // RES, nw), F32)
        scratch = [pltpu.VMEM((D_MODEL // LANES, tm, LANES), F32)]
    else:
        assert cos.shape[0] == tm
        tab = pl.BlockSpec((tm, A_HD), fixed)
        out_spec = pl.BlockSpec((tm, nw), lambda i: (i, 0))
        out_shape = jax.ShapeDtypeStruct((n, nw), F32)
        scratch = []
    return pl.pallas_call(
        functools.partial(_proj_a_kernel, residue_major=residue_major),
        grid=(n // tm,),
        in_specs=[pl.BlockSpec((tm, D_MODEL), lambda i: (i, 0))] + common + [tab, tab],
        out_specs=[out_spec] * 3,
        out_shape=[out_shape] * 3,
        scratch_shapes=scratch,
        compiler_params=_cparams(("parallel",)),
        name="proj_a",
    )(x, g1, w, qg, kg, cos, sin)


def _attn_prompt_kernel(q_ref, kc_ref, kp_ref, vc_ref, vp_ref, o_ref, lse_ref, *, parts):
    n = pl.program_id(2)
    per = ROW_BLOCK // parts
    qi = lax.broadcasted_iota(jnp.int32, (ROW_BLOCK, ROW_BLOCK), 0)
    kj = lax.broadcasted_iota(jnp.int32, (ROW_BLOCK, ROW_BLOCK), 1)
    qi = parts * (qi % per) + qi // per
    kj = parts * (kj % per) + kj // per
    cur_ok = kj <= qi
    prev_ok = jnp.logical_and(kj >= qi, n > 0)
    lane = lax.broadcasted_iota(jnp.int32, (ROW_BLOCK, LANES), 1)
    lse_blk = jnp.zeros((ROW_BLOCK, LANES), F32)
    neg = -jnp.inf
    flat = lambda ref, sl: ref[:, :, sl].reshape(ROW_BLOCK, A_HD)
    for hh in range(A_HEADS):
        sl = slice(hh * A_HD, (hh + 1) * A_HD)
        q = flat(q_ref, sl)
        s_cur = jnp.where(cur_ok, _dot_nt(q, flat(kc_ref, sl)), neg)
        s_prev = jnp.where(prev_ok, _dot_nt(q, flat(kp_ref, sl)), neg)
        m = jnp.maximum(jnp.max(s_cur, axis=-1, keepdims=True), jnp.max(s_prev, axis=-1, keepdims=True))
        p_cur = jnp.exp(s_cur - m)
        p_prev = jnp.exp(s_prev - m)
        den = jnp.sum(p_cur, axis=-1, keepdims=True) + jnp.sum(p_prev, axis=-1, keepdims=True)
        o = (_dot(p_cur, flat(vc_ref, sl)) + _dot(p_prev, flat(vp_ref, sl))) / den
        o_ref[:, :, sl] = o.reshape(parts, per, A_HD)
        lse = m + jnp.log(den)
        lse_blk = jnp.where(lane // 32 == hh, lse, lse_blk)
    lse_ref[...] = lse_blk.reshape(parts, per, LANES)


def _attn_prompt(q, k, v, gi, batch, seq_len):
    _, dil = A_GROUPS[gi]
    parts = RES // dil
    per = ROW_BLOCK // parts
    rows = seq_len // RES
    nblk = seq_len // dil // ROW_BLOCK
    split = lambda a: a.reshape(batch, parts, dil, rows, a.shape[-1])
    cur = lambda b, r, n: (b, 0, r, n, gi)
    prev = lambda b, r, n: (b, 0, r, jnp.maximum(n - 1, 0), gi)
    out = lambda b, r, n: (b, 0, r, n, 0)
    blk = (None, parts, None, per, A_WIDTH)
    o, lse = pl.pallas_call(
        functools.partial(_attn_prompt_kernel, parts=parts),
        grid=(batch, dil, nblk),
        in_specs=[pl.BlockSpec(blk, cur), pl.BlockSpec(blk, cur), pl.BlockSpec(blk, prev),
                  pl.BlockSpec(blk, cur), pl.BlockSpec(blk, prev)],
        out_specs=[pl.BlockSpec(blk, out), pl.BlockSpec((None, parts, None, per, LANES), out)],
        out_shape=[jax.ShapeDtypeStruct((batch, parts, dil, rows, A_WIDTH), F32),
                   jax.ShapeDtypeStruct((batch, parts, dil, rows, LANES), F32)],
        compiler_params=_cparams(("parallel", "parallel", "arbitrary")),
        name=f"attn_prompt_g{gi}",
    )(split(q), split(k), split(k), split(v), split(v))
    return o.reshape(batch, RES, rows, A_WIDTH), lse.reshape(batch, RES, rows, LANES)


def _attn_sample_kernel(q_ref, kn_ref, vn_ref, cache_ref, o_ref, lse_ref, *, dil, n_new):
    row = lax.broadcasted_iota(jnp.int32, (A_KEYS, A_HEADS, 1), 0)
    trow = lax.broadcasted_iota(jnp.int32, (n_new, A_HEADS, 1), 0)
    neg = -jnp.inf
    kn = kn_ref[...]
    vn = vn_ref[...]
    for s in range(n_new):
        res = s % dil
        first = s // dil
        q = q_ref[s][None]
        kc = cache_ref[:, res, 0]
        vc = cache_ref[:, res, 1]
        sc = jnp.sum(kc * q, axis=-1, keepdims=True)
        if first > 0:
            sc = jnp.where(row >= first, sc, neg)
        new_ok = jnp.logical_and(trow <= s, (s - trow) % dil == 0)
        sn = jnp.where(new_ok, jnp.sum(kn * q, axis=-1, keepdims=True), neg)
        m = jnp.maximum(jnp.max(sc, axis=0, keepdims=True), jnp.max(sn, axis=0, keepdims=True))
        pc = jnp.exp(sc - m)
        pn = jnp.exp(sn - m)
        den = jnp.sum(pc, axis=0, keepdims=True) + jnp.sum(pn, axis=0, keepdims=True)
        o = (jnp.sum(pc * vc, axis=0, keepdims=True) + jnp.sum(pn * vn, axis=0, keepdims=True)) / den
        o_ref[s] = o[0]
        lse_ref[s] = jnp.broadcast_to((m + jnp.log(den))[0], (A_HEADS, A_HD))


def _attn_sample(q, k, v, cache, layer, gi, batch, n_new):
    win, dil = A_GROUPS[gi]
    depth = cache.shape[0]
    assert cache.shape[2] == win and win // dil == A_KEYS
    n_res = min(dil, n_new)
    cv = cache.reshape(depth, batch, A_KEYS, dil, 2, A_HEADS, A_HD)
    heads = lambda a: a.reshape(batch, n_new, N_GROUPS, A_HEADS, A_HD)
    grp = lambda b: (b, 0, gi, 0, 0)
    new = pl.BlockSpec((None, n_new, None, A_HEADS, A_HD), grp)
    out = pl.BlockSpec((None, n_new, A_HEADS, A_HD), lambda b: (b, 0, 0, 0))
    o, lse = pl.pallas_call(
        functools.partial(_attn_sample_kernel, dil=dil, n_new=n_new),
        grid=(batch,),
        in_specs=[new, new, new,
                  pl.BlockSpec((None, None, A_KEYS, n_res, 2, A_HEADS, A_HD), lambda b: (layer, b, 0, 0, 0, 0, 0))],
        out_specs=[out, out],
        out_shape=[jax.ShapeDtypeStruct((batch, n_new, A_HEADS, A_HD), F32)] * 2,
        compiler_params=_cparams(("parallel",)),
        name=f"attn_sample_g{gi}",
    )(heads(q), heads(k), heads(v), cv)
    lse = jnp.repeat(lse[..., 0], LANES // A_HEADS, axis=-1)
    return o.reshape(batch * n_new, A_WIDTH), lse.reshape(batch * n_new, LANES)


def _proj_b_kernel(x_ref, g1_ref, wqkv_ref, wz_ref, wba_ref, wbat_ref, p_ref, z_ref, bac_ref, bar_ref):
    h = _rms(x_ref[...], g1_ref[...]).astype(BF16)
    for j in range(3):
        sl = slice(j * B_QK, (j + 1) * B_QK)
        p_ref[:, sl] = jnp.dot(h, wqkv_ref[:, sl], preferred_element_type=F32)
    z_ref[...] = jnp.dot(h, wz_ref[...], preferred_element_type=F32)
    bac_ref[...] = jnp.dot(h, wba_ref[...], preferred_element_type=F32)
    bar_ref[...] = lax.dot_general(wbat_ref[...], h, (((1,), (1,)), ((), ())), preferred_element_type=F32)


def _proj_b(x, g1, wqkv, wz, wba, wbat):
    n = x.shape[0]
    tm = 256
    row = lambda i: (i, 0)
    fixed = lambda i: (0, 0)
    return pl.pallas_call(
        _proj_b_kernel,
        grid=(n // tm,),
        in_specs=[
            pl.BlockSpec((tm, D_MODEL), row),
            pl.BlockSpec((1, D_MODEL), fixed),
            pl.BlockSpec((D_MODEL, B_CONV_CH), fixed),
            pl.BlockSpec((D_MODEL, B_V), fixed),
            pl.BlockSpec((D_MODEL, LANES), fixed),
            pl.BlockSpec((2 * SUBLANES, D_MODEL), fixed),
        ],
        out_specs=[pl.BlockSpec((tm, B_CONV_CH), row), pl.BlockSpec((tm, B_V), row),
                   pl.BlockSpec((tm, LANES), row), pl.BlockSpec((2 * SUBLANES, tm), lambda i: (0, i))],
        out_shape=[jax.ShapeDtypeStruct((n, B_CONV_CH), F32), jax.ShapeDtypeStruct((n, B_V), F32),
                   jax.ShapeDtypeStruct((n, LANES), F32), jax.ShapeDtypeStruct((2 * SUBLANES, n), F32)],
        compiler_params=_cparams(("parallel",)),
        name="proj_b",
    )(x, g1, wqkv, wz, wba, wbat)


def _b_prep_kernel(p_ref, halo_ref, cst_ref, bac_ref, bar_ref, cw_ref, alog_r_ref, dt_r_ref, alog_c_ref, dt_c_ref,
                   qg_ref, kd_ref, u_ref, w_ref, attn_ref, egl_ref, e_scr, *, blocks_per_seq, t_valid):
    i = pl.program_id(0)
    blk = i % blocks_per_seq
    rows = ROW_BLOCK

    before = jnp.where(blk == 0, cst_ref[...], halo_ref[...])
    e_scr[0:SUBLANES, :] = before
    e_scr[SUBLANES:SUBLANES + rows, :] = p_ref[...]
    xc = e_scr[SUBLANES:SUBLANES + rows, :] * cw_ref[B_CONV - 1:B_CONV, :]
    for kk in range(1, B_CONV):
        xc = xc + e_scr[SUBLANES - kk:SUBLANES - kk + rows, :] * cw_ref[B_CONV - 1 - kk:B_CONV - kk, :]
    act = _silu(xc)

    ri = lax.broadcasted_iota(jnp.int32, (rows, LANES), 0)
    li = lax.broadcasted_iota(jnp.int32, (rows, LANES), 1)
    li16 = lax.broadcasted_iota(jnp.int32, (2 * SUBLANES, LANES), 1)
    li1 = lax.broadcasted_iota(jnp.int32, (1, LANES), 1)
    masked = t_valid < blocks_per_seq * rows
    if masked:
        row_ok = (blk * rows + ri) < t_valid
        col_ok = (blk * rows + li16) < t_valid
        act = jnp.where(ri[:, 0:1] + blk * rows < t_valid, act, 0.0)

    head_lane = jnp.logical_and(li1 >= B_HEADS, li1 < 2 * B_HEADS)
    a_r = jnp.where(head_lane, -jnp.exp(alog_r_ref[...]), 0.0)
    g_col = a_r * _softplus(bac_ref[...] + dt_r_ref[...])
    si = lax.broadcasted_iota(jnp.int32, (2 * SUBLANES, 1), 0)
    head_sub = jnp.logical_and(si >= B_HEADS, si < 2 * B_HEADS)
    a_c = jnp.where(head_sub, -jnp.exp(alog_c_ref[...]), 0.0)
    g_row = a_c * _softplus(bar_ref[...] + dt_c_ref[...])
    if masked:
        g_col = jnp.where(row_ok, g_col, 0.0)
        g_row = jnp.where(col_ok, g_row, 0.0)

    rpos = ri % CHUNK
    lpos = li16 % CHUNK
    gc = g_col
    rev = g_col
    gcr = g_row
    step = 1
    while step < CHUNK:
        gc = gc + jnp.where(rpos >= step, pltpu.roll(gc, step, 0), 0.0)
        rev = rev + jnp.where(rpos < CHUNK - step, pltpu.roll(rev, rows - step, 0), 0.0)
        gcr = gcr + jnp.where(lpos >= step, pltpu.roll(gcr, step, 1), 0.0)
        step *= 2
    rev = rev - g_col
    egl_ref[...] = jnp.exp(gc + rev)

    same = (ri // CHUNK) == (li // CHUNK)
    incl = jnp.logical_and(same, ri >= li)
    strict = jnp.logical_and(same, ri > li)
    eye = (ri == li).astype(F32)

    lows, rhss = [], []
    for hh in range(B_HEADS):
        sl = slice(hh * B_DK, (hh + 1) * B_DK)
        gc_c = gc[:, B_HEADS + hh:B_HEADS + hh + 1]
        gc_r = gcr[B_HEADS + hh:B_HEADS + hh + 1, :]
        dec = jnp.where(incl, jnp.exp(jnp.where(incl, gc_c - gc_r, 0.0)), 0.0)
        q = act[:, sl]
        q = q * lax.rsqrt(jnp.sum(q * q, axis=-1, keepdims=True) + EPS) * (B_DK ** -0.5)
        k = act[:, B_QK + hh * B_DK:B_QK + (hh + 1) * B_DK]
        k = k * lax.rsqrt(jnp.sum(k * k, axis=-1, keepdims=True) + EPS)
        v = act[:, 2 * B_QK + hh * B_DV:2 * B_QK + (hh + 1) * B_DV]
        beta = jax.nn.sigmoid(bac_ref[:, hh:hh + 1])
        kb = k * beta
        kbf = k.astype(BF16)
        lows.append(jnp.where(strict, dec * _dot_nt(kb, kbf), 0.0))
        attn_ref[:, sl] = dec * _dot_nt(q, kbf)
        rhss.append(jnp.concatenate([v * beta, kb * jnp.exp(gc_c)], axis=1).astype(BF16))
        qg_ref[:, sl] = q * jnp.exp(gc_c)
        kd_ref[:, sl] = k * jnp.exp(rev[:, B_HEADS + hh:B_HEADS + hh + 1])

    tinvs = [eye - low for low in lows]
    pws = lows
    sq = 2
    while sq < CHUNK:
        pws = [_dot(pw, pw) for pw in pws]
        tinvs = [tinv + _dot(tinv, pw) for tinv, pw in zip(tinvs, pws)]
        sq *= 2
    for hh in range(B_HEADS):
        sol = _dot(tinvs[hh], rhss[hh])
        u_ref[:, hh * B_DV:(hh + 1) * B_DV] = sol[:, :B_DV]
        w_ref[:, hh * B_DK:(hh + 1) * B_DK] = sol[:, B_DV:]


def _b_prep(p, cstate, bac, bar, cw, alog_r, dt_r, alog_c, dt_c, seq_len, t_valid):
    n = p.shape[0]
    bps = seq_len // ROW_BLOCK
    row = lambda i: (i, 0)
    fixed = lambda i: (0, 0)
    per_row = ROW_BLOCK // SUBLANES
    wide = jax.ShapeDtypeStruct((n, B_V), F32)
    return pl.pallas_call(
        functools.partial(_b_prep_kernel, blocks_per_seq=bps, t_valid=t_valid),
        grid=(n // ROW_BLOCK,),
        in_specs=[
            pl.BlockSpec((ROW_BLOCK, B_CONV_CH), row),
            pl.BlockSpec((SUBLANES, B_CONV_CH), lambda i: (jnp.maximum(i * per_row - 1, 0), 0)),
            pl.BlockSpec((None, SUBLANES, B_CONV_CH), lambda i: (i // bps, 0, 0)),
            pl.BlockSpec((ROW_BLOCK, LANES), row),
            pl.BlockSpec((2 * SUBLANES, ROW_BLOCK), lambda i: (0, i)),
            pl.BlockSpec((B_CONV, B_CONV_CH), fixed),
            pl.BlockSpec((1, LANES), fixed),
            pl.BlockSpec((1, LANES), fixed),
            pl.BlockSpec((2 * SUBLANES, 1), fixed),
            pl.BlockSpec((2 * SUBLANES, 1), fixed),
        ],
        out_specs=[pl.BlockSpec((ROW_BLOCK, B_V), row)] * 5 + [pl.BlockSpec((ROW_BLOCK, LANES), row)],
        out_shape=[wide] * 5 + [jax.ShapeDtypeStruct((n, LANES), F32)],
        scratch_shapes=[pltpu.VMEM((SUBLANES + ROW_BLOCK, B_CONV_CH), F32)],
        compiler_params=_cparams(("parallel",)),
        name="b_prep",
    )(p, p, cstate, bac, bar, cw, alog_r, dt_r, alog_c, dt_c)


def _b_scan_kernel(qg_ref, kd_ref, u_ref, w_ref, attn_ref, egl_ref, z_ref, s0_ref, gout_ref, o_ref, s_ref, *, nb):
    c = pl.program_id(1)

    @pl.when(c == 0)
    def _():
        s_ref[...] = s0_ref[...]

    half = c % (ROW_BLOCK // CHUNK)
    rgrp = lax.broadcasted_iota(jnp.int32, (ROW_BLOCK, B_DV), 0) // CHUNK
    here = rgrp == half
    for b in range(nb):
        for hh in range(B_HEADS):
            sl = slice(hh * B_DV, (hh + 1) * B_DV)
            s = s_ref[b, hh]
            v_new = u_ref[b, :, sl] - _dot(w_ref[b, :, sl], s)
            v_full = jnp.where(here, jnp.concatenate([v_new] * (ROW_BLOCK // CHUNK), axis=0), 0.0)
            o = _dot(qg_ref[b, :, sl], s) + _dot(attn_ref[b, :, sl], v_full)
            decay = egl_ref[b, 0:1, B_HEADS + hh:B_HEADS + hh + 1]
            s_ref[b, hh] = s * decay + _dot_tn(kd_ref[b, :, sl], v_new)
            o_ref[b, :, sl] = (_rms(o, gout_ref[...]) * _silu(z_ref[b, :, sl])).astype(o_ref.dtype)


def _b_scan(qg, kd, u, w, attn, egl, z, s0, gout, batch, seq_len):
    nb = 4
    nchunk = seq_len // CHUNK
    v3 = lambda a: a.reshape(batch, seq_len, a.shape[-1])
    rows = lambda bi, c: (bi, c, 0)
    state = lambda bi, c: (bi, 0, 0, 0)
    wide = pl.BlockSpec((nb, CHUNK, B_V), rows)
    o, s_new = pl.pallas_call(
        functools.partial(_b_scan_kernel, nb=nb),
        grid=(batch // nb, nchunk),
        in_specs=[wide] * 5 + [pl.BlockSpec((nb, CHUNK, LANES), rows), wide,
                               pl.BlockSpec((nb, B_HEADS, B_DK, B_DV), state),
                               pl.BlockSpec((1, B_DV), lambda bi, c: (0, 0))],
        out_specs=[wide, pl.BlockSpec((nb, B_HEADS, B_DK, B_DV), state)],
        out_shape=[jax.ShapeDtypeStruct((batch, seq_len, B_V), BF16),
                   jax.ShapeDtypeStruct((batch, B_HEADS, B_DK, B_DV), F32)],
        compiler_params=_cparams(("parallel", "arbitrary")),
        name="b_scan",
    )(v3(qg), v3(kd), v3(u), v3(w), v3(attn), v3(egl), v3(z), s0, gout)
    return o.reshape(batch * seq_len, B_V), s_new


def _proj_c_kernel(x_ref, g1_ref, w_ref, cos_ref, sin_ref, q_ref, k_ref, v_ref, z_ref):
    h = _rms(x_ref[...], g1_ref[...]).astype(BF16)
    cos = cos_ref[...]
    sin = sin_ref[...]
    lane = lax.broadcasted_iota(jnp.int32, cos.shape, 1)
    first_half = (lane % C_DK) < (C_DK // 2)

    def rope(seg):
        swapped = jnp.where(first_half, pltpu.roll(seg, LANES - C_DK // 2, 1), pltpu.roll(seg, C_DK // 2, 1))
        return seg * cos + swapped * sin

    qk = jnp.dot(h, w_ref[:, 0:2 * C_QK], preferred_element_type=F32)
    for j in range(2 * C_QK // LANES):
        seg = rope(qk[:, j * LANES:(j + 1) * LANES])
        if j < C_QK // LANES:
            q_ref[:, j * LANES:(j + 1) * LANES] = seg
        else:
            jj = j - C_QK // LANES
            k_ref[:, jj * LANES:(jj + 1) * LANES] = seg * (C_DK ** -0.5)
    v_ref[...] = jnp.dot(h, w_ref[:, 2 * C_QK:2 * C_QK + C_V], preferred_element_type=F32)
    z_ref[...] = jnp.dot(h, w_ref[:, 2 * C_QK + C_V:2 * C_QK + 2 * C_V], preferred_element_type=F32)


def _proj_c(x, g1, w, cos, sin):
    n = x.shape[0]
    tm = 256
    tab_blocks = cos.shape[0] // tm
    row = lambda i: (i, 0)
    fixed = lambda i: (0, 0)
    tab = (lambda i: (i % tab_blocks, 0)) if tab_blocks > 1 else fixed
    return pl.pallas_call(
        _proj_c_kernel,
        grid=(n // tm,),
        in_specs=[
            pl.BlockSpec((tm, D_MODEL), row),
            pl.BlockSpec((1, D_MODEL), fixed),
            pl.BlockSpec((D_MODEL, 2 * C_QK + 2 * C_V), fixed),
            pl.BlockSpec((tm, LANES), tab),
            pl.BlockSpec((tm, LANES), tab),
        ],
        out_specs=[pl.BlockSpec((tm, C_QK), row), pl.BlockSpec((tm, C_QK), row),
                   pl.BlockSpec((tm, C_V), row), pl.BlockSpec((tm, C_V), row)],
        out_shape=[jax.ShapeDtypeStruct((n, C_QK), F32), jax.ShapeDtypeStruct((n, C_QK), F32),
                   jax.ShapeDtypeStruct((n, C_V), F32), jax.ShapeDtypeStruct((n, C_V), F32)],
        compiler_params=_cparams(("parallel",)),
        name="proj_c",
    )(x, g1, w, cos, sin)


def _log_gamma(hh):
    return math.log1p(-(2.0 ** (-5.0 - hh)))


def _c_scan_kernel(q_ref, k_ref, v_ref, z_ref, r0_ref, gout_ref, o_ref, r_ref, *, nb, t_valid):
    c = pl.program_id(1)
    rows = ROW_BLOCK

    @pl.when(c == 0)
    def _():
        r_ref[...] = r0_ref[...]

    left = jnp.clip(t_valid - c * rows, 0, rows)
    ri = lax.broadcasted_iota(jnp.int32, (rows, rows), 0)
    ci = lax.broadcasted_iota(jnp.int32, (rows, rows), 1)
    cnt_i = jnp.minimum(ri + 1, left).astype(F32)
    cnt_j = jnp.minimum(ci + 1, left).astype(F32)
    incl = ri >= ci
    steps = jnp.where(incl, cnt_i - cnt_j, 0.0)
    cnt_col = cnt_i[:, 0:1]
    left_f = left.astype(F32)
    qk_lane = lax.broadcasted_iota(jnp.int32, (1, C_QK), 1) // C_DK
    lg_lane = jnp.zeros((1, C_QK), F32)
    for hh in range(C_HEADS):
        lg_lane = jnp.where(qk_lane == hh, _log_gamma(hh), lg_lane)
    qk_sub = lax.broadcasted_iota(jnp.int32, (C_QK, 1), 0) // C_DK
    lg_sub = jnp.zeros((C_QK, 1), F32)
    for hh in range(C_HEADS):
        lg_sub = jnp.where(qk_sub == hh, _log_gamma(hh), lg_sub)
    q_scale = jnp.exp(cnt_col * lg_lane)
    k_scale = jnp.exp((left_f - cnt_col) * lg_lane)
    r_scale = jnp.exp(left_f * lg_sub)
    row_ok = (lax.broadcasted_iota(jnp.int32, (rows, 1), 0) + c * rows) < t_valid
    diag = (lax.broadcasted_iota(jnp.int32, (C_QK, C_V), 0) // C_DK) == (
        lax.broadcasted_iota(jnp.int32, (C_QK, C_V), 1) // C_DV)

    for b in range(nb):
        q = q_ref[b]
        k = jnp.where(row_ok, k_ref[b], 0.0)
        v = v_ref[b]
        r = r_ref[b]
        inter = _dot(q * q_scale, r)
        r_ref[b] = r * r_scale + jnp.where(diag, _dot_tn(k * k_scale, v), 0.0)
        for hh in range(C_HEADS):
            sl = slice(hh * C_DV, (hh + 1) * C_DV)
            k_h = jnp.where(qk_lane == hh, k, 0.0)
            att = jnp.exp(steps * _log_gamma(hh)) * _dot_nt(q, k_h)
            att = jnp.where(incl, att, 0.0)
            o = inter[:, sl] + _dot(att, v[:, sl])
            o_ref[b, :, sl] = (_rms(o, gout_ref[...]) * _silu(z_ref[b, :, sl])).astype(o_ref.dtype)


def _c_scan(q, k, v, z, r0, gout, batch, seq_len, t_valid):
    nb = 4
    nblk = seq_len // ROW_BLOCK
    v3 = lambda a: a.reshape(batch, seq_len, a.shape[-1])
    rows = lambda bi, c: (bi, c, 0)
    state = lambda bi, c: (bi, 0, 0)
    o, r_new = pl.pallas_call(
        functools.partial(_c_scan_kernel, nb=nb, t_valid=t_valid),
        grid=(batch // nb, nblk),
        in_specs=[pl.BlockSpec((nb, ROW_BLOCK, C_QK), rows), pl.BlockSpec((nb, ROW_BLOCK, C_QK), rows),
                  pl.BlockSpec((nb, ROW_BLOCK, C_V), rows), pl.BlockSpec((nb, ROW_BLOCK, C_V), rows),
                  pl.BlockSpec((nb, C_QK, C_V), state), pl.BlockSpec((1, C_DV), lambda bi, c: (0, 0))],
        out_specs=[pl.BlockSpec((nb, ROW_BLOCK, C_V), rows), pl.BlockSpec((nb, C_QK, C_V), state)],
        out_shape=[jax.ShapeDtypeStruct((batch, seq_len, C_V), BF16),
                   jax.ShapeDtypeStruct((batch, C_QK, C_V), F32)],
        compiler_params=_cparams(("parallel", "arbitrary")),
        name="c_scan",
    )(v3(q), v3(k), v3(v), v3(z), r0, gout)
    return o.reshape(batch * seq_len, C_V), r_new


def _merge_kernel(x_ref, g1_ref, wg_ref, o0_ref, o1_ref, o2_ref, l0_ref, l1_ref, l2_ref, ob_ref, oc_ref,
                  wa_ref, wb_ref, wc_ref, wo_ref, y_ref, *scr, residue_major):
    x = x_ref[...]
    tm = x.shape[0]
    h = _rms(x, g1_ref[...]).astype(BF16)
    lses = [r[...].reshape(tm, LANES) for r in (l0_ref, l1_ref, l2_ref)]
    outs = [r[...].reshape(tm, A_WIDTH) for r in (o0_ref, o1_ref, o2_ref)]
    heads = []
    for hh in range(A_HEADS):
        sl = slice(hh * A_HD, (hh + 1) * A_HD)
        ls = [l[:, 32 * hh:32 * hh + 1] for l in lses]
        m = jnp.maximum(jnp.maximum(ls[0], ls[1]), ls[2])
        es = [jnp.exp(l - m) for l in ls]
        tot = es[0] + es[1] + es[2]
        acc = (es[0] / tot) * outs[0][:, sl].astype(F32)
        acc = acc + (es[1] / tot) * outs[1][:, sl].astype(F32)
        acc = acc + (es[2] / tot) * outs[2][:, sl].astype(F32)
        heads.append(acc)
    o_a = jnp.concatenate(heads, axis=1)
    if residue_major:
        o_a = _swap_row_grid(scr[0], o_a)
    o_a = o_a.astype(BF16)
    merged = None
    for gi, (o_g, w_ref) in enumerate(((o_a, wa_ref), (ob_ref[...], wb_ref), (oc_ref[...], wc_ref))):
        gate = jax.nn.sigmoid(jnp.dot(h, wg_ref[:, gi * D_MODEL:(gi + 1) * D_MODEL], preferred_element_type=F32))
        term = gate * jnp.dot(o_g, w_ref[...], preferred_element_type=F32)
        merged = term if merged is None else merged + term
    y_ref[...] = x + jnp.dot(merged.astype(BF16), wo_ref[...], preferred_element_type=F32)


def _merge(x, g1, wg, o_groups, lses, o_b, o_c, wa, wb, wc, wo, seq_len, residue_major):
    n = x.shape[0]
    tm = RES * RES
    row = lambda i: (i, 0)
    fixed = lambda i: (0, 0)
    half = pl.BlockSpec((tm, A_WIDTH), row)
    wbr = pl.BlockSpec((A_WIDTH, D_MODEL), fixed)
    if residue_major:
        tiles = seq_len // tm
        grp = lambda i: (i // tiles, 0, i % tiles, 0)
        o_spec = pl.BlockSpec((None, RES, tm // RES, A_WIDTH), grp)
        lse = pl.BlockSpec((None, RES, tm // RES, LANES), grp)
        scratch = [pltpu.VMEM((A_WIDTH // LANES, tm, LANES), F32)]
    else:
        o_spec = half
        lse = pl.BlockSpec((tm, LANES), row)
        scratch = []
    return pl.pallas_call(
        functools.partial(_merge_kernel, residue_major=residue_major),
        grid=(n // tm,),
        in_specs=[pl.BlockSpec((tm, D_MODEL), row), pl.BlockSpec((1, D_MODEL), fixed),
                  pl.BlockSpec((D_MODEL, 3 * D_MODEL), fixed),
                  o_spec, o_spec, o_spec, lse, lse, lse, half, half, wbr, wbr, wbr,
                  pl.BlockSpec((D_MODEL, D_MODEL), fixed)],
        out_specs=pl.BlockSpec((tm, D_MODEL), row),
        out_shape=jax.ShapeDtypeStruct((n, D_MODEL), F32),
        scratch_shapes=scratch,
        compiler_params=_cparams(("parallel",)),
        name="merge",
    )(x, g1, wg, *o_groups, *lses, o_b, o_c, wa, wb, wc, wo)


def _ffn_kernel(x_ref, g2_ref, wg_ref, wu_ref, wo_ref, y_ref, h_scr, acc_scr):
    j = pl.program_id(1)

    @pl.when(j == 0)
    def _():
        h_scr[...] = _rms(x_ref[...], g2_ref[...]).astype(BF16)
        acc_scr[...] = jnp.zeros_like(acc_scr)

    h = h_scr[...]
    gate = jnp.dot(h, wg_ref[...], preferred_element_type=F32)
    up = jnp.dot(h, wu_ref[...], preferred_element_type=F32)
    acc_scr[...] += jnp.dot((_silu(gate) * up).astype(BF16), wo_ref[...], preferred_element_type=F32)

    @pl.when(j == pl.num_programs(1) - 1)
    def _():
        y_ref[...] = x_ref[...] + acc_scr[...]


def _ffn(x, g2, w_in, w_out):
    n = x.shape[0]
    tm = min(n, 1024)
    tf = 256
    nf = D_FF // tf
    row = lambda i, j: (i, 0)
    return pl.pallas_call(
        _ffn_kernel,
        grid=(n // tm, nf),
        in_specs=[pl.BlockSpec((tm, D_MODEL), row), pl.BlockSpec((1, D_MODEL), lambda i, j: (0, 0)),
                  pl.BlockSpec((D_MODEL, tf), lambda i, j: (0, j)),
                  pl.BlockSpec((D_MODEL, tf), lambda i, j: (0, nf + j)),
                  pl.BlockSpec((tf, D_MODEL), lambda i, j: (j, 0))],
        out_specs=pl.BlockSpec((tm, D_MODEL), row),
        out_shape=jax.ShapeDtypeStruct((n, D_MODEL), F32),
        scratch_shapes=[pltpu.VMEM((tm, D_MODEL), BF16), pltpu.VMEM((tm, D_MODEL), F32)],
        compiler_params=_cparams(("parallel", "arbitrary")),
        name="ffn",
    )(x, g2, w_in, w_in, w_out)


def _rope_tables(pos, hd, reps):
    inv = ROPE_THETA ** (-jnp.arange(0, hd, 2, dtype=F32) / hd)
    ang = pos.astype(F32)[:, None] * inv[None, :]
    cos = jnp.cos(ang)
    sin = jnp.sin(ang)
    cos2 = jnp.concatenate([cos, cos], axis=1)
    sin2 = jnp.concatenate([-sin, sin], axis=1)
    return jnp.tile(cos2, (1, reps)), jnp.tile(sin2, (1, reps))


def _pad_rows(a, batch, t, t_pad):
    if t == t_pad:
        return a
    a = a.reshape(batch, t, a.shape[-1])
    a = jnp.pad(a, ((0, 0), (0, t_pad - t), (0, 0)))
    return a.reshape(batch * t_pad, a.shape[-1])


def _unpad_rows(a, batch, t, t_pad):
    if t == t_pad:
        return a
    return a.reshape(batch, t_pad, a.shape[-1])[:, :t].reshape(batch * t, a.shape[-1])


def _layer(x, pos, batch, t, lw, caches, layer, conv_state, s0, r0):
    n = batch * t
    prompt = caches is None
    reps = max(1, 256 // t)
    cos_a, sin_a = _rope_tables(pos, A_HD, 1)
    cos_c, sin_c = _rope_tables(pos, C_DK, LANES // C_DK)
    if reps > 1:
        cos_a, sin_a, cos_c, sin_c = (jnp.tile(a, (reps, 1)) for a in (cos_a, sin_a, cos_c, sin_c))

    if prompt:
        to_rm = lambda a: a.reshape(t // RES, RES, A_HD).transpose(1, 0, 2)
        q, k, v = _proj_a(x, lw["g1"], lw["w_a"], lw["qn"], lw["kn"], to_rm(cos_a), to_rm(sin_a), batch, t, True)
    else:
        q, k, v = _proj_a(x, lw["g1"], lw["w_a"], lw["qn"], lw["kn"], cos_a, sin_a, batch, t, False)
    outs, lses = [], []
    for gi in range(N_GROUPS):
        if prompt:
            o, lse = _attn_prompt(q, k, v, gi, batch, t)
        else:
            o, lse = _attn_sample(q, k, v, caches[gi], layer, gi, batch, t)
        outs.append(o)
        lses.append(lse)
    new_kv = []
    for gi, (win, _) in enumerate(A_GROUPS):
        cols = slice(gi * A_WIDTH, (gi + 1) * A_WIDTH)
        if prompt:
            keep = min(win, t)
            tail = lambda a: a[:, :, (t - keep) // RES:, cols].transpose(0, 2, 1, 3).reshape(batch, keep, A_HEADS, A_HD)
        else:
            tail = lambda a: a[:, cols].reshape(batch, t, A_HEADS, A_HD)
        new_kv.append(jnp.stack([tail(k), tail(v)], axis=2))

    t_pad = -(-t // ROW_BLOCK) * ROW_BLOCK
    p, z_b, bac, bar = _proj_b(x, lw["g1"], lw["w_bqkv"], lw["w_bz"], lw["w_ba"], lw["w_bat"])
    conv_new = jnp.concatenate([conv_state, p.reshape(batch, t, B_CONV_CH)], axis=1)[:, -(B_CONV - 1):]
    cst = jnp.pad(conv_state, ((0, 0), (SUBLANES - (B_CONV - 1), 0), (0, 0)))
    bar_p = _pad_rows(bar.T, batch, t, t_pad).T if t_pad != t else bar
    qg, kd, u, w, attn, egl = _b_prep(
        _pad_rows(p, batch, t, t_pad), cst, _pad_rows(bac, batch, t, t_pad), bar_p, lw["conv_w"],
        lw["alog_r"], lw["dt_r"], lw["alog_c"], lw["dt_c"], t_pad, t)
    o_b, s_new = _b_scan(qg, kd, u, w, attn, egl, _pad_rows(z_b, batch, t, t_pad), s0, lw["gb"], batch, t_pad)
    o_b = _unpad_rows(o_b, batch, t, t_pad)

    cq, ck, cv, cz = _proj_c(x, lw["g1"], lw["w_c"], cos_c, sin_c)
    r_bd = jnp.zeros((batch, C_HEADS, C_DK, C_HEADS, C_DV), F32)
    for hh in range(C_HEADS):
        r_bd = r_bd.at[:, hh, :, hh, :].set(r0[:, hh])
    o_c, r_new = _c_scan(*(_pad_rows(a, batch, t, t_pad) for a in (cq, ck, cv, cz)),
                         r_bd.reshape(batch, C_QK, C_V), lw["gc"], batch, t_pad, t)
    o_c = _unpad_rows(o_c, batch, t, t_pad)
    r_new = r_new.reshape(batch, C_HEADS, C_DK, C_HEADS, C_DV)
    r_new = jnp.stack([r_new[:, hh, :, hh, :] for hh in range(C_HEADS)], axis=1)

    x = _merge(x, lw["g1"], lw["w_g"], outs, lses, o_b, o_c, lw["w_oa"], lw["w_ob"], lw["w_oc"], lw["w_o"], t, prompt)
    x = _ffn(x, lw["g2"], lw["w_fi"], lw["w_fo"])
    return x, new_kv, conv_new, s_new, r_new


def _layer_weights(l, norm1_g, w_in, a_q_norm_g, a_k_norm_g, b_conv_w, b_a_log, b_dt_bias, b_out_norm_g,
                   c_out_norm_g, w_out_a, w_out_b, w_out_c, w_out, norm2_g, w_ffn_in, w_ffn_out):
    o = IN_OFFS
    wl = w_in[l]
    w_ba = wl[:, o[3]:o[5]]
    pad_r = lambda a: jnp.pad(a.reshape(1, B_HEADS), ((0, 0), (B_HEADS, LANES - 2 * B_HEADS)))
    pad_c = lambda a: jnp.pad(a.reshape(B_HEADS, 1), ((B_HEADS, 2 * SUBLANES - 2 * B_HEADS), (0, 0)))
    return dict(
        g1=norm1_g[l].reshape(1, D_MODEL), g2=norm2_g[l].reshape(1, D_MODEL),
        w_a=wl[:, o[0]:o[1]].astype(BF16),
        w_bqkv=wl[:, o[1]:o[2]].astype(BF16), w_bz=wl[:, o[2]:o[3]].astype(BF16),
        w_ba=jnp.pad(w_ba, ((0, 0), (0, LANES - 2 * B_HEADS))).astype(BF16),
        w_bat=jnp.pad(w_ba.T, ((0, 2 * SUBLANES - 2 * B_HEADS), (0, 0))).astype(BF16),
        w_c=wl[:, o[5]:o[9]].astype(BF16), w_g=wl[:, o[9]:o[10]].astype(BF16),
        qn=a_q_norm_g[l].reshape(1, A_HD), kn=a_k_norm_g[l].reshape(1, A_HD),
        conv_w=b_conv_w[l],
        alog_r=pad_r(b_a_log[l]), dt_r=pad_r(b_dt_bias[l]), alog_c=pad_c(b_a_log[l]), dt_c=pad_c(b_dt_bias[l]),
        gb=b_out_norm_g[l].reshape(1, B_DV), gc=c_out_norm_g[l].reshape(1, C_DV),
        w_oa=w_out_a[l].astype(BF16), w_ob=w_out_b[l].astype(BF16), w_oc=w_out_c[l].astype(BF16),
        w_o=w_out[l].astype(BF16), w_fi=w_ffn_in[l].astype(BF16), w_fo=w_ffn_out[l].astype(BF16),
    )


def kernel(x_prompt, x_sample, cache_a_kv0, cache_a_kv1, cache_a_kv2, state_b_conv, state_b_S, state_c_R, norm1_g, w_in, a_q_norm_g, a_k_norm_g, b_conv_w, b_a_log, b_dt_bias, b_out_norm_g, c_out_norm_g, w_out_a, w_out_b, w_out_c, w_out, norm2_g, w_ffn_in, w_ffn_out):
    bp, t = x_prompt.shape[:2]
    bs, s = x_sample.shape[:2]
    depth = w_in.shape[0]
    pos_p = jnp.arange(t)
    pos_s = PAST_LEN + jnp.arange(s)
    yp = x_prompt.reshape(bp * t, D_MODEL)
    ys = x_sample.reshape(bs * s, D_MODEL)
    caches = (cache_a_kv0, cache_a_kv1, cache_a_kv2)
    zeros_conv = jnp.zeros((bp, B_CONV - 1, B_CONV_CH), F32)
    zeros_s = jnp.zeros((bp, B_HEADS, B_DK, B_DV), F32)
    zeros_r = jnp.zeros((bp, C_HEADS, C_DK, C_DV), F32)
    acc = [[] for _ in range(12)]
    for l in range(depth):
        lw = _layer_weights(l, norm1_g, w_in, a_q_norm_g, a_k_norm_g, b_conv_w, b_a_log, b_dt_bias,
                            b_out_norm_g, c_out_norm_g, w_out_a, w_out_b, w_out_c, w_out, norm2_g,
                            w_ffn_in, w_ffn_out)
        yp, kv, cv, sn, rn = _layer(yp, pos_p, bp, t, lw, None, l, zeros_conv, zeros_s, zeros_r)
        for i, a in enumerate((kv[0], kv[1], kv[2], cv, sn, rn)):
            acc[i].append(a)
        ys, kv, cv, sn, rn = _layer(ys, pos_s, bs, s, lw, caches, l, state_b_conv[l], state_b_S[l], state_c_R[l])
        for i, a in enumerate((kv[0], kv[1], kv[2], cv, sn, rn)):
            acc[6 + i].append(a)
    return (yp.reshape(bp, t, D_MODEL), ys.reshape(bs, s, D_MODEL)) + tuple(jnp.stack(a) for a in acc)
```

```python
import functools
import math

import jax
import jax.numpy as jnp
import numpy as np
from jax import lax
from jax.experimental import pallas as pl
from jax.experimental.pallas import tpu as pltpu

F32 = jnp.float32
BF16 = jnp.bfloat16

D_MODEL = 1024
PAST_LEN = 8192
A_GROUPS = ((128, 1), (512, 4), (2048, 16))
N_GROUPS = 3
A_HEADS = 4
A_HD = 128
A_WIDTH = A_HEADS * A_HD
A_KEYS = 128
B_HEADS = 4
B_DK = 128
B_DV = 128
B_CONV = 4
B_QK = B_HEADS * B_DK
B_V = B_HEADS * B_DV
B_CONV_CH = 2 * B_QK + B_V
C_HEADS = 4
C_DK = 64
C_DV = 128
C_QK = C_HEADS * C_DK
C_V = C_HEADS * C_DV
CHUNK = 64
ROPE_THETA = 10000.0
EPS = 1e-6
D_FF = 2816
IN_SIZES = (3 * N_GROUPS * A_WIDTH, B_CONV_CH, B_V, B_HEADS, B_HEADS, C_QK, C_QK, C_V, C_V, 3 * D_MODEL)
IN_OFFS = tuple(int(v) for v in np.cumsum((0,) + IN_SIZES))

ROW_BLOCK = 128
RES = 16
SUBLANES = 8
LANES = 128
VMEM_LIMIT = 48 * 1024 * 1024


def _cparams(sem):
    return pltpu.CompilerParams(dimension_semantics=sem, vmem_limit_bytes=VMEM_LIMIT)


def _rms(x, g):
    return x * lax.rsqrt(jnp.mean(x * x, axis=-1, keepdims=True) + EPS) * g


def _silu(x):
    return x * jax.nn.sigmoid(x)


def _softplus(x):
    return jnp.maximum(x, 0.0) + jnp.log(1.0 + jnp.exp(-jnp.abs(x)))


def _dot(a, b):
    return jnp.dot(a.astype(BF16), b.astype(BF16), preferred_element_type=F32)


def _dot_nt(a, b):
    return lax.dot_general(a.astype(BF16), b.astype(BF16), (((1,), (1,)), ((), ())), preferred_element_type=F32)


def _dot_tn(a, b):
    return lax.dot_general(a.astype(BF16), b.astype(BF16), (((0,), (0,)), ((), ())), preferred_element_type=F32)


def _swap_row_grid(scr, val):
    slabs = val.shape[1] // LANES
    for c in range(slabs):
        scr[c] = val[:, c * LANES:(c + 1) * LANES]
    cols = [jnp.concatenate([scr[c, pl.ds(r, RES, stride=RES), :] for r in range(RES)], axis=0)
            for c in range(slabs)]
    return jnp.concatenate(cols, axis=1)


def _proj_a_kernel(x_ref, g1_ref, w_ref, qg_ref, kg_ref, cos_ref, sin_ref, q_ref, k_ref, v_ref, *scr, residue_major):
    x = x_ref[...]
    tm = x.shape[0]
    if residue_major:
        x = _swap_row_grid(scr[0], x)
    h = _rms(x, g1_ref[...]).astype(BF16)
    cos = cos_ref[...].reshape(tm, A_HD)
    sin = sin_ref[...].reshape(tm, A_HD)

    def norm_rope(seg, g):
        y = _rms(seg, g)
        return y * cos + pltpu.roll(y, A_HD // 2, 1) * sin

    def put(ref, col, val):
        if residue_major:
            ref[:, :, col:col + val.shape[1]] = val.reshape(RES, tm // RES, val.shape[1])
        else:
            ref[:, col:col + val.shape[1]] = val

    for j in range(3 * N_GROUPS):
        acc = jnp.dot(h, w_ref[:, j * A_WIDTH:(j + 1) * A_WIDTH], preferred_element_type=F32)
        if j < N_GROUPS:
            for hh in range(A_HEADS):
                sl = slice(hh * A_HD, (hh + 1) * A_HD)
                put(q_ref, j * A_WIDTH + hh * A_HD, norm_rope(acc[:, sl], qg_ref[...]) * (A_HD ** -0.5))
        elif j < 2 * N_GROUPS:
            jj = j - N_GROUPS
            for hh in range(A_HEADS):
                sl = slice(hh * A_HD, (hh + 1) * A_HD)
                put(k_ref, jj * A_WIDTH + hh * A_HD, norm_rope(acc[:, sl], kg_ref[...]))
        else:
            jj = j - 2 * N_GROUPS
            put(v_ref, jj * A_WIDTH, acc)


def _proj_a(x, g1, w, qg, kg, cos, sin, batch, seq_len, residue_major):
    n = x.shape[0]
    tm = RES * RES
    nw = N_GROUPS * A_WIDTH
    fixed = lambda i: (0, 0)
    common = [pl.BlockSpec((1, D_MODEL), fixed), pl.BlockSpec((D_MODEL, 3 * nw), fixed),
              pl.BlockSpec((1, A_HD), fixed), pl.BlockSpec((1, A_HD), fixed)]
    if residue_major:
        tiles = seq_len // tm
        tab = pl.BlockSpec((RES, tm // RES, A_HD), lambda i: (0, i % tiles, 0))
        out_spec = pl.BlockSpec((None, RES, tm // RES, nw), lambda i: (i // tiles, 0, i % tiles, 0))
        out_shape = jax.ShapeDtypeStruct((batch, RES, seq_len // RES, nw), F32)
        scratch = [pltpu.VMEM((D_MODEL // LANES, tm, LANES), F32)]
    else:
        assert cos.shape[0] == tm
        tab = pl.BlockSpec((tm, A_HD), fixed)
        out_spec = pl.BlockSpec((tm, nw), lambda i: (i, 0))
        out_shape = jax.ShapeDtypeStruct((n, nw), F32)
        scratch = []
    return pl.pallas_call(
        functools.partial(_proj_a_kernel, residue_major=residue_major),
        grid=(n // tm,),
        in_specs=[pl.BlockSpec((tm, D_MODEL), lambda i: (i, 0))] + common + [tab, tab],
        out_specs=[out_spec] * 3,
        out_shape=[out_shape] * 3,
        scratch_shapes=scratch,
        compiler_params=_cparams(("parallel",)),
        name="proj_a",
    )(x, g1, w, qg, kg, cos, sin)


ATTN_SUBS = 2


def _attn_prompt_kernel(q_ref, kc_ref, kp_ref, vc_ref, vp_ref, o_ref, lse_ref, *, parts):
    n = pl.program_id(2)
    per = ROW_BLOCK // parts
    qi = lax.broadcasted_iota(jnp.int32, (ROW_BLOCK, ROW_BLOCK), 0)
    kj = lax.broadcasted_iota(jnp.int32, (ROW_BLOCK, ROW_BLOCK), 1)
    qi = parts * (qi % per) + qi // per
    kj = parts * (kj % per) + kj // per
    cur_ok = kj <= qi
    prev_ok = kj >= qi
    first_ok = jnp.logical_and(prev_ok, n > 0)
    lane = lax.broadcasted_iota(jnp.int32, (ROW_BLOCK, LANES), 1)
    neg = -jnp.inf
    sub = lambda ref, half, sl: ref[:, half * per:(half + 1) * per, sl].reshape(ROW_BLOCK, A_HD)
    units = [(half, hh) for half in range(ATTN_SUBS) for hh in range(A_HEADS)]
    head = lambda hh: slice(hh * A_HD, (hh + 1) * A_HD)

    keys = {(half, hh): sub(kc_ref, half, head(hh)).astype(BF16) for half, hh in units}
    vals = {(half, hh): sub(vc_ref, half, head(hh)).astype(BF16) for half, hh in units}
    for hh in range(A_HEADS):
        keys[(-1, hh)] = sub(kp_ref, 0, head(hh)).astype(BF16)
        vals[(-1, hh)] = sub(vp_ref, 0, head(hh)).astype(BF16)

    scores = []
    for half, hh in units:
        q = sub(q_ref, half, head(hh)).astype(BF16)
        s_cur = jnp.where(cur_ok, _dot_nt(q, keys[(half, hh)]), neg)
        s_prev = jnp.where(first_ok if half == 0 else prev_ok, _dot_nt(q, keys[(half - 1, hh)]), neg)
        scores.append((s_cur, s_prev))
    probs = []
    for s_cur, s_prev in scores:
        m = jnp.maximum(jnp.max(s_cur, axis=-1, keepdims=True), jnp.max(s_prev, axis=-1, keepdims=True))
        p_cur = jnp.exp(s_cur - m)
        p_prev = jnp.exp(s_prev - m)
        den = jnp.sum(p_cur, axis=-1, keepdims=True) + jnp.sum(p_prev, axis=-1, keepdims=True)
        probs.append((p_cur, p_prev, m, den))
    lse_blk = [jnp.zeros((ROW_BLOCK, LANES), F32) for _ in range(ATTN_SUBS)]
    for (half, hh), (p_cur, p_prev, m, den) in zip(units, probs):
        o = (_dot(p_cur, vals[(half, hh)]) + _dot(p_prev, vals[(half - 1, hh)])) / den
        o_ref[:, half * per:(half + 1) * per, head(hh)] = o.reshape(parts, per, A_HD)
        lse_blk[half] = jnp.where(lane // 32 == hh, m + jnp.log(den), lse_blk[half])
    for half in range(ATTN_SUBS):
        lse_ref[:, half * per:(half + 1) * per, :] = lse_blk[half].reshape(parts, per, LANES)


def _attn_prompt(q, k, v, gi, batch, seq_len):
    _, dil = A_GROUPS[gi]
    parts = RES // dil
    per = ROW_BLOCK // parts
    rows = seq_len // RES
    nblk = seq_len // dil // (ATTN_SUBS * ROW_BLOCK)
    split = lambda a: a.reshape(batch, parts, dil, rows, a.shape[-1])
    cur = lambda b, r, n: (b, 0, r, n, gi)
    prev = lambda b, r, n: (b, 0, r, jnp.maximum(ATTN_SUBS * n - 1, 0), gi)
    out = lambda b, r, n: (b, 0, r, n, 0)
    blk = (None, parts, None, ATTN_SUBS * per, A_WIDTH)
    blk_prev = (None, parts, None, per, A_WIDTH)
    o, lse = pl.pallas_call(
        functools.partial(_attn_prompt_kernel, parts=parts),
        grid=(batch, dil, nblk),
        in_specs=[pl.BlockSpec(blk, cur), pl.BlockSpec(blk, cur), pl.BlockSpec(blk_prev, prev),
                  pl.BlockSpec(blk, cur), pl.BlockSpec(blk_prev, prev)],
        out_specs=[pl.BlockSpec(blk, out), pl.BlockSpec((None, parts, None, ATTN_SUBS * per, LANES), out)],
        out_shape=[jax.ShapeDtypeStruct((batch, parts, dil, rows, A_WIDTH), F32),
                   jax.ShapeDtypeStruct((batch, parts, dil, rows, LANES), F32)],
        compiler_params=_cparams(("parallel", "parallel", "arbitrary")),
        name=f"attn_prompt_g{gi}",
    )(split(q), split(k), split(k), split(v), split(v))
    return o.reshape(batch, RES, rows, A_WIDTH), lse.reshape(batch, RES, rows, LANES)


def _attn_sample_kernel(q_ref, kn_ref, vn_ref, cache_ref, o_ref, lse_ref, *, dil, n_new):
    row = lax.broadcasted_iota(jnp.int32, (A_KEYS, A_HEADS, 1), 0)
    trow = lax.broadcasted_iota(jnp.int32, (n_new, A_HEADS, 1), 0)
    neg = -jnp.inf
    kn = kn_ref[...]
    vn = vn_ref[...]
    for s in range(n_new):
        res = s % dil
        first = s // dil
        q = q_ref[s][None]
        kc = cache_ref[:, res, 0]
        vc = cache_ref[:, res, 1]
        sc = jnp.sum(kc * q, axis=-1, keepdims=True)
        if first > 0:
            sc = jnp.where(row >= first, sc, neg)
        new_ok = jnp.logical_and(trow <= s, (s - trow) % dil == 0)
        sn = jnp.where(new_ok, jnp.sum(kn * q, axis=-1, keepdims=True), neg)
        m = jnp.maximum(jnp.max(sc, axis=0, keepdims=True), jnp.max(sn, axis=0, keepdims=True))
        pc = jnp.exp(sc - m)
        pn = jnp.exp(sn - m)
        den = jnp.sum(pc, axis=0, keepdims=True) + jnp.sum(pn, axis=0, keepdims=True)
        o = (jnp.sum(pc * vc, axis=0, keepdims=True) + jnp.sum(pn * vn, axis=0, keepdims=True)) / den
        o_ref[s] = o[0]
        lse_ref[s] = jnp.broadcast_to((m + jnp.log(den))[0], (A_HEADS, A_HD))


def _attn_sample(q, k, v, cache, layer, gi, batch, n_new):
    win, dil = A_GROUPS[gi]
    depth = cache.shape[0]
    assert cache.shape[2] == win and win // dil == A_KEYS
    n_res = min(dil, n_new)
    cv = cache.reshape(depth, batch, A_KEYS, dil, 2, A_HEADS, A_HD)
    heads = lambda a: a.reshape(batch, n_new, N_GROUPS, A_HEADS, A_HD)
    grp = lambda b: (b, 0, gi, 0, 0)
    new = pl.BlockSpec((None, n_new, None, A_HEADS, A_HD), grp)
    out = pl.BlockSpec((None, n_new, A_HEADS, A_HD), lambda b: (b, 0, 0, 0))
    o, lse = pl.pallas_call(
        functools.partial(_attn_sample_kernel, dil=dil, n_new=n_new),
        grid=(batch,),
        in_specs=[new, new, new,
                  pl.BlockSpec((None, None, A_KEYS, n_res, 2, A_HEADS, A_HD), lambda b: (layer, b, 0, 0, 0, 0, 0))],
        out_specs=[out, out],
        out_shape=[jax.ShapeDtypeStruct((batch, n_new, A_HEADS, A_HD), F32)] * 2,
        compiler_params=_cparams(("parallel",)),
        name=f"attn_sample_g{gi}",
    )(heads(q), heads(k), heads(v), cv)
    lse = jnp.repeat(lse[..., 0], LANES // A_HEADS, axis=-1)
    return o.reshape(batch * n_new, A_WIDTH), lse.reshape(batch * n_new, LANES)


def _proj_b_kernel(x_ref, g1_ref, wqkv_ref, wz_ref, wba_ref, wbat_ref, p_ref, z_ref, bac_ref, bar_ref):
    h = _rms(x_ref[...], g1_ref[...]).astype(BF16)
    for j in range(3):
        sl = slice(j * B_QK, (j + 1) * B_QK)
        p_ref[:, sl] = jnp.dot(h, wqkv_ref[:, sl], preferred_element_type=F32)
    z_ref[...] = jnp.dot(h, wz_ref[...], preferred_element_type=F32)
    bac_ref[...] = jnp.dot(h, wba_ref[...], preferred_element_type=F32)
    bar_ref[...] = lax.dot_general(wbat_ref[...], h, (((1,), (1,)), ((), ())), preferred_element_type=F32)


def _proj_b(x, g1, wqkv, wz, wba, wbat):
    n = x.shape[0]
    tm = 256
    row = lambda i: (i, 0)
    fixed = lambda i: (0, 0)
    return pl.pallas_call(
        _proj_b_kernel,
        grid=(n // tm,),
        in_specs=[
            pl.BlockSpec((tm, D_MODEL), row),
            pl.BlockSpec((1, D_MODEL), fixed),
            pl.BlockSpec((D_MODEL, B_CONV_CH), fixed),
            pl.BlockSpec((D_MODEL, B_V), fixed),
            pl.BlockSpec((D_MODEL, LANES), fixed),
            pl.BlockSpec((2 * SUBLANES, D_MODEL), fixed),
        ],
        out_specs=[pl.BlockSpec((tm, B_CONV_CH), row), pl.BlockSpec((tm, B_V), row),
                   pl.BlockSpec((tm, LANES), row), pl.BlockSpec((2 * SUBLANES, tm), lambda i: (0, i))],
        out_shape=[jax.ShapeDtypeStruct((n, B_CONV_CH), F32), jax.ShapeDtypeStruct((n, B_V), F32),
                   jax.ShapeDtypeStruct((n, LANES), F32), jax.ShapeDtypeStruct((2 * SUBLANES, n), F32)],
        compiler_params=_cparams(("parallel",)),
        name="proj_b",
    )(x, g1, wqkv, wz, wba, wbat)


def _b_prep_kernel(p_ref, halo_ref, cst_ref, bac_ref, bar_ref, cw_ref, alog_r_ref, dt_r_ref, alog_c_ref, dt_c_ref,
                   qg_ref, kd_ref, u_ref, w_ref, attn_ref, egl_ref, e_scr, *, blocks_per_seq, t_valid):
    i = pl.program_id(0)
    blk = i % blocks_per_seq
    rows = ROW_BLOCK

    before = jnp.where(blk == 0, cst_ref[...], halo_ref[...])
    e_scr[0:SUBLANES, :] = before
    e_scr[SUBLANES:SUBLANES + rows, :] = p_ref[...]
    xc = e_scr[SUBLANES:SUBLANES + rows, :] * cw_ref[B_CONV - 1:B_CONV, :]
    for kk in range(1, B_CONV):
        xc = xc + e_scr[SUBLANES - kk:SUBLANES - kk + rows, :] * cw_ref[B_CONV - 1 - kk:B_CONV - kk, :]
    act = _silu(xc)

    ri = lax.broadcasted_iota(jnp.int32, (rows, LANES), 0)
    li = lax.broadcasted_iota(jnp.int32, (rows, LANES), 1)
    li16 = lax.broadcasted_iota(jnp.int32, (2 * SUBLANES, LANES), 1)
    li1 = lax.broadcasted_iota(jnp.int32, (1, LANES), 1)
    masked = t_valid < blocks_per_seq * rows
    if masked:
        row_ok = (blk * rows + ri) < t_valid
        col_ok = (blk * rows + li16) < t_valid
        act = jnp.where(ri[:, 0:1] + blk * rows < t_valid, act, 0.0)

    head_lane = jnp.logical_and(li1 >= B_HEADS, li1 < 2 * B_HEADS)
    a_r = jnp.where(head_lane, -jnp.exp(alog_r_ref[...]), 0.0)
    g_col = a_r * _softplus(bac_ref[...] + dt_r_ref[...])
    si = lax.broadcasted_iota(jnp.int32, (2 * SUBLANES, 1), 0)
    head_sub = jnp.logical_and(si >= B_HEADS, si < 2 * B_HEADS)
    a_c = jnp.where(head_sub, -jnp.exp(alog_c_ref[...]), 0.0)
    g_row = a_c * _softplus(bar_ref[...] + dt_c_ref[...])
    if masked:
        g_col = jnp.where(row_ok, g_col, 0.0)
        g_row = jnp.where(col_ok, g_row, 0.0)

    rpos = ri % CHUNK
    lpos = li16 % CHUNK
    gc = g_col
    rev = g_col
    gcr = g_row
    step = 1
    while step < CHUNK:
        gc = gc + jnp.where(rpos >= step, pltpu.roll(gc, step, 0), 0.0)
        rev = rev + jnp.where(rpos < CHUNK - step, pltpu.roll(rev, rows - step, 0), 0.0)
        gcr = gcr + jnp.where(lpos >= step, pltpu.roll(gcr, step, 1), 0.0)
        step *= 2
    rev = rev - g_col
    egl_ref[...] = jnp.exp(gc + rev)

    same = (ri // CHUNK) == (li // CHUNK)
    incl = jnp.logical_and(same, ri >= li)
    strict = jnp.logical_and(same, ri > li)
    eye = (ri == li).astype(F32)

    lows, rhss = [], []
    for hh in range(B_HEADS):
        sl = slice(hh * B_DK, (hh + 1) * B_DK)
        gc_c = gc[:, B_HEADS + hh:B_HEADS + hh + 1]
        gc_r = gcr[B_HEADS + hh:B_HEADS + hh + 1, :]
        dec = jnp.where(incl, jnp.exp(jnp.where(incl, gc_c - gc_r, 0.0)), 0.0)
        q = act[:, sl]
        q = q * lax.rsqrt(jnp.sum(q * q, axis=-1, keepdims=True) + EPS) * (B_DK ** -0.5)
        k = act[:, B_QK + hh * B_DK:B_QK + (hh + 1) * B_DK]
        k = k * lax.rsqrt(jnp.sum(k * k, axis=-1, keepdims=True) + EPS)
        v = act[:, 2 * B_QK + hh * B_DV:2 * B_QK + (hh + 1) * B_DV]
        beta = jax.nn.sigmoid(bac_ref[:, hh:hh + 1])
        kb = k * beta
        kbf = k.astype(BF16)
        lows.append(jnp.where(strict, dec * _dot_nt(kb, kbf), 0.0))
        attn_ref[:, sl] = dec * _dot_nt(q, kbf)
        rhss.append(jnp.concatenate([v * beta, kb * jnp.exp(gc_c)], axis=1).astype(BF16))
        qg_ref[:, sl] = q * jnp.exp(gc_c)
        kd_ref[:, sl] = k * jnp.exp(rev[:, B_HEADS + hh:B_HEADS + hh + 1])

    tinvs = [eye - low for low in lows]
    pws = lows
    sq = 2
    while sq < CHUNK:
        pws = [_dot(pw, pw) for pw in pws]
        tinvs = [tinv + _dot(tinv, pw) for tinv, pw in zip(tinvs, pws)]
        sq *= 2
    for hh in range(B_HEADS):
        sol = _dot(tinvs[hh], rhss[hh])
        u_ref[:, hh * B_DV:(hh + 1) * B_DV] = sol[:, :B_DV]
        w_ref[:, hh * B_DK:(hh + 1) * B_DK] = sol[:, B_DV:]


def _b_prep(p, cstate, bac, bar, cw, alog_r, dt_r, alog_c, dt_c, seq_len, t_valid):
    n = p.shape[0]
    bps = seq_len // ROW_BLOCK
    row = lambda i: (i, 0)
    fixed = lambda i: (0, 0)
    per_row = ROW_BLOCK // SUBLANES
    wide = jax.ShapeDtypeStruct((n, B_V), F32)
    return pl.pallas_call(
        functools.partial(_b_prep_kernel, blocks_per_seq=bps, t_valid=t_valid),
        grid=(n // ROW_BLOCK,),
        in_specs=[
            pl.BlockSpec((ROW_BLOCK, B_CONV_CH), row),
            pl.BlockSpec((SUBLANES, B_CONV_CH), lambda i: (jnp.maximum(i * per_row - 1, 0), 0)),
            pl.BlockSpec((None, SUBLANES, B_CONV_CH), lambda i: (i // bps, 0, 0)),
            pl.BlockSpec((ROW_BLOCK, LANES), row),
            pl.BlockSpec((2 * SUBLANES, ROW_BLOCK), lambda i: (0, i)),
            pl.BlockSpec((B_CONV, B_CONV_CH), fixed),
            pl.BlockSpec((1, LANES), fixed),
            pl.BlockSpec((1, LANES), fixed),
            pl.BlockSpec((2 * SUBLANES, 1), fixed),
            pl.BlockSpec((2 * SUBLANES, 1), fixed),
        ],
        out_specs=[pl.BlockSpec((ROW_BLOCK, B_V), row)] * 5 + [pl.BlockSpec((ROW_BLOCK, LANES), row)],
        out_shape=[wide] * 5 + [jax.ShapeDtypeStruct((n, LANES), F32)],
        scratch_shapes=[pltpu.VMEM((SUBLANES + ROW_BLOCK, B_CONV_CH), F32)],
        compiler_params=_cparams(("parallel",)),
        name="b_prep",
    )(p, p, cstate, bac, bar, cw, alog_r, dt_r, alog_c, dt_c)


def _b_scan_kernel(qg_ref, kd_ref, u_ref, w_ref, attn_ref, egl_ref, z_ref, s0_ref, gout_ref, o_ref, s_ref, *, nb):
    c = pl.program_id(1)

    @pl.when(c == 0)
    def _():
        s_ref[...] = s0_ref[...]

    half = c % (ROW_BLOCK // CHUNK)
    rgrp = lax.broadcasted_iota(jnp.int32, (ROW_BLOCK, B_DV), 0) // CHUNK
    here = rgrp == half
    units = [(b, hh) for b in range(nb) for hh in range(B_HEADS)]
    head = lambda hh: slice(hh * B_DV, (hh + 1) * B_DV)
    states = [s_ref[b, hh] for b, hh in units]
    proj = [_dot(jnp.concatenate([w_ref[b, :, head(hh)], qg_ref[b, :, head(hh)]], axis=0), s)
            for (b, hh), s in zip(units, states)]
    v_new = [u_ref[b, :, head(hh)] - pr[:CHUNK] for (b, hh), pr in zip(units, proj)]
    outs = []
    for (b, hh), pr, vn in zip(units, proj, v_new):
        v_full = jnp.where(here, jnp.concatenate([vn] * (ROW_BLOCK // CHUNK), axis=0), 0.0)
        outs.append(pr[CHUNK:] + _dot(attn_ref[b, :, head(hh)], v_full))
    for (b, hh), s, vn in zip(units, states, v_new):
        decay = egl_ref[b, 0:1, B_HEADS + hh:B_HEADS + hh + 1]
        s_ref[b, hh] = s * decay + _dot_tn(kd_ref[b, :, head(hh)], vn)
    for (b, hh), o in zip(units, outs):
        o_ref[b, :, head(hh)] = (_rms(o, gout_ref[...]) * _silu(z_ref[b, :, head(hh)])).astype(o_ref.dtype)


def _b_scan(qg, kd, u, w, attn, egl, z, s0, gout, batch, seq_len):
    nb = 4
    nchunk = seq_len // CHUNK
    v3 = lambda a: a.reshape(batch, seq_len, a.shape[-1])
    rows = lambda bi, c: (bi, c, 0)
    state = lambda bi, c: (bi, 0, 0, 0)
    wide = pl.BlockSpec((nb, CHUNK, B_V), rows)
    o, s_new = pl.pallas_call(
        functools.partial(_b_scan_kernel, nb=nb),
        grid=(batch // nb, nchunk),
        in_specs=[wide] * 5 + [pl.BlockSpec((nb, CHUNK, LANES), rows), wide,
                               pl.BlockSpec((nb, B_HEADS, B_DK, B_DV), state),
                               pl.BlockSpec((1, B_DV), lambda bi, c: (0, 0))],
        out_specs=[wide, pl.BlockSpec((nb, B_HEADS, B_DK, B_DV), state)],
        out_shape=[jax.ShapeDtypeStruct((batch, seq_len, B_V), BF16),
                   jax.ShapeDtypeStruct((batch, B_HEADS, B_DK, B_DV), F32)],
        compiler_params=_cparams(("parallel", "arbitrary")),
        name="b_scan",
    )(v3(qg), v3(kd), v3(u), v3(w), v3(attn), v3(egl), v3(z), s0, gout)
    return o.reshape(batch * seq_len, B_V), s_new


def _proj_c_kernel(x_ref, g1_ref, w_ref, cos_ref, sin_ref, q_ref, k_ref, v_ref, z_ref):
    h = _rms(x_ref[...], g1_ref[...]).astype(BF16)
    cos = cos_ref[...]
    sin = sin_ref[...]
    lane = lax.broadcasted_iota(jnp.int32, cos.shape, 1)
    first_half = (lane % C_DK) < (C_DK // 2)

    def rope(seg):
        swapped = jnp.where(first_half, pltpu.roll(seg, LANES - C_DK // 2, 1), pltpu.roll(seg, C_DK // 2, 1))
        return seg * cos + swapped * sin

    qk = jnp.dot(h, w_ref[:, 0:2 * C_QK], preferred_element_type=F32)
    for j in range(2 * C_QK // LANES):
        seg = rope(qk[:, j * LANES:(j + 1) * LANES])
        if j < C_QK // LANES:
            q_ref[:, j * LANES:(j + 1) * LANES] = seg
        else:
            jj = j - C_QK // LANES
            k_ref[:, jj * LANES:(jj + 1) * LANES] = seg * (C_DK ** -0.5)
    v_ref[...] = jnp.dot(h, w_ref[:, 2 * C_QK:2 * C_QK + C_V], preferred_element_type=F32)
    z_ref[...] = jnp.dot(h, w_ref[:, 2 * C_QK + C_V:2 * C_QK + 2 * C_V], preferred_element_type=F32)


def _proj_c(x, g1, w, cos, sin):
    n = x.shape[0]
    tm = 256
    tab_blocks = cos.shape[0] // tm
    row = lambda i: (i, 0)
    fixed = lambda i: (0, 0)
    tab = (lambda i: (i % tab_blocks, 0)) if tab_blocks > 1 else fixed
    return pl.pallas_call(
        _proj_c_kernel,
        grid=(n // tm,),
        in_specs=[
            pl.BlockSpec((tm, D_MODEL), row),
            pl.BlockSpec((1, D_MODEL), fixed),
            pl.BlockSpec((D_MODEL, 2 * C_QK + 2 * C_V), fixed),
            pl.BlockSpec((tm, LANES), tab),
            pl.BlockSpec((tm, LANES), tab),
        ],
        out_specs=[pl.BlockSpec((tm, C_QK), row), pl.BlockSpec((tm, C_QK), row),
                   pl.BlockSpec((tm, C_V), row), pl.BlockSpec((tm, C_V), row)],
        out_shape=[jax.ShapeDtypeStruct((n, C_QK), F32), jax.ShapeDtypeStruct((n, C_QK), F32),
                   jax.ShapeDtypeStruct((n, C_V), F32), jax.ShapeDtypeStruct((n, C_V), F32)],
        compiler_params=_cparams(("parallel",)),
        name="proj_c",
    )(x, g1, w, cos, sin)


def _log_gamma(hh):
    return math.log1p(-(2.0 ** (-5.0 - hh)))


def _c_scan_kernel(q_ref, k_ref, v_ref, z_ref, r0_ref, gout_ref, o_ref, r_ref, *, nb, t_valid):
    c = pl.program_id(1)
    rows = ROW_BLOCK

    @pl.when(c == 0)
    def _():
        r_ref[...] = r0_ref[...]

    left = jnp.clip(t_valid - c * rows, 0, rows)
    ri = lax.broadcasted_iota(jnp.int32, (rows, rows), 0)
    ci = lax.broadcasted_iota(jnp.int32, (rows, rows), 1)
    cnt_i = jnp.minimum(ri + 1, left).astype(F32)
    cnt_j = jnp.minimum(ci + 1, left).astype(F32)
    incl = ri >= ci
    steps = jnp.where(incl, cnt_i - cnt_j, 0.0)
    cnt_col = cnt_i[:, 0:1]
    left_f = left.astype(F32)
    qk_lane = lax.broadcasted_iota(jnp.int32, (1, C_QK), 1) // C_DK
    lg_lane = jnp.zeros((1, C_QK), F32)
    for hh in range(C_HEADS):
        lg_lane = jnp.where(qk_lane == hh, _log_gamma(hh), lg_lane)
    qk_sub = lax.broadcasted_iota(jnp.int32, (C_QK, 1), 0) // C_DK
    lg_sub = jnp.zeros((C_QK, 1), F32)
    for hh in range(C_HEADS):
        lg_sub = jnp.where(qk_sub == hh, _log_gamma(hh), lg_sub)
    q_scale = jnp.exp(cnt_col * lg_lane)
    k_scale = jnp.exp((left_f - cnt_col) * lg_lane)
    r_scale = jnp.exp(left_f * lg_sub)
    row_ok = (lax.broadcasted_iota(jnp.int32, (rows, 1), 0) + c * rows) < t_valid
    diag = (lax.broadcasted_iota(jnp.int32, (C_QK, C_V), 0) // C_DK) == (
        lax.broadcasted_iota(jnp.int32, (C_QK, C_V), 1) // C_DV)

    head = lambda hh: slice(hh * C_DV, (hh + 1) * C_DV)
    decays = [jnp.where(incl, jnp.exp(steps * _log_gamma(hh)), 0.0) for hh in range(C_HEADS)]
    qs = [q_ref[b] for b in range(nb)]
    ks = [jnp.where(row_ok, k_ref[b], 0.0) for b in range(nb)]
    vs = [v_ref[b].astype(BF16) for b in range(nb)]
    rs = [r_ref[b] for b in range(nb)]
    inters = [_dot(q * q_scale, r) for q, r in zip(qs, rs)]
    units = [(b, hh) for b in range(nb) for hh in range(C_HEADS)]
    atts = [decays[hh] * _dot_nt(qs[b], jnp.where(qk_lane == hh, ks[b], 0.0)) for b, hh in units]
    outs = [inters[b][:, head(hh)] + _dot(att, vs[b][:, head(hh)]) for (b, hh), att in zip(units, atts)]
    for b in range(nb):
        r_ref[b] = rs[b] * r_scale + jnp.where(diag, _dot_tn(ks[b] * k_scale, vs[b]), 0.0)
    for (b, hh), o in zip(units, outs):
        o_ref[b, :, head(hh)] = (_rms(o, gout_ref[...]) * _silu(z_ref[b, :, head(hh)])).astype(o_ref.dtype)


def _c_scan(q, k, v, z, r0, gout, batch, seq_len, t_valid):
    nb = 4
    nblk = seq_len // ROW_BLOCK
    v3 = lambda a: a.reshape(batch, seq_len, a.shape[-1])
    rows = lambda bi, c: (bi, c, 0)
    state = lambda bi, c: (bi, 0, 0)
    o, r_new = pl.pallas_call(
        functools.partial(_c_scan_kernel, nb=nb, t_valid=t_valid),
        grid=(batch // nb, nblk),
        in_specs=[pl.BlockSpec((nb, ROW_BLOCK, C_QK), rows), pl.BlockSpec((nb, ROW_BLOCK, C_QK), rows),
                  pl.BlockSpec((nb, ROW_BLOCK, C_V), rows), pl.BlockSpec((nb, ROW_BLOCK, C_V), rows),
                  pl.BlockSpec((nb, C_QK, C_V), state), pl.BlockSpec((1, C_DV), lambda bi, c: (0, 0))],
        out_specs=[pl.BlockSpec((nb, ROW_BLOCK, C_V), rows), pl.BlockSpec((nb, C_QK, C_V), state)],
        out_shape=[jax.ShapeDtypeStruct((batch, seq_len, C_V), BF16),
                   jax.ShapeDtypeStruct((batch, C_QK, C_V), F32)],
        compiler_params=_cparams(("parallel", "arbitrary")),
        name="c_scan",
    )(v3(q), v3(k), v3(v), v3(z), r0, gout)
    return o.reshape(batch * seq_len, C_V), r_new


def _merge_kernel(x_ref, g1_ref, wg_ref, o0_ref, o1_ref, o2_ref, l0_ref, l1_ref, l2_ref, ob_ref, oc_ref,
                  wa_ref, wb_ref, wc_ref, wo_ref, y_ref, *scr, residue_major):
    x = x_ref[...]
    tm = x.shape[0]
    h = _rms(x, g1_ref[...]).astype(BF16)
    lses = [r[...].reshape(tm, LANES) for r in (l0_ref, l1_ref, l2_ref)]
    outs = [r[...].reshape(tm, A_WIDTH) for r in (o0_ref, o1_ref, o2_ref)]
    heads = []
    for hh in range(A_HEADS):
        sl = slice(hh * A_HD, (hh + 1) * A_HD)
        ls = [l[:, 32 * hh:32 * hh + 1] for l in lses]
        m = jnp.maximum(jnp.maximum(ls[0], ls[1]), ls[2])
        es = [jnp.exp(l - m) for l in ls]
        tot = es[0] + es[1] + es[2]
        acc = (es[0] / tot) * outs[0][:, sl].astype(F32)
        acc = acc + (es[1] / tot) * outs[1][:, sl].astype(F32)
        acc = acc + (es[2] / tot) * outs[2][:, sl].astype(F32)
        heads.append(acc)
    o_a = jnp.concatenate(heads, axis=1)
    if residue_major:
        o_a = _swap_row_grid(scr[0], o_a)
    o_a = o_a.astype(BF16)
    merged = None
    for gi, (o_g, w_ref) in enumerate(((o_a, wa_ref), (ob_ref[...], wb_ref), (oc_ref[...], wc_ref))):
        gate = jax.nn.sigmoid(jnp.dot(h, wg_ref[:, gi * D_MODEL:(gi + 1) * D_MODEL], preferred_element_type=F32))
        term = gate * jnp.dot(o_g, w_ref[...], preferred_element_type=F32)
        merged = term if merged is None else merged + term
    y_ref[...] = x + jnp.dot(merged.astype(BF16), wo_ref[...], preferred_element_type=F32)


def _merge(x, g1, wg, o_groups, lses, o_b, o_c, wa, wb, wc, wo, seq_len, residue_major):
    n = x.shape[0]
    tm = RES * RES
    row = lambda i: (i, 0)
    fixed = lambda i: (0, 0)
    half = pl.BlockSpec((tm, A_WIDTH), row)
    wbr = pl.BlockSpec((A_WIDTH, D_MODEL), fixed)
    if residue_major:
        tiles = seq_len // tm
        grp = lambda i: (i // tiles, 0, i % tiles, 0)
        o_spec = pl.BlockSpec((None, RES, tm // RES, A_WIDTH), grp)
        lse = pl.BlockSpec((None, RES, tm // RES, LANES), grp)
        scratch = [pltpu.VMEM((A_WIDTH // LANES, tm, LANES), F32)]
    else:
        o_spec = half
        lse = pl.BlockSpec((tm, LANES), row)
        scratch = []
    return pl.pallas_call(
        functools.partial(_merge_kernel, residue_major=residue_major),
        grid=(n // tm,),
        in_specs=[pl.BlockSpec((tm, D_MODEL), row), pl.BlockSpec((1, D_MODEL), fixed),
                  pl.BlockSpec((D_MODEL, 3 * D_MODEL), fixed),
                  o_spec, o_spec, o_spec, lse, lse, lse, half, half, wbr, wbr, wbr,
                  pl.BlockSpec((D_MODEL, D_MODEL), fixed)],
        out_specs=pl.BlockSpec((tm, D_MODEL), row),
        out_shape=jax.ShapeDtypeStruct((n, D_MODEL), F32),
        scratch_shapes=scratch,
        compiler_params=_cparams(("parallel",)),
        name="merge",
    )(x, g1, wg, *o_groups, *lses, o_b, o_c, wa, wb, wc, wo)


def _ffn_kernel(x_ref, g2_ref, wg_ref, wu_ref, wo_ref, y_ref, h_scr, acc_scr):
    j = pl.program_id(1)

    @pl.when(j == 0)
    def _():
        h_scr[...] = _rms(x_ref[...], g2_ref[...]).astype(BF16)
        acc_scr[...] = jnp.zeros_like(acc_scr)

    h = h_scr[...]
    gate = jnp.dot(h, wg_ref[...], preferred_element_type=F32)
    up = jnp.dot(h, wu_ref[...], preferred_element_type=F32)
    acc_scr[...] += jnp.dot((_silu(gate) * up).astype(BF16), wo_ref[...], preferred_element_type=F32)

    @pl.when(j == pl.num_programs(1) - 1)
    def _():
        y_ref[...] = x_ref[...] + acc_scr[...]


def _ffn(x, g2, w_in, w_out):
    n = x.shape[0]
    tm = min(n, 1024)
    tf = 256
    nf = D_FF // tf
    row = lambda i, j: (i, 0)
    return pl.pallas_call(
        _ffn_kernel,
        grid=(n // tm, nf),
        in_specs=[pl.BlockSpec((tm, D_MODEL), row), pl.BlockSpec((1, D_MODEL), lambda i, j: (0, 0)),
                  pl.BlockSpec((D_MODEL, tf), lambda i, j: (0, j)),
                  pl.BlockSpec((D_MODEL, tf), lambda i, j: (0, nf + j)),
                  pl.BlockSpec((tf, D_MODEL), lambda i, j: (j, 0))],
        out_specs=pl.BlockSpec((tm, D_MODEL), row),
        out_shape=jax.ShapeDtypeStruct((n, D_MODEL), F32),
        scratch_shapes=[pltpu.VMEM((tm, D_MODEL), BF16), pltpu.VMEM((tm, D_MODEL), F32)],
        compiler_params=_cparams(("parallel", "arbitrary")),
        name="ffn",
    )(x, g2, w_in, w_in, w_out)


def _rope_tables(pos, hd, reps):
    inv = ROPE_THETA ** (-jnp.arange(0, hd, 2, dtype=F32) / hd)
    ang = pos.astype(F32)[:, None] * inv[None, :]
    cos = jnp.cos(ang)
    sin = jnp.sin(ang)
    cos2 = jnp.concatenate([cos, cos], axis=1)
    sin2 = jnp.concatenate([-sin, sin], axis=1)
    return jnp.tile(cos2, (1, reps)), jnp.tile(sin2, (1, reps))


def _pad_rows(a, batch, t, t_pad):
    if t == t_pad:
        return a
    a = a.reshape(batch, t, a.shape[-1])
    a = jnp.pad(a, ((0, 0), (0, t_pad - t), (0, 0)))
    return a.reshape(batch * t_pad, a.shape[-1])


def _unpad_rows(a, batch, t, t_pad):
    if t == t_pad:
        return a
    return a.reshape(batch, t_pad, a.shape[-1])[:, :t].reshape(batch * t, a.shape[-1])


def _layer(x, pos, batch, t, lw, caches, layer, conv_state, s0, r0):
    n = batch * t
    prompt = caches is None
    reps = max(1, 256 // t)
    cos_a, sin_a = _rope_tables(pos, A_HD, 1)
    cos_c, sin_c = _rope_tables(pos, C_DK, LANES // C_DK)
    if reps > 1:
        cos_a, sin_a, cos_c, sin_c = (jnp.tile(a, (reps, 1)) for a in (cos_a, sin_a, cos_c, sin_c))

    if prompt:
        to_rm = lambda a: a.reshape(t // RES, RES, A_HD).transpose(1, 0, 2)
        q, k, v = _proj_a(x, lw["g1"], lw["w_a"], lw["qn"], lw["kn"], to_rm(cos_a), to_rm(sin_a), batch, t, True)
    else:
        q, k, v = _proj_a(x, lw["g1"], lw["w_a"], lw["qn"], lw["kn"], cos_a, sin_a, batch, t, False)
    outs, lses = [], []
    for gi in range(N_GROUPS):
        if prompt:
            o, lse = _attn_prompt(q, k, v, gi, batch, t)
        else:
            o, lse = _attn_sample(q, k, v, caches[gi], layer, gi, batch, t)
        outs.append(o)
        lses.append(lse)
    new_kv = []
    for gi, (win, _) in enumerate(A_GROUPS):
        cols = slice(gi * A_WIDTH, (gi + 1) * A_WIDTH)
        if prompt:
            keep = min(win, t)
            tail = lambda a: a[:, :, (t - keep) // RES:, cols].transpose(0, 2, 1, 3).reshape(batch, keep, A_HEADS, A_HD)
        else:
            tail = lambda a: a[:, cols].reshape(batch, t, A_HEADS, A_HD)
        new_kv.append(jnp.stack([tail(k), tail(v)], axis=2))

    t_pad = -(-t // ROW_BLOCK) * ROW_BLOCK
    p, z_b, bac, bar = _proj_b(x, lw["g1"], lw["w_bqkv"], lw["w_bz"], lw["w_ba"], lw["w_bat"])
    conv_new = jnp.concatenate([conv_state, p.reshape(batch, t, B_CONV_CH)], axis=1)[:, -(B_CONV - 1):]
    cst = jnp.pad(conv_state, ((0, 0), (SUBLANES - (B_CONV - 1), 0), (0, 0)))
    bar_p = _pad_rows(bar.T, batch, t, t_pad).T if t_pad != t else bar
    qg, kd, u, w, attn, egl = _b_prep(
        _pad_rows(p, batch, t, t_pad), cst, _pad_rows(bac, batch, t, t_pad), bar_p, lw["conv_w"],
        lw["alog_r"], lw["dt_r"], lw["alog_c"], lw["dt_c"], t_pad, t)
    o_b, s_new = _b_scan(qg, kd, u, w, attn, egl, _pad_rows(z_b, batch, t, t_pad), s0, lw["gb"], batch, t_pad)
    o_b = _unpad_rows(o_b, batch, t, t_pad)

    cq, ck, cv, cz = _proj_c(x, lw["g1"], lw["w_c"], cos_c, sin_c)
    r_bd = jnp.zeros((batch, C_HEADS, C_DK, C_HEADS, C_DV), F32)
    for hh in range(C_HEADS):
        r_bd = r_bd.at[:, hh, :, hh, :].set(r0[:, hh])
    o_c, r_new = _c_scan(*(_pad_rows(a, batch, t, t_pad) for a in (cq, ck, cv, cz)),
                         r_bd.reshape(batch, C_QK, C_V), lw["gc"], batch, t_pad, t)
    o_c = _unpad_rows(o_c, batch, t, t_pad)
    r_new = r_new.reshape(batch, C_HEADS, C_DK, C_HEADS, C_DV)
    r_new = jnp.stack([r_new[:, hh, :, hh, :] for hh in range(C_HEADS)], axis=1)

    x = _merge(x, lw["g1"], lw["w_g"], outs, lses, o_b, o_c, lw["w_oa"], lw["w_ob"], lw["w_oc"], lw["w_o"], t, prompt)
    x = _ffn(x, lw["g2"], lw["w_fi"], lw["w_fo"])
    return x, new_kv, conv_new, s_new, r_new


def _layer_weights(l, norm1_g, w_in, a_q_norm_g, a_k_norm_g, b_conv_w, b_a_log, b_dt_bias, b_out_norm_g,
                   c_out_norm_g, w_out_a, w_out_b, w_out_c, w_out, norm2_g, w_ffn_in, w_ffn_out):
    o = IN_OFFS
    wl = w_in[l]
    w_ba = wl[:, o[3]:o[5]]
    pad_r = lambda a: jnp.pad(a.reshape(1, B_HEADS), ((0, 0), (B_HEADS, LANES - 2 * B_HEADS)))
    pad_c = lambda a: jnp.pad(a.reshape(B_HEADS, 1), ((B_HEADS, 2 * SUBLANES - 2 * B_HEADS), (0, 0)))
    return dict(
        g1=norm1_g[l].reshape(1, D_MODEL), g2=norm2_g[l].reshape(1, D_MODEL),
        w_a=wl[:, o[0]:o[1]].astype(BF16),
        w_bqkv=wl[:, o[1]:o[2]].astype(BF16), w_bz=wl[:, o[2]:o[3]].astype(BF16),
        w_ba=jnp.pad(w_ba, ((0, 0), (0, LANES - 2 * B_HEADS))).astype(BF16),
        w_bat=jnp.pad(w_ba.T, ((0, 2 * SUBLANES - 2 * B_HEADS), (0, 0))).astype(BF16),
        w_c=wl[:, o[5]:o[9]].astype(BF16), w_g=wl[:, o[9]:o[10]].astype(BF16),
        qn=a_q_norm_g[l].reshape(1, A_HD), kn=a_k_norm_g[l].reshape(1, A_HD),
        conv_w=b_conv_w[l],
        alog_r=pad_r(b_a_log[l]), dt_r=pad_r(b_dt_bias[l]), alog_c=pad_c(b_a_log[l]), dt_c=pad_c(b_dt_bias[l]),
        gb=b_out_norm_g[l].reshape(1, B_DV), gc=c_out_norm_g[l].reshape(1, C_DV),
        w_oa=w_out_a[l].astype(BF16), w_ob=w_out_b[l].astype(BF16), w_oc=w_out_c[l].astype(BF16),
        w_o=w_out[l].astype(BF16), w_fi=w_ffn_in[l].astype(BF16), w_fo=w_ffn_out[l].astype(BF16),
    )


def kernel(x_prompt, x_sample, cache_a_kv0, cache_a_kv1, cache_a_kv2, state_b_conv, state_b_S, state_c_R, norm1_g, w_in, a_q_norm_g, a_k_norm_g, b_conv_w, b_a_log, b_dt_bias, b_out_norm_g, c_out_norm_g, w_out_a, w_out_b, w_out_c, w_out, norm2_g, w_ffn_in, w_ffn_out):
    bp, t = x_prompt.shape[:2]
    bs, s = x_sample.shape[:2]
    depth = w_in.shape[0]
    pos_p = jnp.arange(t)
    pos_s = PAST_LEN + jnp.arange(s)
    yp = x_prompt.reshape(bp * t, D_MODEL)
    ys = x_sample.reshape(bs * s, D_MODEL)
    caches = (cache_a_kv0, cache_a_kv1, cache_a_kv2)
    zeros_conv = jnp.zeros((bp, B_CONV - 1, B_CONV_CH), F32)
    zeros_s = jnp.zeros((bp, B_HEADS, B_DK, B_DV), F32)
    zeros_r = jnp.zeros((bp, C_HEADS, C_DK, C_DV), F32)
    acc = [[] for _ in range(12)]
    for l in range(depth):
        lw = _layer_weights(l, norm1_g, w_in, a_q_norm_g, a_k_norm_g, b_conv_w, b_a_log, b_dt_bias,
                            b_out_norm_g, c_out_norm_g, w_out_a, w_out_b, w_out_c, w_out, norm2_g,
                            w_ffn_in, w_ffn_out)
        yp, kv, cv, sn, rn = _layer(yp, pos_p, bp, t, lw, None, l, zeros_conv, zeros_s, zeros_r)
        for i, a in enumerate((kv[0], kv[1], kv[2], cv, sn, rn)):
            acc[i].append(a)
        ys, kv, cv, sn, rn = _layer(ys, pos_s, bs, s, lw, caches, l, state_b_conv[l], state_b_S[l], state_c_R[l])
        for i, a in enumerate((kv[0], kv[1], kv[2], cv, sn, rn)):
            acc[6 + i].append(a)
    return (yp.reshape(bp, t, D_MODEL), ys.reshape(bs, s, D_MODEL)) + tuple(jnp.stack(a) for a in acc)
```

```python
import functools
import math

import jax
import jax.numpy as jnp
import numpy as np
from jax import lax
from jax.experimental import pallas as pl
from jax.experimental.pallas import tpu as pltpu

F32 = jnp.float32
BF16 = jnp.bfloat16

D_MODEL = 1024
PAST_LEN = 8192
A_GROUPS = ((128, 1), (512, 4), (2048, 16))
N_GROUPS = 3
A_HEADS = 4
A_HD = 128
A_WIDTH = A_HEADS * A_HD
A_KEYS = 128
B_HEADS = 4
B_DK = 128
B_DV = 128
B_CONV = 4
B_QK = B_HEADS * B_DK
B_V = B_HEADS * B_DV
B_CONV_CH = 2 * B_QK + B_V
C_HEADS = 4
C_DK = 64
C_DV = 128
C_QK = C_HEADS * C_DK
C_V = C_HEADS * C_DV
CHUNK = 64
ROPE_THETA = 10000.0
EPS = 1e-6
D_FF = 2816
IN_SIZES = (3 * N_GROUPS * A_WIDTH, B_CONV_CH, B_V, B_HEADS, B_HEADS, C_QK, C_QK, C_V, C_V, 3 * D_MODEL)
IN_OFFS = tuple(int(v) for v in np.cumsum((0,) + IN_SIZES))

ROW_BLOCK = 128
RES = 16
SUBLANES = 8
LANES = 128
VMEM_LIMIT = 48 * 1024 * 1024


def _cparams(sem):
    return pltpu.CompilerParams(dimension_semantics=sem, vmem_limit_bytes=VMEM_LIMIT)


def _rms(x, g):
    return x * lax.rsqrt(jnp.mean(x * x, axis=-1, keepdims=True) + EPS) * g


def _silu(x):
    return x * jax.nn.sigmoid(x)


def _softplus(x):
    return jnp.maximum(x, 0.0) + jnp.log(1.0 + jnp.exp(-jnp.abs(x)))


def _dot(a, b):
    return jnp.dot(a.astype(BF16), b.astype(BF16), preferred_element_type=F32)


def _dot_nt(a, b):
    return lax.dot_general(a.astype(BF16), b.astype(BF16), (((1,), (1,)), ((), ())), preferred_element_type=F32)


def _dot_tn(a, b):
    return lax.dot_general(a.astype(BF16), b.astype(BF16), (((0,), (0,)), ((), ())), preferred_element_type=F32)


def _swap_row_grid(scr, val):
    slabs = val.shape[1] // LANES
    for c in range(slabs):
        scr[c] = val[:, c * LANES:(c + 1) * LANES]
    cols = [jnp.concatenate([scr[c, pl.ds(r, RES, stride=RES), :] for r in range(RES)], axis=0)
            for c in range(slabs)]
    return jnp.concatenate(cols, axis=1)


def _proj_a_kernel(x_ref, g1_ref, w_ref, qg_ref, kg_ref, cos_ref, sin_ref, q_ref, k_ref, v_ref, *rest, residue_major):
    x = x_ref[...]
    tm = x.shape[0]
    if residue_major:
        kt_ref, vt_ref, scr = rest
        x = _swap_row_grid(scr, x)
    h = _rms(x, g1_ref[...]).astype(BF16)
    cos = cos_ref[...].reshape(tm, A_HD)
    sin = sin_ref[...].reshape(tm, A_HD)

    def norm_rope(seg, g):
        y = _rms(seg, g)
        return y * cos + pltpu.roll(y, A_HD // 2, 1) * sin

    def put(ref, col, val):
        if residue_major:
            ref[:, :, col:col + val.shape[1]] = val.reshape(RES, tm // RES, val.shape[1]).astype(ref.dtype)
        else:
            ref[:, col:col + val.shape[1]] = val

    for j in range(3 * N_GROUPS):
        acc = jnp.dot(h, w_ref[:, j * A_WIDTH:(j + 1) * A_WIDTH], preferred_element_type=F32)
        if j < N_GROUPS:
            for hh in range(A_HEADS):
                sl = slice(hh * A_HD, (hh + 1) * A_HD)
                put(q_ref, j * A_WIDTH + hh * A_HD, norm_rope(acc[:, sl], qg_ref[...]) * (A_HD ** -0.5))
        elif j < 2 * N_GROUPS:
            jj = j - N_GROUPS
            for hh in range(A_HEADS):
                sl = slice(hh * A_HD, (hh + 1) * A_HD)
                val = norm_rope(acc[:, sl], kg_ref[...])
                put(k_ref, jj * A_WIDTH + hh * A_HD, val)
                if residue_major:
                    put(kt_ref, jj * A_WIDTH + hh * A_HD, val)
        else:
            jj = j - 2 * N_GROUPS
            put(v_ref, jj * A_WIDTH, acc)
            if residue_major:
                put(vt_ref, jj * A_WIDTH, acc)


def _proj_a(x, g1, w, qg, kg, cos, sin, batch, seq_len, residue_major):
    n = x.shape[0]
    tm = RES * RES
    nw = N_GROUPS * A_WIDTH
    fixed = lambda i: (0, 0)
    common = [pl.BlockSpec((1, D_MODEL), fixed), pl.BlockSpec((D_MODEL, 3 * nw), fixed),
              pl.BlockSpec((1, A_HD), fixed), pl.BlockSpec((1, A_HD), fixed)]
    if residue_major:
        tiles = seq_len // tm
        tail_tiles = min(max(wd for wd, _ in A_GROUPS), seq_len) // tm
        blk = (None, RES, tm // RES, nw)
        tab = pl.BlockSpec((RES, tm // RES, A_HD), lambda i: (0, i % tiles, 0))
        main = pl.BlockSpec(blk, lambda i: (i // tiles, 0, i % tiles, 0))
        tail = pl.BlockSpec(blk, lambda i: (i // tiles, 0, jnp.maximum(i % tiles - (tiles - tail_tiles), 0), 0))
        out_specs = [main] * 3 + [tail] * 2
        out_shape = ([jax.ShapeDtypeStruct((batch, RES, seq_len // RES, nw), BF16)] * 3
                     + [jax.ShapeDtypeStruct((batch, RES, tail_tiles * tm // RES, nw), F32)] * 2)
        scratch = [pltpu.VMEM((D_MODEL // LANES, tm, LANES), F32)]
    else:
        assert cos.shape[0] == tm
        tab = pl.BlockSpec((tm, A_HD), fixed)
        out_specs = [pl.BlockSpec((tm, nw), lambda i: (i, 0))] * 3
        out_shape = [jax.ShapeDtypeStruct((n, nw), F32)] * 3
        scratch = []
    return pl.pallas_call(
        functools.partial(_proj_a_kernel, residue_major=residue_major),
        grid=(n // tm,),
        in_specs=[pl.BlockSpec((tm, D_MODEL), lambda i: (i, 0))] + common + [tab, tab],
        out_specs=out_specs,
        out_shape=out_shape,
        scratch_shapes=scratch,
        compiler_params=_cparams(("arbitrary",)),
        name="proj_a",
    )(x, g1, w, qg, kg, cos, sin)


ATTN_SUBS = 2


def _attn_prompt_kernel(q_ref, kc_ref, kp_ref, vc_ref, vp_ref, o_ref, lse_ref, *, parts):
    n = pl.program_id(2)
    per = ROW_BLOCK // parts
    qi = lax.broadcasted_iota(jnp.int32, (ROW_BLOCK, ROW_BLOCK), 0)
    kj = lax.broadcasted_iota(jnp.int32, (ROW_BLOCK, ROW_BLOCK), 1)
    qi = parts * (qi % per) + qi // per
    kj = parts * (kj % per) + kj // per
    cur_ok = kj <= qi
    prev_ok = kj >= qi
    first_ok = jnp.logical_and(prev_ok, n > 0)
    lane = lax.broadcasted_iota(jnp.int32, (ROW_BLOCK, LANES), 1)
    neg = -jnp.inf
    packed_rows = 2 * SUBLANES

    def sub(ref, half, sl):
        full = ref[:, :, sl]
        if per % packed_rows == 0:
            return full[:, half * per:(half + 1) * per].reshape(ROW_BLOCK, A_HD)
        return full.astype(F32)[:, half * per:(half + 1) * per].reshape(ROW_BLOCK, A_HD).astype(BF16)

    units = [(half, hh) for half in range(ATTN_SUBS) for hh in range(A_HEADS)]
    head = lambda hh: slice(hh * A_HD, (hh + 1) * A_HD)

    keys = {(half, hh): sub(kc_ref, half, head(hh)) for half, hh in units}
    vals = {(half, hh): sub(vc_ref, half, head(hh)) for half, hh in units}
    for hh in range(A_HEADS):
        keys[(-1, hh)] = sub(kp_ref, ATTN_SUBS - 1, head(hh))
        vals[(-1, hh)] = sub(vp_ref, ATTN_SUBS - 1, head(hh))

    scores = []
    for half, hh in units:
        q = sub(q_ref, half, head(hh))
        s_cur = jnp.where(cur_ok, _dot_nt(q, keys[(half, hh)]), neg)
        s_prev = jnp.where(first_ok if half == 0 else prev_ok, _dot_nt(q, keys[(half - 1, hh)]), neg)
        scores.append((s_cur, s_prev))
    probs = []
    for s_cur, s_prev in scores:
        m = jnp.maximum(jnp.max(s_cur, axis=-1, keepdims=True), jnp.max(s_prev, axis=-1, keepdims=True))
        p_cur = jnp.exp(s_cur - m)
        p_prev = jnp.exp(s_prev - m)
        den = jnp.sum(p_cur, axis=-1, keepdims=True) + jnp.sum(p_prev, axis=-1, keepdims=True)
        probs.append((p_cur, p_prev, m, den))
    lse_blk = [jnp.zeros((ROW_BLOCK, LANES), F32) for _ in range(ATTN_SUBS)]
    outs = {}
    for (half, hh), (p_cur, p_prev, m, den) in zip(units, probs):
        o = (_dot(p_cur, vals[(half, hh)]) + _dot(p_prev, vals[(half - 1, hh)])) / den
        outs[(half, hh)] = o.reshape(parts, per, A_HD)
        lse_blk[half] = jnp.where(lane // 32 == hh, m + jnp.log(den), lse_blk[half])
    for hh in range(A_HEADS):
        both = jnp.concatenate([outs[(half, hh)] for half in range(ATTN_SUBS)], axis=1)
        o_ref[:, :, head(hh)] = both.astype(o_ref.dtype)
    for half in range(ATTN_SUBS):
        lse_ref[:, half * per:(half + 1) * per, :] = lse_blk[half].reshape(parts, per, LANES)


def _attn_prompt(q, k, v, gi, batch, seq_len):
    _, dil = A_GROUPS[gi]
    parts = RES // dil
    per = ROW_BLOCK // parts
    rows = seq_len // RES
    nblk = seq_len // dil // (ATTN_SUBS * ROW_BLOCK)
    split = lambda a: a.reshape(batch, parts, dil, rows, a.shape[-1])
    cur = lambda b, r, n: (b, 0, r, n, gi)
    prev = lambda b, r, n: (b, 0, r, jnp.maximum(n - 1, 0), gi)
    out = lambda b, r, n: (b, 0, r, n, 0)
    blk = (None, parts, None, ATTN_SUBS * per, A_WIDTH)
    o, lse = pl.pallas_call(
        functools.partial(_attn_prompt_kernel, parts=parts),
        grid=(batch, dil, nblk),
        in_specs=[pl.BlockSpec(blk, cur), pl.BlockSpec(blk, cur), pl.BlockSpec(blk, prev),
                  pl.BlockSpec(blk, cur), pl.BlockSpec(blk, prev)],
        out_specs=[pl.BlockSpec(blk, out), pl.BlockSpec((None, parts, None, ATTN_SUBS * per, LANES), out)],
        out_shape=[jax.ShapeDtypeStruct((batch, parts, dil, rows, A_WIDTH), BF16),
                   jax.ShapeDtypeStruct((batch, parts, dil, rows, LANES), F32)],
        compiler_params=_cparams(("parallel", "parallel", "arbitrary")),
        name=f"attn_prompt_g{gi}",
    )(split(q), split(k), split(k), split(v), split(v))
    return o.reshape(batch, RES, rows, A_WIDTH), lse.reshape(batch, RES, rows, LANES)


def _attn_sample_kernel(q_ref, kn_ref, vn_ref, cache_ref, o_ref, lse_ref, *, dil, n_new):
    row = lax.broadcasted_iota(jnp.int32, (A_KEYS, A_HEADS, 1), 0)
    trow = lax.broadcasted_iota(jnp.int32, (n_new, A_HEADS, 1), 0)
    neg = -jnp.inf
    kn = kn_ref[...]
    vn = vn_ref[...]
    for s in range(n_new):
        res = s % dil
        first = s // dil
        q = q_ref[s][None]
        kc = cache_ref[:, res, 0]
        vc = cache_ref[:, res, 1]
        sc = jnp.sum(kc * q, axis=-1, keepdims=True)
        if first > 0:
            sc = jnp.where(row >= first, sc, neg)
        new_ok = jnp.logical_and(trow <= s, (s - trow) % dil == 0)
        sn = jnp.where(new_ok, jnp.sum(kn * q, axis=-1, keepdims=True), neg)
        m = jnp.maximum(jnp.max(sc, axis=0, keepdims=True), jnp.max(sn, axis=0, keepdims=True))
        pc = jnp.exp(sc - m)
        pn = jnp.exp(sn - m)
        den = jnp.sum(pc, axis=0, keepdims=True) + jnp.sum(pn, axis=0, keepdims=True)
        o = (jnp.sum(pc * vc, axis=0, keepdims=True) + jnp.sum(pn * vn, axis=0, keepdims=True)) / den
        o_ref[s] = o[0]
        lse_ref[s] = jnp.broadcast_to((m + jnp.log(den))[0], (A_HEADS, A_HD))


def _attn_sample(q, k, v, cache, layer, gi, batch, n_new):
    win, dil = A_GROUPS[gi]
    depth = cache.shape[0]
    assert cache.shape[2] == win and win // dil == A_KEYS
    n_res = min(dil, n_new)
    cv = cache.reshape(depth, batch, A_KEYS, dil, 2, A_HEADS, A_HD)
    heads = lambda a: a.reshape(batch, n_new, N_GROUPS, A_HEADS, A_HD)
    grp = lambda b: (b, 0, gi, 0, 0)
    new = pl.BlockSpec((None, n_new, None, A_HEADS, A_HD), grp)
    out = pl.BlockSpec((None, n_new, A_HEADS, A_HD), lambda b: (b, 0, 0, 0))
    o, lse = pl.pallas_call(
        functools.partial(_attn_sample_kernel, dil=dil, n_new=n_new),
        grid=(batch,),
        in_specs=[new, new, new,
                  pl.BlockSpec((None, None, A_KEYS, n_res, 2, A_HEADS, A_HD), lambda b: (layer, b, 0, 0, 0, 0, 0))],
        out_specs=[out, out],
        out_shape=[jax.ShapeDtypeStruct((batch, n_new, A_HEADS, A_HD), F32)] * 2,
        compiler_params=_cparams(("parallel",)),
        name=f"attn_sample_g{gi}",
    )(heads(q), heads(k), heads(v), cv)
    lse = jnp.repeat(lse[..., 0], LANES // A_HEADS, axis=-1)
    return o.reshape(batch * n_new, A_WIDTH), lse.reshape(batch * n_new, LANES)


def _proj_b_kernel(x_ref, g1_ref, wqkv_ref, wz_ref, wba_ref, wbat_ref, p_ref, z_ref, bac_ref, bar_ref):
    h = _rms(x_ref[...], g1_ref[...]).astype(BF16)
    for j in range(3):
        sl = slice(j * B_QK, (j + 1) * B_QK)
        p_ref[:, sl] = jnp.dot(h, wqkv_ref[:, sl], preferred_element_type=F32)
    z_ref[...] = jnp.dot(h, wz_ref[...], preferred_element_type=F32).astype(z_ref.dtype)
    bac_ref[...] = jnp.dot(h, wba_ref[...], preferred_element_type=F32)
    bar_ref[...] = lax.dot_general(wbat_ref[...], h, (((1,), (1,)), ((), ())), preferred_element_type=F32)


def _proj_b(x, g1, wqkv, wz, wba, wbat):
    n = x.shape[0]
    tm = 256
    row = lambda i: (i, 0)
    fixed = lambda i: (0, 0)
    return pl.pallas_call(
        _proj_b_kernel,
        grid=(n // tm,),
        in_specs=[
            pl.BlockSpec((tm, D_MODEL), row),
            pl.BlockSpec((1, D_MODEL), fixed),
            pl.BlockSpec((D_MODEL, B_CONV_CH), fixed),
            pl.BlockSpec((D_MODEL, B_V), fixed),
            pl.BlockSpec((D_MODEL, LANES), fixed),
            pl.BlockSpec((2 * SUBLANES, D_MODEL), fixed),
        ],
        out_specs=[pl.BlockSpec((tm, B_CONV_CH), row), pl.BlockSpec((tm, B_V), row),
                   pl.BlockSpec((tm, LANES), row), pl.BlockSpec((2 * SUBLANES, tm), lambda i: (0, i))],
        out_shape=[jax.ShapeDtypeStruct((n, B_CONV_CH), F32), jax.ShapeDtypeStruct((n, B_V), BF16),
                   jax.ShapeDtypeStruct((n, LANES), F32), jax.ShapeDtypeStruct((2 * SUBLANES, n), F32)],
        compiler_params=_cparams(("parallel",)),
        name="proj_b",
    )(x, g1, wqkv, wz, wba, wbat)


def _b_prep_kernel(p_ref, halo_ref, cst_ref, bac_ref, bar_ref, cw_ref, alog_r_ref, dt_r_ref, alog_c_ref, dt_c_ref,
                   qg_ref, kd_ref, u_ref, w_ref, attn_ref, egl_ref, e_scr, *, blocks_per_seq, t_valid):
    i = pl.program_id(0)
    blk = i % blocks_per_seq
    rows = ROW_BLOCK

    before = jnp.where(blk == 0, cst_ref[...], halo_ref[...])
    e_scr[0:SUBLANES, :] = before
    e_scr[SUBLANES:SUBLANES + rows, :] = p_ref[...]
    xc = e_scr[SUBLANES:SUBLANES + rows, :] * cw_ref[B_CONV - 1:B_CONV, :]
    for kk in range(1, B_CONV):
        xc = xc + e_scr[SUBLANES - kk:SUBLANES - kk + rows, :] * cw_ref[B_CONV - 1 - kk:B_CONV - kk, :]
    act = _silu(xc)

    ri = lax.broadcasted_iota(jnp.int32, (rows, LANES), 0)
    li = lax.broadcasted_iota(jnp.int32, (rows, LANES), 1)
    li16 = lax.broadcasted_iota(jnp.int32, (2 * SUBLANES, LANES), 1)
    li1 = lax.broadcasted_iota(jnp.int32, (1, LANES), 1)
    masked = t_valid < blocks_per_seq * rows
    if masked:
        row_ok = (blk * rows + ri) < t_valid
        col_ok = (blk * rows + li16) < t_valid
        act = jnp.where(ri[:, 0:1] + blk * rows < t_valid, act, 0.0)

    head_lane = jnp.logical_and(li1 >= B_HEADS, li1 < 2 * B_HEADS)
    a_r = jnp.where(head_lane, -jnp.exp(alog_r_ref[...]), 0.0)
    g_col = a_r * _softplus(bac_ref[...] + dt_r_ref[...])
    si = lax.broadcasted_iota(jnp.int32, (2 * SUBLANES, 1), 0)
    head_sub = jnp.logical_and(si >= B_HEADS, si < 2 * B_HEADS)
    a_c = jnp.where(head_sub, -jnp.exp(alog_c_ref[...]), 0.0)
    g_row = a_c * _softplus(bar_ref[...] + dt_c_ref[...])
    if masked:
        g_col = jnp.where(row_ok, g_col, 0.0)
        g_row = jnp.where(col_ok, g_row, 0.0)

    rpos = ri % CHUNK
    lpos = li16 % CHUNK
    gc = g_col
    rev = g_col
    gcr = g_row
    step = 1
    while step < CHUNK:
        gc = gc + jnp.where(rpos >= step, pltpu.roll(gc, step, 0), 0.0)
        rev = rev + jnp.where(rpos < CHUNK - step, pltpu.roll(rev, rows - step, 0), 0.0)
        gcr = gcr + jnp.where(lpos >= step, pltpu.roll(gcr, step, 1), 0.0)
        step *= 2
    rev = rev - g_col
    egl_ref[...] = jnp.exp(gc + rev)

    same = (ri // CHUNK) == (li // CHUNK)
    incl = jnp.logical_and(same, ri >= li)
    strict = jnp.logical_and(same, ri > li)
    eye = (ri == li).astype(F32)

    lows, rhss = [], []
    for hh in range(B_HEADS):
        sl = slice(hh * B_DK, (hh + 1) * B_DK)
        gc_c = gc[:, B_HEADS + hh:B_HEADS + hh + 1]
        gc_r = gcr[B_HEADS + hh:B_HEADS + hh + 1, :]
        dec = jnp.where(incl, jnp.exp(jnp.where(incl, gc_c - gc_r, 0.0)), 0.0)
        q = act[:, sl]
        q = q * lax.rsqrt(jnp.sum(q * q, axis=-1, keepdims=True) + EPS) * (B_DK ** -0.5)
        k = act[:, B_QK + hh * B_DK:B_QK + (hh + 1) * B_DK]
        k = k * lax.rsqrt(jnp.sum(k * k, axis=-1, keepdims=True) + EPS)
        v = act[:, 2 * B_QK + hh * B_DV:2 * B_QK + (hh + 1) * B_DV]
        beta = jax.nn.sigmoid(bac_ref[:, hh:hh + 1])
        kb = k * beta
        kbf = k.astype(BF16)
        lows.append(jnp.where(strict, dec * _dot_nt(kb, kbf), 0.0))
        attn_ref[:, sl] = (dec * _dot_nt(q, kbf)).astype(attn_ref.dtype)
        rhss.append(jnp.concatenate([v * beta, kb * jnp.exp(gc_c)], axis=1).astype(BF16))
        qg_ref[:, sl] = (q * jnp.exp(gc_c)).astype(qg_ref.dtype)
        kd_ref[:, sl] = (k * jnp.exp(rev[:, B_HEADS + hh:B_HEADS + hh + 1])).astype(kd_ref.dtype)

    tinvs = [eye - low for low in lows]
    pws = lows
    sq = 2
    while sq < CHUNK:
        pws = [_dot(pw, pw) for pw in pws]
        tinvs = [tinv + _dot(tinv, pw) for tinv, pw in zip(tinvs, pws)]
        sq *= 2
    for hh in range(B_HEADS):
        sol = _dot(tinvs[hh], rhss[hh])
        u_ref[:, hh * B_DV:(hh + 1) * B_DV] = sol[:, :B_DV]
        w_ref[:, hh * B_DK:(hh + 1) * B_DK] = sol[:, B_DV:].astype(w_ref.dtype)


def _b_prep(p, cstate, bac, bar, cw, alog_r, dt_r, alog_c, dt_c, seq_len, t_valid):
    n = p.shape[0]
    bps = seq_len // ROW_BLOCK
    row = lambda i: (i, 0)
    fixed = lambda i: (0, 0)
    per_row = ROW_BLOCK // SUBLANES
    wide = lambda dt: jax.ShapeDtypeStruct((n, B_V), dt)
    return pl.pallas_call(
        functools.partial(_b_prep_kernel, blocks_per_seq=bps, t_valid=t_valid),
        grid=(n // ROW_BLOCK,),
        in_specs=[
            pl.BlockSpec((ROW_BLOCK, B_CONV_CH), row),
            pl.BlockSpec((SUBLANES, B_CONV_CH), lambda i: (jnp.maximum(i * per_row - 1, 0), 0)),
            pl.BlockSpec((None, SUBLANES, B_CONV_CH), lambda i: (i // bps, 0, 0)),
            pl.BlockSpec((ROW_BLOCK, LANES), row),
            pl.BlockSpec((2 * SUBLANES, ROW_BLOCK), lambda i: (0, i)),
            pl.BlockSpec((B_CONV, B_CONV_CH), fixed),
            pl.BlockSpec((1, LANES), fixed),
            pl.BlockSpec((1, LANES), fixed),
            pl.BlockSpec((2 * SUBLANES, 1), fixed),
            pl.BlockSpec((2 * SUBLANES, 1), fixed),
        ],
        out_specs=[pl.BlockSpec((ROW_BLOCK, B_V), row)] * 5 + [pl.BlockSpec((ROW_BLOCK, LANES), row)],
        out_shape=[wide(BF16), wide(BF16), wide(F32), wide(BF16), wide(BF16), jax.ShapeDtypeStruct((n, LANES), F32)],
        scratch_shapes=[pltpu.VMEM((SUBLANES + ROW_BLOCK, B_CONV_CH), F32)],
        compiler_params=_cparams(("parallel",)),
        name="b_prep",
    )(p, p, cstate, bac, bar, cw, alog_r, dt_r, alog_c, dt_c)


def _b_scan_kernel(qg_ref, kd_ref, u_ref, w_ref, attn_ref, egl_ref, z_ref, s0_ref, gout_ref, o_ref, s_ref, *, nb):
    c = pl.program_id(1)

    @pl.when(c == 0)
    def _():
        s_ref[...] = s0_ref[...]

    half = c % (ROW_BLOCK // CHUNK)
    rgrp = lax.broadcasted_iota(jnp.int32, (ROW_BLOCK, B_DV), 0) // CHUNK
    here = rgrp == half
    units = [(b, hh) for b in range(nb) for hh in range(B_HEADS)]
    head = lambda hh: slice(hh * B_DV, (hh + 1) * B_DV)
    states = [s_ref[b, hh] for b, hh in units]
    proj = [_dot(jnp.concatenate([w_ref[b, :, head(hh)], qg_ref[b, :, head(hh)]], axis=0), s)
            for (b, hh), s in zip(units, states)]
    v_new = [u_ref[b, :, head(hh)] - pr[:CHUNK] for (b, hh), pr in zip(units, proj)]
    outs = []
    for (b, hh), pr, vn in zip(units, proj, v_new):
        v_full = jnp.where(here, jnp.concatenate([vn] * (ROW_BLOCK // CHUNK), axis=0), 0.0)
        outs.append(pr[CHUNK:] + _dot(attn_ref[b, :, head(hh)], v_full))
    for (b, hh), s, vn in zip(units, states, v_new):
        decay = egl_ref[b, 0:1, B_HEADS + hh:B_HEADS + hh + 1]
        s_ref[b, hh] = s * decay + _dot_tn(kd_ref[b, :, head(hh)], vn)
    for (b, hh), o in zip(units, outs):
        gate = _silu(z_ref[b, :, head(hh)].astype(F32))
        o_ref[b, :, head(hh)] = (_rms(o, gout_ref[...]) * gate).astype(o_ref.dtype)


def _b_scan(qg, kd, u, w, attn, egl, z, s0, gout, batch, seq_len):
    nb = 4
    nchunk = seq_len // CHUNK
    v3 = lambda a: a.reshape(batch, seq_len, a.shape[-1])
    rows = lambda bi, c: (bi, c, 0)
    state = lambda bi, c: (bi, 0, 0, 0)
    wide = pl.BlockSpec((nb, CHUNK, B_V), rows)
    o, s_new = pl.pallas_call(
        functools.partial(_b_scan_kernel, nb=nb),
        grid=(batch // nb, nchunk),
        in_specs=[wide] * 5 + [pl.BlockSpec((nb, CHUNK, LANES), rows), wide,
                               pl.BlockSpec((nb, B_HEADS, B_DK, B_DV), state),
                               pl.BlockSpec((1, B_DV), lambda bi, c: (0, 0))],
        out_specs=[wide, pl.BlockSpec((nb, B_HEADS, B_DK, B_DV), state)],
        out_shape=[jax.ShapeDtypeStruct((batch, seq_len, B_V), BF16),
                   jax.ShapeDtypeStruct((batch, B_HEADS, B_DK, B_DV), F32)],
        compiler_params=_cparams(("parallel", "arbitrary")),
        name="b_scan",
    )(v3(qg), v3(kd), v3(u), v3(w), v3(attn), v3(egl), v3(z), s0, gout)
    return o.reshape(batch * seq_len, B_V), s_new


def _proj_c_kernel(x_ref, g1_ref, w_ref, cos_ref, sin_ref, q_ref, k_ref, v_ref, z_ref):
    h = _rms(x_ref[...], g1_ref[...]).astype(BF16)
    cos = cos_ref[...]
    sin = sin_ref[...]
    lane = lax.broadcasted_iota(jnp.int32, cos.shape, 1)
    first_half = (lane % C_DK) < (C_DK // 2)

    def rope(seg):
        swapped = jnp.where(first_half, pltpu.roll(seg, LANES - C_DK // 2, 1), pltpu.roll(seg, C_DK // 2, 1))
        return seg * cos + swapped * sin

    qk = jnp.dot(h, w_ref[:, 0:2 * C_QK], preferred_element_type=F32)
    for j in range(2 * C_QK // LANES):
        seg = rope(qk[:, j * LANES:(j + 1) * LANES])
        if j < C_QK // LANES:
            q_ref[:, j * LANES:(j + 1) * LANES] = seg
        else:
            jj = j - C_QK // LANES
            k_ref[:, jj * LANES:(jj + 1) * LANES] = seg * (C_DK ** -0.5)
    v_ref[...] = jnp.dot(h, w_ref[:, 2 * C_QK:2 * C_QK + C_V], preferred_element_type=F32).astype(v_ref.dtype)
    z_ref[...] = jnp.dot(h, w_ref[:, 2 * C_QK + C_V:2 * C_QK + 2 * C_V],
                         preferred_element_type=F32).astype(z_ref.dtype)


def _proj_c(x, g1, w, cos, sin):
    n = x.shape[0]
    tm = 256
    tab_blocks = cos.shape[0] // tm
    row = lambda i: (i, 0)
    fixed = lambda i: (0, 0)
    tab = (lambda i: (i % tab_blocks, 0)) if tab_blocks > 1 else fixed
    return pl.pallas_call(
        _proj_c_kernel,
        grid=(n // tm,),
        in_specs=[
            pl.BlockSpec((tm, D_MODEL), row),
            pl.BlockSpec((1, D_MODEL), fixed),
            pl.BlockSpec((D_MODEL, 2 * C_QK + 2 * C_V), fixed),
            pl.BlockSpec((tm, LANES), tab),
            pl.BlockSpec((tm, LANES), tab),
        ],
        out_specs=[pl.BlockSpec((tm, C_QK), row), pl.BlockSpec((tm, C_QK), row),
                   pl.BlockSpec((tm, C_V), row), pl.BlockSpec((tm, C_V), row)],
        out_shape=[jax.ShapeDtypeStruct((n, C_QK), F32), jax.ShapeDtypeStruct((n, C_QK), F32),
                   jax.ShapeDtypeStruct((n, C_V), BF16), jax.ShapeDtypeStruct((n, C_V), BF16)],
        compiler_params=_cparams(("parallel",)),
        name="proj_c",
    )(x, g1, w, cos, sin)


def _log_gamma(hh):
    return math.log1p(-(2.0 ** (-5.0 - hh)))


def _c_scan_kernel(q_ref, k_ref, v_ref, z_ref, r0_ref, gout_ref, o_ref, r_ref, *, nb, t_valid):
    c = pl.program_id(1)
    rows = ROW_BLOCK

    @pl.when(c == 0)
    def _():
        r_ref[...] = r0_ref[...]

    left = jnp.clip(t_valid - c * rows, 0, rows)
    ri = lax.broadcasted_iota(jnp.int32, (rows, rows), 0)
    ci = lax.broadcasted_iota(jnp.int32, (rows, rows), 1)
    cnt_i = jnp.minimum(ri + 1, left).astype(F32)
    cnt_j = jnp.minimum(ci + 1, left).astype(F32)
    incl = ri >= ci
    steps = jnp.where(incl, cnt_i - cnt_j, 0.0)
    cnt_col = cnt_i[:, 0:1]
    left_f = left.astype(F32)
    qk_lane = lax.broadcasted_iota(jnp.int32, (1, C_QK), 1) // C_DK
    lg_lane = jnp.zeros((1, C_QK), F32)
    for hh in range(C_HEADS):
        lg_lane = jnp.where(qk_lane == hh, _log_gamma(hh), lg_lane)
    qk_sub = lax.broadcasted_iota(jnp.int32, (C_QK, 1), 0) // C_DK
    lg_sub = jnp.zeros((C_QK, 1), F32)
    for hh in range(C_HEADS):
        lg_sub = jnp.where(qk_sub == hh, _log_gamma(hh), lg_sub)
    q_scale = jnp.exp(cnt_col * lg_lane)
    k_scale = jnp.exp((left_f - cnt_col) * lg_lane)
    r_scale = jnp.exp(left_f * lg_sub)
    row_ok = (lax.broadcasted_iota(jnp.int32, (rows, 1), 0) + c * rows) < t_valid
    diag = (lax.broadcasted_iota(jnp.int32, (C_QK, C_V), 0) // C_DK) == (
        lax.broadcasted_iota(jnp.int32, (C_QK, C_V), 1) // C_DV)

    head = lambda hh: slice(hh * C_DV, (hh + 1) * C_DV)
    decays = [jnp.where(incl, jnp.exp(steps * _log_gamma(hh)), 0.0) for hh in range(C_HEADS)]
    qs = [q_ref[b] for b in range(nb)]
    ks = [jnp.where(row_ok, k_ref[b], 0.0) for b in range(nb)]
    vs = [v_ref[b].astype(BF16) for b in range(nb)]
    rs = [r_ref[b] for b in range(nb)]
    inters = [_dot(q * q_scale, r) for q, r in zip(qs, rs)]
    units = [(b, hh) for b in range(nb) for hh in range(C_HEADS)]
    atts = [decays[hh] * _dot_nt(qs[b], jnp.where(qk_lane == hh, ks[b], 0.0)) for b, hh in units]
    outs = [inters[b][:, head(hh)] + _dot(att, vs[b][:, head(hh)]) for (b, hh), att in zip(units, atts)]
    for b in range(nb):
        r_ref[b] = rs[b] * r_scale + jnp.where(diag, _dot_tn(ks[b] * k_scale, vs[b]), 0.0)
    for (b, hh), o in zip(units, outs):
        gate = _silu(z_ref[b, :, head(hh)].astype(F32))
        o_ref[b, :, head(hh)] = (_rms(o, gout_ref[...]) * gate).astype(o_ref.dtype)


def _c_scan(q, k, v, z, r0, gout, batch, seq_len, t_valid):
    nb = 4
    nblk = seq_len // ROW_BLOCK
    v3 = lambda a: a.reshape(batch, seq_len, a.shape[-1])
    rows = lambda bi, c: (bi, c, 0)
    state = lambda bi, c: (bi, 0, 0)
    o, r_new = pl.pallas_call(
        functools.partial(_c_scan_kernel, nb=nb, t_valid=t_valid),
        grid=(batch // nb, nblk),
        in_specs=[pl.BlockSpec((nb, ROW_BLOCK, C_QK), rows), pl.BlockSpec((nb, ROW_BLOCK, C_QK), rows),
                  pl.BlockSpec((nb, ROW_BLOCK, C_V), rows), pl.BlockSpec((nb, ROW_BLOCK, C_V), rows),
                  pl.BlockSpec((nb, C_QK, C_V), state), pl.BlockSpec((1, C_DV), lambda bi, c: (0, 0))],
        out_specs=[pl.BlockSpec((nb, ROW_BLOCK, C_V), rows), pl.BlockSpec((nb, C_QK, C_V), state)],
        out_shape=[jax.ShapeDtypeStruct((batch, seq_len, C_V), BF16),
                   jax.ShapeDtypeStruct((batch, C_QK, C_V), F32)],
        compiler_params=_cparams(("parallel", "arbitrary")),
        name="c_scan",
    )(v3(q), v3(k), v3(v), v3(z), r0, gout)
    return o.reshape(batch * seq_len, C_V), r_new


def _merge_kernel(x_ref, g1_ref, wg_ref, o0_ref, o1_ref, o2_ref, l0_ref, l1_ref, l2_ref, ob_ref, oc_ref,
                  wa_ref, wb_ref, wc_ref, wo_ref, y_ref, *scr, residue_major):
    x = x_ref[...]
    tm = x.shape[0]
    h = _rms(x, g1_ref[...]).astype(BF16)
    lses = [r[...].reshape(tm, LANES) for r in (l0_ref, l1_ref, l2_ref)]
    outs = [r[...].reshape(tm, A_WIDTH) for r in (o0_ref, o1_ref, o2_ref)]
    heads = []
    for hh in range(A_HEADS):
        sl = slice(hh * A_HD, (hh + 1) * A_HD)
        ls = [l[:, 32 * hh:32 * hh + 1] for l in lses]
        m = jnp.maximum(jnp.maximum(ls[0], ls[1]), ls[2])
        es = [jnp.exp(l - m) for l in ls]
        tot = es[0] + es[1] + es[2]
        acc = (es[0] / tot) * outs[0][:, sl].astype(F32)
        acc = acc + (es[1] / tot) * outs[1][:, sl].astype(F32)
        acc = acc + (es[2] / tot) * outs[2][:, sl].astype(F32)
        heads.append(acc)
    o_a = jnp.concatenate(heads, axis=1)
    if residue_major:
        o_a = _swap_row_grid(scr[0], o_a)
    o_a = o_a.astype(BF16)
    merged = None
    for gi, (o_g, w_ref) in enumerate(((o_a, wa_ref), (ob_ref[...], wb_ref), (oc_ref[...], wc_ref))):
        gate = jax.nn.sigmoid(jnp.dot(h, wg_ref[:, gi * D_MODEL:(gi + 1) * D_MODEL], preferred_element_type=F32))
        term = gate * jnp.dot(o_g, w_ref[...], preferred_element_type=F32)
        merged = term if merged is None else merged + term
    y_ref[...] = x + jnp.dot(merged.astype(BF16), wo_ref[...], preferred_element_type=F32)


def _merge(x, g1, wg, o_groups, lses, o_b, o_c, wa, wb, wc, wo, seq_len, residue_major):
    n = x.shape[0]
    tm = RES * RES
    row = lambda i: (i, 0)
    fixed = lambda i: (0, 0)
    half = pl.BlockSpec((tm, A_WIDTH), row)
    wbr = pl.BlockSpec((A_WIDTH, D_MODEL), fixed)
    if residue_major:
        tiles = seq_len // tm
        grp = lambda i: (i // tiles, 0, i % tiles, 0)
        o_spec = pl.BlockSpec((None, RES, tm // RES, A_WIDTH), grp)
        lse = pl.BlockSpec((None, RES, tm // RES, LANES), grp)
        scratch = [pltpu.VMEM((A_WIDTH // LANES, tm, LANES), F32)]
    else:
        o_spec = half
        lse = pl.BlockSpec((tm, LANES), row)
        scratch = []
    return pl.pallas_call(
        functools.partial(_merge_kernel, residue_major=residue_major),
        grid=(n // tm,),
        in_specs=[pl.BlockSpec((tm, D_MODEL), row), pl.BlockSpec((1, D_MODEL), fixed),
                  pl.BlockSpec((D_MODEL, 3 * D_MODEL), fixed),
                  o_spec, o_spec, o_spec, lse, lse, lse, half, half, wbr, wbr, wbr,
                  pl.BlockSpec((D_MODEL, D_MODEL), fixed)],
        out_specs=pl.BlockSpec((tm, D_MODEL), row),
        out_shape=jax.ShapeDtypeStruct((n, D_MODEL), F32),
        scratch_shapes=scratch,
        compiler_params=_cparams(("parallel",)),
        name="merge",
    )(x, g1, wg, *o_groups, *lses, o_b, o_c, wa, wb, wc, wo)


def _ffn_kernel(x_ref, g2_ref, wg_ref, wu_ref, wo_ref, y_ref, h_scr, acc_scr):
    j = pl.program_id(1)

    @pl.when(j == 0)
    def _():
        h_scr[...] = _rms(x_ref[...], g2_ref[...]).astype(BF16)
        acc_scr[...] = jnp.zeros_like(acc_scr)

    h = h_scr[...]
    gate = jnp.dot(h, wg_ref[...], preferred_element_type=F32)
    up = jnp.dot(h, wu_ref[...], preferred_element_type=F32)
    acc_scr[...] += jnp.dot((_silu(gate) * up).astype(BF16), wo_ref[...], preferred_element_type=F32)

    @pl.when(j == pl.num_programs(1) - 1)
    def _():
        y_ref[...] = x_ref[...] + acc_scr[...]


def _ffn(x, g2, w_in, w_out):
    n = x.shape[0]
    tm = min(n, 1024)
    tf = 256
    nf = D_FF // tf
    row = lambda i, j: (i, 0)
    return pl.pallas_call(
        _ffn_kernel,
        grid=(n // tm, nf),
        in_specs=[pl.BlockSpec((tm, D_MODEL), row), pl.BlockSpec((1, D_MODEL), lambda i, j: (0, 0)),
                  pl.BlockSpec((D_MODEL, tf), lambda i, j: (0, j)),
                  pl.BlockSpec((D_MODEL, tf), lambda i, j: (0, nf + j)),
                  pl.BlockSpec((tf, D_MODEL), lambda i, j: (j, 0))],
        out_specs=pl.BlockSpec((tm, D_MODEL), row),
        out_shape=jax.ShapeDtypeStruct((n, D_MODEL), F32),
        scratch_shapes=[pltpu.VMEM((tm, D_MODEL), BF16), pltpu.VMEM((tm, D_MODEL), F32)],
        compiler_params=_cparams(("parallel", "arbitrary")),
        name="ffn",
    )(x, g2, w_in, w_in, w_out)


def _rope_tables(pos, hd, reps):
    inv = ROPE_THETA ** (-jnp.arange(0, hd, 2, dtype=F32) / hd)
    ang = pos.astype(F32)[:, None] * inv[None, :]
    cos = jnp.cos(ang)
    sin = jnp.sin(ang)
    cos2 = jnp.concatenate([cos, cos], axis=1)
    sin2 = jnp.concatenate([-sin, sin], axis=1)
    return jnp.tile(cos2, (1, reps)), jnp.tile(sin2, (1, reps))


def _pad_rows(a, batch, t, t_pad):
    if t == t_pad:
        return a
    a = a.reshape(batch, t, a.shape[-1])
    a = jnp.pad(a, ((0, 0), (0, t_pad - t), (0, 0)))
    return a.reshape(batch * t_pad, a.shape[-1])


def _unpad_rows(a, batch, t, t_pad):
    if t == t_pad:
        return a
    return a.reshape(batch, t_pad, a.shape[-1])[:, :t].reshape(batch * t, a.shape[-1])


def _layer(x, pos, batch, t, lw, caches, layer, conv_state, s0, r0):
    n = batch * t
    prompt = caches is None
    reps = max(1, 256 // t)
    cos_a, sin_a = _rope_tables(pos, A_HD, 1)
    cos_c, sin_c = _rope_tables(pos, C_DK, LANES // C_DK)
    if reps > 1:
        cos_a, sin_a, cos_c, sin_c = (jnp.tile(a, (reps, 1)) for a in (cos_a, sin_a, cos_c, sin_c))

    if prompt:
        to_rm = lambda a: a.reshape(t // RES, RES, A_HD).transpose(1, 0, 2)
        q, k, v, k_tail, v_tail = _proj_a(x, lw["g1"], lw["w_a"], lw["qn"], lw["kn"], to_rm(cos_a), to_rm(sin_a),
                                          batch, t, True)
    else:
        q, k, v = _proj_a(x, lw["g1"], lw["w_a"], lw["qn"], lw["kn"], cos_a, sin_a, batch, t, False)
        k_tail, v_tail = k, v
    outs, lses = [], []
    for gi in range(N_GROUPS):
        if prompt:
            o, lse = _attn_prompt(q, k, v, gi, batch, t)
        else:
            o, lse = _attn_sample(q, k, v, caches[gi], layer, gi, batch, t)
        outs.append(o)
        lses.append(lse)
    new_kv = []
    for gi, (win, _) in enumerate(A_GROUPS):
        cols = slice(gi * A_WIDTH, (gi + 1) * A_WIDTH)
        if prompt:
            keep = min(win, t)
            first = k_tail.shape[2] - keep // RES
            tail = lambda a: a[:, :, first:, cols].transpose(0, 2, 1, 3).reshape(batch, keep, A_HEADS, A_HD)
        else:
            tail = lambda a: a[:, cols].reshape(batch, t, A_HEADS, A_HD)
        new_kv.append(jnp.stack([tail(k_tail), tail(v_tail)], axis=2))

    t_pad = -(-t // ROW_BLOCK) * ROW_BLOCK
    p, z_b, bac, bar = _proj_b(x, lw["g1"], lw["w_bqkv"], lw["w_bz"], lw["w_ba"], lw["w_bat"])
    conv_new = jnp.concatenate([conv_state, p.reshape(batch, t, B_CONV_CH)], axis=1)[:, -(B_CONV - 1):]
    cst = jnp.pad(conv_state, ((0, 0), (SUBLANES - (B_CONV - 1), 0), (0, 0)))
    bar_p = _pad_rows(bar.T, batch, t, t_pad).T if t_pad != t else bar
    qg, kd, u, w, attn, egl = _b_prep(
        _pad_rows(p, batch, t, t_pad), cst, _pad_rows(bac, batch, t, t_pad), bar_p, lw["conv_w"],
        lw["alog_r"], lw["dt_r"], lw["alog_c"], lw["dt_c"], t_pad, t)
    o_b, s_new = _b_scan(qg, kd, u, w, attn, egl, _pad_rows(z_b, batch, t, t_pad), s0, lw["gb"], batch, t_pad)
    o_b = _unpad_rows(o_b, batch, t, t_pad)

    cq, ck, cv, cz = _proj_c(x, lw["g1"], lw["w_c"], cos_c, sin_c)
    r_bd = jnp.zeros((batch, C_HEADS, C_DK, C_HEADS, C_DV), F32)
    for hh in range(C_HEADS):
        r_bd = r_bd.at[:, hh, :, hh, :].set(r0[:, hh])
    o_c, r_new = _c_scan(*(_pad_rows(a, batch, t, t_pad) for a in (cq, ck, cv, cz)),
                         r_bd.reshape(batch, C_QK, C_V), lw["gc"], batch, t_pad, t)
    o_c = _unpad_rows(o_c, batch, t, t_pad)
    r_new = r_new.reshape(batch, C_HEADS, C_DK, C_HEADS, C_DV)
    r_new = jnp.stack([r_new[:, hh, :, hh, :] for hh in range(C_HEADS)], axis=1)

    x = _merge(x, lw["g1"], lw["w_g"], outs, lses, o_b, o_c, lw["w_oa"], lw["w_ob"], lw["w_oc"], lw["w_o"], t, prompt)
    x = _ffn(x, lw["g2"], lw["w_fi"], lw["w_fo"])
    return x, new_kv, conv_new, s_new, r_new


def _layer_weights(l, norm1_g, w_in, a_q_norm_g, a_k_norm_g, b_conv_w, b_a_log, b_dt_bias, b_out_norm_g,
                   c_out_norm_g, w_out_a, w_out_b, w_out_c, w_out, norm2_g, w_ffn_in, w_ffn_out):
    o = IN_OFFS
    wl = w_in[l]
    w_ba = wl[:, o[3]:o[5]]
    pad_r = lambda a: jnp.pad(a.reshape(1, B_HEADS), ((0, 0), (B_HEADS, LANES - 2 * B_HEADS)))
    pad_c = lambda a: jnp.pad(a.reshape(B_HEADS, 1), ((B_HEADS, 2 * SUBLANES - 2 * B_HEADS), (0, 0)))
    return dict(
        g1=norm1_g[l].reshape(1, D_MODEL), g2=norm2_g[l].reshape(1, D_MODEL),
        w_a=wl[:, o[0]:o[1]].astype(BF16),
        w_bqkv=wl[:, o[1]:o[2]].astype(BF16), w_bz=wl[:, o[2]:o[3]].astype(BF16),
        w_ba=jnp.pad(w_ba, ((0, 0), (0, LANES - 2 * B_HEADS))).astype(BF16),
        w_bat=jnp.pad(w_ba.T, ((0, 2 * SUBLANES - 2 * B_HEADS), (0, 0))).astype(BF16),
        w_c=wl[:, o[5]:o[9]].astype(BF16), w_g=wl[:, o[9]:o[10]].astype(BF16),
        qn=a_q_norm_g[l].reshape(1, A_HD), kn=a_k_norm_g[l].reshape(1, A_HD),
        conv_w=b_conv_w[l],
        alog_r=pad_r(b_a_log[l]), dt_r=pad_r(b_dt_bias[l]), alog_c=pad_c(b_a_log[l]), dt_c=pad_c(b_dt_bias[l]),
        gb=b_out_norm_g[l].reshape(1, B_DV), gc=c_out_norm_g[l].reshape(1, C_DV),
        w_oa=w_out_a[l].astype(BF16), w_ob=w_out_b[l].astype(BF16), w_oc=w_out_c[l].astype(BF16),
        w_o=w_out[l].astype(BF16), w_fi=w_ffn_in[l].astype(BF16), w_fo=w_ffn_out[l].astype(BF16),
    )


def kernel(x_prompt, x_sample, cache_a_kv0, cache_a_kv1, cache_a_kv2, state_b_conv, state_b_S, state_c_R, norm1_g, w_in, a_q_norm_g, a_k_norm_g, b_conv_w, b_a_log, b_dt_bias, b_out_norm_g, c_out_norm_g, w_out_a, w_out_b, w_out_c, w_out, norm2_g, w_ffn_in, w_ffn_out):
    bp, t = x_prompt.shape[:2]
    bs, s = x_sample.shape[:2]
    depth = w_in.shape[0]
    pos_p = jnp.arange(t)
    pos_s = PAST_LEN + jnp.arange(s)
    yp = x_prompt.reshape(bp * t, D_MODEL)
    ys = x_sample.reshape(bs * s, D_MODEL)
    caches = (cache_a_kv0, cache_a_kv1, cache_a_kv2)
    zeros_conv = jnp.zeros((bp, B_CONV - 1, B_CONV_CH), F32)
    zeros_s = jnp.zeros((bp, B_HEADS, B_DK, B_DV), F32)
    zeros_r = jnp.zeros((bp, C_HEADS, C_DK, C_DV), F32)
    acc = [[] for _ in range(12)]
    for l in range(depth):
        lw = _layer_weights(l, norm1_g, w_in, a_q_norm_g, a_k_norm_g, b_conv_w, b_a_log, b_dt_bias,
                            b_out_norm_g, c_out_norm_g, w_out_a, w_out_b, w_out_c, w_out, norm2_g,
                            w_ffn_in, w_ffn_out)
        yp, kv, cv, sn, rn = _layer(yp, pos_p, bp, t, lw, None, l, zeros_conv, zeros_s, zeros_r)
        for i, a in enumerate((kv[0], kv[1], kv[2], cv, sn, rn)):
            acc[i].append(a)
        ys, kv, cv, sn, rn = _layer(ys, pos_s, bs, s, lw, caches, l, state_b_conv[l], state_b_S[l], state_c_R[l])
        for i, a in enumerate((kv[0], kv[1], kv[2], cv, sn, rn)):
            acc[6 + i].append(a)
    return (yp.reshape(bp, t, D_MODEL), ys.reshape(bs, s, D_MODEL)) + tuple(jnp.stack(a) for a in acc)
```

```python
import functools
import math

import jax
import jax.numpy as jnp
import numpy as np
from jax import lax
from jax.experimental import pallas as pl
from jax.experimental.pallas import tpu as pltpu

F32 = jnp.float32
BF16 = jnp.bfloat16

D_MODEL = 1024
PAST_LEN = 8192
A_GROUPS = ((128, 1), (512, 4), (2048, 16))
N_GROUPS = 3
A_HEADS = 4
A_HD = 128
A_WIDTH = A_HEADS * A_HD
A_KEYS = 128
B_HEADS = 4
B_DK = 128
B_DV = 128
B_CONV = 4
B_QK = B_HEADS * B_DK
B_V = B_HEADS * B_DV
B_CONV_CH = 2 * B_QK + B_V
C_HEADS = 4
C_DK = 64
C_DV = 128
C_QK = C_HEADS * C_DK
C_V = C_HEADS * C_DV
CHUNK = 64
ROPE_THETA = 10000.0
EPS = 1e-6
D_FF = 2816
IN_SIZES = (3 * N_GROUPS * A_WIDTH, B_CONV_CH, B_V, B_HEADS, B_HEADS, C_QK, C_QK, C_V, C_V, 3 * D_MODEL)
IN_OFFS = tuple(int(v) for v in np.cumsum((0,) + IN_SIZES))

W_COLS = {"a": 0, "b_qkv": 4608, "gates": 6144, "c": 9216, "b_z": 10752, "b_ba": 11264}
W_ALL = 11392

ROW_BLOCK = 128
RES = 16
SUBLANES = 8
LANES = 128
VMEM_LIMIT = 48 * 1024 * 1024


def _cparams(sem):
    return pltpu.CompilerParams(dimension_semantics=sem, vmem_limit_bytes=VMEM_LIMIT)


def _rms(x, g):
    return x * lax.rsqrt(jnp.mean(x * x, axis=-1, keepdims=True) + EPS) * g


def _silu(x):
    return x * jax.nn.sigmoid(x)


def _softplus(x):
    return jnp.maximum(x, 0.0) + jnp.log(1.0 + jnp.exp(-jnp.abs(x)))


def _dot(a, b):
    return jnp.dot(a.astype(BF16), b.astype(BF16), preferred_element_type=F32)


def _dot_nt(a, b):
    return lax.dot_general(a.astype(BF16), b.astype(BF16), (((1,), (1,)), ((), ())), preferred_element_type=F32)


def _dot_tn(a, b):
    return lax.dot_general(a.astype(BF16), b.astype(BF16), (((0,), (0,)), ((), ())), preferred_element_type=F32)


def _swap_row_grid(scr, val):
    slabs = val.shape[1] // LANES
    for c in range(slabs):
        scr[c] = val[:, c * LANES:(c + 1) * LANES]
    cols = [jnp.concatenate([scr[c, pl.ds(r, RES, stride=RES), :] for r in range(RES)], axis=0)
            for c in range(slabs)]
    return jnp.concatenate(cols, axis=1)


def _proj_a_kernel(x_ref, g1_ref, w_ref, qg_ref, kg_ref, cos_ref, sin_ref, q_ref, k_ref, v_ref, *rest, residue_major):
    x = x_ref[...]
    tm = x.shape[0]
    if residue_major:
        kt_ref, vt_ref, scr = rest
        x = _swap_row_grid(scr, x)
    h = _rms(x, g1_ref[...]).astype(BF16)
    cos = cos_ref[...].reshape(tm, A_HD)
    sin = sin_ref[...].reshape(tm, A_HD)

    def norm_rope(seg, g):
        y = _rms(seg, g)
        return y * cos + pltpu.roll(y, A_HD // 2, 1) * sin

    def put(ref, col, val):
        if residue_major:
            ref[:, :, col:col + val.shape[1]] = val.reshape(RES, tm // RES, val.shape[1]).astype(ref.dtype)
        else:
            ref[:, col:col + val.shape[1]] = val

    for j in range(3 * N_GROUPS):
        acc = jnp.dot(h, w_ref[:, j * A_WIDTH:(j + 1) * A_WIDTH], preferred_element_type=F32)
        if j < N_GROUPS:
            for hh in range(A_HEADS):
                sl = slice(hh * A_HD, (hh + 1) * A_HD)
                put(q_ref, j * A_WIDTH + hh * A_HD, norm_rope(acc[:, sl], qg_ref[...]) * (A_HD ** -0.5))
        elif j < 2 * N_GROUPS:
            jj = j - N_GROUPS
            for hh in range(A_HEADS):
                sl = slice(hh * A_HD, (hh + 1) * A_HD)
                val = norm_rope(acc[:, sl], kg_ref[...])
                put(k_ref, jj * A_WIDTH + hh * A_HD, val)
                if residue_major:
                    put(kt_ref, jj * A_WIDTH + hh * A_HD, val)
        else:
            jj = j - 2 * N_GROUPS
            put(v_ref, jj * A_WIDTH, acc)
            if residue_major:
                put(vt_ref, jj * A_WIDTH, acc)


def _proj_a(x, g1, w, qg, kg, cos, sin, batch, seq_len, residue_major):
    n = x.shape[0]
    tm = RES * RES
    nw = N_GROUPS * A_WIDTH
    fixed = lambda i: (0, 0)
    common = [pl.BlockSpec((1, D_MODEL), fixed), pl.BlockSpec((D_MODEL, 3 * nw), fixed),
              pl.BlockSpec((1, A_HD), fixed), pl.BlockSpec((1, A_HD), fixed)]
    if residue_major:
        tiles = seq_len // tm
        tail_tiles = min(max(wd for wd, _ in A_GROUPS), seq_len) // tm
        blk = (None, RES, tm // RES, nw)
        tab = pl.BlockSpec((RES, tm // RES, A_HD), lambda i: (0, i % tiles, 0))
        main = pl.BlockSpec(blk, lambda i: (i // tiles, 0, i % tiles, 0))
        tail = pl.BlockSpec(blk, lambda i: (i // tiles, 0, jnp.maximum(i % tiles - (tiles - tail_tiles), 0), 0))
        out_specs = [main] * 3 + [tail] * 2
        out_shape = ([jax.ShapeDtypeStruct((batch, RES, seq_len // RES, nw), BF16)] * 3
                     + [jax.ShapeDtypeStruct((batch, RES, tail_tiles * tm // RES, nw), F32)] * 2)
        scratch = [pltpu.VMEM((D_MODEL // LANES, tm, LANES), F32)]
    else:
        assert cos.shape[0] == tm
        tab = pl.BlockSpec((tm, A_HD), fixed)
        out_specs = [pl.BlockSpec((tm, nw), lambda i: (i, 0))] * 3
        out_shape = [jax.ShapeDtypeStruct((n, nw), F32)] * 3
        scratch = []
    return pl.pallas_call(
        functools.partial(_proj_a_kernel, residue_major=residue_major),
        grid=(n // tm,),
        in_specs=[pl.BlockSpec((tm, D_MODEL), lambda i: (i, 0))] + common + [tab, tab],
        out_specs=out_specs,
        out_shape=out_shape,
        scratch_shapes=scratch,
        compiler_params=_cparams(("arbitrary",)),
        name="proj_a",
    )(x, g1, w, qg, kg, cos, sin)


ATTN_SUBS = 2


def _attn_prompt_kernel(q_ref, kc_ref, kp_ref, vc_ref, vp_ref, o_ref, lse_ref, *, parts):
    n = pl.program_id(2)
    per = ROW_BLOCK // parts
    qi = lax.broadcasted_iota(jnp.int32, (ROW_BLOCK, ROW_BLOCK), 0)
    kj = lax.broadcasted_iota(jnp.int32, (ROW_BLOCK, ROW_BLOCK), 1)
    qi = parts * (qi % per) + qi // per
    kj = parts * (kj % per) + kj // per
    cur_ok = kj <= qi
    prev_ok = kj >= qi
    first_ok = jnp.logical_and(prev_ok, n > 0)
    lane = lax.broadcasted_iota(jnp.int32, (ROW_BLOCK, LANES), 1)
    neg = -jnp.inf
    packed_rows = 2 * SUBLANES

    def sub(ref, half, sl):
        full = ref[:, :, sl]
        if per % packed_rows == 0:
            return full[:, half * per:(half + 1) * per].reshape(ROW_BLOCK, A_HD)
        return full.astype(F32)[:, half * per:(half + 1) * per].reshape(ROW_BLOCK, A_HD).astype(BF16)

    units = [(half, hh) for half in range(ATTN_SUBS) for hh in range(A_HEADS)]
    head = lambda hh: slice(hh * A_HD, (hh + 1) * A_HD)

    keys = {(half, hh): sub(kc_ref, half, head(hh)) for half, hh in units}
    vals = {(half, hh): sub(vc_ref, half, head(hh)) for half, hh in units}
    for hh in range(A_HEADS):
        keys[(-1, hh)] = sub(kp_ref, ATTN_SUBS - 1, head(hh))
        vals[(-1, hh)] = sub(vp_ref, ATTN_SUBS - 1, head(hh))

    scores = []
    for half, hh in units:
        q = sub(q_ref, half, head(hh))
        s_cur = jnp.where(cur_ok, _dot_nt(q, keys[(half, hh)]), neg)
        s_prev = jnp.where(first_ok if half == 0 else prev_ok, _dot_nt(q, keys[(half - 1, hh)]), neg)
        scores.append((s_cur, s_prev))
    probs = []
    for s_cur, s_prev in scores:
        m = jnp.maximum(jnp.max(s_cur, axis=-1, keepdims=True), jnp.max(s_prev, axis=-1, keepdims=True))
        p_cur = jnp.exp(s_cur - m)
        p_prev = jnp.exp(s_prev - m)
        den = jnp.sum(p_cur, axis=-1, keepdims=True) + jnp.sum(p_prev, axis=-1, keepdims=True)
        probs.append((p_cur, p_prev, m, den))
    lse_blk = [jnp.zeros((ROW_BLOCK, LANES), F32) for _ in range(ATTN_SUBS)]
    outs = {}
    for (half, hh), (p_cur, p_prev, m, den) in zip(units, probs):
        o = (_dot(p_cur, vals[(half, hh)]) + _dot(p_prev, vals[(half - 1, hh)])) / den
        outs[(half, hh)] = o.reshape(parts, per, A_HD)
        lse_blk[half] = jnp.where(lane // 32 == hh, m + jnp.log(den), lse_blk[half])
    for hh in range(A_HEADS):
        both = jnp.concatenate([outs[(half, hh)] for half in range(ATTN_SUBS)], axis=1)
        o_ref[:, :, head(hh)] = both.astype(o_ref.dtype)
    for half in range(ATTN_SUBS):
        lse_ref[:, half * per:(half + 1) * per, :] = lse_blk[half].reshape(parts, per, LANES)


def _attn_prompt(q, k, v, gi, batch, seq_len):
    _, dil = A_GROUPS[gi]
    parts = RES // dil
    per = ROW_BLOCK // parts
    rows = seq_len // RES
    nblk = seq_len // dil // (ATTN_SUBS * ROW_BLOCK)
    split = lambda a: a.reshape(batch, parts, dil, rows, a.shape[-1])
    cur = lambda b, r, n: (b, 0, r, n, gi)
    prev = lambda b, r, n: (b, 0, r, jnp.maximum(n - 1, 0), gi)
    out = lambda b, r, n: (b, 0, r, n, 0)
    blk = (None, parts, None, ATTN_SUBS * per, A_WIDTH)
    o, lse = pl.pallas_call(
        functools.partial(_attn_prompt_kernel, parts=parts),
        grid=(batch, dil, nblk),
        in_specs=[pl.BlockSpec(blk, cur), pl.BlockSpec(blk, cur), pl.BlockSpec(blk, prev),
                  pl.BlockSpec(blk, cur), pl.BlockSpec(blk, prev)],
        out_specs=[pl.BlockSpec(blk, out), pl.BlockSpec((None, parts, None, ATTN_SUBS * per, LANES), out)],
        out_shape=[jax.ShapeDtypeStruct((batch, parts, dil, rows, A_WIDTH), BF16),
                   jax.ShapeDtypeStruct((batch, parts, dil, rows, LANES), F32)],
        compiler_params=_cparams(("parallel", "parallel", "arbitrary")),
        name=f"attn_prompt_g{gi}",
    )(split(q), split(k), split(k), split(v), split(v))
    return o.reshape(batch, RES, rows, A_WIDTH), lse.reshape(batch, RES, rows, LANES)


def _attn_sample_kernel(q_ref, kn_ref, vn_ref, cache_ref, o_ref, lse_ref, *, dil, n_new):
    n_res = min(dil, n_new)
    neg = -jnp.inf
    srow = lax.broadcasted_iota(jnp.int32, (n_new, A_KEYS), 0)
    mcol = lax.broadcasted_iota(jnp.int32, (n_new, A_KEYS), 1)
    in_window = mcol >= srow // dil
    row_res = [srow % dil == res for res in range(n_res)]
    srow_n = lax.broadcasted_iota(jnp.int32, (n_new, n_new), 0)
    tcol_n = lax.broadcasted_iota(jnp.int32, (n_new, n_new), 1)
    new_ok = jnp.logical_and(tcol_n <= srow_n, (srow_n - tcol_n) % dil == 0)
    heads = range(A_HEADS)

    qs = [q_ref[hh].astype(BF16) for hh in heads]
    keys = {(res, hh): cache_ref[:, res, 0, hh, :].astype(BF16) for res in range(n_res) for hh in heads}
    vals = {(res, hh): cache_ref[:, res, 1, hh, :].astype(BF16) for res in range(n_res) for hh in heads}
    raw = {key: _dot_nt(qs[key[1]], kmat) for key, kmat in keys.items()}
    s_new = [jnp.where(new_ok, _dot_nt(qs[hh], kn_ref[hh]), neg) for hh in heads]
    probs = []
    for hh in heads:
        s_buf = raw[(0, hh)]
        for res in range(1, n_res):
            s_buf = jnp.where(row_res[res], raw[(res, hh)], s_buf)
        s_buf = jnp.where(in_window, s_buf, neg)
        m = jnp.maximum(jnp.max(s_buf, axis=-1, keepdims=True), jnp.max(s_new[hh], axis=-1, keepdims=True))
        p_buf = jnp.exp(s_buf - m)
        p_new = jnp.exp(s_new[hh] - m)
        den = jnp.sum(p_buf, axis=-1, keepdims=True) + jnp.sum(p_new, axis=-1, keepdims=True)
        probs.append((p_buf, p_new, m, den))
    for hh in heads:
        p_buf, p_new, m, den = probs[hh]
        acc = _dot(p_new, vn_ref[hh])
        for res in range(n_res):
            p_res = p_buf if n_res == 1 else jnp.where(row_res[res], p_buf, 0.0)
            acc = acc + _dot(p_res, vals[(res, hh)])
        o_ref[hh] = acc / den
        lse_ref[hh] = jnp.broadcast_to(m + jnp.log(den), (n_new, LANES))


def _attn_sample_rows_kernel(q_ref, kn_ref, vn_ref, cache_ref, o_ref, lse_ref, *, dil, n_new):
    row = lax.broadcasted_iota(jnp.int32, (A_KEYS, A_HEADS, 1), 0)
    trow = lax.broadcasted_iota(jnp.int32, (n_new, A_HEADS, 1), 0)
    neg = -jnp.inf
    kn = kn_ref[...]
    vn = vn_ref[...]
    for s in range(n_new):
        res = s % dil
        first = s // dil
        q = q_ref[s][None]
        kc = cache_ref[:, res, 0]
        vc = cache_ref[:, res, 1]
        sc = jnp.sum(kc * q, axis=-1, keepdims=True)
        if first > 0:
            sc = jnp.where(row >= first, sc, neg)
        new_ok = jnp.logical_and(trow <= s, (s - trow) % dil == 0)
        sn = jnp.where(new_ok, jnp.sum(kn * q, axis=-1, keepdims=True), neg)
        m = jnp.maximum(jnp.max(sc, axis=0, keepdims=True), jnp.max(sn, axis=0, keepdims=True))
        pc = jnp.exp(sc - m)
        pn = jnp.exp(sn - m)
        den = jnp.sum(pc, axis=0, keepdims=True) + jnp.sum(pn, axis=0, keepdims=True)
        o = (jnp.sum(pc * vc, axis=0, keepdims=True) + jnp.sum(pn * vn, axis=0, keepdims=True)) / den
        o_ref[s] = o[0]
        lse_ref[s] = jnp.broadcast_to((m + jnp.log(den))[0], (A_HEADS, A_HD))


def _attn_sample(q, k, v, cache, layer, gi, batch, n_new):
    win, dil = A_GROUPS[gi]
    depth = cache.shape[0]
    assert cache.shape[2] == win and win // dil == A_KEYS
    n_res = min(dil, n_new)
    cv = cache.reshape(depth, batch, A_KEYS, dil, 2, A_HEADS, A_HD)
    cache_spec = pl.BlockSpec((None, None, A_KEYS, n_res, 2, A_HEADS, A_HD), lambda b: (layer, b, 0, 0, 0, 0, 0))
    if n_res > 1:
        heads = lambda a: a.reshape(batch, n_new, N_GROUPS, A_HEADS, A_HD)
        new = pl.BlockSpec((None, n_new, None, A_HEADS, A_HD), lambda b: (b, 0, gi, 0, 0))
        out = pl.BlockSpec((None, n_new, A_HEADS, A_HD), lambda b: (b, 0, 0, 0))
        o, lse = pl.pallas_call(
            functools.partial(_attn_sample_rows_kernel, dil=dil, n_new=n_new),
            grid=(batch,),
            in_specs=[new, new, new, cache_spec],
            out_specs=[out, out],
            out_shape=[jax.ShapeDtypeStruct((batch, n_new, A_HEADS, A_HD), F32)] * 2,
            compiler_params=_cparams(("parallel",)),
            name=f"attn_sample_g{gi}",
        )(heads(q), heads(k), heads(v), cv)
        lse = jnp.repeat(lse[..., 0], LANES // A_HEADS, axis=-1)
        return o.reshape(batch * n_new, A_WIDTH), lse.reshape(batch * n_new, LANES)
    heads = lambda a: a.reshape(batch, n_new, N_GROUPS, A_HEADS, A_HD).transpose(0, 2, 3, 1, 4)
    new = pl.BlockSpec((None, None, A_HEADS, n_new, A_HD), lambda b: (b, gi, 0, 0, 0))
    out = pl.BlockSpec((None, A_HEADS, n_new, LANES), lambda b: (b, 0, 0, 0))
    o, lse = pl.pallas_call(
        functools.partial(_attn_sample_kernel, dil=dil, n_new=n_new),
        grid=(batch,),
        in_specs=[new, new, new,
                  pl.BlockSpec((None, None, A_KEYS, n_res, 2, A_HEADS, A_HD), lambda b: (layer, b, 0, 0, 0, 0, 0))],
        out_specs=[out, out],
        out_shape=[jax.ShapeDtypeStruct((batch, A_HEADS, n_new, A_HD), F32)] * 2,
        compiler_params=_cparams(("parallel",)),
        name=f"attn_sample_g{gi}",
    )(heads(q), heads(k), heads(v), cv)
    o = o.transpose(0, 2, 1, 3)
    lse = jnp.repeat(lse[..., 0].transpose(0, 2, 1), LANES // A_HEADS, axis=-1)
    return o.reshape(batch * n_new, A_WIDTH), lse.reshape(batch * n_new, LANES)


def _proj_b_kernel(x_ref, g1_ref, wqkv_ref, wz_ref, wba_ref, wbat_ref, p_ref, z_ref, bac_ref, bar_ref):
    h = _rms(x_ref[...], g1_ref[...]).astype(BF16)
    for j in range(3):
        sl = slice(j * B_QK, (j + 1) * B_QK)
        p_ref[:, sl] = jnp.dot(h, wqkv_ref[:, sl], preferred_element_type=F32)
    z_ref[...] = jnp.dot(h, wz_ref[...], preferred_element_type=F32).astype(z_ref.dtype)
    bac_ref[...] = jnp.dot(h, wba_ref[...], preferred_element_type=F32)
    bar_ref[...] = lax.dot_general(wbat_ref[...], h, (((1,), (1,)), ((), ())), preferred_element_type=F32)


def _proj_b(x, g1, wqkv, wz, wba, wbat):
    n = x.shape[0]
    tm = 256
    row = lambda i: (i, 0)
    fixed = lambda i: (0, 0)
    return pl.pallas_call(
        _proj_b_kernel,
        grid=(n // tm,),
        in_specs=[
            pl.BlockSpec((tm, D_MODEL), row),
            pl.BlockSpec((1, D_MODEL), fixed),
            pl.BlockSpec((D_MODEL, B_CONV_CH), lambda i: (0, W_COLS["b_qkv"] // B_CONV_CH)),
            pl.BlockSpec((D_MODEL, B_V), lambda i: (0, W_COLS["b_z"] // B_V)),
            pl.BlockSpec((D_MODEL, LANES), lambda i: (0, W_COLS["b_ba"] // LANES)),
            pl.BlockSpec((2 * SUBLANES, D_MODEL), fixed),
        ],
        out_specs=[pl.BlockSpec((tm, B_CONV_CH), row), pl.BlockSpec((tm, B_V), row),
                   pl.BlockSpec((tm, LANES), row), pl.BlockSpec((2 * SUBLANES, tm), lambda i: (0, i))],
        out_shape=[jax.ShapeDtypeStruct((n, B_CONV_CH), F32), jax.ShapeDtypeStruct((n, B_V), BF16),
                   jax.ShapeDtypeStruct((n, LANES), F32), jax.ShapeDtypeStruct((2 * SUBLANES, n), F32)],
        compiler_params=_cparams(("parallel",)),
        name="proj_b",
    )(x, g1, wqkv, wz, wba, wbat)


def _b_prep_kernel(p_ref, halo_ref, cst_ref, bac_ref, bar_ref, cw_ref, alog_r_ref, dt_r_ref, alog_c_ref, dt_c_ref,
                   qg_ref, kd_ref, u_ref, w_ref, attn_ref, egl_ref, e_scr, *, rows, blocks_per_seq, t_valid):
    i = pl.program_id(0)
    blk = i % blocks_per_seq

    before = jnp.where(blk == 0, cst_ref[...], halo_ref[...])
    e_scr[0:SUBLANES, :] = before
    e_scr[SUBLANES:SUBLANES + rows, :] = p_ref[...]
    xc = e_scr[SUBLANES:SUBLANES + rows, :] * cw_ref[B_CONV - 1:B_CONV, :]
    for kk in range(1, B_CONV):
        xc = xc + e_scr[SUBLANES - kk:SUBLANES - kk + rows, :] * cw_ref[B_CONV - 1 - kk:B_CONV - kk, :]
    act = _silu(xc)

    ri = lax.broadcasted_iota(jnp.int32, (rows, LANES), 0)
    li16 = lax.broadcasted_iota(jnp.int32, (2 * SUBLANES, rows), 1)
    li1 = lax.broadcasted_iota(jnp.int32, (1, LANES), 1)
    masked = t_valid < blocks_per_seq * rows
    if masked:
        row_ok = (blk * rows + ri) < t_valid
        col_ok = (blk * rows + li16) < t_valid
        act = jnp.where(ri[:, 0:1] + blk * rows < t_valid, act, 0.0)

    head_lane = jnp.logical_and(li1 >= B_HEADS, li1 < 2 * B_HEADS)
    a_r = jnp.where(head_lane, -jnp.exp(alog_r_ref[...]), 0.0)
    g_col = a_r * _softplus(bac_ref[...] + dt_r_ref[...])
    si = lax.broadcasted_iota(jnp.int32, (2 * SUBLANES, 1), 0)
    head_sub = jnp.logical_and(si >= B_HEADS, si < 2 * B_HEADS)
    a_c = jnp.where(head_sub, -jnp.exp(alog_c_ref[...]), 0.0)
    g_row = a_c * _softplus(bar_ref[...] + dt_c_ref[...])
    if masked:
        g_col = jnp.where(row_ok, g_col, 0.0)
        g_row = jnp.where(col_ok, g_row, 0.0)

    rpos = ri % CHUNK
    lpos = li16 % CHUNK
    gc = g_col
    rev = g_col
    gcr = g_row
    step = 1
    while step < CHUNK:
        gc = gc + jnp.where(rpos >= step, pltpu.roll(gc, step, 0), 0.0)
        rev = rev + jnp.where(rpos < CHUNK - step, pltpu.roll(rev, rows - step, 0), 0.0)
        gcr = gcr + jnp.where(lpos >= step, pltpu.roll(gcr, step, 1), 0.0)
        step *= 2
    rev = rev - g_col
    egl_ref[...] = jnp.exp(gc + rev)

    bi = lax.broadcasted_iota(jnp.int32, (ROW_BLOCK, ROW_BLOCK), 0)
    bj = lax.broadcasted_iota(jnp.int32, (ROW_BLOCK, ROW_BLOCK), 1)
    same = (bi // CHUNK) == (bj // CHUNK)
    incl = jnp.logical_and(same, bi >= bj)
    strict = jnp.logical_and(same, bi > bj)
    eye = (bi == bj).astype(F32)

    units = [(sb, hh) for sb in range(rows // ROW_BLOCK) for hh in range(B_HEADS)]
    lows, rhss = [], []
    for sb, hh in units:
        rs = slice(sb * ROW_BLOCK, (sb + 1) * ROW_BLOCK)
        sl = slice(hh * B_DK, (hh + 1) * B_DK)
        gc_c = gc[rs, B_HEADS + hh:B_HEADS + hh + 1]
        gc_r = gcr[B_HEADS + hh:B_HEADS + hh + 1, rs]
        dec = jnp.where(incl, jnp.exp(jnp.where(incl, gc_c - gc_r, 0.0)), 0.0)
        q = act[rs, sl]
        q = q * lax.rsqrt(jnp.sum(q * q, axis=-1, keepdims=True) + EPS) * (B_DK ** -0.5)
        k = act[rs, B_QK + hh * B_DK:B_QK + (hh + 1) * B_DK]
        k = k * lax.rsqrt(jnp.sum(k * k, axis=-1, keepdims=True) + EPS)
        v = act[rs, 2 * B_QK + hh * B_DV:2 * B_QK + (hh + 1) * B_DV]
        beta = jax.nn.sigmoid(bac_ref[rs, hh:hh + 1])
        kb = k * beta
        kbf = k.astype(BF16)
        lows.append(jnp.where(strict, dec * _dot_nt(kb, kbf), 0.0))
        attn_ref[rs, sl] = (dec * _dot_nt(q, kbf)).astype(attn_ref.dtype)
        rhss.append(jnp.concatenate([v * beta, kb * jnp.exp(gc_c)], axis=1).astype(BF16))
        qg_ref[rs, sl] = (q * jnp.exp(gc_c)).astype(qg_ref.dtype)
        kd_ref[rs, sl] = (k * jnp.exp(rev[rs, B_HEADS + hh:B_HEADS + hh + 1])).astype(kd_ref.dtype)

    tinvs = [eye - low for low in lows]
    pws = lows
    sq = 2
    while sq < CHUNK:
        pws = [_dot(pw, pw) for pw in pws]
        tinvs = [tinv + _dot(tinv, pw) for tinv, pw in zip(tinvs, pws)]
        sq *= 2
    for (sb, hh), tinv, rhs in zip(units, tinvs, rhss):
        rs = slice(sb * ROW_BLOCK, (sb + 1) * ROW_BLOCK)
        sol = _dot(tinv, rhs)
        u_ref[rs, hh * B_DV:(hh + 1) * B_DV] = sol[:, :B_DV]
        w_ref[rs, hh * B_DK:(hh + 1) * B_DK] = sol[:, B_DV:].astype(w_ref.dtype)


def _b_prep(p, cstate, bac, bar, cw, alog_r, dt_r, alog_c, dt_c, seq_len, t_valid):
    n = p.shape[0]
    rows = min(seq_len, 2 * ROW_BLOCK)
    bps = seq_len // rows
    row = lambda i: (i, 0)
    fixed = lambda i: (0, 0)
    per_row = rows // SUBLANES
    wide = lambda dt: jax.ShapeDtypeStruct((n, B_V), dt)
    return pl.pallas_call(
        functools.partial(_b_prep_kernel, rows=rows, blocks_per_seq=bps, t_valid=t_valid),
        grid=(n // rows,),
        in_specs=[
            pl.BlockSpec((rows, B_CONV_CH), row),
            pl.BlockSpec((SUBLANES, B_CONV_CH), lambda i: (jnp.maximum(i * per_row - 1, 0), 0)),
            pl.BlockSpec((None, SUBLANES, B_CONV_CH), lambda i: (i // bps, 0, 0)),
            pl.BlockSpec((rows, LANES), row),
            pl.BlockSpec((2 * SUBLANES, rows), lambda i: (0, i)),
            pl.BlockSpec((B_CONV, B_CONV_CH), fixed),
            pl.BlockSpec((1, LANES), fixed),
            pl.BlockSpec((1, LANES), fixed),
            pl.BlockSpec((2 * SUBLANES, 1), fixed),
            pl.BlockSpec((2 * SUBLANES, 1), fixed),
        ],
        out_specs=[pl.BlockSpec((rows, B_V), row)] * 5 + [pl.BlockSpec((rows, LANES), row)],
        out_shape=[wide(BF16), wide(BF16), wide(F32), wide(BF16), wide(BF16), jax.ShapeDtypeStruct((n, LANES), F32)],
        scratch_shapes=[pltpu.VMEM((SUBLANES + rows, B_CONV_CH), F32)],
        compiler_params=_cparams(("parallel",)),
        name="b_prep",
    )(p, p, cstate, bac, bar, cw, alog_r, dt_r, alog_c, dt_c)


def _b_scan_kernel(qg_ref, kd_ref, u_ref, w_ref, attn_ref, egl_ref, z_ref, s0_ref, gout_ref, o_ref, s_ref, *, nb):
    c = pl.program_id(1)

    @pl.when(c == 0)
    def _():
        s_ref[...] = s0_ref[...]

    half = c % (ROW_BLOCK // CHUNK)
    rgrp = lax.broadcasted_iota(jnp.int32, (ROW_BLOCK, B_DV), 0) // CHUNK
    here = rgrp == half
    units = [(b, hh) for b in range(nb) for hh in range(B_HEADS)]
    head = lambda hh: slice(hh * B_DV, (hh + 1) * B_DV)
    states = [s_ref[b, hh] for b, hh in units]
    proj = [_dot(jnp.concatenate([w_ref[b, :, head(hh)], qg_ref[b, :, head(hh)]], axis=0), s)
            for (b, hh), s in zip(units, states)]
    v_new = [u_ref[b, :, head(hh)] - pr[:CHUNK] for (b, hh), pr in zip(units, proj)]
    outs = []
    for (b, hh), pr, vn in zip(units, proj, v_new):
        v_full = jnp.where(here, jnp.concatenate([vn] * (ROW_BLOCK // CHUNK), axis=0), 0.0)
        outs.append(pr[CHUNK:] + _dot(attn_ref[b, :, head(hh)], v_full))
    for (b, hh), s, vn in zip(units, states, v_new):
        decay = egl_ref[b, 0:1, B_HEADS + hh:B_HEADS + hh + 1]
        s_ref[b, hh] = s * decay + _dot_tn(kd_ref[b, :, head(hh)], vn)
    for (b, hh), o in zip(units, outs):
        gate = _silu(z_ref[b, :, head(hh)].astype(F32))
        o_ref[b, :, head(hh)] = (_rms(o, gout_ref[...]) * gate).astype(o_ref.dtype)


def _b_scan(qg, kd, u, w, attn, egl, z, s0, gout, batch, seq_len):
    nb = 4
    nchunk = seq_len // CHUNK
    v3 = lambda a: a.reshape(batch, seq_len, a.shape[-1])
    rows = lambda bi, c: (bi, c, 0)
    state = lambda bi, c: (bi, 0, 0, 0)
    wide = pl.BlockSpec((nb, CHUNK, B_V), rows)
    o, s_new = pl.pallas_call(
        functools.partial(_b_scan_kernel, nb=nb),
        grid=(batch // nb, nchunk),
        in_specs=[wide] * 5 + [pl.BlockSpec((nb, CHUNK, LANES), rows), wide,
                               pl.BlockSpec((nb, B_HEADS, B_DK, B_DV), state),
                               pl.BlockSpec((1, B_DV), lambda bi, c: (0, 0))],
        out_specs=[wide, pl.BlockSpec((nb, B_HEADS, B_DK, B_DV), state)],
        out_shape=[jax.ShapeDtypeStruct((batch, seq_len, B_V), BF16),
                   jax.ShapeDtypeStruct((batch, B_HEADS, B_DK, B_DV), F32)],
        compiler_params=_cparams(("parallel", "arbitrary")),
        name="b_scan",
    )(v3(qg), v3(kd), v3(u), v3(w), v3(attn), v3(egl), v3(z), s0, gout)
    return o.reshape(batch * seq_len, B_V), s_new


def _proj_c_kernel(x_ref, g1_ref, w_ref, cos_ref, sin_ref, q_ref, k_ref, v_ref, z_ref):
    h = _rms(x_ref[...], g1_ref[...]).astype(BF16)
    cos = cos_ref[...]
    sin = sin_ref[...]
    lane = lax.broadcasted_iota(jnp.int32, cos.shape, 1)
    first_half = (lane % C_DK) < (C_DK // 2)

    def rope(seg):
        swapped = jnp.where(first_half, pltpu.roll(seg, LANES - C_DK // 2, 1), pltpu.roll(seg, C_DK // 2, 1))
        return seg * cos + swapped * sin

    qk = jnp.dot(h, w_ref[:, 0:2 * C_QK], preferred_element_type=F32)
    for j in range(2 * C_QK // LANES):
        seg = rope(qk[:, j * LANES:(j + 1) * LANES])
        if j < C_QK // LANES:
            q_ref[:, j * LANES:(j + 1) * LANES] = seg
        else:
            jj = j - C_QK // LANES
            k_ref[:, jj * LANES:(jj + 1) * LANES] = seg * (C_DK ** -0.5)
    v_ref[...] = jnp.dot(h, w_ref[:, 2 * C_QK:2 * C_QK + C_V], preferred_element_type=F32).astype(v_ref.dtype)
    z_ref[...] = jnp.dot(h, w_ref[:, 2 * C_QK + C_V:2 * C_QK + 2 * C_V],
                         preferred_element_type=F32).astype(z_ref.dtype)


def _proj_c(x, g1, w, cos, sin):
    n = x.shape[0]
    tm = 256
    tab_blocks = cos.shape[0] // tm
    row = lambda i: (i, 0)
    fixed = lambda i: (0, 0)
    tab = (lambda i: (i % tab_blocks, 0)) if tab_blocks > 1 else fixed
    return pl.pallas_call(
        _proj_c_kernel,
        grid=(n // tm,),
        in_specs=[
            pl.BlockSpec((tm, D_MODEL), row),
            pl.BlockSpec((1, D_MODEL), fixed),
            pl.BlockSpec((D_MODEL, 2 * C_QK + 2 * C_V), lambda i: (0, W_COLS["c"] // (2 * C_QK + 2 * C_V))),
            pl.BlockSpec((tm, LANES), tab),
            pl.BlockSpec((tm, LANES), tab),
        ],
        out_specs=[pl.BlockSpec((tm, C_QK), row), pl.BlockSpec((tm, C_QK), row),
                   pl.BlockSpec((tm, C_V), row), pl.BlockSpec((tm, C_V), row)],
        out_shape=[jax.ShapeDtypeStruct((n, C_QK), F32), jax.ShapeDtypeStruct((n, C_QK), F32),
                   jax.ShapeDtypeStruct((n, C_V), BF16), jax.ShapeDtypeStruct((n, C_V), BF16)],
        compiler_params=_cparams(("parallel",)),
        name="proj_c",
    )(x, g1, w, cos, sin)


def _log_gamma(hh):
    return math.log1p(-(2.0 ** (-5.0 - hh)))


def _c_scan_kernel(q_ref, k_ref, v_ref, z_ref, r0_ref, gout_ref, o_ref, rout_ref, r_ref, *, nb, t_valid):
    c = pl.program_id(1)
    rows = ROW_BLOCK
    state_blocks = [(b, hh, slice(hh * C_DK, (hh + 1) * C_DK), slice(hh * C_DV, (hh + 1) * C_DV))
                    for b in range(nb) for hh in range(C_HEADS)]

    @pl.when(c == 0)
    def _():
        r_ref[...] = jnp.zeros_like(r_ref)
        for b, hh, rsl, csl in state_blocks:
            r_ref[b, rsl, csl] = r0_ref[b, hh]

    left = jnp.clip(t_valid - c * rows, 0, rows)
    ri = lax.broadcasted_iota(jnp.int32, (rows, rows), 0)
    ci = lax.broadcasted_iota(jnp.int32, (rows, rows), 1)
    cnt_i = jnp.minimum(ri + 1, left).astype(F32)
    cnt_j = jnp.minimum(ci + 1, left).astype(F32)
    incl = ri >= ci
    steps = jnp.where(incl, cnt_i - cnt_j, 0.0)
    cnt_col = cnt_i[:, 0:1]
    left_f = left.astype(F32)
    qk_lane = lax.broadcasted_iota(jnp.int32, (1, C_QK), 1) // C_DK
    lg_lane = jnp.zeros((1, C_QK), F32)
    for hh in range(C_HEADS):
        lg_lane = jnp.where(qk_lane == hh, _log_gamma(hh), lg_lane)
    qk_sub = lax.broadcasted_iota(jnp.int32, (C_QK, 1), 0) // C_DK
    lg_sub = jnp.zeros((C_QK, 1), F32)
    for hh in range(C_HEADS):
        lg_sub = jnp.where(qk_sub == hh, _log_gamma(hh), lg_sub)
    q_scale = jnp.exp(cnt_col * lg_lane)
    k_scale = jnp.exp((left_f - cnt_col) * lg_lane)
    r_scale = jnp.exp(left_f * lg_sub)
    row_ok = (lax.broadcasted_iota(jnp.int32, (rows, 1), 0) + c * rows) < t_valid
    diag = (lax.broadcasted_iota(jnp.int32, (C_QK, C_V), 0) // C_DK) == (
        lax.broadcasted_iota(jnp.int32, (C_QK, C_V), 1) // C_DV)

    head = lambda hh: slice(hh * C_DV, (hh + 1) * C_DV)
    decays = [jnp.where(incl, jnp.exp(steps * _log_gamma(hh)), 0.0) for hh in range(C_HEADS)]
    qs = [q_ref[b] for b in range(nb)]
    ks = [jnp.where(row_ok, k_ref[b], 0.0) for b in range(nb)]
    vs = [v_ref[b].astype(BF16) for b in range(nb)]
    rs = [r_ref[b] for b in range(nb)]
    inters = [_dot(q * q_scale, r) for q, r in zip(qs, rs)]
    units = [(b, hh) for b in range(nb) for hh in range(C_HEADS)]
    atts = [decays[hh] * _dot_nt(qs[b], jnp.where(qk_lane == hh, ks[b], 0.0)) for b, hh in units]
    outs = [inters[b][:, head(hh)] + _dot(att, vs[b][:, head(hh)]) for (b, hh), att in zip(units, atts)]
    for b in range(nb):
        r_ref[b] = rs[b] * r_scale + jnp.where(diag, _dot_tn(ks[b] * k_scale, vs[b]), 0.0)
    for (b, hh), o in zip(units, outs):
        gate = _silu(z_ref[b, :, head(hh)].astype(F32))
        o_ref[b, :, head(hh)] = (_rms(o, gout_ref[...]) * gate).astype(o_ref.dtype)

    @pl.when(c == pl.num_programs(1) - 1)
    def _():
        for b, hh, rsl, csl in state_blocks:
            rout_ref[b, hh] = r_ref[b, rsl, csl]


def _c_scan(q, k, v, z, r0, gout, batch, seq_len, t_valid):
    nb = 4
    nblk = seq_len // ROW_BLOCK
    v3 = lambda a: a.reshape(batch, seq_len, a.shape[-1])
    rows = lambda bi, c: (bi, c, 0)
    state = pl.BlockSpec((nb, C_HEADS, C_DK, C_DV), lambda bi, c: (bi, 0, 0, 0))
    o, r_new = pl.pallas_call(
        functools.partial(_c_scan_kernel, nb=nb, t_valid=t_valid),
        grid=(batch // nb, nblk),
        in_specs=[pl.BlockSpec((nb, ROW_BLOCK, C_QK), rows), pl.BlockSpec((nb, ROW_BLOCK, C_QK), rows),
                  pl.BlockSpec((nb, ROW_BLOCK, C_V), rows), pl.BlockSpec((nb, ROW_BLOCK, C_V), rows),
                  state, pl.BlockSpec((1, C_DV), lambda bi, c: (0, 0))],
        out_specs=[pl.BlockSpec((nb, ROW_BLOCK, C_V), rows), state],
        out_shape=[jax.ShapeDtypeStruct((batch, seq_len, C_V), BF16),
                   jax.ShapeDtypeStruct((batch, C_HEADS, C_DK, C_DV), F32)],
        scratch_shapes=[pltpu.VMEM((nb, C_QK, C_V), F32)],
        compiler_params=_cparams(("parallel", "arbitrary")),
        name="c_scan",
    )(v3(q), v3(k), v3(v), v3(z), r0, gout)
    return o.reshape(batch * seq_len, C_V), r_new


def _merge_kernel(x_ref, g1_ref, wg_ref, o0_ref, o1_ref, o2_ref, l0_ref, l1_ref, l2_ref, ob_ref, oc_ref,
                  wa_ref, wb_ref, wc_ref, wo_ref, y_ref, *scr, residue_major):
    x = x_ref[...]
    tm = x.shape[0]
    h = _rms(x, g1_ref[...]).astype(BF16)
    lses = [r[...].reshape(tm, LANES) for r in (l0_ref, l1_ref, l2_ref)]
    outs = [r[...].reshape(tm, A_WIDTH) for r in (o0_ref, o1_ref, o2_ref)]
    heads = []
    for hh in range(A_HEADS):
        sl = slice(hh * A_HD, (hh + 1) * A_HD)
        ls = [l[:, 32 * hh:32 * hh + 1] for l in lses]
        m = jnp.maximum(jnp.maximum(ls[0], ls[1]), ls[2])
        es = [jnp.exp(l - m) for l in ls]
        tot = es[0] + es[1] + es[2]
        acc = (es[0] / tot) * outs[0][:, sl].astype(F32)
        acc = acc + (es[1] / tot) * outs[1][:, sl].astype(F32)
        acc = acc + (es[2] / tot) * outs[2][:, sl].astype(F32)
        heads.append(acc)
    o_a = jnp.concatenate(heads, axis=1)
    if residue_major:
        o_a = _swap_row_grid(scr[0], o_a)
    o_a = o_a.astype(BF16)
    merged = None
    for gi, (o_g, w_ref) in enumerate(((o_a, wa_ref), (ob_ref[...], wb_ref), (oc_ref[...], wc_ref))):
        gate = jax.nn.sigmoid(jnp.dot(h, wg_ref[:, gi * D_MODEL:(gi + 1) * D_MODEL], preferred_element_type=F32))
        term = gate * jnp.dot(o_g, w_ref[...], preferred_element_type=F32)
        merged = term if merged is None else merged + term
    y_ref[...] = x + jnp.dot(merged.astype(BF16), wo_ref[...], preferred_element_type=F32)


def _merge(x, g1, wg, o_groups, lses, o_b, o_c, wa, wb, wc, wo, seq_len, residue_major):
    n = x.shape[0]
    tm = RES * RES
    row = lambda i: (i, 0)
    fixed = lambda i: (0, 0)
    half = pl.BlockSpec((tm, A_WIDTH), row)
    wbr = pl.BlockSpec((A_WIDTH, D_MODEL), fixed)
    if residue_major:
        tiles = seq_len // tm
        grp = lambda i: (i // tiles, 0, i % tiles, 0)
        o_spec = pl.BlockSpec((None, RES, tm // RES, A_WIDTH), grp)
        lse = pl.BlockSpec((None, RES, tm // RES, LANES), grp)
        scratch = [pltpu.VMEM((A_WIDTH // LANES, tm, LANES), F32)]
    else:
        o_spec = half
        lse = pl.BlockSpec((tm, LANES), row)
        scratch = []
    return pl.pallas_call(
        functools.partial(_merge_kernel, residue_major=residue_major),
        grid=(n // tm,),
        in_specs=[pl.BlockSpec((tm, D_MODEL), row), pl.BlockSpec((1, D_MODEL), fixed),
                  pl.BlockSpec((D_MODEL, 3 * D_MODEL), lambda i: (0, W_COLS["gates"] // (3 * D_MODEL))),
                  o_spec, o_spec, o_spec, lse, lse, lse, half, half, wbr, wbr, wbr,
                  pl.BlockSpec((D_MODEL, D_MODEL), fixed)],
        out_specs=pl.BlockSpec((tm, D_MODEL), row),
        out_shape=jax.ShapeDtypeStruct((n, D_MODEL), F32),
        scratch_shapes=scratch,
        compiler_params=_cparams(("parallel",)),
        name="merge",
    )(x, g1, wg, *o_groups, *lses, o_b, o_c, wa, wb, wc, wo)


def _ffn_kernel(x_ref, g2_ref, wg_ref, wu_ref, wo_ref, y_ref, h_scr, acc_scr):
    j = pl.program_id(1)

    @pl.when(j == 0)
    def _():
        h_scr[...] = _rms(x_ref[...], g2_ref[...]).astype(BF16)
        acc_scr[...] = jnp.zeros_like(acc_scr)

    h = h_scr[...]
    gate = jnp.dot(h, wg_ref[...], preferred_element_type=F32)
    up = jnp.dot(h, wu_ref[...], preferred_element_type=F32)
    acc_scr[...] += jnp.dot((_silu(gate) * up).astype(BF16), wo_ref[...], preferred_element_type=F32)

    @pl.when(j == pl.num_programs(1) - 1)
    def _():
        y_ref[...] = x_ref[...] + acc_scr[...]


def _ffn(x, g2, w_in, w_out):
    n = x.shape[0]
    tm = min(n, 1024)
    tf = 256
    nf = D_FF // tf
    row = lambda i, j: (i, 0)
    return pl.pallas_call(
        _ffn_kernel,
        grid=(n // tm, nf),
        in_specs=[pl.BlockSpec((tm, D_MODEL), row), pl.BlockSpec((1, D_MODEL), lambda i, j: (0, 0)),
                  pl.BlockSpec((D_MODEL, tf), lambda i, j: (0, j)),
                  pl.BlockSpec((D_MODEL, tf), lambda i, j: (0, nf + j)),
                  pl.BlockSpec((tf, D_MODEL), lambda i, j: (j, 0))],
        out_specs=pl.BlockSpec((tm, D_MODEL), row),
        out_shape=jax.ShapeDtypeStruct((n, D_MODEL), F32),
        scratch_shapes=[pltpu.VMEM((tm, D_MODEL), BF16), pltpu.VMEM((tm, D_MODEL), F32)],
        compiler_params=_cparams(("parallel", "arbitrary")),
        name="ffn",
    )(x, g2, w_in, w_in, w_out)


def _rope_tables(pos, hd, reps):
    inv = ROPE_THETA ** (-jnp.arange(0, hd, 2, dtype=F32) / hd)
    ang = pos.astype(F32)[:, None] * inv[None, :]
    cos = jnp.cos(ang)
    sin = jnp.sin(ang)
    cos2 = jnp.concatenate([cos, cos], axis=1)
    sin2 = jnp.concatenate([-sin, sin], axis=1)
    return jnp.tile(cos2, (1, reps)), jnp.tile(sin2, (1, reps))


def _pad_rows(a, batch, t, t_pad):
    if t == t_pad:
        return a
    a = a.reshape(batch, t, a.shape[-1])
    a = jnp.pad(a, ((0, 0), (0, t_pad - t), (0, 0)))
    return a.reshape(batch * t_pad, a.shape[-1])


def _unpad_rows(a, batch, t, t_pad):
    if t == t_pad:
        return a
    return a.reshape(batch, t_pad, a.shape[-1])[:, :t].reshape(batch * t, a.shape[-1])


def _layer(x, pos, batch, t, lw, caches, layer, conv_state, s0, r0):
    n = batch * t
    prompt = caches is None
    reps = max(1, 256 // t)
    cos_a, sin_a = _rope_tables(pos, A_HD, 1)
    cos_c, sin_c = _rope_tables(pos, C_DK, LANES // C_DK)
    if reps > 1:
        cos_a, sin_a, cos_c, sin_c = (jnp.tile(a, (reps, 1)) for a in (cos_a, sin_a, cos_c, sin_c))

    if prompt:
        to_rm = lambda a: a.reshape(t // RES, RES, A_HD).transpose(1, 0, 2)
        q, k, v, k_tail, v_tail = _proj_a(x, lw["g1"], lw["w_all"], lw["qn"], lw["kn"], to_rm(cos_a), to_rm(sin_a),
                                          batch, t, True)
    else:
        q, k, v = _proj_a(x, lw["g1"], lw["w_all"], lw["qn"], lw["kn"], cos_a, sin_a, batch, t, False)
        k_tail, v_tail = k, v
    outs, lses = [], []
    for gi in range(N_GROUPS):
        if prompt:
            o, lse = _attn_prompt(q, k, v, gi, batch, t)
        else:
            o, lse = _attn_sample(q, k, v, caches[gi], layer, gi, batch, t)
        outs.append(o)
        lses.append(lse)
    new_kv = []
    for gi, (win, _) in enumerate(A_GROUPS):
        cols = slice(gi * A_WIDTH, (gi + 1) * A_WIDTH)
        if prompt:
            keep = min(win, t)
            first = k_tail.shape[2] - keep // RES
            tail = lambda a: a[:, :, first:, cols].transpose(0, 2, 1, 3).reshape(batch, keep, A_HEADS, A_HD)
        else:
            tail = lambda a: a[:, cols].reshape(batch, t, A_HEADS, A_HD)
        new_kv.append(jnp.stack([tail(k_tail), tail(v_tail)], axis=2))

    t_pad = -(-t // ROW_BLOCK) * ROW_BLOCK
    p, z_b, bac, bar = _proj_b(x, lw["g1"], lw["w_all"], lw["w_all"], lw["w_all"], lw["w_bat"])
    conv_new = jnp.concatenate([conv_state, p.reshape(batch, t, B_CONV_CH)], axis=1)[:, -(B_CONV - 1):]
    cst = jnp.pad(conv_state, ((0, 0), (SUBLANES - (B_CONV - 1), 0), (0, 0)))
    bar_p = _pad_rows(bar.T, batch, t, t_pad).T if t_pad != t else bar
    qg, kd, u, w, attn, egl = _b_prep(
        _pad_rows(p, batch, t, t_pad), cst, _pad_rows(bac, batch, t, t_pad), bar_p, lw["conv_w"],
        lw["alog_r"], lw["dt_r"], lw["alog_c"], lw["dt_c"], t_pad, t)
    o_b, s_new = _b_scan(qg, kd, u, w, attn, egl, _pad_rows(z_b, batch, t, t_pad), s0, lw["gb"], batch, t_pad)
    o_b = _unpad_rows(o_b, batch, t, t_pad)

    cq, ck, cv, cz = _proj_c(x, lw["g1"], lw["w_all"], cos_c, sin_c)
    o_c, r_new = _c_scan(*(_pad_rows(a, batch, t, t_pad) for a in (cq, ck, cv, cz)), r0, lw["gc"], batch, t_pad, t)
    o_c = _unpad_rows(o_c, batch, t, t_pad)

    x = _merge(x, lw["g1"], lw["w_all"], outs, lses, o_b, o_c, lw["w_oa"], lw["w_ob"], lw["w_oc"], lw["w_o"], t, prompt)
    x = _ffn(x, lw["g2"], lw["w_fi"], lw["w_fo"])
    return x, new_kv, conv_new, s_new, r_new


def _layer_weights(l, norm1_g, w_in, a_q_norm_g, a_k_norm_g, b_conv_w, b_a_log, b_dt_bias, b_out_norm_g,
                   c_out_norm_g, w_out_a, w_out_b, w_out_c, w_out, norm2_g, w_ffn_in, w_ffn_out):
    o = IN_OFFS
    wl = w_in[l]
    w_ba = wl[:, o[3]:o[5]]
    pad_r = lambda a: jnp.pad(a.reshape(1, B_HEADS), ((0, 0), (B_HEADS, LANES - 2 * B_HEADS)))
    pad_c = lambda a: jnp.pad(a.reshape(B_HEADS, 1), ((B_HEADS, 2 * SUBLANES - 2 * B_HEADS), (0, 0)))
    return dict(
        g1=norm1_g[l].reshape(1, D_MODEL), g2=norm2_g[l].reshape(1, D_MODEL),
        w_all=jnp.concatenate([wl[:, o[0]:o[1]], wl[:, o[1]:o[2]], wl[:, o[9]:o[10]], wl[:, o[5]:o[9]], wl[:, o[2]:o[3]],
                               jnp.pad(w_ba, ((0, 0), (0, LANES - 2 * B_HEADS)))], axis=1).astype(BF16),
        w_bat=jnp.pad(w_ba.T, ((0, 2 * SUBLANES - 2 * B_HEADS), (0, 0))).astype(BF16),
        qn=a_q_norm_g[l].reshape(1, A_HD), kn=a_k_norm_g[l].reshape(1, A_HD),
        conv_w=b_conv_w[l],
        alog_r=pad_r(b_a_log[l]), dt_r=pad_r(b_dt_bias[l]), alog_c=pad_c(b_a_log[l]), dt_c=pad_c(b_dt_bias[l]),
        gb=b_out_norm_g[l].reshape(1, B_DV), gc=c_out_norm_g[l].reshape(1, C_DV),
        w_oa=w_out_a[l].astype(BF16), w_ob=w_out_b[l].astype(BF16), w_oc=w_out_c[l].astype(BF16),
        w_o=w_out[l].astype(BF16), w_fi=w_ffn_in[l].astype(BF16), w_fo=w_ffn_out[l].astype(BF16),
    )


def kernel(x_prompt, x_sample, cache_a_kv0, cache_a_kv1, cache_a_kv2, state_b_conv, state_b_S, state_c_R, norm1_g, w_in, a_q_norm_g, a_k_norm_g, b_conv_w, b_a_log, b_dt_bias, b_out_norm_g, c_out_norm_g, w_out_a, w_out_b, w_out_c, w_out, norm2_g, w_ffn_in, w_ffn_out):
    bp, t = x_prompt.shape[:2]
    bs, s = x_sample.shape[:2]
    depth = w_in.shape[0]
    pos_p = jnp.arange(t)
    pos_s = PAST_LEN + jnp.arange(s)
    yp = x_prompt.reshape(bp * t, D_MODEL)
    ys = x_sample.reshape(bs * s, D_MODEL)
    caches = (cache_a_kv0, cache_a_kv1, cache_a_kv2)
    zeros_conv = jnp.zeros((bp, B_CONV - 1, B_CONV_CH), F32)
    zeros_s = jnp.zeros((bp, B_HEADS, B_DK, B_DV), F32)
    zeros_r = jnp.zeros((bp, C_HEADS, C_DK, C_DV), F32)
    acc = [[] for _ in range(12)]
    for l in range(depth):
        lw = _layer_weights(l, norm1_g, w_in, a_q_norm_g, a_k_norm_g, b_conv_w, b_a_log, b_dt_bias,
                            b_out_norm_g, c_out_norm_g, w_out_a, w_out_b, w_out_c, w_out, norm2_g,
                            w_ffn_in, w_ffn_out)
        yp, kv, cv, sn, rn = _layer(yp, pos_p, bp, t, lw, None, l, zeros_conv, zeros_s, zeros_r)
        for i, a in enumerate((kv[0], kv[1], kv[2], cv, sn, rn)):
            acc[i].append(a)
        ys, kv, cv, sn, rn = _layer(ys, pos_s, bs, s, lw, caches, l, state_b_conv[l], state_b_S[l], state_c_R[l])
        for i, a in enumerate((kv[0], kv[1], kv[2], cv, sn, rn)):
            acc[6 + i].append(a)
    return (yp.reshape(bp, t, D_MODEL), ys.reshape(bs, s, D_MODEL)) + tuple(jnp.stack(a) for a in acc)
```

```python
import functools
import math

import jax
import jax.numpy as jnp
import numpy as np
from jax import lax
from jax.experimental import pallas as pl
from jax.experimental.pallas import tpu as pltpu

F32 = jnp.float32
BF16 = jnp.bfloat16

D_MODEL = 1024
PAST_LEN = 8192
A_GROUPS = ((128, 1), (512, 4), (2048, 16))
N_GROUPS = 3
A_HEADS = 4
A_HD = 128
A_WIDTH = A_HEADS * A_HD
A_KEYS = 128
B_HEADS = 4
B_DK = 128
B_DV = 128
B_CONV = 4
B_QK = B_HEADS * B_DK
B_V = B_HEADS * B_DV
B_CONV_CH = 2 * B_QK + B_V
C_HEADS = 4
C_DK = 64
C_DV = 128
C_QK = C_HEADS * C_DK
C_V = C_HEADS * C_DV
CHUNK = 64
ROPE_THETA = 10000.0
EPS = 1e-6
D_FF = 2816
IN_SIZES = (3 * N_GROUPS * A_WIDTH, B_CONV_CH, B_V, B_HEADS, B_HEADS, C_QK, C_QK, C_V, C_V, 3 * D_MODEL)
IN_OFFS = tuple(int(v) for v in np.cumsum((0,) + IN_SIZES))

W_COLS = {"a": 0, "b_qkv": 4608, "gates": 6144, "c": 9216, "b_z": 10752, "b_ba": 11264}
W_ALL = 11392

ROW_BLOCK = 128
RES = 16
SUBLANES = 8
LANES = 128
VMEM_LIMIT = 48 * 1024 * 1024


def _cparams(sem):
    return pltpu.CompilerParams(dimension_semantics=sem, vmem_limit_bytes=VMEM_LIMIT)


def _rms(x, g):
    return x * lax.rsqrt(jnp.mean(x * x, axis=-1, keepdims=True) + EPS) * g


def _silu(x):
    return x * jax.nn.sigmoid(x)


def _softplus(x):
    return jnp.maximum(x, 0.0) + jnp.log(1.0 + jnp.exp(-jnp.abs(x)))


def _dot(a, b):
    return jnp.dot(a.astype(BF16), b.astype(BF16), preferred_element_type=F32)


def _dot_nt(a, b):
    return lax.dot_general(a.astype(BF16), b.astype(BF16), (((1,), (1,)), ((), ())), preferred_element_type=F32)


def _dot_tn(a, b):
    return lax.dot_general(a.astype(BF16), b.astype(BF16), (((0,), (0,)), ((), ())), preferred_element_type=F32)


def _swap_row_grid(scr, val):
    slabs = val.shape[1] // LANES
    for c in range(slabs):
        scr[c] = val[:, c * LANES:(c + 1) * LANES]
    cols = [jnp.concatenate([scr[c, pl.ds(r, RES, stride=RES), :] for r in range(RES)], axis=0)
            for c in range(slabs)]
    return jnp.concatenate(cols, axis=1)


def _proj_a_kernel(x_ref, g1_ref, w_ref, qg_ref, kg_ref, cos_ref, sin_ref, q_ref, k_ref, v_ref, *rest, residue_major):
    x = x_ref[...]
    tm = x.shape[0]
    if residue_major:
        tail_ref, scr = rest
        x = _swap_row_grid(scr, x)
    h = _rms(x, g1_ref[...]).astype(BF16)
    cos = cos_ref[...].reshape(tm, A_HD)
    sin = sin_ref[...].reshape(tm, A_HD)

    def norm_rope(seg, g):
        y = _rms(seg, g)
        return y * cos + pltpu.roll(y, A_HD // 2, 1) * sin

    def put(ref, col, val):
        if residue_major:
            ref[:, :, col:col + val.shape[1]] = val.reshape(RES, tm // RES, val.shape[1]).astype(ref.dtype)
        else:
            ref[:, col:col + val.shape[1]] = val

    for j in range(3 * N_GROUPS):
        acc = jnp.dot(h, w_ref[:, j * A_WIDTH:(j + 1) * A_WIDTH], preferred_element_type=F32)
        if j < N_GROUPS:
            for hh in range(A_HEADS):
                sl = slice(hh * A_HD, (hh + 1) * A_HD)
                put(q_ref, j * A_WIDTH + hh * A_HD, norm_rope(acc[:, sl], qg_ref[...]) * (A_HD ** -0.5))
        elif j < 2 * N_GROUPS:
            jj = j - N_GROUPS
            for hh in range(A_HEADS):
                sl = slice(hh * A_HD, (hh + 1) * A_HD)
                val = norm_rope(acc[:, sl], kg_ref[...])
                put(k_ref, jj * A_WIDTH + hh * A_HD, val)
                if residue_major:
                    put(tail_ref, 2 * jj * A_WIDTH + hh * A_HD, val)
        else:
            jj = j - 2 * N_GROUPS
            put(v_ref, jj * A_WIDTH, acc)
            if residue_major:
                put(tail_ref, (2 * jj + 1) * A_WIDTH, acc)


def _proj_a(x, g1, w, qg, kg, cos, sin, batch, seq_len, residue_major):
    n = x.shape[0]
    tm = RES * RES
    nw = N_GROUPS * A_WIDTH
    fixed = lambda i: (0, 0)
    common = [pl.BlockSpec((1, D_MODEL), fixed), pl.BlockSpec((D_MODEL, 3 * nw), fixed),
              pl.BlockSpec((1, A_HD), fixed), pl.BlockSpec((1, A_HD), fixed)]
    if residue_major:
        tiles = seq_len // tm
        tail_tiles = min(max(wd for wd, _ in A_GROUPS), seq_len) // tm
        blk = (None, RES, tm // RES, nw)
        tab = pl.BlockSpec((RES, tm // RES, A_HD), lambda i: (0, i % tiles, 0))
        main = pl.BlockSpec(blk, lambda i: (i // tiles, 0, i % tiles, 0))
        tail = pl.BlockSpec((None, RES, tm // RES, 2 * nw),
                            lambda i: (i // tiles, 0, jnp.maximum(i % tiles - (tiles - tail_tiles), 0), 0))
        out_specs = [main] * 3 + [tail]
        out_shape = ([jax.ShapeDtypeStruct((batch, RES, seq_len // RES, nw), BF16)] * 3
                     + [jax.ShapeDtypeStruct((batch, RES, tail_tiles * tm // RES, 2 * nw), F32)])
        scratch = [pltpu.VMEM((D_MODEL // LANES, tm, LANES), F32)]
    else:
        assert cos.shape[0] == tm
        tab = pl.BlockSpec((tm, A_HD), fixed)
        out_specs = [pl.BlockSpec((tm, nw), lambda i: (i, 0))] * 3
        out_shape = [jax.ShapeDtypeStruct((n, nw), F32)] * 3
        scratch = []
    return pl.pallas_call(
        functools.partial(_proj_a_kernel, residue_major=residue_major),
        grid=(n // tm,),
        in_specs=[pl.BlockSpec((tm, D_MODEL), lambda i: (i, 0))] + common + [tab, tab],
        out_specs=out_specs,
        out_shape=out_shape,
        scratch_shapes=scratch,
        compiler_params=_cparams(("arbitrary",)),
        name="proj_a",
    )(x, g1, w, qg, kg, cos, sin)


ATTN_SUBS = 2


def _attn_prompt_kernel(q_ref, kc_ref, kp_ref, vc_ref, vp_ref, o_ref, lse_ref, *, parts):
    n = pl.program_id(2)
    per = ROW_BLOCK // parts
    qi = lax.broadcasted_iota(jnp.int32, (ROW_BLOCK, ROW_BLOCK), 0)
    kj = lax.broadcasted_iota(jnp.int32, (ROW_BLOCK, ROW_BLOCK), 1)
    qi = parts * (qi % per) + qi // per
    kj = parts * (kj % per) + kj // per
    cur_ok = kj <= qi
    prev_ok = kj >= qi
    first_ok = jnp.logical_and(prev_ok, n > 0)
    lane = lax.broadcasted_iota(jnp.int32, (ROW_BLOCK, LANES), 1)
    neg = -jnp.inf
    packed_rows = 2 * SUBLANES

    def sub(ref, half, sl):
        full = ref[:, :, sl]
        if per % packed_rows == 0:
            return full[:, half * per:(half + 1) * per].reshape(ROW_BLOCK, A_HD)
        return full.astype(F32)[:, half * per:(half + 1) * per].reshape(ROW_BLOCK, A_HD).astype(BF16)

    units = [(half, hh) for half in range(ATTN_SUBS) for hh in range(A_HEADS)]
    head = lambda hh: slice(hh * A_HD, (hh + 1) * A_HD)

    keys = {(half, hh): sub(kc_ref, half, head(hh)) for half, hh in units}
    vals = {(half, hh): sub(vc_ref, half, head(hh)) for half, hh in units}
    for hh in range(A_HEADS):
        keys[(-1, hh)] = sub(kp_ref, ATTN_SUBS - 1, head(hh))
        vals[(-1, hh)] = sub(vp_ref, ATTN_SUBS - 1, head(hh))

    scores = []
    for half, hh in units:
        q = sub(q_ref, half, head(hh))
        s_cur = jnp.where(cur_ok, _dot_nt(q, keys[(half, hh)]), neg)
        s_prev = jnp.where(first_ok if half == 0 else prev_ok, _dot_nt(q, keys[(half - 1, hh)]), neg)
        scores.append((s_cur, s_prev))
    probs = []
    for s_cur, s_prev in scores:
        m = jnp.maximum(jnp.max(s_cur, axis=-1, keepdims=True), jnp.max(s_prev, axis=-1, keepdims=True))
        p_cur = jnp.exp(s_cur - m)
        p_prev = jnp.exp(s_prev - m)
        den = jnp.sum(p_cur, axis=-1, keepdims=True) + jnp.sum(p_prev, axis=-1, keepdims=True)
        probs.append((p_cur, p_prev, m, den))
    lse_blk = [jnp.zeros((ROW_BLOCK, LANES), F32) for _ in range(ATTN_SUBS)]
    outs = {}
    for (half, hh), (p_cur, p_prev, m, den) in zip(units, probs):
        o = (_dot(p_cur, vals[(half, hh)]) + _dot(p_prev, vals[(half - 1, hh)])) / den
        outs[(half, hh)] = o.reshape(parts, per, A_HD)
        lse_blk[half] = jnp.where(lane // 32 == hh, m + jnp.log(den), lse_blk[half])
    for hh in range(A_HEADS):
        both = jnp.concatenate([outs[(half, hh)] for half in range(ATTN_SUBS)], axis=1)
        o_ref[:, :, head(hh)] = both.astype(o_ref.dtype)
    for half in range(ATTN_SUBS):
        lse_ref[:, half * per:(half + 1) * per, :] = lse_blk[half].reshape(parts, per, LANES)


def _attn_prompt(q, k, v, gi, batch, seq_len):
    _, dil = A_GROUPS[gi]
    parts = RES // dil
    per = ROW_BLOCK // parts
    rows = seq_len // RES
    nblk = seq_len // dil // (ATTN_SUBS * ROW_BLOCK)
    split = lambda a: a.reshape(batch, parts, dil, rows, a.shape[-1])
    cur = lambda b, r, n: (b, 0, r, n, gi)
    prev = lambda b, r, n: (b, 0, r, jnp.maximum(n - 1, 0), gi)
    out = lambda b, r, n: (b, 0, r, n, 0)
    blk = (None, parts, None, ATTN_SUBS * per, A_WIDTH)
    o, lse = pl.pallas_call(
        functools.partial(_attn_prompt_kernel, parts=parts),
        grid=(batch, dil, nblk),
        in_specs=[pl.BlockSpec(blk, cur), pl.BlockSpec(blk, cur), pl.BlockSpec(blk, prev),
                  pl.BlockSpec(blk, cur), pl.BlockSpec(blk, prev)],
        out_specs=[pl.BlockSpec(blk, out), pl.BlockSpec((None, parts, None, ATTN_SUBS * per, LANES), out)],
        out_shape=[jax.ShapeDtypeStruct((batch, parts, dil, rows, A_WIDTH), BF16),
                   jax.ShapeDtypeStruct((batch, parts, dil, rows, LANES), F32)],
        compiler_params=_cparams(("parallel", "parallel", "arbitrary")),
        name=f"attn_prompt_g{gi}",
    )(split(q), split(k), split(k), split(v), split(v))
    return o.reshape(batch, RES, rows, A_WIDTH), lse.reshape(batch, RES, rows, LANES)


def _attn_sample_kernel(q_ref, kn_ref, vn_ref, cache_ref, o_ref, lse_ref, *, dil, n_new):
    n_res = min(dil, n_new)
    neg = -jnp.inf
    srow = lax.broadcasted_iota(jnp.int32, (n_new, A_KEYS), 0)
    mcol = lax.broadcasted_iota(jnp.int32, (n_new, A_KEYS), 1)
    in_window = mcol >= srow // dil
    row_res = [srow % dil == res for res in range(n_res)]
    srow_n = lax.broadcasted_iota(jnp.int32, (n_new, n_new), 0)
    tcol_n = lax.broadcasted_iota(jnp.int32, (n_new, n_new), 1)
    new_ok = jnp.logical_and(tcol_n <= srow_n, (srow_n - tcol_n) % dil == 0)
    heads = range(A_HEADS)

    qs = [q_ref[hh].astype(BF16) for hh in heads]
    keys = {(res, hh): cache_ref[:, res, 0, hh, :].astype(BF16) for res in range(n_res) for hh in heads}
    vals = {(res, hh): cache_ref[:, res, 1, hh, :].astype(BF16) for res in range(n_res) for hh in heads}
    raw = {key: _dot_nt(qs[key[1]], kmat) for key, kmat in keys.items()}
    s_new = [jnp.where(new_ok, _dot_nt(qs[hh], kn_ref[hh]), neg) for hh in heads]
    probs = []
    for hh in heads:
        s_buf = raw[(0, hh)]
        for res in range(1, n_res):
            s_buf = jnp.where(row_res[res], raw[(res, hh)], s_buf)
        s_buf = jnp.where(in_window, s_buf, neg)
        m = jnp.maximum(jnp.max(s_buf, axis=-1, keepdims=True), jnp.max(s_new[hh], axis=-1, keepdims=True))
        p_buf = jnp.exp(s_buf - m)
        p_new = jnp.exp(s_new[hh] - m)
        den = jnp.sum(p_buf, axis=-1, keepdims=True) + jnp.sum(p_new, axis=-1, keepdims=True)
        probs.append((p_buf, p_new, m, den))
    for hh in heads:
        p_buf, p_new, m, den = probs[hh]
        acc = _dot(p_new, vn_ref[hh])
        for res in range(n_res):
            p_res = p_buf if n_res == 1 else jnp.where(row_res[res], p_buf, 0.0)
            acc = acc + _dot(p_res, vals[(res, hh)])
        o_ref[hh] = acc / den
        lse_ref[hh] = jnp.broadcast_to(m + jnp.log(den), (n_new, LANES))


def _attn_sample_rows_kernel(q_ref, kn_ref, vn_ref, cache_ref, o_ref, lse_ref, *, dil, n_new):
    row = lax.broadcasted_iota(jnp.int32, (A_KEYS, A_HEADS, 1), 0)
    trow = lax.broadcasted_iota(jnp.int32, (n_new, A_HEADS, 1), 0)
    neg = -jnp.inf
    kn = kn_ref[...]
    vn = vn_ref[...]
    for s in range(n_new):
        res = s % dil
        first = s // dil
        q = q_ref[s][None]
        kc = cache_ref[:, res, 0]
        vc = cache_ref[:, res, 1]
        sc = jnp.sum(kc * q, axis=-1, keepdims=True)
        if first > 0:
            sc = jnp.where(row >= first, sc, neg)
        new_ok = jnp.logical_and(trow <= s, (s - trow) % dil == 0)
        sn = jnp.where(new_ok, jnp.sum(kn * q, axis=-1, keepdims=True), neg)
        m = jnp.maximum(jnp.max(sc, axis=0, keepdims=True), jnp.max(sn, axis=0, keepdims=True))
        pc = jnp.exp(sc - m)
        pn = jnp.exp(sn - m)
        den = jnp.sum(pc, axis=0, keepdims=True) + jnp.sum(pn, axis=0, keepdims=True)
        o = (jnp.sum(pc * vc, axis=0, keepdims=True) + jnp.sum(pn * vn, axis=0, keepdims=True)) / den
        o_ref[s] = o[0]
        lse_ref[s] = jnp.broadcast_to((m + jnp.log(den))[0], (A_HEADS, A_HD))


def _attn_sample(q, k, v, cache, layer, gi, batch, n_new):
    win, dil = A_GROUPS[gi]
    depth = cache.shape[0]
    assert cache.shape[2] == win and win // dil == A_KEYS
    n_res = min(dil, n_new)
    cv = cache.reshape(depth, batch, A_KEYS, dil, 2, A_HEADS, A_HD)
    cache_spec = pl.BlockSpec((None, None, A_KEYS, n_res, 2, A_HEADS, A_HD), lambda b: (layer, b, 0, 0, 0, 0, 0))
    if n_res > 1:
        heads = lambda a: a.reshape(batch, n_new, N_GROUPS, A_HEADS, A_HD)
        new = pl.BlockSpec((None, n_new, None, A_HEADS, A_HD), lambda b: (b, 0, gi, 0, 0))
        out = pl.BlockSpec((None, n_new, A_HEADS, A_HD), lambda b: (b, 0, 0, 0))
        o, lse = pl.pallas_call(
            functools.partial(_attn_sample_rows_kernel, dil=dil, n_new=n_new),
            grid=(batch,),
            in_specs=[new, new, new, cache_spec],
            out_specs=[out, out],
            out_shape=[jax.ShapeDtypeStruct((batch, n_new, A_HEADS, A_HD), F32)] * 2,
            compiler_params=_cparams(("parallel",)),
            name=f"attn_sample_g{gi}",
        )(heads(q), heads(k), heads(v), cv)
        lse = jnp.repeat(lse[..., 0], LANES // A_HEADS, axis=-1)
        return o.reshape(batch * n_new, A_WIDTH), lse.reshape(batch * n_new, LANES)
    heads = lambda a: a.reshape(batch, n_new, N_GROUPS, A_HEADS, A_HD).transpose(0, 2, 3, 1, 4)
    new = pl.BlockSpec((None, None, A_HEADS, n_new, A_HD), lambda b: (b, gi, 0, 0, 0))
    out = pl.BlockSpec((None, A_HEADS, n_new, LANES), lambda b: (b, 0, 0, 0))
    o, lse = pl.pallas_call(
        functools.partial(_attn_sample_kernel, dil=dil, n_new=n_new),
        grid=(batch,),
        in_specs=[new, new, new,
                  pl.BlockSpec((None, None, A_KEYS, n_res, 2, A_HEADS, A_HD), lambda b: (layer, b, 0, 0, 0, 0, 0))],
        out_specs=[out, out],
        out_shape=[jax.ShapeDtypeStruct((batch, A_HEADS, n_new, A_HD), F32)] * 2,
        compiler_params=_cparams(("parallel",)),
        name=f"attn_sample_g{gi}",
    )(heads(q), heads(k), heads(v), cv)
    o = o.transpose(0, 2, 1, 3)
    lse = jnp.repeat(lse[..., 0].transpose(0, 2, 1), LANES // A_HEADS, axis=-1)
    return o.reshape(batch * n_new, A_WIDTH), lse.reshape(batch * n_new, LANES)


def _causal_conv(e_scr, cw_ref, rows):
    xc = e_scr[SUBLANES:SUBLANES + rows, :] * cw_ref[B_CONV - 1:B_CONV, :]
    for kk in range(1, B_CONV):
        xc = xc + e_scr[SUBLANES - kk:SUBLANES - kk + rows, :] * cw_ref[B_CONV - 1 - kk:B_CONV - kk, :]
    return xc


def _proj_b_kernel(x_ref, g1_ref, wqkv_ref, wz_ref, wba_ref, wbat_ref, *rest, fuse_conv, tiles_per_seq):
    h = _rms(x_ref[...], g1_ref[...]).astype(BF16)
    tm = h.shape[0]
    if fuse_conv:
        cst_ref, cw_ref, p_ref, z_ref, bac_ref, bar_ref, ptail_ref, e_scr = rest
        first = pl.program_id(0) % tiles_per_seq == 0

        @pl.when(first)
        def _():
            e_scr[0:SUBLANES, :] = cst_ref[...]

        @pl.when(jnp.logical_not(first))
        def _():
            e_scr[0:SUBLANES, :] = e_scr[tm:tm + SUBLANES, :]

        for j in range(3):
            sl = slice(j * B_QK, (j + 1) * B_QK)
            e_scr[SUBLANES:SUBLANES + tm, sl] = jnp.dot(h, wqkv_ref[:, sl], preferred_element_type=F32)
        p_ref[...] = _silu(_causal_conv(e_scr, cw_ref, tm)).astype(p_ref.dtype)
        ptail_ref[...] = e_scr[tm:tm + SUBLANES, :]
    else:
        p_ref, z_ref, bac_ref, bar_ref = rest
        for j in range(3):
            sl = slice(j * B_QK, (j + 1) * B_QK)
            p_ref[:, sl] = jnp.dot(h, wqkv_ref[:, sl], preferred_element_type=F32)
    z_ref[...] = jnp.dot(h, wz_ref[...], preferred_element_type=F32).astype(z_ref.dtype)
    bac_ref[...] = jnp.dot(h, wba_ref[...], preferred_element_type=F32)
    bar_ref[...] = lax.dot_general(wbat_ref[...], h, (((1,), (1,)), ((), ())), preferred_element_type=F32)


def _proj_b(x, g1, wqkv, wz, wba, wbat, cstate=None, conv_w=None, seq_len=None):
    n = x.shape[0]
    tm = 256
    row = lambda i: (i, 0)
    fixed = lambda i: (0, 0)
    fuse_conv = cstate is not None
    in_specs = [
        pl.BlockSpec((tm, D_MODEL), row),
        pl.BlockSpec((1, D_MODEL), fixed),
        pl.BlockSpec((D_MODEL, B_CONV_CH), lambda i: (0, W_COLS["b_qkv"] // B_CONV_CH)),
        pl.BlockSpec((D_MODEL, B_V), lambda i: (0, W_COLS["b_z"] // B_V)),
        pl.BlockSpec((D_MODEL, LANES), lambda i: (0, W_COLS["b_ba"] // LANES)),
        pl.BlockSpec((2 * SUBLANES, D_MODEL), fixed),
    ]
    out_specs = [pl.BlockSpec((tm, B_CONV_CH), row), pl.BlockSpec((tm, B_V), row),
                 pl.BlockSpec((tm, LANES), row), pl.BlockSpec((2 * SUBLANES, tm), lambda i: (0, i))]
    out_shape = [jax.ShapeDtypeStruct((n, B_CONV_CH), BF16 if fuse_conv else F32),
                 jax.ShapeDtypeStruct((n, B_V), BF16),
                 jax.ShapeDtypeStruct((n, LANES), F32), jax.ShapeDtypeStruct((2 * SUBLANES, n), F32)]
    args = [x, g1, wqkv, wz, wba, wbat]
    scratch = []
    tiles = None
    if fuse_conv:
        tiles = seq_len // tm
        per_seq = pl.BlockSpec((None, SUBLANES, B_CONV_CH), lambda i: (i // tiles, 0, 0))
        in_specs += [per_seq, pl.BlockSpec((B_CONV, B_CONV_CH), fixed)]
        out_specs.append(per_seq)
        out_shape.append(jax.ShapeDtypeStruct((n // seq_len, SUBLANES, B_CONV_CH), F32))
        args += [cstate, conv_w]
        scratch = [pltpu.VMEM((SUBLANES + tm, B_CONV_CH), F32)]
    return pl.pallas_call(
        functools.partial(_proj_b_kernel, fuse_conv=fuse_conv, tiles_per_seq=tiles),
        grid=(n // tm,),
        in_specs=in_specs,
        out_specs=out_specs,
        out_shape=out_shape,
        scratch_shapes=scratch,
        compiler_params=_cparams(("arbitrary",) if fuse_conv else ("parallel",)),
        name="proj_b",
    )(*args)


def _b_prep_kernel(*refs, rows, blocks_per_seq, t_valid, conv_done):
    i = pl.program_id(0)
    blk = i % blocks_per_seq
    if conv_done:
        (p_ref, bac_ref, bar_ref, alog_r_ref, dt_r_ref, alog_c_ref, dt_c_ref,
         qg_ref, kd_ref, u_ref, w_ref, attn_ref, egl_ref) = refs
        act = p_ref[...].astype(F32)
    else:
        (p_ref, halo_ref, cst_ref, cw_ref, bac_ref, bar_ref, alog_r_ref, dt_r_ref, alog_c_ref, dt_c_ref,
         qg_ref, kd_ref, u_ref, w_ref, attn_ref, egl_ref, e_scr) = refs
        e_scr[0:SUBLANES, :] = jnp.where(blk == 0, cst_ref[...], halo_ref[...])
        e_scr[SUBLANES:SUBLANES + rows, :] = p_ref[...]
        act = _silu(_causal_conv(e_scr, cw_ref, rows))

    ri = lax.broadcasted_iota(jnp.int32, (rows, LANES), 0)
    li16 = lax.broadcasted_iota(jnp.int32, (2 * SUBLANES, rows), 1)
    li1 = lax.broadcasted_iota(jnp.int32, (1, LANES), 1)
    masked = t_valid < blocks_per_seq * rows
    if masked:
        row_ok = (blk * rows + ri) < t_valid
        col_ok = (blk * rows + li16) < t_valid
        act = jnp.where(ri[:, 0:1] + blk * rows < t_valid, act, 0.0)

    head_lane = jnp.logical_and(li1 >= B_HEADS, li1 < 2 * B_HEADS)
    a_r = jnp.where(head_lane, -jnp.exp(alog_r_ref[...]), 0.0)
    g_col = a_r * _softplus(bac_ref[...] + dt_r_ref[...])
    si = lax.broadcasted_iota(jnp.int32, (2 * SUBLANES, 1), 0)
    head_sub = jnp.logical_and(si >= B_HEADS, si < 2 * B_HEADS)
    a_c = jnp.where(head_sub, -jnp.exp(alog_c_ref[...]), 0.0)
    g_row = a_c * _softplus(bar_ref[...] + dt_c_ref[...])
    if masked:
        g_col = jnp.where(row_ok, g_col, 0.0)
        g_row = jnp.where(col_ok, g_row, 0.0)

    rpos = ri % CHUNK
    lpos = li16 % CHUNK
    gc = g_col
    rev = g_col
    gcr = g_row
    step = 1
    while step < CHUNK:
        gc = gc + jnp.where(rpos >= step, pltpu.roll(gc, step, 0), 0.0)
        rev = rev + jnp.where(rpos < CHUNK - step, pltpu.roll(rev, rows - step, 0), 0.0)
        gcr = gcr + jnp.where(lpos >= step, pltpu.roll(gcr, step, 1), 0.0)
        step *= 2
    rev = rev - g_col
    egl_ref[...] = jnp.exp(gc + rev)

    bi = lax.broadcasted_iota(jnp.int32, (ROW_BLOCK, ROW_BLOCK), 0)
    bj = lax.broadcasted_iota(jnp.int32, (ROW_BLOCK, ROW_BLOCK), 1)
    same = (bi // CHUNK) == (bj // CHUNK)
    incl = jnp.logical_and(same, bi >= bj)
    strict = jnp.logical_and(same, bi > bj)
    eye = (bi == bj).astype(F32)

    units = [(sb, hh) for sb in range(rows // ROW_BLOCK) for hh in range(B_HEADS)]
    lows, rhss = [], []
    for sb, hh in units:
        rs = slice(sb * ROW_BLOCK, (sb + 1) * ROW_BLOCK)
        sl = slice(hh * B_DK, (hh + 1) * B_DK)
        gc_c = gc[rs, B_HEADS + hh:B_HEADS + hh + 1]
        gc_r = gcr[B_HEADS + hh:B_HEADS + hh + 1, rs]
        dec = jnp.where(incl, jnp.exp(jnp.where(incl, gc_c - gc_r, 0.0)), 0.0)
        q = act[rs, sl]
        q = q * lax.rsqrt(jnp.sum(q * q, axis=-1, keepdims=True) + EPS) * (B_DK ** -0.5)
        k = act[rs, B_QK + hh * B_DK:B_QK + (hh + 1) * B_DK]
        k = k * lax.rsqrt(jnp.sum(k * k, axis=-1, keepdims=True) + EPS)
        v = act[rs, 2 * B_QK + hh * B_DV:2 * B_QK + (hh + 1) * B_DV]
        beta = jax.nn.sigmoid(bac_ref[rs, hh:hh + 1])
        kb = k * beta
        kbf = k.astype(BF16)
        lows.append(jnp.where(strict, dec * _dot_nt(kb, kbf), 0.0))
        attn_ref[rs, sl] = (dec * _dot_nt(q, kbf)).astype(attn_ref.dtype)
        rhss.append(jnp.concatenate([v * beta, kb * jnp.exp(gc_c)], axis=1).astype(BF16))
        qg_ref[rs, sl] = (q * jnp.exp(gc_c)).astype(qg_ref.dtype)
        kd_ref[rs, sl] = (k * jnp.exp(rev[rs, B_HEADS + hh:B_HEADS + hh + 1])).astype(kd_ref.dtype)

    tinvs = [eye - low for low in lows]
    pws = lows
    sq = 2
    while sq < CHUNK:
        pws = [_dot(pw, pw) for pw in pws]
        tinvs = [tinv + _dot(tinv, pw) for tinv, pw in zip(tinvs, pws)]
        sq *= 2
    for (sb, hh), tinv, rhs in zip(units, tinvs, rhss):
        rs = slice(sb * ROW_BLOCK, (sb + 1) * ROW_BLOCK)
        sol = _dot(tinv, rhs)
        u_ref[rs, hh * B_DV:(hh + 1) * B_DV] = sol[:, :B_DV]
        w_ref[rs, hh * B_DK:(hh + 1) * B_DK] = sol[:, B_DV:].astype(w_ref.dtype)


def _b_prep(p, bac, bar, alog_r, dt_r, alog_c, dt_c, seq_len, t_valid, cstate=None, cw=None):
    n = p.shape[0]
    rows = min(seq_len, 2 * ROW_BLOCK)
    bps = seq_len // rows
    row = lambda i: (i, 0)
    fixed = lambda i: (0, 0)
    per_row = rows // SUBLANES
    conv_done = cstate is None
    wide = lambda dt: jax.ShapeDtypeStruct((n, B_V), dt)
    in_specs = [pl.BlockSpec((rows, B_CONV_CH), row)]
    args = [p]
    scratch = []
    if not conv_done:
        in_specs += [pl.BlockSpec((SUBLANES, B_CONV_CH), lambda i: (jnp.maximum(i * per_row - 1, 0), 0)),
                     pl.BlockSpec((None, SUBLANES, B_CONV_CH), lambda i: (i // bps, 0, 0)),
                     pl.BlockSpec((B_CONV, B_CONV_CH), fixed)]
        args += [p, cstate, cw]
        scratch = [pltpu.VMEM((SUBLANES + rows, B_CONV_CH), F32)]
    in_specs += [pl.BlockSpec((rows, LANES), row),
                 pl.BlockSpec((2 * SUBLANES, rows), lambda i: (0, i)),
                 pl.BlockSpec((1, LANES), fixed),
                 pl.BlockSpec((1, LANES), fixed),
                 pl.BlockSpec((2 * SUBLANES, 1), fixed),
                 pl.BlockSpec((2 * SUBLANES, 1), fixed)]
    args += [bac, bar, alog_r, dt_r, alog_c, dt_c]
    return pl.pallas_call(
        functools.partial(_b_prep_kernel, rows=rows, blocks_per_seq=bps, t_valid=t_valid, conv_done=conv_done),
        grid=(n // rows,),
        in_specs=in_specs,
        out_specs=[pl.BlockSpec((rows, B_V), row)] * 5 + [pl.BlockSpec((rows, LANES), row)],
        out_shape=[wide(BF16), wide(BF16), wide(F32), wide(BF16), wide(BF16), jax.ShapeDtypeStruct((n, LANES), F32)],
        scratch_shapes=scratch,
        compiler_params=_cparams(("parallel",)),
        name="b_prep",
    )(*args)


def _b_scan_kernel(qg_ref, kd_ref, u_ref, w_ref, attn_ref, egl_ref, z_ref, s0_ref, gout_ref, o_ref, s_ref, *, nb):
    c = pl.program_id(1)

    @pl.when(c == 0)
    def _():
        s_ref[...] = s0_ref[...]

    half = c % (ROW_BLOCK // CHUNK)
    rgrp = lax.broadcasted_iota(jnp.int32, (ROW_BLOCK, B_DV), 0) // CHUNK
    here = rgrp == half
    units = [(b, hh) for b in range(nb) for hh in range(B_HEADS)]
    head = lambda hh: slice(hh * B_DV, (hh + 1) * B_DV)
    states = [s_ref[b, hh] for b, hh in units]
    proj = [_dot(jnp.concatenate([w_ref[b, :, head(hh)], qg_ref[b, :, head(hh)]], axis=0), s)
            for (b, hh), s in zip(units, states)]
    v_new = [u_ref[b, :, head(hh)] - pr[:CHUNK] for (b, hh), pr in zip(units, proj)]
    outs = []
    for (b, hh), pr, vn in zip(units, proj, v_new):
        v_full = jnp.where(here, jnp.concatenate([vn] * (ROW_BLOCK // CHUNK), axis=0), 0.0)
        outs.append(pr[CHUNK:] + _dot(attn_ref[b, :, head(hh)], v_full))
    for (b, hh), s, vn in zip(units, states, v_new):
        decay = egl_ref[b, 0:1, B_HEADS + hh:B_HEADS + hh + 1]
        s_ref[b, hh] = s * decay + _dot_tn(kd_ref[b, :, head(hh)], vn)
    for (b, hh), o in zip(units, outs):
        gate = _silu(z_ref[b, :, head(hh)].astype(F32))
        o_ref[b, :, head(hh)] = (_rms(o, gout_ref[...]) * gate).astype(o_ref.dtype)


def _b_scan(qg, kd, u, w, attn, egl, z, s0, gout, batch, seq_len):
    nb = 4
    nchunk = seq_len // CHUNK
    v3 = lambda a: a.reshape(batch, seq_len, a.shape[-1])
    rows = lambda bi, c: (bi, c, 0)
    state = lambda bi, c: (bi, 0, 0, 0)
    wide = pl.BlockSpec((nb, CHUNK, B_V), rows)
    o, s_new = pl.pallas_call(
        functools.partial(_b_scan_kernel, nb=nb),
        grid=(batch // nb, nchunk),
        in_specs=[wide] * 5 + [pl.BlockSpec((nb, CHUNK, LANES), rows), wide,
                               pl.BlockSpec((nb, B_HEADS, B_DK, B_DV), state),
                               pl.BlockSpec((1, B_DV), lambda bi, c: (0, 0))],
        out_specs=[wide, pl.BlockSpec((nb, B_HEADS, B_DK, B_DV), state)],
        out_shape=[jax.ShapeDtypeStruct((batch, seq_len, B_V), BF16),
                   jax.ShapeDtypeStruct((batch, B_HEADS, B_DK, B_DV), F32)],
        compiler_params=_cparams(("parallel", "arbitrary")),
        name="b_scan",
    )(v3(qg), v3(kd), v3(u), v3(w), v3(attn), v3(egl), v3(z), s0, gout)
    return o.reshape(batch * seq_len, B_V), s_new


def _proj_c_kernel(x_ref, g1_ref, w_ref, cos_ref, sin_ref, q_ref, k_ref, v_ref, z_ref):
    h = _rms(x_ref[...], g1_ref[...]).astype(BF16)
    cos = cos_ref[...]
    sin = sin_ref[...]
    lane = lax.broadcasted_iota(jnp.int32, cos.shape, 1)
    first_half = (lane % C_DK) < (C_DK // 2)

    def rope(seg):
        swapped = jnp.where(first_half, pltpu.roll(seg, LANES - C_DK // 2, 1), pltpu.roll(seg, C_DK // 2, 1))
        return seg * cos + swapped * sin

    qk = jnp.dot(h, w_ref[:, 0:2 * C_QK], preferred_element_type=F32)
    for j in range(2 * C_QK // LANES):
        seg = rope(qk[:, j * LANES:(j + 1) * LANES])
        if j < C_QK // LANES:
            q_ref[:, j * LANES:(j + 1) * LANES] = seg
        else:
            jj = j - C_QK // LANES
            k_ref[:, jj * LANES:(jj + 1) * LANES] = seg * (C_DK ** -0.5)
    v_ref[...] = jnp.dot(h, w_ref[:, 2 * C_QK:2 * C_QK + C_V], preferred_element_type=F32).astype(v_ref.dtype)
    z_ref[...] = jnp.dot(h, w_ref[:, 2 * C_QK + C_V:2 * C_QK + 2 * C_V],
                         preferred_element_type=F32).astype(z_ref.dtype)


def _proj_c(x, g1, w, cos, sin):
    n = x.shape[0]
    tm = 256
    tab_blocks = cos.shape[0] // tm
    row = lambda i: (i, 0)
    fixed = lambda i: (0, 0)
    tab = (lambda i: (i % tab_blocks, 0)) if tab_blocks > 1 else fixed
    return pl.pallas_call(
        _proj_c_kernel,
        grid=(n // tm,),
        in_specs=[
            pl.BlockSpec((tm, D_MODEL), row),
            pl.BlockSpec((1, D_MODEL), fixed),
            pl.BlockSpec((D_MODEL, 2 * C_QK + 2 * C_V), lambda i: (0, W_COLS["c"] // (2 * C_QK + 2 * C_V))),
            pl.BlockSpec((tm, LANES), tab),
            pl.BlockSpec((tm, LANES), tab),
        ],
        out_specs=[pl.BlockSpec((tm, C_QK), row), pl.BlockSpec((tm, C_QK), row),
                   pl.BlockSpec((tm, C_V), row), pl.BlockSpec((tm, C_V), row)],
        out_shape=[jax.ShapeDtypeStruct((n, C_QK), F32), jax.ShapeDtypeStruct((n, C_QK), F32),
                   jax.ShapeDtypeStruct((n, C_V), BF16), jax.ShapeDtypeStruct((n, C_V), BF16)],
        compiler_params=_cparams(("parallel",)),
        name="proj_c",
    )(x, g1, w, cos, sin)


def _log_gamma(hh):
    return math.log1p(-(2.0 ** (-5.0 - hh)))


def _c_scan_kernel(q_ref, k_ref, v_ref, z_ref, r0_ref, gout_ref, o_ref, rout_ref, r_ref, *, nb, t_valid):
    c = pl.program_id(1)
    rows = ROW_BLOCK
    state_blocks = [(b, hh, slice(hh * C_DK, (hh + 1) * C_DK), slice(hh * C_DV, (hh + 1) * C_DV))
                    for b in range(nb) for hh in range(C_HEADS)]

    @pl.when(c == 0)
    def _():
        r_ref[...] = jnp.zeros_like(r_ref)
        for b, hh, rsl, csl in state_blocks:
            r_ref[b, rsl, csl] = r0_ref[b, hh]

    left = jnp.clip(t_valid - c * rows, 0, rows)
    ri = lax.broadcasted_iota(jnp.int32, (rows, rows), 0)
    ci = lax.broadcasted_iota(jnp.int32, (rows, rows), 1)
    cnt_i = jnp.minimum(ri + 1, left).astype(F32)
    cnt_j = jnp.minimum(ci + 1, left).astype(F32)
    incl = ri >= ci
    steps = jnp.where(incl, cnt_i - cnt_j, 0.0)
    cnt_col = cnt_i[:, 0:1]
    left_f = left.astype(F32)
    qk_lane = lax.broadcasted_iota(jnp.int32, (1, C_QK), 1) // C_DK
    lg_lane = jnp.zeros((1, C_QK), F32)
    for hh in range(C_HEADS):
        lg_lane = jnp.where(qk_lane == hh, _log_gamma(hh), lg_lane)
    qk_sub = lax.broadcasted_iota(jnp.int32, (C_QK, 1), 0) // C_DK
    lg_sub = jnp.zeros((C_QK, 1), F32)
    for hh in range(C_HEADS):
        lg_sub = jnp.where(qk_sub == hh, _log_gamma(hh), lg_sub)
    q_scale = jnp.exp(cnt_col * lg_lane)
    k_scale = jnp.exp((left_f - cnt_col) * lg_lane)
    r_scale = jnp.exp(left_f * lg_sub)
    row_ok = (lax.broadcasted_iota(jnp.int32, (rows, 1), 0) + c * rows) < t_valid
    diag = (lax.broadcasted_iota(jnp.int32, (C_QK, C_V), 0) // C_DK) == (
        lax.broadcasted_iota(jnp.int32, (C_QK, C_V), 1) // C_DV)

    head = lambda hh: slice(hh * C_DV, (hh + 1) * C_DV)
    decays = [jnp.where(incl, jnp.exp(steps * _log_gamma(hh)), 0.0) for hh in range(C_HEADS)]
    qs = [q_ref[b] for b in range(nb)]
    ks = [jnp.where(row_ok, k_ref[b], 0.0) for b in range(nb)]
    vs = [v_ref[b].astype(BF16) for b in range(nb)]
    rs = [r_ref[b] for b in range(nb)]
    inters = [_dot(q * q_scale, r) for q, r in zip(qs, rs)]
    units = [(b, hh) for b in range(nb) for hh in range(C_HEADS)]
    atts = [decays[hh] * _dot_nt(qs[b], jnp.where(qk_lane == hh, ks[b], 0.0)) for b, hh in units]
    outs = [inters[b][:, head(hh)] + _dot(att, vs[b][:, head(hh)]) for (b, hh), att in zip(units, atts)]
    for b in range(nb):
        r_ref[b] = rs[b] * r_scale + jnp.where(diag, _dot_tn(ks[b] * k_scale, vs[b]), 0.0)
    for (b, hh), o in zip(units, outs):
        gate = _silu(z_ref[b, :, head(hh)].astype(F32))
        o_ref[b, :, head(hh)] = (_rms(o, gout_ref[...]) * gate).astype(o_ref.dtype)

    @pl.when(c == pl.num_programs(1) - 1)
    def _():
        for b, hh, rsl, csl in state_blocks:
            rout_ref[b, hh] = r_ref[b, rsl, csl]


def _c_scan(q, k, v, z, r0, gout, batch, seq_len, t_valid):
    nb = 4
    nblk = seq_len // ROW_BLOCK
    v3 = lambda a: a.reshape(batch, seq_len, a.shape[-1])
    rows = lambda bi, c: (bi, c, 0)
    state = pl.BlockSpec((nb, C_HEADS, C_DK, C_DV), lambda bi, c: (bi, 0, 0, 0))
    o, r_new = pl.pallas_call(
        functools.partial(_c_scan_kernel, nb=nb, t_valid=t_valid),
        grid=(batch // nb, nblk),
        in_specs=[pl.BlockSpec((nb, ROW_BLOCK, C_QK), rows), pl.BlockSpec((nb, ROW_BLOCK, C_QK), rows),
                  pl.BlockSpec((nb, ROW_BLOCK, C_V), rows), pl.BlockSpec((nb, ROW_BLOCK, C_V), rows),
                  state, pl.BlockSpec((1, C_DV), lambda bi, c: (0, 0))],
        out_specs=[pl.BlockSpec((nb, ROW_BLOCK, C_V), rows), state],
        out_shape=[jax.ShapeDtypeStruct((batch, seq_len, C_V), BF16),
                   jax.ShapeDtypeStruct((batch, C_HEADS, C_DK, C_DV), F32)],
        scratch_shapes=[pltpu.VMEM((nb, C_QK, C_V), F32)],
        compiler_params=_cparams(("parallel", "arbitrary")),
        name="c_scan",
    )(v3(q), v3(k), v3(v), v3(z), r0, gout)
    return o.reshape(batch * seq_len, C_V), r_new


def _merge_kernel(x_ref, g1_ref, g2_ref, wg_ref, o0_ref, o1_ref, o2_ref, l0_ref, l1_ref, l2_ref, ob_ref, oc_ref,
                  wa_ref, wb_ref, wc_ref, wo_ref, y_ref, h2_ref, *scr, residue_major):
    x = x_ref[...]
    tm = x.shape[0]
    h = _rms(x, g1_ref[...]).astype(BF16)
    lses = [r[...].reshape(tm, LANES) for r in (l0_ref, l1_ref, l2_ref)]
    outs = [r[...].reshape(tm, A_WIDTH) for r in (o0_ref, o1_ref, o2_ref)]
    heads = []
    for hh in range(A_HEADS):
        sl = slice(hh * A_HD, (hh + 1) * A_HD)
        ls = [l[:, 32 * hh:32 * hh + 1] for l in lses]
        m = jnp.maximum(jnp.maximum(ls[0], ls[1]), ls[2])
        es = [jnp.exp(l - m) for l in ls]
        tot = es[0] + es[1] + es[2]
        acc = (es[0] / tot) * outs[0][:, sl].astype(F32)
        acc = acc + (es[1] / tot) * outs[1][:, sl].astype(F32)
        acc = acc + (es[2] / tot) * outs[2][:, sl].astype(F32)
        heads.append(acc)
    o_a = jnp.concatenate(heads, axis=1)
    if residue_major:
        o_a = _swap_row_grid(scr[0], o_a)
    o_a = o_a.astype(BF16)
    merged = None
    for gi, (o_g, w_ref) in enumerate(((o_a, wa_ref), (ob_ref[...], wb_ref), (oc_ref[...], wc_ref))):
        gate = jax.nn.sigmoid(jnp.dot(h, wg_ref[:, gi * D_MODEL:(gi + 1) * D_MODEL], preferred_element_type=F32))
        term = gate * jnp.dot(o_g, w_ref[...], preferred_element_type=F32)
        merged = term if merged is None else merged + term
    y = x + jnp.dot(merged.astype(BF16), wo_ref[...], preferred_element_type=F32)
    y_ref[...] = y
    h2_ref[...] = _rms(y, g2_ref[...]).astype(h2_ref.dtype)


def _merge(x, g1, g2, wg, o_groups, lses, o_b, o_c, wa, wb, wc, wo, seq_len, residue_major):
    n = x.shape[0]
    tm = RES * RES
    row = lambda i: (i, 0)
    fixed = lambda i: (0, 0)
    half = pl.BlockSpec((tm, A_WIDTH), row)
    wbr = pl.BlockSpec((A_WIDTH, D_MODEL), fixed)
    if residue_major:
        tiles = seq_len // tm
        grp = lambda i: (i // tiles, 0, i % tiles, 0)
        o_spec = pl.BlockSpec((None, RES, tm // RES, A_WIDTH), grp)
        lse = pl.BlockSpec((None, RES, tm // RES, LANES), grp)
        scratch = [pltpu.VMEM((A_WIDTH // LANES, tm, LANES), F32)]
    else:
        o_spec = half
        lse = pl.BlockSpec((tm, LANES), row)
        scratch = []
    return pl.pallas_call(
        functools.partial(_merge_kernel, residue_major=residue_major),
        grid=(n // tm,),
        in_specs=[pl.BlockSpec((tm, D_MODEL), row), pl.BlockSpec((1, D_MODEL), fixed),
                  pl.BlockSpec((1, D_MODEL), fixed),
                  pl.BlockSpec((D_MODEL, 3 * D_MODEL), lambda i: (0, W_COLS["gates"] // (3 * D_MODEL))),
                  o_spec, o_spec, o_spec, lse, lse, lse, half, half, wbr, wbr, wbr,
                  pl.BlockSpec((D_MODEL, D_MODEL), fixed)],
        out_specs=[pl.BlockSpec((tm, D_MODEL), row)] * 2,
        out_shape=[jax.ShapeDtypeStruct((n, D_MODEL), F32), jax.ShapeDtypeStruct((n, D_MODEL), BF16)],
        scratch_shapes=scratch,
        compiler_params=_cparams(("parallel",)),
        name="merge",
    )(x, g1, g2, wg, *o_groups, *lses, o_b, o_c, wa, wb, wc, wo)


def _ffn_kernel(x_ref, h_ref, wg_ref, wu_ref, wo_ref, y_ref, acc_scr):
    j = pl.program_id(1)

    @pl.when(j == 0)
    def _():
        acc_scr[...] = jnp.zeros_like(acc_scr)

    h = h_ref[...]
    gate = jnp.dot(h, wg_ref[...], preferred_element_type=F32)
    up = jnp.dot(h, wu_ref[...], preferred_element_type=F32)
    acc_scr[...] += jnp.dot((_silu(gate) * up).astype(BF16), wo_ref[...], preferred_element_type=F32)

    @pl.when(j == pl.num_programs(1) - 1)
    def _():
        y_ref[...] = x_ref[...] + acc_scr[...]


def _ffn(x, h, w_in, w_out):
    n = x.shape[0]
    tm = min(n, 1024)
    tf = 256
    nf = D_FF // tf
    row = lambda i, j: (i, 0)
    return pl.pallas_call(
        _ffn_kernel,
        grid=(n // tm, nf),
        in_specs=[pl.BlockSpec((tm, D_MODEL), row), pl.BlockSpec((tm, D_MODEL), row),
                  pl.BlockSpec((D_MODEL, tf), lambda i, j: (0, j)),
                  pl.BlockSpec((D_MODEL, tf), lambda i, j: (0, nf + j)),
                  pl.BlockSpec((tf, D_MODEL), lambda i, j: (j, 0))],
        out_specs=pl.BlockSpec((tm, D_MODEL), row),
        out_shape=jax.ShapeDtypeStruct((n, D_MODEL), F32),
        scratch_shapes=[pltpu.VMEM((tm, D_MODEL), F32)],
        compiler_params=_cparams(("parallel", "arbitrary")),
        name="ffn",
    )(x, h, w_in, w_in, w_out)


def _rope_tables(pos, hd, reps):
    inv = ROPE_THETA ** (-jnp.arange(0, hd, 2, dtype=F32) / hd)
    ang = pos.astype(F32)[:, None] * inv[None, :]
    cos = jnp.cos(ang)
    sin = jnp.sin(ang)
    cos2 = jnp.concatenate([cos, cos], axis=1)
    sin2 = jnp.concatenate([-sin, sin], axis=1)
    return jnp.tile(cos2, (1, reps)), jnp.tile(sin2, (1, reps))


def _pad_rows(a, batch, t, t_pad):
    if t == t_pad:
        return a
    a = a.reshape(batch, t, a.shape[-1])
    a = jnp.pad(a, ((0, 0), (0, t_pad - t), (0, 0)))
    return a.reshape(batch * t_pad, a.shape[-1])


def _unpad_rows(a, batch, t, t_pad):
    if t == t_pad:
        return a
    return a.reshape(batch, t_pad, a.shape[-1])[:, :t].reshape(batch * t, a.shape[-1])


def _layer(x, pos, batch, t, lw, caches, layer, conv_state, s0, r0):
    n = batch * t
    prompt = caches is None
    reps = max(1, 256 // t)
    cos_a, sin_a = _rope_tables(pos, A_HD, 1)
    cos_c, sin_c = _rope_tables(pos, C_DK, LANES // C_DK)
    if reps > 1:
        cos_a, sin_a, cos_c, sin_c = (jnp.tile(a, (reps, 1)) for a in (cos_a, sin_a, cos_c, sin_c))

    if prompt:
        to_rm = lambda a: a.reshape(t // RES, RES, A_HD).transpose(1, 0, 2)
        q, k, v, kv_tail = _proj_a(x, lw["g1"], lw["w_all"], lw["qn"], lw["kn"], to_rm(cos_a), to_rm(sin_a),
                                   batch, t, True)
    else:
        q, k, v = _proj_a(x, lw["g1"], lw["w_all"], lw["qn"], lw["kn"], cos_a, sin_a, batch, t, False)
    outs, lses = [], []
    for gi in range(N_GROUPS):
        if prompt:
            o, lse = _attn_prompt(q, k, v, gi, batch, t)
        else:
            o, lse = _attn_sample(q, k, v, caches[gi], layer, gi, batch, t)
        outs.append(o)
        lses.append(lse)
    new_kv = []
    for gi, (win, _) in enumerate(A_GROUPS):
        if prompt:
            keep = min(win, t)
            first = kv_tail.shape[2] - keep // RES
            rows = kv_tail[:, :, first:, 2 * gi * A_WIDTH:2 * (gi + 1) * A_WIDTH]
            new_kv.append(rows.transpose(0, 2, 1, 3).reshape(batch, keep, 2, A_HEADS, A_HD))
        else:
            cols = slice(gi * A_WIDTH, (gi + 1) * A_WIDTH)
            tail = lambda a: a[:, cols].reshape(batch, t, A_HEADS, A_HD)
            new_kv.append(jnp.stack([tail(k), tail(v)], axis=2))

    t_pad = -(-t // ROW_BLOCK) * ROW_BLOCK
    cst = jnp.pad(conv_state, ((0, 0), (SUBLANES - (B_CONV - 1), 0), (0, 0)))
    decay = (lw["alog_r"], lw["dt_r"], lw["alog_c"], lw["dt_c"])
    if t_pad == t:
        act, z_b, bac, bar, p_last = _proj_b(x, lw["g1"], lw["w_all"], lw["w_all"], lw["w_all"], lw["w_bat"],
                                             cst, lw["conv_w"], t)
        conv_new = p_last[:, -(B_CONV - 1):]
        qg, kd, u, w, attn, egl = _b_prep(act, bac, bar, *decay, t, t)
    else:
        p, z_b, bac, bar = _proj_b(x, lw["g1"], lw["w_all"], lw["w_all"], lw["w_all"], lw["w_bat"])
        conv_new = jnp.concatenate([conv_state, p.reshape(batch, t, B_CONV_CH)], axis=1)[:, -(B_CONV - 1):]
        qg, kd, u, w, attn, egl = _b_prep(
            _pad_rows(p, batch, t, t_pad), _pad_rows(bac, batch, t, t_pad), _pad_rows(bar.T, batch, t, t_pad).T,
            *decay, t_pad, t, cst, lw["conv_w"])
    o_b, s_new = _b_scan(qg, kd, u, w, attn, egl, _pad_rows(z_b, batch, t, t_pad), s0, lw["gb"], batch, t_pad)
    o_b = _unpad_rows(o_b, batch, t, t_pad)

    cq, ck, cv, cz = _proj_c(x, lw["g1"], lw["w_all"], cos_c, sin_c)
    o_c, r_new = _c_scan(*(_pad_rows(a, batch, t, t_pad) for a in (cq, ck, cv, cz)), r0, lw["gc"], batch, t_pad, t)
    o_c = _unpad_rows(o_c, batch, t, t_pad)

    x, h2 = _merge(x, lw["g1"], lw["g2"], lw["w_all"], outs, lses, o_b, o_c, lw["w_oa"], lw["w_ob"], lw["w_oc"],
                   lw["w_o"], t, prompt)
    x = _ffn(x, h2, lw["w_fi"], lw["w_fo"])
    return x, new_kv, conv_new, s_new, r_new


def _layer_weights(l, norm1_g, w_in, a_q_norm_g, a_k_norm_g, b_conv_w, b_a_log, b_dt_bias, b_out_norm_g,
                   c_out_norm_g, w_out_a, w_out_b, w_out_c, w_out, norm2_g, w_ffn_in, w_ffn_out):
    o = IN_OFFS
    wl = w_in[l]
    w_ba = wl[:, o[3]:o[5]]
    pad_r = lambda a: jnp.pad(a.reshape(1, B_HEADS), ((0, 0), (B_HEADS, LANES - 2 * B_HEADS)))
    pad_c = lambda a: jnp.pad(a.reshape(B_HEADS, 1), ((B_HEADS, 2 * SUBLANES - 2 * B_HEADS), (0, 0)))
    return dict(
        g1=norm1_g[l].reshape(1, D_MODEL), g2=norm2_g[l].reshape(1, D_MODEL),
        w_all=jnp.concatenate([wl[:, o[0]:o[1]], wl[:, o[1]:o[2]], wl[:, o[9]:o[10]], wl[:, o[5]:o[9]], wl[:, o[2]:o[3]],
                               jnp.pad(w_ba, ((0, 0), (0, LANES - 2 * B_HEADS)))], axis=1).astype(BF16),
        w_bat=jnp.pad(w_ba.T, ((0, 2 * SUBLANES - 2 * B_HEADS), (0, 0))).astype(BF16),
        qn=a_q_norm_g[l].reshape(1, A_HD), kn=a_k_norm_g[l].reshape(1, A_HD),
        conv_w=b_conv_w[l],
        alog_r=pad_r(b_a_log[l]), dt_r=pad_r(b_dt_bias[l]), alog_c=pad_c(b_a_log[l]), dt_c=pad_c(b_dt_bias[l]),
        gb=b_out_norm_g[l].reshape(1, B_DV), gc=c_out_norm_g[l].reshape(1, C_DV),
        w_oa=w_out_a[l].astype(BF16), w_ob=w_out_b[l].astype(BF16), w_oc=w_out_c[l].astype(BF16),
        w_o=w_out[l].astype(BF16), w_fi=w_ffn_in[l].astype(BF16), w_fo=w_ffn_out[l].astype(BF16),
    )


def kernel(x_prompt, x_sample, cache_a_kv0, cache_a_kv1, cache_a_kv2, state_b_conv, state_b_S, state_c_R, norm1_g, w_in, a_q_norm_g, a_k_norm_g, b_conv_w, b_a_log, b_dt_bias, b_out_norm_g, c_out_norm_g, w_out_a, w_out_b, w_out_c, w_out, norm2_g, w_ffn_in, w_ffn_out):
    bp, t = x_prompt.shape[:2]
    bs, s = x_sample.shape[:2]
    depth = w_in.shape[0]
    pos_p = jnp.arange(t)
    pos_s = PAST_LEN + jnp.arange(s)
    yp = x_prompt.reshape(bp * t, D_MODEL)
    ys = x_sample.reshape(bs * s, D_MODEL)
    caches = (cache_a_kv0, cache_a_kv1, cache_a_kv2)
    zeros_conv = jnp.zeros((bp, B_CONV - 1, B_CONV_CH), F32)
    zeros_s = jnp.zeros((bp, B_HEADS, B_DK, B_DV), F32)
    zeros_r = jnp.zeros((bp, C_HEADS, C_DK, C_DV), F32)
    acc = [[] for _ in range(12)]
    for l in range(depth):
        lw = _layer_weights(l, norm1_g, w_in, a_q_norm_g, a_k_norm_g, b_conv_w, b_a_log, b_dt_bias,
                            b_out_norm_g, c_out_norm_g, w_out_a, w_out_b, w_out_c, w_out, norm2_g,
                            w_ffn_in, w_ffn_out)
        yp, kv, cv, sn, rn = _layer(yp, pos_p, bp, t, lw, None, l, zeros_conv, zeros_s, zeros_r)
        for i, a in enumerate((kv[0], kv[1], kv[2], cv, sn, rn)):
            acc[i].append(a)
        ys, kv, cv, sn, rn = _layer(ys, pos_s, bs, s, lw, caches, l, state_b_conv[l], state_b_S[l], state_c_R[l])
        for i, a in enumerate((kv[0], kv[1], kv[2], cv, sn, rn)):
            acc[6 + i].append(a)
    return (yp.reshape(bp, t, D_MODEL), ys.reshape(bs, s, D_MODEL)) + tuple(jnp.stack(a) for a in acc)
```

```python
import functools
import math

import jax
import jax.numpy as jnp
import numpy as np
from jax import lax
from jax.experimental import pallas as pl
from jax.experimental.pallas import tpu as pltpu

F32 = jnp.float32
BF16 = jnp.bfloat16

D_MODEL = 1024
PAST_LEN = 8192
A_GROUPS = ((128, 1), (512, 4), (2048, 16))
N_GROUPS = 3
A_HEADS = 4
A_HD = 128
A_WIDTH = A_HEADS * A_HD
A_KEYS = 128
B_HEADS = 4
B_DK = 128
B_DV = 128
B_CONV = 4
B_QK = B_HEADS * B_DK
B_V = B_HEADS * B_DV
B_CONV_CH = 2 * B_QK + B_V
C_HEADS = 4
C_DK = 64
C_DV = 128
C_QK = C_HEADS * C_DK
C_V = C_HEADS * C_DV
CHUNK = 64
ROPE_THETA = 10000.0
EPS = 1e-6
D_FF = 2816
IN_SIZES = (3 * N_GROUPS * A_WIDTH, B_CONV_CH, B_V, B_HEADS, B_HEADS, C_QK, C_QK, C_V, C_V, 3 * D_MODEL)
IN_OFFS = tuple(int(v) for v in np.cumsum((0,) + IN_SIZES))

W_COLS = {"a": 0, "b_qkv": 4608, "gates": 6144, "c": 9216, "b_z": 10752, "b_ba": 11264}
W_ALL = 11392

ROW_BLOCK = 128
RES = 16
SUBLANES = 8
LANES = 128
VMEM_LIMIT = 48 * 1024 * 1024


def _cparams(sem):
    return pltpu.CompilerParams(dimension_semantics=sem, vmem_limit_bytes=VMEM_LIMIT)


def _rms(x, g):
    return x * lax.rsqrt(jnp.mean(x * x, axis=-1, keepdims=True) + EPS) * g


def _silu(x):
    return x * jax.nn.sigmoid(x)


def _softplus(x):
    return jnp.maximum(x, 0.0) + jnp.log(1.0 + jnp.exp(-jnp.abs(x)))


def _dot(a, b):
    return jnp.dot(a.astype(BF16), b.astype(BF16), preferred_element_type=F32)


def _dot_nt(a, b):
    return lax.dot_general(a.astype(BF16), b.astype(BF16), (((1,), (1,)), ((), ())), preferred_element_type=F32)


def _dot_tn(a, b):
    return lax.dot_general(a.astype(BF16), b.astype(BF16), (((0,), (0,)), ((), ())), preferred_element_type=F32)


def _swap_row_grid(scr, val):
    slabs = val.shape[1] // LANES
    for c in range(slabs):
        scr[c] = val[:, c * LANES:(c + 1) * LANES]
    cols = [jnp.concatenate([scr[c, pl.ds(r, RES, stride=RES), :] for r in range(RES)], axis=0)
            for c in range(slabs)]
    return jnp.concatenate(cols, axis=1)


def _proj_a_kernel(x_ref, g1_ref, w_ref, qg_ref, kg_ref, cos_ref, sin_ref, q_ref, k_ref, v_ref, *rest, residue_major):
    x = x_ref[...]
    tm = x.shape[0]
    if residue_major:
        tail_ref, scr = rest
        x = _swap_row_grid(scr, x)
    h = _rms(x, g1_ref[...]).astype(BF16)
    cos = cos_ref[...].reshape(tm, A_HD)
    sin = sin_ref[...].reshape(tm, A_HD)

    def norm_rope(seg, g):
        y = _rms(seg, g)
        return y * cos + pltpu.roll(y, A_HD // 2, 1) * sin

    def put(ref, col, val):
        if residue_major:
            ref[:, :, col:col + val.shape[1]] = val.reshape(RES, tm // RES, val.shape[1]).astype(ref.dtype)
        else:
            ref[:, col:col + val.shape[1]] = val

    for j in range(3 * N_GROUPS):
        acc = jnp.dot(h, w_ref[:, j * A_WIDTH:(j + 1) * A_WIDTH], preferred_element_type=F32)
        if j < N_GROUPS:
            for hh in range(A_HEADS):
                sl = slice(hh * A_HD, (hh + 1) * A_HD)
                put(q_ref, j * A_WIDTH + hh * A_HD, norm_rope(acc[:, sl], qg_ref[...]) * (A_HD ** -0.5))
        elif j < 2 * N_GROUPS:
            jj = j - N_GROUPS
            for hh in range(A_HEADS):
                sl = slice(hh * A_HD, (hh + 1) * A_HD)
                val = norm_rope(acc[:, sl], kg_ref[...])
                put(k_ref, jj * A_WIDTH + hh * A_HD, val)
                if residue_major:
                    put(tail_ref, 2 * jj * A_WIDTH + hh * A_HD, val)
        else:
            jj = j - 2 * N_GROUPS
            put(v_ref, jj * A_WIDTH, acc)
            if residue_major:
                put(tail_ref, (2 * jj + 1) * A_WIDTH, acc)


def _proj_a(x, g1, w, qg, kg, cos, sin, batch, seq_len, residue_major):
    n = x.shape[0]
    tm = RES * RES
    nw = N_GROUPS * A_WIDTH
    fixed = lambda i: (0, 0)
    common = [pl.BlockSpec((1, D_MODEL), fixed), pl.BlockSpec((D_MODEL, 3 * nw), fixed),
              pl.BlockSpec((1, A_HD), fixed), pl.BlockSpec((1, A_HD), fixed)]
    if residue_major:
        tiles = seq_len // tm
        tail_tiles = min(max(wd for wd, _ in A_GROUPS), seq_len) // tm
        blk = (None, RES, tm // RES, nw)
        tab = pl.BlockSpec((RES, tm // RES, A_HD), lambda i: (0, i % tiles, 0))
        main = pl.BlockSpec(blk, lambda i: (i // tiles, 0, i % tiles, 0))
        tail = pl.BlockSpec((None, RES, tm // RES, 2 * nw),
                            lambda i: (i // tiles, 0, jnp.maximum(i % tiles - (tiles - tail_tiles), 0), 0))
        out_specs = [main] * 3 + [tail]
        out_shape = ([jax.ShapeDtypeStruct((batch, RES, seq_len // RES, nw), BF16)] * 3
                     + [jax.ShapeDtypeStruct((batch, RES, tail_tiles * tm // RES, 2 * nw), F32)])
        scratch = [pltpu.VMEM((D_MODEL // LANES, tm, LANES), F32)]
    else:
        assert cos.shape[0] == tm
        tab = pl.BlockSpec((tm, A_HD), fixed)
        out_specs = [pl.BlockSpec((tm, nw), lambda i: (i, 0))] * 3
        out_shape = [jax.ShapeDtypeStruct((n, nw), F32)] * 3
        scratch = []
    return pl.pallas_call(
        functools.partial(_proj_a_kernel, residue_major=residue_major),
        grid=(n // tm,),
        in_specs=[pl.BlockSpec((tm, D_MODEL), lambda i: (i, 0))] + common + [tab, tab],
        out_specs=out_specs,
        out_shape=out_shape,
        scratch_shapes=scratch,
        compiler_params=_cparams(("arbitrary",)),
        name="proj_a",
    )(x, g1, w, qg, kg, cos, sin)


ATTN_SUBS = 2


def _attn_prompt_kernel(q_ref, kc_ref, kp_ref, vc_ref, vp_ref, o_ref, lse_ref, *, parts):
    n = pl.program_id(2)
    per = ROW_BLOCK // parts
    qi = lax.broadcasted_iota(jnp.int32, (ROW_BLOCK, ROW_BLOCK), 0)
    kj = lax.broadcasted_iota(jnp.int32, (ROW_BLOCK, ROW_BLOCK), 1)
    qi = parts * (qi % per) + qi // per
    kj = parts * (kj % per) + kj // per
    cur_ok = kj <= qi
    prev_ok = kj >= qi
    first_ok = jnp.logical_and(prev_ok, n > 0)
    lane = lax.broadcasted_iota(jnp.int32, (ROW_BLOCK, LANES), 1)
    neg = -jnp.inf
    packed_rows = 2 * SUBLANES

    def sub(ref, half, sl):
        full = ref[:, :, sl]
        if per % packed_rows == 0:
            return full[:, half * per:(half + 1) * per].reshape(ROW_BLOCK, A_HD)
        return full.astype(F32)[:, half * per:(half + 1) * per].reshape(ROW_BLOCK, A_HD).astype(BF16)

    units = [(half, hh) for half in range(ATTN_SUBS) for hh in range(A_HEADS)]
    head = lambda hh: slice(hh * A_HD, (hh + 1) * A_HD)

    keys = {(half, hh): sub(kc_ref, half, head(hh)) for half, hh in units}
    vals = {(half, hh): sub(vc_ref, half, head(hh)) for half, hh in units}
    for hh in range(A_HEADS):
        keys[(-1, hh)] = sub(kp_ref, ATTN_SUBS - 1, head(hh))
        vals[(-1, hh)] = sub(vp_ref, ATTN_SUBS - 1, head(hh))

    scores = []
    for half, hh in units:
        q = sub(q_ref, half, head(hh))
        s_cur = jnp.where(cur_ok, _dot_nt(q, keys[(half, hh)]), neg)
        s_prev = jnp.where(first_ok if half == 0 else prev_ok, _dot_nt(q, keys[(half - 1, hh)]), neg)
        scores.append((s_cur, s_prev))
    probs = []
    for s_cur, s_prev in scores:
        m = jnp.maximum(jnp.max(s_cur, axis=-1, keepdims=True), jnp.max(s_prev, axis=-1, keepdims=True))
        p_cur = jnp.exp(s_cur - m)
        p_prev = jnp.exp(s_prev - m)
        den = jnp.sum(p_cur, axis=-1, keepdims=True) + jnp.sum(p_prev, axis=-1, keepdims=True)
        probs.append((p_cur, p_prev, m, den))
    lse_blk = [jnp.zeros((ROW_BLOCK, LANES), F32) for _ in range(ATTN_SUBS)]
    outs = {}
    for (half, hh), (p_cur, p_prev, m, den) in zip(units, probs):
        o = (_dot(p_cur, vals[(half, hh)]) + _dot(p_prev, vals[(half - 1, hh)])) / den
        outs[(half, hh)] = o.reshape(parts, per, A_HD)
        lse_blk[half] = jnp.where(lane // 32 == hh, m + jnp.log(den), lse_blk[half])
    for hh in range(A_HEADS):
        both = jnp.concatenate([outs[(half, hh)] for half in range(ATTN_SUBS)], axis=1)
        o_ref[:, :, head(hh)] = both.astype(o_ref.dtype)
    for half in range(ATTN_SUBS):
        lse_ref[:, half * per:(half + 1) * per, :] = lse_blk[half].reshape(parts, per, LANES)


def _attn_prompt(q, k, v, gi, batch, seq_len):
    _, dil = A_GROUPS[gi]
    parts = RES // dil
    per = ROW_BLOCK // parts
    rows = seq_len // RES
    nblk = seq_len // dil // (ATTN_SUBS * ROW_BLOCK)
    split = lambda a: a.reshape(batch, parts, dil, rows, a.shape[-1])
    cur = lambda b, r, n: (b, 0, r, n, gi)
    prev = lambda b, r, n: (b, 0, r, jnp.maximum(n - 1, 0), gi)
    out = lambda b, r, n: (b, 0, r, n, 0)
    blk = (None, parts, None, ATTN_SUBS * per, A_WIDTH)
    o, lse = pl.pallas_call(
        functools.partial(_attn_prompt_kernel, parts=parts),
        grid=(batch, dil, nblk),
        in_specs=[pl.BlockSpec(blk, cur), pl.BlockSpec(blk, cur), pl.BlockSpec(blk, prev),
                  pl.BlockSpec(blk, cur), pl.BlockSpec(blk, prev)],
        out_specs=[pl.BlockSpec(blk, out), pl.BlockSpec((None, parts, None, ATTN_SUBS * per, LANES), out)],
        out_shape=[jax.ShapeDtypeStruct((batch, parts, dil, rows, A_WIDTH), BF16),
                   jax.ShapeDtypeStruct((batch, parts, dil, rows, LANES), F32)],
        compiler_params=_cparams(("parallel", "parallel", "arbitrary")),
        name=f"attn_prompt_g{gi}",
    )(split(q), split(k), split(k), split(v), split(v))
    return o.reshape(batch, RES, rows, A_WIDTH), lse.reshape(batch, RES, rows, LANES)


def _attn_sample_kernel(q_ref, kn_ref, vn_ref, cache_ref, o_ref, lse_ref, *, dil, n_new):
    n_res = min(dil, n_new)
    neg = -jnp.inf
    srow = lax.broadcasted_iota(jnp.int32, (n_new, A_KEYS), 0)
    mcol = lax.broadcasted_iota(jnp.int32, (n_new, A_KEYS), 1)
    in_window = mcol >= srow // dil
    row_res = [srow % dil == res for res in range(n_res)]
    srow_n = lax.broadcasted_iota(jnp.int32, (n_new, n_new), 0)
    tcol_n = lax.broadcasted_iota(jnp.int32, (n_new, n_new), 1)
    new_ok = jnp.logical_and(tcol_n <= srow_n, (srow_n - tcol_n) % dil == 0)
    heads = range(A_HEADS)

    qs = [q_ref[hh].astype(BF16) for hh in heads]
    keys = {(res, hh): cache_ref[:, res, 0, hh, :].astype(BF16) for res in range(n_res) for hh in heads}
    vals = {(res, hh): cache_ref[:, res, 1, hh, :].astype(BF16) for res in range(n_res) for hh in heads}
    raw = {key: _dot_nt(qs[key[1]], kmat) for key, kmat in keys.items()}
    s_new = [jnp.where(new_ok, _dot_nt(qs[hh], kn_ref[hh]), neg) for hh in heads]
    probs = []
    for hh in heads:
        s_buf = raw[(0, hh)]
        for res in range(1, n_res):
            s_buf = jnp.where(row_res[res], raw[(res, hh)], s_buf)
        s_buf = jnp.where(in_window, s_buf, neg)
        m = jnp.maximum(jnp.max(s_buf, axis=-1, keepdims=True), jnp.max(s_new[hh], axis=-1, keepdims=True))
        p_buf = jnp.exp(s_buf - m)
        p_new = jnp.exp(s_new[hh] - m)
        den = jnp.sum(p_buf, axis=-1, keepdims=True) + jnp.sum(p_new, axis=-1, keepdims=True)
        probs.append((p_buf, p_new, m, den))
    for hh in heads:
        p_buf, p_new, m, den = probs[hh]
        acc = _dot(p_new, vn_ref[hh])
        for res in range(n_res):
            p_res = p_buf if n_res == 1 else jnp.where(row_res[res], p_buf, 0.0)
            acc = acc + _dot(p_res, vals[(res, hh)])
        o_ref[hh] = acc / den
        lse_ref[hh] = jnp.broadcast_to(m + jnp.log(den), (n_new, LANES))


def _attn_sample_rows_kernel(q_ref, kn_ref, vn_ref, cache_ref, o_ref, lse_ref, *, dil, n_new):
    row = lax.broadcasted_iota(jnp.int32, (A_KEYS, A_HEADS, 1), 0)
    trow = lax.broadcasted_iota(jnp.int32, (n_new, A_HEADS, 1), 0)
    neg = -jnp.inf
    kn = kn_ref[...]
    vn = vn_ref[...]
    for s in range(n_new):
        res = s % dil
        first = s // dil
        q = q_ref[s][None]
        kc = cache_ref[:, res, 0]
        vc = cache_ref[:, res, 1]
        sc = jnp.sum(kc * q, axis=-1, keepdims=True)
        if first > 0:
            sc = jnp.where(row >= first, sc, neg)
        new_ok = jnp.logical_and(trow <= s, (s - trow) % dil == 0)
        sn = jnp.where(new_ok, jnp.sum(kn * q, axis=-1, keepdims=True), neg)
        m = jnp.maximum(jnp.max(sc, axis=0, keepdims=True), jnp.max(sn, axis=0, keepdims=True))
        pc = jnp.exp(sc - m)
        pn = jnp.exp(sn - m)
        den = jnp.sum(pc, axis=0, keepdims=True) + jnp.sum(pn, axis=0, keepdims=True)
        o = (jnp.sum(pc * vc, axis=0, keepdims=True) + jnp.sum(pn * vn, axis=0, keepdims=True)) / den
        o_ref[s] = o[0]
        lse_ref[s] = jnp.broadcast_to((m + jnp.log(den))[0], (A_HEADS, A_HD))


def _attn_sample(q, k, v, cache, layer, gi, batch, n_new):
    win, dil = A_GROUPS[gi]
    depth = cache.shape[0]
    assert cache.shape[2] == win and win // dil == A_KEYS
    n_res = min(dil, n_new)
    cv = cache.reshape(depth, batch, A_KEYS, dil, 2, A_HEADS, A_HD)
    cache_spec = pl.BlockSpec((None, None, A_KEYS, n_res, 2, A_HEADS, A_HD), lambda b: (layer, b, 0, 0, 0, 0, 0))
    if n_res > 1:
        heads = lambda a: a.reshape(batch, n_new, N_GROUPS, A_HEADS, A_HD)
        new = pl.BlockSpec((None, n_new, None, A_HEADS, A_HD), lambda b: (b, 0, gi, 0, 0))
        out = pl.BlockSpec((None, n_new, A_HEADS, A_HD), lambda b: (b, 0, 0, 0))
        o, lse = pl.pallas_call(
            functools.partial(_attn_sample_rows_kernel, dil=dil, n_new=n_new),
            grid=(batch,),
            in_specs=[new, new, new, cache_spec],
            out_specs=[out, out],
            out_shape=[jax.ShapeDtypeStruct((batch, n_new, A_HEADS, A_HD), F32)] * 2,
            compiler_params=_cparams(("parallel",)),
            name=f"attn_sample_g{gi}",
        )(heads(q), heads(k), heads(v), cv)
        lse = jnp.repeat(lse[..., 0], LANES // A_HEADS, axis=-1)
        return o.reshape(batch * n_new, A_WIDTH), lse.reshape(batch * n_new, LANES)
    heads = lambda a: a.reshape(batch, n_new, N_GROUPS, A_HEADS, A_HD).transpose(0, 2, 3, 1, 4)
    new = pl.BlockSpec((None, None, A_HEADS, n_new, A_HD), lambda b: (b, gi, 0, 0, 0))
    out = pl.BlockSpec((None, A_HEADS, n_new, LANES), lambda b: (b, 0, 0, 0))
    o, lse = pl.pallas_call(
        functools.partial(_attn_sample_kernel, dil=dil, n_new=n_new),
        grid=(batch,),
        in_specs=[new, new, new,
                  pl.BlockSpec((None, None, A_KEYS, n_res, 2, A_HEADS, A_HD), lambda b: (layer, b, 0, 0, 0, 0, 0))],
        out_specs=[out, out],
        out_shape=[jax.ShapeDtypeStruct((batch, A_HEADS, n_new, A_HD), F32)] * 2,
        compiler_params=_cparams(("parallel",)),
        name=f"attn_sample_g{gi}",
    )(heads(q), heads(k), heads(v), cv)
    o = o.transpose(0, 2, 1, 3)
    lse = jnp.repeat(lse[..., 0].transpose(0, 2, 1), LANES // A_HEADS, axis=-1)
    return o.reshape(batch * n_new, A_WIDTH), lse.reshape(batch * n_new, LANES)


def _causal_conv(e_scr, cw_ref, rows):
    xc = e_scr[SUBLANES:SUBLANES + rows, :] * cw_ref[B_CONV - 1:B_CONV, :]
    for kk in range(1, B_CONV):
        xc = xc + e_scr[SUBLANES - kk:SUBLANES - kk + rows, :] * cw_ref[B_CONV - 1 - kk:B_CONV - kk, :]
    return xc


def _proj_b_kernel(x_ref, g1_ref, wqkv_ref, wz_ref, wba_ref, wbat_ref, *rest, fuse_conv, tiles_per_seq):
    h = _rms(x_ref[...], g1_ref[...]).astype(BF16)
    tm = h.shape[0]
    if fuse_conv:
        cst_ref, cw_ref, p_ref, z_ref, bac_ref, bar_ref, ptail_ref, e_scr = rest
        first = pl.program_id(0) % tiles_per_seq == 0

        @pl.when(first)
        def _():
            e_scr[0:SUBLANES, :] = cst_ref[...]

        @pl.when(jnp.logical_not(first))
        def _():
            e_scr[0:SUBLANES, :] = e_scr[tm:tm + SUBLANES, :]

        for j in range(3):
            sl = slice(j * B_QK, (j + 1) * B_QK)
            e_scr[SUBLANES:SUBLANES + tm, sl] = jnp.dot(h, wqkv_ref[:, sl], preferred_element_type=F32)
        p_ref[...] = _silu(_causal_conv(e_scr, cw_ref, tm)).astype(p_ref.dtype)
        ptail_ref[...] = e_scr[tm:tm + SUBLANES, :]
    else:
        p_ref, z_ref, bac_ref, bar_ref = rest
        for j in range(3):
            sl = slice(j * B_QK, (j + 1) * B_QK)
            p_ref[:, sl] = jnp.dot(h, wqkv_ref[:, sl], preferred_element_type=F32)
    z_ref[...] = jnp.dot(h, wz_ref[...], preferred_element_type=F32).astype(z_ref.dtype)
    bac_ref[...] = jnp.dot(h, wba_ref[...], preferred_element_type=F32)
    bar_ref[...] = lax.dot_general(wbat_ref[...], h, (((1,), (1,)), ((), ())), preferred_element_type=F32)


def _proj_b(x, g1, wqkv, wz, wba, wbat, cstate=None, conv_w=None, seq_len=None):
    n = x.shape[0]
    tm = 256
    row = lambda i: (i, 0)
    fixed = lambda i: (0, 0)
    fuse_conv = cstate is not None
    in_specs = [
        pl.BlockSpec((tm, D_MODEL), row),
        pl.BlockSpec((1, D_MODEL), fixed),
        pl.BlockSpec((D_MODEL, B_CONV_CH), lambda i: (0, W_COLS["b_qkv"] // B_CONV_CH)),
        pl.BlockSpec((D_MODEL, B_V), lambda i: (0, W_COLS["b_z"] // B_V)),
        pl.BlockSpec((D_MODEL, LANES), lambda i: (0, W_COLS["b_ba"] // LANES)),
        pl.BlockSpec((2 * SUBLANES, D_MODEL), fixed),
    ]
    out_specs = [pl.BlockSpec((tm, B_CONV_CH), row), pl.BlockSpec((tm, B_V), row),
                 pl.BlockSpec((tm, LANES), row), pl.BlockSpec((2 * SUBLANES, tm), lambda i: (0, i))]
    out_shape = [jax.ShapeDtypeStruct((n, B_CONV_CH), BF16 if fuse_conv else F32),
                 jax.ShapeDtypeStruct((n, B_V), BF16),
                 jax.ShapeDtypeStruct((n, LANES), F32), jax.ShapeDtypeStruct((2 * SUBLANES, n), F32)]
    args = [x, g1, wqkv, wz, wba, wbat]
    scratch = []
    tiles = None
    if fuse_conv:
        tiles = seq_len // tm
        per_seq = pl.BlockSpec((None, SUBLANES, B_CONV_CH), lambda i: (i // tiles, 0, 0))
        in_specs += [per_seq, pl.BlockSpec((B_CONV, B_CONV_CH), fixed)]
        out_specs.append(per_seq)
        out_shape.append(jax.ShapeDtypeStruct((n // seq_len, SUBLANES, B_CONV_CH), F32))
        args += [cstate, conv_w]
        scratch = [pltpu.VMEM((SUBLANES + tm, B_CONV_CH), F32)]
    return pl.pallas_call(
        functools.partial(_proj_b_kernel, fuse_conv=fuse_conv, tiles_per_seq=tiles),
        grid=(n // tm,),
        in_specs=in_specs,
        out_specs=out_specs,
        out_shape=out_shape,
        scratch_shapes=scratch,
        compiler_params=_cparams(("arbitrary",) if fuse_conv else ("parallel",)),
        name="proj_b",
    )(*args)


def _b_prep_kernel(*refs, rows, blocks_per_seq, t_valid, conv_done):
    i = pl.program_id(0)
    blk = i % blocks_per_seq
    if conv_done:
        (p_ref, bac_ref, bar_ref, alog_r_ref, dt_r_ref, alog_c_ref, dt_c_ref,
         qg_ref, kd_ref, u_ref, w_ref, attn_ref, egl_ref) = refs
        act = p_ref[...].astype(F32)
    else:
        (p_ref, halo_ref, cst_ref, cw_ref, bac_ref, bar_ref, alog_r_ref, dt_r_ref, alog_c_ref, dt_c_ref,
         qg_ref, kd_ref, u_ref, w_ref, attn_ref, egl_ref, e_scr) = refs
        e_scr[0:SUBLANES, :] = jnp.where(blk == 0, cst_ref[...], halo_ref[...])
        e_scr[SUBLANES:SUBLANES + rows, :] = p_ref[...]
        act = _silu(_causal_conv(e_scr, cw_ref, rows))

    ri = lax.broadcasted_iota(jnp.int32, (rows, LANES), 0)
    li16 = lax.broadcasted_iota(jnp.int32, (2 * SUBLANES, rows), 1)
    li1 = lax.broadcasted_iota(jnp.int32, (1, LANES), 1)
    masked = t_valid < blocks_per_seq * rows
    if masked:
        row_ok = (blk * rows + ri) < t_valid
        col_ok = (blk * rows + li16) < t_valid
        act = jnp.where(ri[:, 0:1] + blk * rows < t_valid, act, 0.0)

    head_lane = jnp.logical_and(li1 >= B_HEADS, li1 < 2 * B_HEADS)
    a_r = jnp.where(head_lane, -jnp.exp(alog_r_ref[...]), 0.0)
    g_col = a_r * _softplus(bac_ref[...] + dt_r_ref[...])
    si = lax.broadcasted_iota(jnp.int32, (2 * SUBLANES, 1), 0)
    head_sub = jnp.logical_and(si >= B_HEADS, si < 2 * B_HEADS)
    a_c = jnp.where(head_sub, -jnp.exp(alog_c_ref[...]), 0.0)
    g_row = a_c * _softplus(bar_ref[...] + dt_c_ref[...])
    if masked:
        g_col = jnp.where(row_ok, g_col, 0.0)
        g_row = jnp.where(col_ok, g_row, 0.0)

    rpos = ri % CHUNK
    lpos = li16 % CHUNK
    gc = g_col
    rev = g_col
    gcr = g_row
    step = 1
    while step < CHUNK:
        gc = gc + jnp.where(rpos >= step, pltpu.roll(gc, step, 0), 0.0)
        rev = rev + jnp.where(rpos < CHUNK - step, pltpu.roll(rev, rows - step, 0), 0.0)
        gcr = gcr + jnp.where(lpos >= step, pltpu.roll(gcr, step, 1), 0.0)
        step *= 2
    rev = rev - g_col
    egl_ref[...] = jnp.exp(gc + rev)

    bi = lax.broadcasted_iota(jnp.int32, (ROW_BLOCK, ROW_BLOCK), 0)
    bj = lax.broadcasted_iota(jnp.int32, (ROW_BLOCK, ROW_BLOCK), 1)
    same = (bi // CHUNK) == (bj // CHUNK)
    incl = jnp.logical_and(same, bi >= bj)
    strict = jnp.logical_and(same, bi > bj)
    eye = (bi == bj).astype(F32)

    units = [(sb, hh) for sb in range(rows // ROW_BLOCK) for hh in range(B_HEADS)]
    lows, rhss = [], []
    for sb, hh in units:
        rs = slice(sb * ROW_BLOCK, (sb + 1) * ROW_BLOCK)
        sl = slice(hh * B_DK, (hh + 1) * B_DK)
        gc_c = gc[rs, B_HEADS + hh:B_HEADS + hh + 1]
        gc_r = gcr[B_HEADS + hh:B_HEADS + hh + 1, rs]
        dec = jnp.where(incl, jnp.exp(jnp.where(incl, gc_c - gc_r, 0.0)), 0.0)
        q = act[rs, sl]
        q = q * lax.rsqrt(jnp.sum(q * q, axis=-1, keepdims=True) + EPS) * (B_DK ** -0.5)
        k = act[rs, B_QK + hh * B_DK:B_QK + (hh + 1) * B_DK]
        k = k * lax.rsqrt(jnp.sum(k * k, axis=-1, keepdims=True) + EPS)
        v = act[rs, 2 * B_QK + hh * B_DV:2 * B_QK + (hh + 1) * B_DV]
        beta = jax.nn.sigmoid(bac_ref[rs, hh:hh + 1])
        kb = k * beta
        kbf = k.astype(BF16)
        lows.append(jnp.where(strict, dec * _dot_nt(kb, kbf), 0.0))
        attn_ref[rs, sl] = (dec * _dot_nt(q, kbf)).astype(attn_ref.dtype)
        rhss.append(jnp.concatenate([v * beta, kb * jnp.exp(gc_c)], axis=1).astype(BF16))
        qg_ref[rs, sl] = (q * jnp.exp(gc_c)).astype(qg_ref.dtype)
        kd_ref[rs, sl] = (k * jnp.exp(rev[rs, B_HEADS + hh:B_HEADS + hh + 1])).astype(kd_ref.dtype)

    tinvs = [eye - low for low in lows]
    pws = lows
    sq = 2
    while sq < CHUNK:
        pws = [_dot(pw, pw) for pw in pws]
        tinvs = [tinv + _dot(tinv, pw) for tinv, pw in zip(tinvs, pws)]
        sq *= 2
    for (sb, hh), tinv, rhs in zip(units, tinvs, rhss):
        rs = slice(sb * ROW_BLOCK, (sb + 1) * ROW_BLOCK)
        sol = _dot(tinv, rhs)
        u_ref[rs, hh * B_DV:(hh + 1) * B_DV] = sol[:, :B_DV]
        w_ref[rs, hh * B_DK:(hh + 1) * B_DK] = sol[:, B_DV:].astype(w_ref.dtype)


def _b_prep(p, bac, bar, alog_r, dt_r, alog_c, dt_c, seq_len, t_valid, cstate=None, cw=None):
    n = p.shape[0]
    rows = min(seq_len, 2 * ROW_BLOCK)
    bps = seq_len // rows
    row = lambda i: (i, 0)
    fixed = lambda i: (0, 0)
    per_row = rows // SUBLANES
    conv_done = cstate is None
    wide = lambda dt: jax.ShapeDtypeStruct((n, B_V), dt)
    in_specs = [pl.BlockSpec((rows, B_CONV_CH), row)]
    args = [p]
    scratch = []
    if not conv_done:
        in_specs += [pl.BlockSpec((SUBLANES, B_CONV_CH), lambda i: (jnp.maximum(i * per_row - 1, 0), 0)),
                     pl.BlockSpec((None, SUBLANES, B_CONV_CH), lambda i: (i // bps, 0, 0)),
                     pl.BlockSpec((B_CONV, B_CONV_CH), fixed)]
        args += [p, cstate, cw]
        scratch = [pltpu.VMEM((SUBLANES + rows, B_CONV_CH), F32)]
    in_specs += [pl.BlockSpec((rows, LANES), row),
                 pl.BlockSpec((2 * SUBLANES, rows), lambda i: (0, i)),
                 pl.BlockSpec((1, LANES), fixed),
                 pl.BlockSpec((1, LANES), fixed),
                 pl.BlockSpec((2 * SUBLANES, 1), fixed),
                 pl.BlockSpec((2 * SUBLANES, 1), fixed)]
    args += [bac, bar, alog_r, dt_r, alog_c, dt_c]
    return pl.pallas_call(
        functools.partial(_b_prep_kernel, rows=rows, blocks_per_seq=bps, t_valid=t_valid, conv_done=conv_done),
        grid=(n // rows,),
        in_specs=in_specs,
        out_specs=[pl.BlockSpec((rows, B_V), row)] * 5 + [pl.BlockSpec((rows, LANES), row)],
        out_shape=[wide(BF16), wide(BF16), wide(F32), wide(BF16), wide(BF16), jax.ShapeDtypeStruct((n, LANES), F32)],
        scratch_shapes=scratch,
        compiler_params=_cparams(("parallel",)),
        name="b_prep",
    )(*args)


def _b_scan_kernel(qg_ref, kd_ref, u_ref, w_ref, attn_ref, egl_ref, z_ref, s0_ref, gout_ref, o_ref, s_ref, *, nb):
    c = pl.program_id(1)

    @pl.when(c == 0)
    def _():
        s_ref[...] = s0_ref[...]

    half = c % (ROW_BLOCK // CHUNK)
    rgrp = lax.broadcasted_iota(jnp.int32, (ROW_BLOCK, B_DV), 0) // CHUNK
    here = rgrp == half
    units = [(b, hh) for b in range(nb) for hh in range(B_HEADS)]
    head = lambda hh: slice(hh * B_DV, (hh + 1) * B_DV)
    states = [s_ref[b, hh] for b, hh in units]
    proj = [_dot(jnp.concatenate([w_ref[b, :, head(hh)], qg_ref[b, :, head(hh)]], axis=0), s)
            for (b, hh), s in zip(units, states)]
    v_new = [u_ref[b, :, head(hh)] - pr[:CHUNK] for (b, hh), pr in zip(units, proj)]
    outs = []
    for (b, hh), pr, vn in zip(units, proj, v_new):
        v_full = jnp.where(here, jnp.concatenate([vn] * (ROW_BLOCK // CHUNK), axis=0), 0.0)
        outs.append(pr[CHUNK:] + _dot(attn_ref[b, :, head(hh)], v_full))
    for (b, hh), s, vn in zip(units, states, v_new):
        decay = egl_ref[b, 0:1, B_HEADS + hh:B_HEADS + hh + 1]
        s_ref[b, hh] = s * decay + _dot_tn(kd_ref[b, :, head(hh)], vn)
    for (b, hh), o in zip(units, outs):
        gate = _silu(z_ref[b, :, head(hh)].astype(F32))
        o_ref[b, :, head(hh)] = (_rms(o, gout_ref[...]) * gate).astype(o_ref.dtype)


def _b_scan(qg, kd, u, w, attn, egl, z, s0, gout, batch, seq_len):
    nb = 4
    nchunk = seq_len // CHUNK
    v3 = lambda a: a.reshape(batch, seq_len, a.shape[-1])
    rows = lambda bi, c: (bi, c, 0)
    state = lambda bi, c: (bi, 0, 0, 0)
    wide = pl.BlockSpec((nb, CHUNK, B_V), rows)
    o, s_new = pl.pallas_call(
        functools.partial(_b_scan_kernel, nb=nb),
        grid=(batch // nb, nchunk),
        in_specs=[wide] * 5 + [pl.BlockSpec((nb, CHUNK, LANES), rows), wide,
                               pl.BlockSpec((nb, B_HEADS, B_DK, B_DV), state),
                               pl.BlockSpec((1, B_DV), lambda bi, c: (0, 0))],
        out_specs=[wide, pl.BlockSpec((nb, B_HEADS, B_DK, B_DV), state)],
        out_shape=[jax.ShapeDtypeStruct((batch, seq_len, B_V), BF16),
                   jax.ShapeDtypeStruct((batch, B_HEADS, B_DK, B_DV), F32)],
        compiler_params=_cparams(("parallel", "arbitrary")),
        name="b_scan",
    )(v3(qg), v3(kd), v3(u), v3(w), v3(attn), v3(egl), v3(z), s0, gout)
    return o.reshape(batch * seq_len, B_V), s_new


def _proj_c_kernel(x_ref, g1_ref, w_ref, cos_ref, sin_ref, q_ref, k_ref, v_ref, z_ref):
    h = _rms(x_ref[...], g1_ref[...]).astype(BF16)
    cos = cos_ref[...]
    sin = sin_ref[...]
    lane = lax.broadcasted_iota(jnp.int32, cos.shape, 1)
    first_half = (lane % C_DK) < (C_DK // 2)

    def rope(seg):
        swapped = jnp.where(first_half, pltpu.roll(seg, LANES - C_DK // 2, 1), pltpu.roll(seg, C_DK // 2, 1))
        return seg * cos + swapped * sin

    qk = jnp.dot(h, w_ref[:, 0:2 * C_QK], preferred_element_type=F32)
    for j in range(2 * C_QK // LANES):
        seg = rope(qk[:, j * LANES:(j + 1) * LANES])
        if j < C_QK // LANES:
            q_ref[:, j * LANES:(j + 1) * LANES] = seg
        else:
            jj = j - C_QK // LANES
            k_ref[:, jj * LANES:(jj + 1) * LANES] = seg * (C_DK ** -0.5)
    v_ref[...] = jnp.dot(h, w_ref[:, 2 * C_QK:2 * C_QK + C_V], preferred_element_type=F32).astype(v_ref.dtype)
    z_ref[...] = jnp.dot(h, w_ref[:, 2 * C_QK + C_V:2 * C_QK + 2 * C_V],
                         preferred_element_type=F32).astype(z_ref.dtype)


def _proj_c(x, g1, w, cos, sin):
    n = x.shape[0]
    tm = 256
    tab_blocks = cos.shape[0] // tm
    row = lambda i: (i, 0)
    fixed = lambda i: (0, 0)
    tab = (lambda i: (i % tab_blocks, 0)) if tab_blocks > 1 else fixed
    return pl.pallas_call(
        _proj_c_kernel,
        grid=(n // tm,),
        in_specs=[
            pl.BlockSpec((tm, D_MODEL), row),
            pl.BlockSpec((1, D_MODEL), fixed),
            pl.BlockSpec((D_MODEL, 2 * C_QK + 2 * C_V), lambda i: (0, W_COLS["c"] // (2 * C_QK + 2 * C_V))),
            pl.BlockSpec((tm, LANES), tab),
            pl.BlockSpec((tm, LANES), tab),
        ],
        out_specs=[pl.BlockSpec((tm, C_QK), row), pl.BlockSpec((tm, C_QK), row),
                   pl.BlockSpec((tm, C_V), row), pl.BlockSpec((tm, C_V), row)],
        out_shape=[jax.ShapeDtypeStruct((n, C_QK), F32), jax.ShapeDtypeStruct((n, C_QK), F32),
                   jax.ShapeDtypeStruct((n, C_V), BF16), jax.ShapeDtypeStruct((n, C_V), BF16)],
        compiler_params=_cparams(("parallel",)),
        name="proj_c",
    )(x, g1, w, cos, sin)


def _log_gamma(hh):
    return math.log1p(-(2.0 ** (-5.0 - hh)))


def _c_scan_kernel(q_ref, k_ref, v_ref, z_ref, r0_ref, gout_ref, o_ref, rout_ref, r_ref, *, nb, t_valid):
    c = pl.program_id(1)
    rows = ROW_BLOCK
    state_blocks = [(b, hh, slice(hh * C_DK, (hh + 1) * C_DK), slice(hh * C_DV, (hh + 1) * C_DV))
                    for b in range(nb) for hh in range(C_HEADS)]

    @pl.when(c == 0)
    def _():
        r_ref[...] = jnp.zeros_like(r_ref)
        for b, hh, rsl, csl in state_blocks:
            r_ref[b, rsl, csl] = r0_ref[b, hh]

    left = jnp.clip(t_valid - c * rows, 0, rows)
    ri = lax.broadcasted_iota(jnp.int32, (rows, rows), 0)
    ci = lax.broadcasted_iota(jnp.int32, (rows, rows), 1)
    cnt_i = jnp.minimum(ri + 1, left).astype(F32)
    cnt_j = jnp.minimum(ci + 1, left).astype(F32)
    incl = ri >= ci
    steps = jnp.where(incl, cnt_i - cnt_j, 0.0)
    cnt_col = cnt_i[:, 0:1]
    left_f = left.astype(F32)
    qk_lane = lax.broadcasted_iota(jnp.int32, (1, C_QK), 1) // C_DK
    lg_lane = jnp.zeros((1, C_QK), F32)
    for hh in range(C_HEADS):
        lg_lane = jnp.where(qk_lane == hh, _log_gamma(hh), lg_lane)
    qk_sub = lax.broadcasted_iota(jnp.int32, (C_QK, 1), 0) // C_DK
    lg_sub = jnp.zeros((C_QK, 1), F32)
    for hh in range(C_HEADS):
        lg_sub = jnp.where(qk_sub == hh, _log_gamma(hh), lg_sub)
    q_scale = jnp.exp(cnt_col * lg_lane)
    k_scale = jnp.exp((left_f - cnt_col) * lg_lane)
    r_scale = jnp.exp(left_f * lg_sub)
    row_ok = (lax.broadcasted_iota(jnp.int32, (rows, 1), 0) + c * rows) < t_valid
    diag = (lax.broadcasted_iota(jnp.int32, (C_QK, C_V), 0) // C_DK) == (
        lax.broadcasted_iota(jnp.int32, (C_QK, C_V), 1) // C_DV)

    head = lambda hh: slice(hh * C_DV, (hh + 1) * C_DV)
    decays = [jnp.where(incl, jnp.exp(steps * _log_gamma(hh)), 0.0) for hh in range(C_HEADS)]
    qs = [q_ref[b] for b in range(nb)]
    ks = [jnp.where(row_ok, k_ref[b], 0.0) for b in range(nb)]
    vs = [v_ref[b].astype(BF16) for b in range(nb)]
    rs = [r_ref[b] for b in range(nb)]
    inters = [_dot(q * q_scale, r) for q, r in zip(qs, rs)]
    units = [(b, hh) for b in range(nb) for hh in range(C_HEADS)]
    atts = [decays[hh] * _dot_nt(qs[b], jnp.where(qk_lane == hh, ks[b], 0.0)) for b, hh in units]
    outs = [inters[b][:, head(hh)] + _dot(att, vs[b][:, head(hh)]) for (b, hh), att in zip(units, atts)]
    for b in range(nb):
        r_ref[b] = rs[b] * r_scale + jnp.where(diag, _dot_tn(ks[b] * k_scale, vs[b]), 0.0)
    for (b, hh), o in zip(units, outs):
        gate = _silu(z_ref[b, :, head(hh)].astype(F32))
        o_ref[b, :, head(hh)] = (_rms(o, gout_ref[...]) * gate).astype(o_ref.dtype)

    @pl.when(c == pl.num_programs(1) - 1)
    def _():
        for b, hh, rsl, csl in state_blocks:
            rout_ref[b, hh] = r_ref[b, rsl, csl]


def _c_scan(q, k, v, z, r0, gout, batch, seq_len, t_valid):
    nb = 4
    nblk = seq_len // ROW_BLOCK
    v3 = lambda a: a.reshape(batch, seq_len, a.shape[-1])
    rows = lambda bi, c: (bi, c, 0)
    state = pl.BlockSpec((nb, C_HEADS, C_DK, C_DV), lambda bi, c: (bi, 0, 0, 0))
    o, r_new = pl.pallas_call(
        functools.partial(_c_scan_kernel, nb=nb, t_valid=t_valid),
        grid=(batch // nb, nblk),
        in_specs=[pl.BlockSpec((nb, ROW_BLOCK, C_QK), rows), pl.BlockSpec((nb, ROW_BLOCK, C_QK), rows),
                  pl.BlockSpec((nb, ROW_BLOCK, C_V), rows), pl.BlockSpec((nb, ROW_BLOCK, C_V), rows),
                  state, pl.BlockSpec((1, C_DV), lambda bi, c: (0, 0))],
        out_specs=[pl.BlockSpec((nb, ROW_BLOCK, C_V), rows), state],
        out_shape=[jax.ShapeDtypeStruct((batch, seq_len, C_V), BF16),
                   jax.ShapeDtypeStruct((batch, C_HEADS, C_DK, C_DV), F32)],
        scratch_shapes=[pltpu.VMEM((nb, C_QK, C_V), F32)],
        compiler_params=_cparams(("parallel", "arbitrary")),
        name="c_scan",
    )(v3(q), v3(k), v3(v), v3(z), r0, gout)
    return o.reshape(batch * seq_len, C_V), r_new


def _b_short_kernel(p_ref, st_ref, bac_ref, bar_ref, cw_ref, alog_r_ref, dt_r_ref, alog_c_ref, dt_c_ref,
                    z_ref, s0_ref, gout_ref, o_ref, s_ref, e_new, e_old, *, t):
    rows = ROW_BLOCK
    nseq = rows // t
    e_new[0:SUBLANES, :] = jnp.zeros((SUBLANES, B_CONV_CH), F32)
    e_new[SUBLANES:SUBLANES + rows, :] = p_ref[...]
    e_old[0:rows, :] = st_ref[...]
    e_old[rows:rows + SUBLANES, :] = jnp.zeros((SUBLANES, B_CONV_CH), F32)
    pos = lax.broadcasted_iota(jnp.int32, (rows, 1), 0) % t
    xc = e_new[SUBLANES:SUBLANES + rows, :] * cw_ref[B_CONV - 1:B_CONV, :]
    for kk in range(1, B_CONV):
        window = slice(SUBLANES - kk, SUBLANES - kk + rows)
        src = jnp.where(pos >= kk, e_new[window, :], e_old[window, :])
        xc = xc + src * cw_ref[B_CONV - 1 - kk:B_CONV - kk, :]
    act = _silu(xc)

    ri = lax.broadcasted_iota(jnp.int32, (rows, LANES), 0)
    li16 = lax.broadcasted_iota(jnp.int32, (2 * SUBLANES, rows), 1)
    li1 = lax.broadcasted_iota(jnp.int32, (1, LANES), 1)
    head_lane = jnp.logical_and(li1 >= B_HEADS, li1 < 2 * B_HEADS)
    a_r = jnp.where(head_lane, -jnp.exp(alog_r_ref[...]), 0.0)
    g_col = a_r * _softplus(bac_ref[...] + dt_r_ref[...])
    si = lax.broadcasted_iota(jnp.int32, (2 * SUBLANES, 1), 0)
    head_sub = jnp.logical_and(si >= B_HEADS, si < 2 * B_HEADS)
    a_c = jnp.where(head_sub, -jnp.exp(alog_c_ref[...]), 0.0)
    g_row = a_c * _softplus(bar_ref[...] + dt_c_ref[...])
    rpos = ri % t
    lpos = li16 % t
    gc, rev, gcr = g_col, g_col, g_row
    step = 1
    while step < t:
        gc = gc + jnp.where(rpos >= step, pltpu.roll(gc, step, 0), 0.0)
        rev = rev + jnp.where(rpos < t - step, pltpu.roll(rev, rows - step, 0), 0.0)
        gcr = gcr + jnp.where(lpos >= step, pltpu.roll(gcr, step, 1), 0.0)
        step *= 2
    rev = rev - g_col
    egl = jnp.exp(gc + rev)

    bi = lax.broadcasted_iota(jnp.int32, (rows, rows), 0)
    bj = lax.broadcasted_iota(jnp.int32, (rows, rows), 1)
    same = (bi // t) == (bj // t)
    incl = jnp.logical_and(same, bi >= bj)
    strict = jnp.logical_and(same, bi > bj)
    eye = (bi == bj).astype(F32)

    lows, rhss, attns, qgs, kds = [], [], [], [], []
    for hh in range(B_HEADS):
        sl = slice(hh * B_DK, (hh + 1) * B_DK)
        gc_c = gc[:, B_HEADS + hh:B_HEADS + hh + 1]
        gc_r = gcr[B_HEADS + hh:B_HEADS + hh + 1, :]
        dec = jnp.where(incl, jnp.exp(jnp.where(incl, gc_c - gc_r, 0.0)), 0.0)
        q = act[:, sl]
        q = q * lax.rsqrt(jnp.sum(q * q, axis=-1, keepdims=True) + EPS) * (B_DK ** -0.5)
        k = act[:, B_QK + hh * B_DK:B_QK + (hh + 1) * B_DK]
        k = k * lax.rsqrt(jnp.sum(k * k, axis=-1, keepdims=True) + EPS)
        v = act[:, 2 * B_QK + hh * B_DV:2 * B_QK + (hh + 1) * B_DV]
        beta = jax.nn.sigmoid(bac_ref[:, hh:hh + 1])
        kb = k * beta
        kbf = k.astype(BF16)
        lows.append(jnp.where(strict, dec * _dot_nt(kb, kbf), 0.0))
        attns.append(dec * _dot_nt(q, kbf))
        rhss.append(jnp.concatenate([v * beta, kb * jnp.exp(gc_c)], axis=1))
        qgs.append(q * jnp.exp(gc_c))
        kds.append(k * jnp.exp(rev[:, B_HEADS + hh:B_HEADS + hh + 1]))
    tinvs = [eye - low for low in lows]
    pws = lows
    sq = 2
    while sq < t:
        pws = [_dot(pw, pw) for pw in pws]
        tinvs = [tinv + _dot(tinv, pw) for tinv, pw in zip(tinvs, pws)]
        sq *= 2
    sols = [_dot(tinv, rhs) for tinv, rhs in zip(tinvs, rhss)]

    rgrp = lax.broadcasted_iota(jnp.int32, (rows, B_DV), 0) // t
    units = [(j, hh) for hh in range(B_HEADS) for j in range(nseq)]
    rws = lambda j: slice(j * t, (j + 1) * t)
    states = [s0_ref[j, hh] for j, hh in units]
    proj = [_dot(jnp.concatenate([sols[hh][rws(j), B_DV:], qgs[hh][rws(j)]], axis=0), s)
            for (j, hh), s in zip(units, states)]
    v_new = [sols[hh][rws(j), :B_DV] - pr[:t] for (j, hh), pr in zip(units, proj)]
    outs = []
    for (j, hh), pr, vn in zip(units, proj, v_new):
        v_full = jnp.where(rgrp == j, jnp.concatenate([vn] * nseq, axis=0), 0.0)
        outs.append(pr[t:] + _dot(attns[hh][rws(j)], v_full))
    for (j, hh), s, vn in zip(units, states, v_new):
        decay = egl[j * t:j * t + 1, B_HEADS + hh:B_HEADS + hh + 1]
        s_ref[j, hh] = s * decay + _dot_tn(kds[hh][rws(j)], vn)
    for hh in range(B_HEADS):
        sl = slice(hh * B_DV, (hh + 1) * B_DV)
        o = jnp.concatenate(outs[hh * nseq:(hh + 1) * nseq], axis=0)
        gate = _silu(z_ref[:, sl].astype(F32))
        o_ref[:, sl] = (_rms(o, gout_ref[...]) * gate).astype(o_ref.dtype)


def _b_short(p, st, bac, bar, cw, alog_r, dt_r, alog_c, dt_c, z, s0, gout, t):
    assert t == SUBLANES
    n = p.shape[0]
    nseq = ROW_BLOCK // t
    row = lambda i: (i, 0)
    fixed = lambda i: (0, 0)
    state = pl.BlockSpec((nseq, B_HEADS, B_DK, B_DV), lambda i: (i, 0, 0, 0))
    return pl.pallas_call(
        functools.partial(_b_short_kernel, t=t),
        grid=(n // ROW_BLOCK,),
        in_specs=[pl.BlockSpec((ROW_BLOCK, B_CONV_CH), row), pl.BlockSpec((ROW_BLOCK, B_CONV_CH), row),
                  pl.BlockSpec((ROW_BLOCK, LANES), row), pl.BlockSpec((2 * SUBLANES, ROW_BLOCK), lambda i: (0, i)),
                  pl.BlockSpec((B_CONV, B_CONV_CH), fixed),
                  pl.BlockSpec((1, LANES), fixed), pl.BlockSpec((1, LANES), fixed),
                  pl.BlockSpec((2 * SUBLANES, 1), fixed), pl.BlockSpec((2 * SUBLANES, 1), fixed),
                  pl.BlockSpec((ROW_BLOCK, B_V), row), state, pl.BlockSpec((1, B_DV), fixed)],
        out_specs=[pl.BlockSpec((ROW_BLOCK, B_V), row), state],
        out_shape=[jax.ShapeDtypeStruct((n, B_V), BF16), jax.ShapeDtypeStruct(s0.shape, F32)],
        scratch_shapes=[pltpu.VMEM((SUBLANES + ROW_BLOCK, B_CONV_CH), F32)] * 2,
        compiler_params=_cparams(("parallel",)),
        name="b_short",
    )(p, st, bac, bar, cw, alog_r, dt_r, alog_c, dt_c, z, s0, gout)


def _c_short_kernel(q_ref, k_ref, v_ref, z_ref, r0_ref, gout_ref, o_ref, rout_ref, *, t):
    rows = ROW_BLOCK
    nseq = rows // t
    ri = lax.broadcasted_iota(jnp.int32, (rows, rows), 0)
    ci = lax.broadcasted_iota(jnp.int32, (rows, rows), 1)
    causal = jnp.logical_and(ri // t == ci // t, ri >= ci)
    steps = jnp.where(causal, (ri - ci).astype(F32), 0.0)
    cnt = (lax.broadcasted_iota(jnp.int32, (rows, 1), 0) % t + 1).astype(F32)
    qk_lane = lax.broadcasted_iota(jnp.int32, (1, C_QK), 1) // C_DK
    lg_lane = jnp.zeros((1, C_QK), F32)
    qk_sub = lax.broadcasted_iota(jnp.int32, (C_QK, 1), 0) // C_DK
    lg_sub = jnp.zeros((C_QK, 1), F32)
    for hh in range(C_HEADS):
        lg_lane = jnp.where(qk_lane == hh, _log_gamma(hh), lg_lane)
        lg_sub = jnp.where(qk_sub == hh, _log_gamma(hh), lg_sub)
    q = q_ref[...]
    k = k_ref[...]
    v = v_ref[...].astype(F32)
    qd = q * jnp.exp(cnt * lg_lane)
    kd = k * jnp.exp((t - cnt) * lg_lane)
    r_scale = jnp.exp(t * lg_sub)
    head = lambda hh: slice(hh * C_DV, (hh + 1) * C_DV)
    rws = lambda j: slice(j * t, (j + 1) * t)

    intra = []
    for hh in range(C_HEADS):
        att = jnp.exp(steps * _log_gamma(hh)) * _dot_nt(q, jnp.where(qk_lane == hh, k, 0.0))
        intra.append(_dot(jnp.where(causal, att, 0.0), v[:, head(hh)]))
    stacks = [r0_ref[j].reshape(C_QK, C_DV) for j in range(nseq)]
    units = [(j, hh) for hh in range(C_HEADS) for j in range(nseq)]
    inter = [_dot(jnp.where(qk_lane == hh, qd[rws(j)], 0.0), stacks[j]) for j, hh in units]
    upd = [_dot_tn(jnp.where(qk_lane == hh, kd[rws(j)], 0.0), v[rws(j), head(hh)]) for j, hh in units]
    for j in range(nseq):
        new = stacks[j] * r_scale
        for hh in range(C_HEADS):
            new = new + upd[hh * nseq + j]
        rout_ref[j] = new.reshape(C_HEADS, C_DK, C_DV)
    for hh in range(C_HEADS):
        o = intra[hh] + jnp.concatenate(inter[hh * nseq:(hh + 1) * nseq], axis=0)
        gate = _silu(z_ref[:, head(hh)].astype(F32))
        o_ref[:, head(hh)] = (_rms(o, gout_ref[...]) * gate).astype(o_ref.dtype)


def _c_short(q, k, v, z, r0, gout, t):
    n = q.shape[0]
    nseq = ROW_BLOCK // t
    row = lambda i: (i, 0)
    state = pl.BlockSpec((nseq, C_HEADS, C_DK, C_DV), lambda i: (i, 0, 0, 0))
    return pl.pallas_call(
        functools.partial(_c_short_kernel, t=t),
        grid=(n // ROW_BLOCK,),
        in_specs=[pl.BlockSpec((ROW_BLOCK, C_QK), row), pl.BlockSpec((ROW_BLOCK, C_QK), row),
                  pl.BlockSpec((ROW_BLOCK, C_V), row), pl.BlockSpec((ROW_BLOCK, C_V), row),
                  state, pl.BlockSpec((1, C_DV), lambda i: (0, 0))],
        out_specs=[pl.BlockSpec((ROW_BLOCK, C_V), row), state],
        out_shape=[jax.ShapeDtypeStruct((n, C_V), BF16), jax.ShapeDtypeStruct(r0.shape, F32)],
        compiler_params=_cparams(("parallel",)),
        name="c_short",
    )(q, k, v, z, r0, gout)


def _merge_kernel(x_ref, g1_ref, g2_ref, wg_ref, o0_ref, o1_ref, o2_ref, l0_ref, l1_ref, l2_ref, ob_ref, oc_ref,
                  wa_ref, wb_ref, wc_ref, wo_ref, y_ref, h2_ref, *scr, residue_major):
    x = x_ref[...]
    tm = x.shape[0]
    h = _rms(x, g1_ref[...]).astype(BF16)
    lses = [r[...].reshape(tm, LANES) for r in (l0_ref, l1_ref, l2_ref)]
    outs = [r[...].reshape(tm, A_WIDTH) for r in (o0_ref, o1_ref, o2_ref)]
    heads = []
    for hh in range(A_HEADS):
        sl = slice(hh * A_HD, (hh + 1) * A_HD)
        ls = [l[:, 32 * hh:32 * hh + 1] for l in lses]
        m = jnp.maximum(jnp.maximum(ls[0], ls[1]), ls[2])
        es = [jnp.exp(l - m) for l in ls]
        tot = es[0] + es[1] + es[2]
        acc = (es[0] / tot) * outs[0][:, sl].astype(F32)
        acc = acc + (es[1] / tot) * outs[1][:, sl].astype(F32)
        acc = acc + (es[2] / tot) * outs[2][:, sl].astype(F32)
        heads.append(acc)
    o_a = jnp.concatenate(heads, axis=1)
    if residue_major:
        o_a = _swap_row_grid(scr[0], o_a)
    o_a = o_a.astype(BF16)
    merged = None
    for gi, (o_g, w_ref) in enumerate(((o_a, wa_ref), (ob_ref[...], wb_ref), (oc_ref[...], wc_ref))):
        gate = jax.nn.sigmoid(jnp.dot(h, wg_ref[:, gi * D_MODEL:(gi + 1) * D_MODEL], preferred_element_type=F32))
        term = gate * jnp.dot(o_g, w_ref[...], preferred_element_type=F32)
        merged = term if merged is None else merged + term
    y = x + jnp.dot(merged.astype(BF16), wo_ref[...], preferred_element_type=F32)
    y_ref[...] = y
    h2_ref[...] = _rms(y, g2_ref[...]).astype(h2_ref.dtype)


def _merge(x, g1, g2, wg, o_groups, lses, o_b, o_c, wa, wb, wc, wo, seq_len, residue_major):
    n = x.shape[0]
    tm = RES * RES
    row = lambda i: (i, 0)
    fixed = lambda i: (0, 0)
    half = pl.BlockSpec((tm, A_WIDTH), row)
    wbr = pl.BlockSpec((A_WIDTH, D_MODEL), fixed)
    if residue_major:
        tiles = seq_len // tm
        grp = lambda i: (i // tiles, 0, i % tiles, 0)
        o_spec = pl.BlockSpec((None, RES, tm // RES, A_WIDTH), grp)
        lse = pl.BlockSpec((None, RES, tm // RES, LANES), grp)
        scratch = [pltpu.VMEM((A_WIDTH // LANES, tm, LANES), F32)]
    else:
        o_spec = half
        lse = pl.BlockSpec((tm, LANES), row)
        scratch = []
    return pl.pallas_call(
        functools.partial(_merge_kernel, residue_major=residue_major),
        grid=(n // tm,),
        in_specs=[pl.BlockSpec((tm, D_MODEL), row), pl.BlockSpec((1, D_MODEL), fixed),
                  pl.BlockSpec((1, D_MODEL), fixed),
                  pl.BlockSpec((D_MODEL, 3 * D_MODEL), lambda i: (0, W_COLS["gates"] // (3 * D_MODEL))),
                  o_spec, o_spec, o_spec, lse, lse, lse, half, half, wbr, wbr, wbr,
                  pl.BlockSpec((D_MODEL, D_MODEL), fixed)],
        out_specs=[pl.BlockSpec((tm, D_MODEL), row)] * 2,
        out_shape=[jax.ShapeDtypeStruct((n, D_MODEL), F32), jax.ShapeDtypeStruct((n, D_MODEL), BF16)],
        scratch_shapes=scratch,
        compiler_params=_cparams(("parallel",)),
        name="merge",
    )(x, g1, g2, wg, *o_groups, *lses, o_b, o_c, wa, wb, wc, wo)


def _ffn_kernel(x_ref, h_ref, wg_ref, wu_ref, wo_ref, y_ref, acc_scr):
    j = pl.program_id(1)

    @pl.when(j == 0)
    def _():
        acc_scr[...] = jnp.zeros_like(acc_scr)

    h = h_ref[...]
    gate = jnp.dot(h, wg_ref[...], preferred_element_type=F32)
    up = jnp.dot(h, wu_ref[...], preferred_element_type=F32)
    acc_scr[...] += jnp.dot((_silu(gate) * up).astype(BF16), wo_ref[...], preferred_element_type=F32)

    @pl.when(j == pl.num_programs(1) - 1)
    def _():
        y_ref[...] = x_ref[...] + acc_scr[...]


def _ffn(x, h, w_in, w_out):
    n = x.shape[0]
    tm = min(n, 1024)
    tf = 256
    nf = D_FF // tf
    row = lambda i, j: (i, 0)
    return pl.pallas_call(
        _ffn_kernel,
        grid=(n // tm, nf),
        in_specs=[pl.BlockSpec((tm, D_MODEL), row), pl.BlockSpec((tm, D_MODEL), row),
                  pl.BlockSpec((D_MODEL, tf), lambda i, j: (0, j)),
                  pl.BlockSpec((D_MODEL, tf), lambda i, j: (0, nf + j)),
                  pl.BlockSpec((tf, D_MODEL), lambda i, j: (j, 0))],
        out_specs=pl.BlockSpec((tm, D_MODEL), row),
        out_shape=jax.ShapeDtypeStruct((n, D_MODEL), F32),
        scratch_shapes=[pltpu.VMEM((tm, D_MODEL), F32)],
        compiler_params=_cparams(("parallel", "arbitrary")),
        name="ffn",
    )(x, h, w_in, w_in, w_out)


def _rope_tables(pos, hd, reps):
    inv = ROPE_THETA ** (-jnp.arange(0, hd, 2, dtype=F32) / hd)
    ang = pos.astype(F32)[:, None] * inv[None, :]
    cos = jnp.cos(ang)
    sin = jnp.sin(ang)
    cos2 = jnp.concatenate([cos, cos], axis=1)
    sin2 = jnp.concatenate([-sin, sin], axis=1)
    return jnp.tile(cos2, (1, reps)), jnp.tile(sin2, (1, reps))


def _pad_rows(a, batch, t, t_pad):
    if t == t_pad:
        return a
    a = a.reshape(batch, t, a.shape[-1])
    a = jnp.pad(a, ((0, 0), (0, t_pad - t), (0, 0)))
    return a.reshape(batch * t_pad, a.shape[-1])


def _unpad_rows(a, batch, t, t_pad):
    if t == t_pad:
        return a
    return a.reshape(batch, t_pad, a.shape[-1])[:, :t].reshape(batch * t, a.shape[-1])


def _layer(x, pos, batch, t, lw, caches, layer, conv_state, s0, r0):
    n = batch * t
    prompt = caches is None
    reps = max(1, 256 // t)
    cos_a, sin_a = _rope_tables(pos, A_HD, 1)
    cos_c, sin_c = _rope_tables(pos, C_DK, LANES // C_DK)
    if reps > 1:
        cos_a, sin_a, cos_c, sin_c = (jnp.tile(a, (reps, 1)) for a in (cos_a, sin_a, cos_c, sin_c))

    if prompt:
        to_rm = lambda a: a.reshape(t // RES, RES, A_HD).transpose(1, 0, 2)
        q, k, v, kv_tail = _proj_a(x, lw["g1"], lw["w_all"], lw["qn"], lw["kn"], to_rm(cos_a), to_rm(sin_a),
                                   batch, t, True)
    else:
        q, k, v = _proj_a(x, lw["g1"], lw["w_all"], lw["qn"], lw["kn"], cos_a, sin_a, batch, t, False)
    outs, lses = [], []
    for gi in range(N_GROUPS):
        if prompt:
            o, lse = _attn_prompt(q, k, v, gi, batch, t)
        else:
            o, lse = _attn_sample(q, k, v, caches[gi], layer, gi, batch, t)
        outs.append(o)
        lses.append(lse)
    new_kv = []
    for gi, (win, _) in enumerate(A_GROUPS):
        if prompt:
            keep = min(win, t)
            first = kv_tail.shape[2] - keep // RES
            rows = kv_tail[:, :, first:, 2 * gi * A_WIDTH:2 * (gi + 1) * A_WIDTH]
            new_kv.append(rows.transpose(0, 2, 1, 3).reshape(batch, keep, 2, A_HEADS, A_HD))
        else:
            cols = slice(gi * A_WIDTH, (gi + 1) * A_WIDTH)
            tail = lambda a: a[:, cols].reshape(batch, t, A_HEADS, A_HD)
            new_kv.append(jnp.stack([tail(k), tail(v)], axis=2))

    t_pad = -(-t // ROW_BLOCK) * ROW_BLOCK
    cst = jnp.pad(conv_state, ((0, 0), (SUBLANES - (B_CONV - 1), 0), (0, 0)))
    decay = (lw["alog_r"], lw["dt_r"], lw["alog_c"], lw["dt_c"])
    short = t == SUBLANES and n % ROW_BLOCK == 0
    if t_pad == t:
        act, z_b, bac, bar, p_last = _proj_b(x, lw["g1"], lw["w_all"], lw["w_all"], lw["w_all"], lw["w_bat"],
                                             cst, lw["conv_w"], t)
        conv_new = p_last[:, -(B_CONV - 1):]
        qg, kd, u, w, attn, egl = _b_prep(act, bac, bar, *decay, t, t)
    else:
        p, z_b, bac, bar = _proj_b(x, lw["g1"], lw["w_all"], lw["w_all"], lw["w_all"], lw["w_bat"])
        conv_new = jnp.concatenate([conv_state, p.reshape(batch, t, B_CONV_CH)], axis=1)[:, -(B_CONV - 1):]
        if not short:
            qg, kd, u, w, attn, egl = _b_prep(
                _pad_rows(p, batch, t, t_pad), _pad_rows(bac, batch, t, t_pad),
                _pad_rows(bar.T, batch, t, t_pad).T, *decay, t_pad, t, cst, lw["conv_w"])
    if short:
        o_b, s_new = _b_short(p, cst.reshape(n, B_CONV_CH), bac, bar, lw["conv_w"], *decay, z_b, s0, lw["gb"], t)
    else:
        o_b, s_new = _b_scan(qg, kd, u, w, attn, egl, _pad_rows(z_b, batch, t, t_pad), s0, lw["gb"], batch, t_pad)
        o_b = _unpad_rows(o_b, batch, t, t_pad)

    cq, ck, cv, cz = _proj_c(x, lw["g1"], lw["w_all"], cos_c, sin_c)
    if short:
        o_c, r_new = _c_short(cq, ck, cv, cz, r0, lw["gc"], t)
    else:
        o_c, r_new = _c_scan(*(_pad_rows(a, batch, t, t_pad) for a in (cq, ck, cv, cz)), r0, lw["gc"], batch, t_pad,
                             t)
        o_c = _unpad_rows(o_c, batch, t, t_pad)

    x, h2 = _merge(x, lw["g1"], lw["g2"], lw["w_all"], outs, lses, o_b, o_c, lw["w_oa"], lw["w_ob"], lw["w_oc"],
                   lw["w_o"], t, prompt)
    x = _ffn(x, h2, lw["w_fi"], lw["w_fo"])
    return x, new_kv, conv_new, s_new, r_new


def _layer_weights(l, norm1_g, w_in, a_q_norm_g, a_k_norm_g, b_conv_w, b_a_log, b_dt_bias, b_out_norm_g,
                   c_out_norm_g, w_out_a, w_out_b, w_out_c, w_out, norm2_g, w_ffn_in, w_ffn_out):
    o = IN_OFFS
    wl = w_in[l]
    w_ba = wl[:, o[3]:o[5]]
    pad_r = lambda a: jnp.pad(a.reshape(1, B_HEADS), ((0, 0), (B_HEADS, LANES - 2 * B_HEADS)))
    pad_c = lambda a: jnp.pad(a.reshape(B_HEADS, 1), ((B_HEADS, 2 * SUBLANES - 2 * B_HEADS), (0, 0)))
    return dict(
        g1=norm1_g[l].reshape(1, D_MODEL), g2=norm2_g[l].reshape(1, D_MODEL),
        w_all=jnp.concatenate([wl[:, o[0]:o[1]], wl[:, o[1]:o[2]], wl[:, o[9]:o[10]], wl[:, o[5]:o[9]], wl[:, o[2]:o[3]],
                               jnp.pad(w_ba, ((0, 0), (0, LANES - 2 * B_HEADS)))], axis=1).astype(BF16),
        w_bat=jnp.pad(w_ba.T, ((0, 2 * SUBLANES - 2 * B_HEADS), (0, 0))).astype(BF16),
        qn=a_q_norm_g[l].reshape(1, A_HD), kn=a_k_norm_g[l].reshape(1, A_HD),
        conv_w=b_conv_w[l],
        alog_r=pad_r(b_a_log[l]), dt_r=pad_r(b_dt_bias[l]), alog_c=pad_c(b_a_log[l]), dt_c=pad_c(b_dt_bias[l]),
        gb=b_out_norm_g[l].reshape(1, B_DV), gc=c_out_norm_g[l].reshape(1, C_DV),
        w_oa=w_out_a[l].astype(BF16), w_ob=w_out_b[l].astype(BF16), w_oc=w_out_c[l].astype(BF16),
        w_o=w_out[l].astype(BF16), w_fi=w_ffn_in[l].astype(BF16), w_fo=w_ffn_out[l].astype(BF16),
    )


def kernel(x_prompt, x_sample, cache_a_kv0, cache_a_kv1, cache_a_kv2, state_b_conv, state_b_S, state_c_R, norm1_g, w_in, a_q_norm_g, a_k_norm_g, b_conv_w, b_a_log, b_dt_bias, b_out_norm_g, c_out_norm_g, w_out_a, w_out_b, w_out_c, w_out, norm2_g, w_ffn_in, w_ffn_out):
    bp, t = x_prompt.shape[:2]
    bs, s = x_sample.shape[:2]
    depth = w_in.shape[0]
    pos_p = jnp.arange(t)
    pos_s = PAST_LEN + jnp.arange(s)
    yp = x_prompt.reshape(bp * t, D_MODEL)
    ys = x_sample.reshape(bs * s, D_MODEL)
    caches = (cache_a_kv0, cache_a_kv1, cache_a_kv2)
    zeros_conv = jnp.zeros((bp, B_CONV - 1, B_CONV_CH), F32)
    zeros_s = jnp.zeros((bp, B_HEADS, B_DK, B_DV), F32)
    zeros_r = jnp.zeros((bp, C_HEADS, C_DK, C_DV), F32)
    acc = [[] for _ in range(12)]
    for l in range(depth):
        lw = _layer_weights(l, norm1_g, w_in, a_q_norm_g, a_k_norm_g, b_conv_w, b_a_log, b_dt_bias,
                            b_out_norm_g, c_out_norm_g, w_out_a, w_out_b, w_out_c, w_out, norm2_g,
                            w_ffn_in, w_ffn_out)
        yp, kv, cv, sn, rn = _layer(yp, pos_p, bp, t, lw, None, l, zeros_conv, zeros_s, zeros_r)
        for i, a in enumerate((kv[0], kv[1], kv[2], cv, sn, rn)):
            acc[i].append(a)
        ys, kv, cv, sn, rn = _layer(ys, pos_s, bs, s, lw, caches, l, state_b_conv[l], state_b_S[l], state_c_R[l])
        for i, a in enumerate((kv[0], kv[1], kv[2], cv, sn, rn)):
            acc[6 + i].append(a)
    return (yp.reshape(bp, t, D_MODEL), ys.reshape(bs, s, D_MODEL)) + tuple(jnp.stack(a) for a in acc)
```

```python
import functools
import math

import jax
import jax.numpy as jnp
import numpy as np
from jax import lax
from jax.experimental import pallas as pl
from jax.experimental.pallas import tpu as pltpu

F32 = jnp.float32
BF16 = jnp.bfloat16

D_MODEL = 1024
PAST_LEN = 8192
A_GROUPS = ((128, 1), (512, 4), (2048, 16))
N_GROUPS = 3
A_HEADS = 4
A_HD = 128
A_WIDTH = A_HEADS * A_HD
A_KEYS = 128
B_HEADS = 4
B_DK = 128
B_DV = 128
B_CONV = 4
B_QK = B_HEADS * B_DK
B_V = B_HEADS * B_DV
B_CONV_CH = 2 * B_QK + B_V
C_HEADS = 4
C_DK = 64
C_DV = 128
C_QK = C_HEADS * C_DK
C_V = C_HEADS * C_DV
CHUNK = 64
ROPE_THETA = 10000.0
EPS = 1e-6
D_FF = 2816
IN_SIZES = (3 * N_GROUPS * A_WIDTH, B_CONV_CH, B_V, B_HEADS, B_HEADS, C_QK, C_QK, C_V, C_V, 3 * D_MODEL)
IN_OFFS = tuple(int(v) for v in np.cumsum((0,) + IN_SIZES))

W_COLS = {"a": 0, "b_qkv": 4608, "gates": 6144, "c": 9216, "b_z": 10752, "b_ba": 11264}
W_ALL = 11392

ROW_BLOCK = 128
RES = 16
SUBLANES = 8
LANES = 128
VMEM_LIMIT = 48 * 1024 * 1024


def _cparams(sem):
    return pltpu.CompilerParams(dimension_semantics=sem, vmem_limit_bytes=VMEM_LIMIT)


def _rms(x, g):
    return x * lax.rsqrt(jnp.mean(x * x, axis=-1, keepdims=True) + EPS) * g


def _silu(x):
    return x * jax.nn.sigmoid(x)


def _softplus(x):
    return jnp.maximum(x, 0.0) + jnp.log(1.0 + jnp.exp(-jnp.abs(x)))


def _dot(a, b):
    return jnp.dot(a.astype(BF16), b.astype(BF16), preferred_element_type=F32)


def _dot_nt(a, b):
    return lax.dot_general(a.astype(BF16), b.astype(BF16), (((1,), (1,)), ((), ())), preferred_element_type=F32)


def _dot_tn(a, b):
    return lax.dot_general(a.astype(BF16), b.astype(BF16), (((0,), (0,)), ((), ())), preferred_element_type=F32)


def _swap_row_grid(scr, val):
    slabs = val.shape[1] // LANES
    for c in range(slabs):
        scr[c] = val[:, c * LANES:(c + 1) * LANES]
    cols = [jnp.concatenate([scr[c, pl.ds(r, RES, stride=RES), :] for r in range(RES)], axis=0)
            for c in range(slabs)]
    return jnp.concatenate(cols, axis=1)


def _proj_a_kernel(x_ref, g1_ref, w_ref, qg_ref, kg_ref, cos_ref, sin_ref, q_ref, k_ref, v_ref, *rest, residue_major):
    x = x_ref[...]
    tm = x.shape[0]
    if residue_major:
        tail_ref, scr = rest
        x = _swap_row_grid(scr, x)
    h = _rms(x, g1_ref[...]).astype(BF16)
    cos = cos_ref[...].reshape(tm, A_HD)
    sin = sin_ref[...].reshape(tm, A_HD)

    def norm_rope(seg, g):
        y = _rms(seg, g)
        return y * cos + pltpu.roll(y, A_HD // 2, 1) * sin

    def put(ref, col, val):
        if residue_major:
            ref[:, :, col:col + val.shape[1]] = val.reshape(RES, tm // RES, val.shape[1]).astype(ref.dtype)
        else:
            ref[:, col:col + val.shape[1]] = val

    for j in range(3 * N_GROUPS):
        acc = jnp.dot(h, w_ref[:, j * A_WIDTH:(j + 1) * A_WIDTH], preferred_element_type=F32)
        if j < N_GROUPS:
            for hh in range(A_HEADS):
                sl = slice(hh * A_HD, (hh + 1) * A_HD)
                put(q_ref, j * A_WIDTH + hh * A_HD, norm_rope(acc[:, sl], qg_ref[...]) * (A_HD ** -0.5))
        elif j < 2 * N_GROUPS:
            jj = j - N_GROUPS
            for hh in range(A_HEADS):
                sl = slice(hh * A_HD, (hh + 1) * A_HD)
                val = norm_rope(acc[:, sl], kg_ref[...])
                put(k_ref, jj * A_WIDTH + hh * A_HD, val)
                if residue_major:
                    put(tail_ref, 2 * jj * A_WIDTH + hh * A_HD, val)
        else:
            jj = j - 2 * N_GROUPS
            put(v_ref, jj * A_WIDTH, acc)
            if residue_major:
                put(tail_ref, (2 * jj + 1) * A_WIDTH, acc)


def _proj_a(x, g1, w, qg, kg, cos, sin, batch, seq_len, residue_major):
    n = x.shape[0]
    tm = RES * RES
    nw = N_GROUPS * A_WIDTH
    fixed = lambda i: (0, 0)
    common = [pl.BlockSpec((1, D_MODEL), fixed), pl.BlockSpec((D_MODEL, 3 * nw), fixed),
              pl.BlockSpec((1, A_HD), fixed), pl.BlockSpec((1, A_HD), fixed)]
    if residue_major:
        tiles = seq_len // tm
        tail_tiles = min(max(wd for wd, _ in A_GROUPS), seq_len) // tm
        blk = (None, RES, tm // RES, nw)
        tab = pl.BlockSpec((RES, tm // RES, A_HD), lambda i: (0, i % tiles, 0))
        main = pl.BlockSpec(blk, lambda i: (i // tiles, 0, i % tiles, 0))
        tail = pl.BlockSpec((None, RES, tm // RES, 2 * nw),
                            lambda i: (i // tiles, 0, jnp.maximum(i % tiles - (tiles - tail_tiles), 0), 0))
        out_specs = [main] * 3 + [tail]
        out_shape = ([jax.ShapeDtypeStruct((batch, RES, seq_len // RES, nw), BF16)] * 3
                     + [jax.ShapeDtypeStruct((batch, RES, tail_tiles * tm // RES, 2 * nw), F32)])
        scratch = [pltpu.VMEM((D_MODEL // LANES, tm, LANES), F32)]
    else:
        assert cos.shape[0] == tm
        tab = pl.BlockSpec((tm, A_HD), fixed)
        out_specs = [pl.BlockSpec((tm, nw), lambda i: (i, 0))] * 3
        out_shape = [jax.ShapeDtypeStruct((n, nw), F32)] * 3
        scratch = []
    return pl.pallas_call(
        functools.partial(_proj_a_kernel, residue_major=residue_major),
        grid=(n // tm,),
        in_specs=[pl.BlockSpec((tm, D_MODEL), lambda i: (i, 0))] + common + [tab, tab],
        out_specs=out_specs,
        out_shape=out_shape,
        scratch_shapes=scratch,
        compiler_params=_cparams(("arbitrary",)),
        name="proj_a",
    )(x, g1, w, qg, kg, cos, sin)


ATTN_SUBS = 2


def _attn_prompt_kernel(q_ref, kc_ref, kp_ref, vc_ref, vp_ref, o_ref, lse_ref, *, parts):
    n = pl.program_id(2)
    per = ROW_BLOCK // parts
    qi = lax.broadcasted_iota(jnp.int32, (ROW_BLOCK, ROW_BLOCK), 0)
    kj = lax.broadcasted_iota(jnp.int32, (ROW_BLOCK, ROW_BLOCK), 1)
    qi = parts * (qi % per) + qi // per
    kj = parts * (kj % per) + kj // per
    cur_ok = kj <= qi
    prev_ok = kj >= qi
    first_ok = jnp.logical_and(prev_ok, n > 0)
    lane = lax.broadcasted_iota(jnp.int32, (ROW_BLOCK, LANES), 1)
    neg = -jnp.inf
    packed_rows = 2 * SUBLANES

    def sub(ref, half, sl):
        full = ref[:, :, sl]
        if per % packed_rows == 0:
            return full[:, half * per:(half + 1) * per].reshape(ROW_BLOCK, A_HD)
        return full.astype(F32)[:, half * per:(half + 1) * per].reshape(ROW_BLOCK, A_HD).astype(BF16)

    units = [(half, hh) for half in range(ATTN_SUBS) for hh in range(A_HEADS)]
    head = lambda hh: slice(hh * A_HD, (hh + 1) * A_HD)

    keys = {(half, hh): sub(kc_ref, half, head(hh)) for half, hh in units}
    vals = {(half, hh): sub(vc_ref, half, head(hh)) for half, hh in units}
    for hh in range(A_HEADS):
        last = kp_ref.shape[1] // per - 1
        keys[(-1, hh)] = sub(kp_ref, last, head(hh))
        vals[(-1, hh)] = sub(vp_ref, last, head(hh))

    scores = []
    for half, hh in units:
        q = sub(q_ref, half, head(hh))
        s_cur = jnp.where(cur_ok, _dot_nt(q, keys[(half, hh)]), neg)
        s_prev = jnp.where(first_ok if half == 0 else prev_ok, _dot_nt(q, keys[(half - 1, hh)]), neg)
        scores.append((s_cur, s_prev))
    probs = []
    for s_cur, s_prev in scores:
        m = jnp.max(jnp.maximum(s_cur, s_prev), axis=-1, keepdims=True)
        p_cur = jnp.exp(s_cur - m)
        p_prev = jnp.exp(s_prev - m)
        den = jnp.sum(p_cur + p_prev, axis=-1, keepdims=True)
        probs.append((p_cur, p_prev, m, den))
    lse_blk = [jnp.zeros((ROW_BLOCK, LANES), F32) for _ in range(ATTN_SUBS)]
    outs = {}
    for (half, hh), (p_cur, p_prev, m, den) in zip(units, probs):
        o = (_dot(p_cur, vals[(half, hh)]) + _dot(p_prev, vals[(half - 1, hh)])) / den
        outs[(half, hh)] = o.reshape(parts, per, A_HD)
        lse_blk[half] = jnp.where(lane // 32 == hh, m + jnp.log(den), lse_blk[half])
    for hh in range(A_HEADS):
        both = jnp.concatenate([outs[(half, hh)] for half in range(ATTN_SUBS)], axis=1)
        o_ref[:, :, head(hh)] = both.astype(o_ref.dtype)
    for half in range(ATTN_SUBS):
        lse_ref[:, half * per:(half + 1) * per, :] = lse_blk[half].reshape(parts, per, LANES)


def _attn_prompt(q, k, v, gi, batch, seq_len):
    _, dil = A_GROUPS[gi]
    parts = RES // dil
    per = ROW_BLOCK // parts
    rows = seq_len // RES
    nblk = seq_len // dil // (ATTN_SUBS * ROW_BLOCK)
    split = lambda a: a.reshape(batch, parts, dil, rows, a.shape[-1])
    cur = lambda b, r, n: (b, 0, r, n, gi)
    out = lambda b, r, n: (b, 0, r, n, 0)
    blk = (None, parts, None, ATTN_SUBS * per, A_WIDTH)
    if per % (2 * SUBLANES) == 0:
        blk_prev = (None, parts, None, per, A_WIDTH)
        prev = lambda b, r, n: (b, 0, r, jnp.maximum(ATTN_SUBS * n - 1, 0), gi)
    else:
        blk_prev = blk
        prev = lambda b, r, n: (b, 0, r, jnp.maximum(n - 1, 0), gi)
    o, lse = pl.pallas_call(
        functools.partial(_attn_prompt_kernel, parts=parts),
        grid=(batch, dil, nblk),
        in_specs=[pl.BlockSpec(blk, cur), pl.BlockSpec(blk, cur), pl.BlockSpec(blk_prev, prev),
                  pl.BlockSpec(blk, cur), pl.BlockSpec(blk_prev, prev)],
        out_specs=[pl.BlockSpec(blk, out), pl.BlockSpec((None, parts, None, ATTN_SUBS * per, LANES), out)],
        out_shape=[jax.ShapeDtypeStruct((batch, parts, dil, rows, A_WIDTH), BF16),
                   jax.ShapeDtypeStruct((batch, parts, dil, rows, LANES), F32)],
        compiler_params=_cparams(("parallel", "parallel", "arbitrary")),
        name=f"attn_prompt_g{gi}",
    )(split(q), split(k), split(k), split(v), split(v))
    return o.reshape(batch, RES, rows, A_WIDTH), lse.reshape(batch, RES, rows, LANES)


def _attn_sample_kernel(q_ref, kn_ref, vn_ref, cache_ref, o_ref, lse_ref, *, dil, n_new):
    n_res = min(dil, n_new)
    neg = -jnp.inf
    srow = lax.broadcasted_iota(jnp.int32, (n_new, A_KEYS), 0)
    mcol = lax.broadcasted_iota(jnp.int32, (n_new, A_KEYS), 1)
    in_window = mcol >= srow // dil
    row_res = [srow % dil == res for res in range(n_res)]
    srow_n = lax.broadcasted_iota(jnp.int32, (n_new, n_new), 0)
    tcol_n = lax.broadcasted_iota(jnp.int32, (n_new, n_new), 1)
    new_ok = jnp.logical_and(tcol_n <= srow_n, (srow_n - tcol_n) % dil == 0)
    heads = range(A_HEADS)

    qs = [q_ref[hh].astype(BF16) for hh in heads]
    keys = {(res, hh): cache_ref[:, res, 0, hh, :].astype(BF16) for res in range(n_res) for hh in heads}
    vals = {(res, hh): cache_ref[:, res, 1, hh, :].astype(BF16) for res in range(n_res) for hh in heads}
    raw = {key: _dot_nt(qs[key[1]], kmat) for key, kmat in keys.items()}
    s_new = [jnp.where(new_ok, _dot_nt(qs[hh], kn_ref[hh]), neg) for hh in heads]
    probs = []
    for hh in heads:
        s_buf = raw[(0, hh)]
        for res in range(1, n_res):
            s_buf = jnp.where(row_res[res], raw[(res, hh)], s_buf)
        s_buf = jnp.where(in_window, s_buf, neg)
        m = jnp.maximum(jnp.max(s_buf, axis=-1, keepdims=True), jnp.max(s_new[hh], axis=-1, keepdims=True))
        p_buf = jnp.exp(s_buf - m)
        p_new = jnp.exp(s_new[hh] - m)
        den = jnp.sum(p_buf, axis=-1, keepdims=True) + jnp.sum(p_new, axis=-1, keepdims=True)
        probs.append((p_buf, p_new, m, den))
    for hh in heads:
        p_buf, p_new, m, den = probs[hh]
        acc = _dot(p_new, vn_ref[hh])
        for res in range(n_res):
            p_res = p_buf if n_res == 1 else jnp.where(row_res[res], p_buf, 0.0)
            acc = acc + _dot(p_res, vals[(res, hh)])
        o_ref[hh] = acc / den
        lse_ref[hh] = jnp.broadcast_to(m + jnp.log(den), (n_new, LANES))


def _attn_sample_rows_kernel(q_ref, kn_ref, vn_ref, cache_ref, o_ref, lse_ref, *, dil, n_new):
    row = lax.broadcasted_iota(jnp.int32, (A_KEYS, A_HEADS, 1), 0)
    trow = lax.broadcasted_iota(jnp.int32, (n_new, A_HEADS, 1), 0)
    neg = -jnp.inf
    kn = kn_ref[...]
    vn = vn_ref[...]
    for s in range(n_new):
        res = s % dil
        first = s // dil
        q = q_ref[s][None]
        kc = cache_ref[:, res, 0]
        vc = cache_ref[:, res, 1]
        sc = jnp.sum(kc * q, axis=-1, keepdims=True)
        if first > 0:
            sc = jnp.where(row >= first, sc, neg)
        new_ok = jnp.logical_and(trow <= s, (s - trow) % dil == 0)
        sn = jnp.where(new_ok, jnp.sum(kn * q, axis=-1, keepdims=True), neg)
        m = jnp.maximum(jnp.max(sc, axis=0, keepdims=True), jnp.max(sn, axis=0, keepdims=True))
        pc = jnp.exp(sc - m)
        pn = jnp.exp(sn - m)
        den = jnp.sum(pc, axis=0, keepdims=True) + jnp.sum(pn, axis=0, keepdims=True)
        o = (jnp.sum(pc * vc, axis=0, keepdims=True) + jnp.sum(pn * vn, axis=0, keepdims=True)) / den
        o_ref[s] = o[0]
        lse_ref[s] = jnp.broadcast_to((m + jnp.log(den))[0], (A_HEADS, A_HD))


def _attn_sample(q, k, v, cache, layer, gi, batch, n_new):
    win, dil = A_GROUPS[gi]
    depth = cache.shape[0]
    assert cache.shape[2] == win and win // dil == A_KEYS
    n_res = min(dil, n_new)
    cv = cache.reshape(depth, batch, A_KEYS, dil, 2, A_HEADS, A_HD)
    cache_spec = pl.BlockSpec((None, None, A_KEYS, n_res, 2, A_HEADS, A_HD), lambda b: (layer, b, 0, 0, 0, 0, 0))
    if n_res > 1:
        heads = lambda a: a.reshape(batch, n_new, N_GROUPS, A_HEADS, A_HD)
        new = pl.BlockSpec((None, n_new, None, A_HEADS, A_HD), lambda b: (b, 0, gi, 0, 0))
        out = pl.BlockSpec((None, n_new, A_HEADS, A_HD), lambda b: (b, 0, 0, 0))
        o, lse = pl.pallas_call(
            functools.partial(_attn_sample_rows_kernel, dil=dil, n_new=n_new),
            grid=(batch,),
            in_specs=[new, new, new, cache_spec],
            out_specs=[out, out],
            out_shape=[jax.ShapeDtypeStruct((batch, n_new, A_HEADS, A_HD), F32)] * 2,
            compiler_params=_cparams(("parallel",)),
            name=f"attn_sample_g{gi}",
        )(heads(q), heads(k), heads(v), cv)
        lse = jnp.repeat(lse[..., 0], LANES // A_HEADS, axis=-1)
        return o.reshape(batch * n_new, A_WIDTH), lse.reshape(batch * n_new, LANES)
    heads = lambda a: a.reshape(batch, n_new, N_GROUPS, A_HEADS, A_HD).transpose(0, 2, 3, 1, 4)
    new = pl.BlockSpec((None, None, A_HEADS, n_new, A_HD), lambda b: (b, gi, 0, 0, 0))
    out = pl.BlockSpec((None, A_HEADS, n_new, LANES), lambda b: (b, 0, 0, 0))
    o, lse = pl.pallas_call(
        functools.partial(_attn_sample_kernel, dil=dil, n_new=n_new),
        grid=(batch,),
        in_specs=[new, new, new,
                  pl.BlockSpec((None, None, A_KEYS, n_res, 2, A_HEADS, A_HD), lambda b: (layer, b, 0, 0, 0, 0, 0))],
        out_specs=[out, out],
        out_shape=[jax.ShapeDtypeStruct((batch, A_HEADS, n_new, A_HD), F32)] * 2,
        compiler_params=_cparams(("parallel",)),
        name=f"attn_sample_g{gi}",
    )(heads(q), heads(k), heads(v), cv)
    o = o.transpose(0, 2, 1, 3)
    lse = jnp.repeat(lse[..., 0].transpose(0, 2, 1), LANES // A_HEADS, axis=-1)
    return o.reshape(batch * n_new, A_WIDTH), lse.reshape(batch * n_new, LANES)


def _causal_conv(e_scr, cw_ref, rows):
    xc = e_scr[SUBLANES:SUBLANES + rows, :] * cw_ref[B_CONV - 1:B_CONV, :]
    for kk in range(1, B_CONV):
        xc = xc + e_scr[SUBLANES - kk:SUBLANES - kk + rows, :] * cw_ref[B_CONV - 1 - kk:B_CONV - kk, :]
    return xc


def _proj_b_kernel(x_ref, g1_ref, wqkv_ref, wz_ref, wba_ref, wbat_ref, *rest, fuse_conv, tiles_per_seq):
    h = _rms(x_ref[...], g1_ref[...]).astype(BF16)
    tm = h.shape[0]
    if fuse_conv:
        cst_ref, cw_ref, p_ref, z_ref, bac_ref, bar_ref, ptail_ref, e_scr = rest
        first = pl.program_id(0) % tiles_per_seq == 0

        @pl.when(first)
        def _():
            e_scr[0:SUBLANES, :] = cst_ref[...]

        @pl.when(jnp.logical_not(first))
        def _():
            e_scr[0:SUBLANES, :] = e_scr[tm:tm + SUBLANES, :]

        for j in range(3):
            sl = slice(j * B_QK, (j + 1) * B_QK)
            e_scr[SUBLANES:SUBLANES + tm, sl] = jnp.dot(h, wqkv_ref[:, sl], preferred_element_type=F32)
        p_ref[...] = _silu(_causal_conv(e_scr, cw_ref, tm)).astype(p_ref.dtype)
        ptail_ref[...] = e_scr[tm:tm + SUBLANES, :]
    else:
        p_ref, z_ref, bac_ref, bar_ref = rest
        for j in range(3):
            sl = slice(j * B_QK, (j + 1) * B_QK)
            p_ref[:, sl] = jnp.dot(h, wqkv_ref[:, sl], preferred_element_type=F32)
    z_ref[...] = jnp.dot(h, wz_ref[...], preferred_element_type=F32).astype(z_ref.dtype)
    bac_ref[...] = jnp.dot(h, wba_ref[...], preferred_element_type=F32)
    bar_ref[...] = lax.dot_general(wbat_ref[...], h, (((1,), (1,)), ((), ())), preferred_element_type=F32)


def _proj_b(x, g1, wqkv, wz, wba, wbat, cstate=None, conv_w=None, seq_len=None):
    n = x.shape[0]
    tm = 256
    row = lambda i: (i, 0)
    fixed = lambda i: (0, 0)
    fuse_conv = cstate is not None
    in_specs = [
        pl.BlockSpec((tm, D_MODEL), row),
        pl.BlockSpec((1, D_MODEL), fixed),
        pl.BlockSpec((D_MODEL, B_CONV_CH), lambda i: (0, W_COLS["b_qkv"] // B_CONV_CH)),
        pl.BlockSpec((D_MODEL, B_V), lambda i: (0, W_COLS["b_z"] // B_V)),
        pl.BlockSpec((D_MODEL, LANES), lambda i: (0, W_COLS["b_ba"] // LANES)),
        pl.BlockSpec((2 * SUBLANES, D_MODEL), fixed),
    ]
    out_specs = [pl.BlockSpec((tm, B_CONV_CH), row), pl.BlockSpec((tm, B_V), row),
                 pl.BlockSpec((tm, LANES), row), pl.BlockSpec((2 * SUBLANES, tm), lambda i: (0, i))]
    out_shape = [jax.ShapeDtypeStruct((n, B_CONV_CH), BF16 if fuse_conv else F32),
                 jax.ShapeDtypeStruct((n, B_V), BF16),
                 jax.ShapeDtypeStruct((n, LANES), F32), jax.ShapeDtypeStruct((2 * SUBLANES, n), F32)]
    args = [x, g1, wqkv, wz, wba, wbat]
    scratch = []
    tiles = None
    if fuse_conv:
        tiles = seq_len // tm
        per_seq = pl.BlockSpec((None, SUBLANES, B_CONV_CH), lambda i: (i // tiles, 0, 0))
        in_specs += [per_seq, pl.BlockSpec((B_CONV, B_CONV_CH), fixed)]
        out_specs.append(per_seq)
        out_shape.append(jax.ShapeDtypeStruct((n // seq_len, SUBLANES, B_CONV_CH), F32))
        args += [cstate, conv_w]
        scratch = [pltpu.VMEM((SUBLANES + tm, B_CONV_CH), F32)]
    return pl.pallas_call(
        functools.partial(_proj_b_kernel, fuse_conv=fuse_conv, tiles_per_seq=tiles),
        grid=(n // tm,),
        in_specs=in_specs,
        out_specs=out_specs,
        out_shape=out_shape,
        scratch_shapes=scratch,
        compiler_params=_cparams(("arbitrary",) if fuse_conv else ("parallel",)),
        name="proj_b",
    )(*args)


def _b_prep_kernel(*refs, rows, blocks_per_seq, t_valid, conv_done):
    i = pl.program_id(0)
    blk = i % blocks_per_seq
    if conv_done:
        (p_ref, bac_ref, bar_ref, alog_r_ref, dt_r_ref, alog_c_ref, dt_c_ref,
         qg_ref, kd_ref, u_ref, w_ref, attn_ref, egl_ref) = refs
        act = p_ref[...].astype(F32)
    else:
        (p_ref, halo_ref, cst_ref, cw_ref, bac_ref, bar_ref, alog_r_ref, dt_r_ref, alog_c_ref, dt_c_ref,
         qg_ref, kd_ref, u_ref, w_ref, attn_ref, egl_ref, e_scr) = refs
        e_scr[0:SUBLANES, :] = jnp.where(blk == 0, cst_ref[...], halo_ref[...])
        e_scr[SUBLANES:SUBLANES + rows, :] = p_ref[...]
        act = _silu(_causal_conv(e_scr, cw_ref, rows))

    ri = lax.broadcasted_iota(jnp.int32, (rows, LANES), 0)
    li16 = lax.broadcasted_iota(jnp.int32, (2 * SUBLANES, rows), 1)
    li1 = lax.broadcasted_iota(jnp.int32, (1, LANES), 1)
    masked = t_valid < blocks_per_seq * rows
    if masked:
        row_ok = (blk * rows + ri) < t_valid
        col_ok = (blk * rows + li16) < t_valid
        act = jnp.where(ri[:, 0:1] + blk * rows < t_valid, act, 0.0)

    head_lane = jnp.logical_and(li1 >= B_HEADS, li1 < 2 * B_HEADS)
    a_r = jnp.where(head_lane, -jnp.exp(alog_r_ref[...]), 0.0)
    g_col = a_r * _softplus(bac_ref[...] + dt_r_ref[...])
    si = lax.broadcasted_iota(jnp.int32, (2 * SUBLANES, 1), 0)
    head_sub = jnp.logical_and(si >= B_HEADS, si < 2 * B_HEADS)
    a_c = jnp.where(head_sub, -jnp.exp(alog_c_ref[...]), 0.0)
    g_row = a_c * _softplus(bar_ref[...] + dt_c_ref[...])
    if masked:
        g_col = jnp.where(row_ok, g_col, 0.0)
        g_row = jnp.where(col_ok, g_row, 0.0)

    rpos = ri % CHUNK
    lpos = li16 % CHUNK
    gc = g_col
    rev = g_col
    gcr = g_row
    step = 1
    while step < CHUNK:
        gc = gc + jnp.where(rpos >= step, pltpu.roll(gc, step, 0), 0.0)
        rev = rev + jnp.where(rpos < CHUNK - step, pltpu.roll(rev, rows - step, 0), 0.0)
        gcr = gcr + jnp.where(lpos >= step, pltpu.roll(gcr, step, 1), 0.0)
        step *= 2
    rev = rev - g_col
    egl_ref[...] = jnp.exp(gc + rev)

    bi = lax.broadcasted_iota(jnp.int32, (ROW_BLOCK, ROW_BLOCK), 0)
    bj = lax.broadcasted_iota(jnp.int32, (ROW_BLOCK, ROW_BLOCK), 1)
    same = (bi // CHUNK) == (bj // CHUNK)
    incl = jnp.logical_and(same, bi >= bj)
    strict = jnp.logical_and(same, bi > bj)
    eye = (bi == bj).astype(F32)

    units = [(sb, hh) for sb in range(rows // ROW_BLOCK) for hh in range(B_HEADS)]
    lows, rhss = [], []
    for sb, hh in units:
        rs = slice(sb * ROW_BLOCK, (sb + 1) * ROW_BLOCK)
        sl = slice(hh * B_DK, (hh + 1) * B_DK)
        gc_c = gc[rs, B_HEADS + hh:B_HEADS + hh + 1]
        gc_r = gcr[B_HEADS + hh:B_HEADS + hh + 1, rs]
        dec = jnp.where(incl, jnp.exp(jnp.where(incl, gc_c - gc_r, 0.0)), 0.0)
        q = act[rs, sl]
        q = q * lax.rsqrt(jnp.sum(q * q, axis=-1, keepdims=True) + EPS) * (B_DK ** -0.5)
        k = act[rs, B_QK + hh * B_DK:B_QK + (hh + 1) * B_DK]
        k = k * lax.rsqrt(jnp.sum(k * k, axis=-1, keepdims=True) + EPS)
        v = act[rs, 2 * B_QK + hh * B_DV:2 * B_QK + (hh + 1) * B_DV]
        beta = jax.nn.sigmoid(bac_ref[rs, hh:hh + 1])
        kb = k * beta
        kbf = k.astype(BF16)
        lows.append(jnp.where(strict, dec * _dot_nt(kb, kbf), 0.0))
        attn_ref[rs, sl] = (dec * _dot_nt(q, kbf)).astype(attn_ref.dtype)
        rhss.append(jnp.concatenate([v * beta, kb * jnp.exp(gc_c)], axis=1).astype(BF16))
        qg_ref[rs, sl] = (q * jnp.exp(gc_c)).astype(qg_ref.dtype)
        kd_ref[rs, sl] = (k * jnp.exp(rev[rs, B_HEADS + hh:B_HEADS + hh + 1])).astype(kd_ref.dtype)

    tinvs = [eye - low for low in lows]
    pws = lows
    sq = 2
    while sq < CHUNK:
        pws = [_dot(pw, pw) for pw in pws]
        tinvs = [tinv + _dot(tinv, pw) for tinv, pw in zip(tinvs, pws)]
        sq *= 2
    for (sb, hh), tinv, rhs in zip(units, tinvs, rhss):
        rs = slice(sb * ROW_BLOCK, (sb + 1) * ROW_BLOCK)
        sol = _dot(tinv, rhs)
        u_ref[rs, hh * B_DV:(hh + 1) * B_DV] = sol[:, :B_DV]
        w_ref[rs, hh * B_DK:(hh + 1) * B_DK] = sol[:, B_DV:].astype(w_ref.dtype)


def _b_prep(p, bac, bar, alog_r, dt_r, alog_c, dt_c, seq_len, t_valid, cstate=None, cw=None):
    n = p.shape[0]
    rows = min(seq_len, 2 * ROW_BLOCK)
    bps = seq_len // rows
    row = lambda i: (i, 0)
    fixed = lambda i: (0, 0)
    per_row = rows // SUBLANES
    conv_done = cstate is None
    wide = lambda dt: jax.ShapeDtypeStruct((n, B_V), dt)
    in_specs = [pl.BlockSpec((rows, B_CONV_CH), row)]
    args = [p]
    scratch = []
    if not conv_done:
        in_specs += [pl.BlockSpec((SUBLANES, B_CONV_CH), lambda i: (jnp.maximum(i * per_row - 1, 0), 0)),
                     pl.BlockSpec((None, SUBLANES, B_CONV_CH), lambda i: (i // bps, 0, 0)),
                     pl.BlockSpec((B_CONV, B_CONV_CH), fixed)]
        args += [p, cstate, cw]
        scratch = [pltpu.VMEM((SUBLANES + rows, B_CONV_CH), F32)]
    in_specs += [pl.BlockSpec((rows, LANES), row),
                 pl.BlockSpec((2 * SUBLANES, rows), lambda i: (0, i)),
                 pl.BlockSpec((1, LANES), fixed),
                 pl.BlockSpec((1, LANES), fixed),
                 pl.BlockSpec((2 * SUBLANES, 1), fixed),
                 pl.BlockSpec((2 * SUBLANES, 1), fixed)]
    args += [bac, bar, alog_r, dt_r, alog_c, dt_c]
    return pl.pallas_call(
        functools.partial(_b_prep_kernel, rows=rows, blocks_per_seq=bps, t_valid=t_valid, conv_done=conv_done),
        grid=(n // rows,),
        in_specs=in_specs,
        out_specs=[pl.BlockSpec((rows, B_V), row)] * 5 + [pl.BlockSpec((rows, LANES), row)],
        out_shape=[wide(BF16), wide(BF16), wide(F32), wide(BF16), wide(BF16), jax.ShapeDtypeStruct((n, LANES), F32)],
        scratch_shapes=scratch,
        compiler_params=_cparams(("parallel",)),
        name="b_prep",
    )(*args)


def _b_scan_kernel(qg_ref, kd_ref, u_ref, w_ref, attn_ref, egl_ref, z_ref, s0_ref, gout_ref, o_ref, s_ref, *, nb):
    c = pl.program_id(1)

    @pl.when(c == 0)
    def _():
        s_ref[...] = s0_ref[...]

    half = c % (ROW_BLOCK // CHUNK)
    rgrp = lax.broadcasted_iota(jnp.int32, (ROW_BLOCK, B_DV), 0) // CHUNK
    here = rgrp == half
    units = [(b, hh) for b in range(nb) for hh in range(B_HEADS)]
    head = lambda hh: slice(hh * B_DV, (hh + 1) * B_DV)
    states = [s_ref[b, hh] for b, hh in units]
    proj = [_dot(jnp.concatenate([w_ref[b, :, head(hh)], qg_ref[b, :, head(hh)]], axis=0), s)
            for (b, hh), s in zip(units, states)]
    v_new = [u_ref[b, :, head(hh)] - pr[:CHUNK] for (b, hh), pr in zip(units, proj)]
    outs = []
    for (b, hh), pr, vn in zip(units, proj, v_new):
        v_full = jnp.where(here, jnp.concatenate([vn] * (ROW_BLOCK // CHUNK), axis=0), 0.0)
        outs.append(pr[CHUNK:] + _dot(attn_ref[b, :, head(hh)], v_full))
    for (b, hh), s, vn in zip(units, states, v_new):
        decay = egl_ref[b, 0:1, B_HEADS + hh:B_HEADS + hh + 1]
        s_ref[b, hh] = s * decay + _dot_tn(kd_ref[b, :, head(hh)], vn)
    for (b, hh), o in zip(units, outs):
        gate = _silu(z_ref[b, :, head(hh)].astype(F32))
        o_ref[b, :, head(hh)] = (_rms(o, gout_ref[...]) * gate).astype(o_ref.dtype)


def _b_scan(qg, kd, u, w, attn, egl, z, s0, gout, batch, seq_len):
    nb = 4
    nchunk = seq_len // CHUNK
    v3 = lambda a: a.reshape(batch, seq_len, a.shape[-1])
    rows = lambda bi, c: (bi, c, 0)
    state = lambda bi, c: (bi, 0, 0, 0)
    wide = pl.BlockSpec((nb, CHUNK, B_V), rows)
    o, s_new = pl.pallas_call(
        functools.partial(_b_scan_kernel, nb=nb),
        grid=(batch // nb, nchunk),
        in_specs=[wide] * 5 + [pl.BlockSpec((nb, CHUNK, LANES), rows), wide,
                               pl.BlockSpec((nb, B_HEADS, B_DK, B_DV), state),
                               pl.BlockSpec((1, B_DV), lambda bi, c: (0, 0))],
        out_specs=[wide, pl.BlockSpec((nb, B_HEADS, B_DK, B_DV), state)],
        out_shape=[jax.ShapeDtypeStruct((batch, seq_len, B_V), BF16),
                   jax.ShapeDtypeStruct((batch, B_HEADS, B_DK, B_DV), F32)],
        compiler_params=_cparams(("parallel", "arbitrary")),
        name="b_scan",
    )(v3(qg), v3(kd), v3(u), v3(w), v3(attn), v3(egl), v3(z), s0, gout)
    return o.reshape(batch * seq_len, B_V), s_new


def _proj_c_kernel(x_ref, g1_ref, w_ref, cos_ref, sin_ref, q_ref, k_ref, v_ref, z_ref):
    h = _rms(x_ref[...], g1_ref[...]).astype(BF16)
    cos = cos_ref[...]
    sin = sin_ref[...]
    lane = lax.broadcasted_iota(jnp.int32, cos.shape, 1)
    first_half = (lane % C_DK) < (C_DK // 2)

    def rope(seg):
        swapped = jnp.where(first_half, pltpu.roll(seg, LANES - C_DK // 2, 1), pltpu.roll(seg, C_DK // 2, 1))
        return seg * cos + swapped * sin

    qk = jnp.dot(h, w_ref[:, 0:2 * C_QK], preferred_element_type=F32)
    for j in range(2 * C_QK // LANES):
        seg = rope(qk[:, j * LANES:(j + 1) * LANES])
        if j < C_QK // LANES:
            q_ref[:, j * LANES:(j + 1) * LANES] = seg
        else:
            jj = j - C_QK // LANES
            k_ref[:, jj * LANES:(jj + 1) * LANES] = seg * (C_DK ** -0.5)
    v_ref[...] = jnp.dot(h, w_ref[:, 2 * C_QK:2 * C_QK + C_V], preferred_element_type=F32).astype(v_ref.dtype)
    z_ref[...] = jnp.dot(h, w_ref[:, 2 * C_QK + C_V:2 * C_QK + 2 * C_V],
                         preferred_element_type=F32).astype(z_ref.dtype)


def _proj_c(x, g1, w, cos, sin):
    n = x.shape[0]
    tm = 256
    tab_blocks = cos.shape[0] // tm
    row = lambda i: (i, 0)
    fixed = lambda i: (0, 0)
    tab = (lambda i: (i % tab_blocks, 0)) if tab_blocks > 1 else fixed
    return pl.pallas_call(
        _proj_c_kernel,
        grid=(n // tm,),
        in_specs=[
            pl.BlockSpec((tm, D_MODEL), row),
            pl.BlockSpec((1, D_MODEL), fixed),
            pl.BlockSpec((D_MODEL, 2 * C_QK + 2 * C_V), lambda i: (0, W_COLS["c"] // (2 * C_QK + 2 * C_V))),
            pl.BlockSpec((tm, LANES), tab),
            pl.BlockSpec((tm, LANES), tab),
        ],
        out_specs=[pl.BlockSpec((tm, C_QK), row), pl.BlockSpec((tm, C_QK), row),
                   pl.BlockSpec((tm, C_V), row), pl.BlockSpec((tm, C_V), row)],
        out_shape=[jax.ShapeDtypeStruct((n, C_QK), F32), jax.ShapeDtypeStruct((n, C_QK), F32),
                   jax.ShapeDtypeStruct((n, C_V), BF16), jax.ShapeDtypeStruct((n, C_V), BF16)],
        compiler_params=_cparams(("parallel",)),
        name="proj_c",
    )(x, g1, w, cos, sin)


def _log_gamma(hh):
    return math.log1p(-(2.0 ** (-5.0 - hh)))


def _c_scan_kernel(q_ref, k_ref, v_ref, z_ref, r0_ref, gout_ref, o_ref, rout_ref, r_ref, *, nb, t_valid):
    c = pl.program_id(1)
    rows = ROW_BLOCK
    state_blocks = [(b, hh, slice(hh * C_DK, (hh + 1) * C_DK), slice(hh * C_DV, (hh + 1) * C_DV))
                    for b in range(nb) for hh in range(C_HEADS)]

    @pl.when(c == 0)
    def _():
        r_ref[...] = jnp.zeros_like(r_ref)
        for b, hh, rsl, csl in state_blocks:
            r_ref[b, rsl, csl] = r0_ref[b, hh]

    left = jnp.clip(t_valid - c * rows, 0, rows)
    ri = lax.broadcasted_iota(jnp.int32, (rows, rows), 0)
    ci = lax.broadcasted_iota(jnp.int32, (rows, rows), 1)
    cnt_i = jnp.minimum(ri + 1, left).astype(F32)
    cnt_j = jnp.minimum(ci + 1, left).astype(F32)
    incl = ri >= ci
    steps = jnp.where(incl, cnt_i - cnt_j, 0.0)
    cnt_col = cnt_i[:, 0:1]
    left_f = left.astype(F32)
    qk_lane = lax.broadcasted_iota(jnp.int32, (1, C_QK), 1) // C_DK
    lg_lane = jnp.zeros((1, C_QK), F32)
    for hh in range(C_HEADS):
        lg_lane = jnp.where(qk_lane == hh, _log_gamma(hh), lg_lane)
    qk_sub = lax.broadcasted_iota(jnp.int32, (C_QK, 1), 0) // C_DK
    lg_sub = jnp.zeros((C_QK, 1), F32)
    for hh in range(C_HEADS):
        lg_sub = jnp.where(qk_sub == hh, _log_gamma(hh), lg_sub)
    q_scale = jnp.exp(cnt_col * lg_lane)
    k_scale = jnp.exp((left_f - cnt_col) * lg_lane)
    r_scale = jnp.exp(left_f * lg_sub)
    row_ok = (lax.broadcasted_iota(jnp.int32, (rows, 1), 0) + c * rows) < t_valid
    diag = (lax.broadcasted_iota(jnp.int32, (C_QK, C_V), 0) // C_DK) == (
        lax.broadcasted_iota(jnp.int32, (C_QK, C_V), 1) // C_DV)

    head = lambda hh: slice(hh * C_DV, (hh + 1) * C_DV)
    decays = [jnp.where(incl, jnp.exp(steps * _log_gamma(hh)), 0.0) for hh in range(C_HEADS)]
    qs = [q_ref[b] for b in range(nb)]
    ks = [jnp.where(row_ok, k_ref[b], 0.0) for b in range(nb)]
    vs = [v_ref[b].astype(BF16) for b in range(nb)]
    rs = [r_ref[b] for b in range(nb)]
    inters = [_dot(q * q_scale, r) for q, r in zip(qs, rs)]
    units = [(b, hh) for b in range(nb) for hh in range(C_HEADS)]
    atts = [decays[hh] * _dot_nt(qs[b], jnp.where(qk_lane == hh, ks[b], 0.0)) for b, hh in units]
    outs = [inters[b][:, head(hh)] + _dot(att, vs[b][:, head(hh)]) for (b, hh), att in zip(units, atts)]
    for b in range(nb):
        r_ref[b] = rs[b] * r_scale + jnp.where(diag, _dot_tn(ks[b] * k_scale, vs[b]), 0.0)
    for (b, hh), o in zip(units, outs):
        gate = _silu(z_ref[b, :, head(hh)].astype(F32))
        o_ref[b, :, head(hh)] = (_rms(o, gout_ref[...]) * gate).astype(o_ref.dtype)

    @pl.when(c == pl.num_programs(1) - 1)
    def _():
        for b, hh, rsl, csl in state_blocks:
            rout_ref[b, hh] = r_ref[b, rsl, csl]


def _c_scan(q, k, v, z, r0, gout, batch, seq_len, t_valid):
    nb = 4
    nblk = seq_len // ROW_BLOCK
    v3 = lambda a: a.reshape(batch, seq_len, a.shape[-1])
    rows = lambda bi, c: (bi, c, 0)
    state = pl.BlockSpec((nb, C_HEADS, C_DK, C_DV), lambda bi, c: (bi, 0, 0, 0))
    o, r_new = pl.pallas_call(
        functools.partial(_c_scan_kernel, nb=nb, t_valid=t_valid),
        grid=(batch // nb, nblk),
        in_specs=[pl.BlockSpec((nb, ROW_BLOCK, C_QK), rows), pl.BlockSpec((nb, ROW_BLOCK, C_QK), rows),
                  pl.BlockSpec((nb, ROW_BLOCK, C_V), rows), pl.BlockSpec((nb, ROW_BLOCK, C_V), rows),
                  state, pl.BlockSpec((1, C_DV), lambda bi, c: (0, 0))],
        out_specs=[pl.BlockSpec((nb, ROW_BLOCK, C_V), rows), state],
        out_shape=[jax.ShapeDtypeStruct((batch, seq_len, C_V), BF16),
                   jax.ShapeDtypeStruct((batch, C_HEADS, C_DK, C_DV), F32)],
        scratch_shapes=[pltpu.VMEM((nb, C_QK, C_V), F32)],
        compiler_params=_cparams(("parallel", "arbitrary")),
        name="c_scan",
    )(v3(q), v3(k), v3(v), v3(z), r0, gout)
    return o.reshape(batch * seq_len, C_V), r_new


def _b_short_kernel(p_ref, st_ref, bac_ref, bar_ref, cw_ref, alog_r_ref, dt_r_ref, alog_c_ref, dt_c_ref,
                    z_ref, s0_ref, gout_ref, o_ref, s_ref, e_new, e_old, *, t):
    rows = ROW_BLOCK
    nseq = rows // t
    e_new[0:SUBLANES, :] = jnp.zeros((SUBLANES, B_CONV_CH), F32)
    e_new[SUBLANES:SUBLANES + rows, :] = p_ref[...]
    e_old[0:rows, :] = st_ref[...]
    e_old[rows:rows + SUBLANES, :] = jnp.zeros((SUBLANES, B_CONV_CH), F32)
    pos = lax.broadcasted_iota(jnp.int32, (rows, 1), 0) % t
    xc = e_new[SUBLANES:SUBLANES + rows, :] * cw_ref[B_CONV - 1:B_CONV, :]
    for kk in range(1, B_CONV):
        window = slice(SUBLANES - kk, SUBLANES - kk + rows)
        src = jnp.where(pos >= kk, e_new[window, :], e_old[window, :])
        xc = xc + src * cw_ref[B_CONV - 1 - kk:B_CONV - kk, :]
    act = _silu(xc)

    ri = lax.broadcasted_iota(jnp.int32, (rows, LANES), 0)
    li16 = lax.broadcasted_iota(jnp.int32, (2 * SUBLANES, rows), 1)
    li1 = lax.broadcasted_iota(jnp.int32, (1, LANES), 1)
    head_lane = jnp.logical_and(li1 >= B_HEADS, li1 < 2 * B_HEADS)
    a_r = jnp.where(head_lane, -jnp.exp(alog_r_ref[...]), 0.0)
    g_col = a_r * _softplus(bac_ref[...] + dt_r_ref[...])
    si = lax.broadcasted_iota(jnp.int32, (2 * SUBLANES, 1), 0)
    head_sub = jnp.logical_and(si >= B_HEADS, si < 2 * B_HEADS)
    a_c = jnp.where(head_sub, -jnp.exp(alog_c_ref[...]), 0.0)
    g_row = a_c * _softplus(bar_ref[...] + dt_c_ref[...])
    rpos = ri % t
    lpos = li16 % t
    gc, rev, gcr = g_col, g_col, g_row
    step = 1
    while step < t:
        gc = gc + jnp.where(rpos >= step, pltpu.roll(gc, step, 0), 0.0)
        rev = rev + jnp.where(rpos < t - step, pltpu.roll(rev, rows - step, 0), 0.0)
        gcr = gcr + jnp.where(lpos >= step, pltpu.roll(gcr, step, 1), 0.0)
        step *= 2
    rev = rev - g_col
    egl = jnp.exp(gc + rev)

    bi = lax.broadcasted_iota(jnp.int32, (rows, rows), 0)
    bj = lax.broadcasted_iota(jnp.int32, (rows, rows), 1)
    same = (bi // t) == (bj // t)
    incl = jnp.logical_and(same, bi >= bj)
    strict = jnp.logical_and(same, bi > bj)
    eye = (bi == bj).astype(F32)

    lows, rhss, attns, qgs, kds = [], [], [], [], []
    for hh in range(B_HEADS):
        sl = slice(hh * B_DK, (hh + 1) * B_DK)
        gc_c = gc[:, B_HEADS + hh:B_HEADS + hh + 1]
        gc_r = gcr[B_HEADS + hh:B_HEADS + hh + 1, :]
        dec = jnp.where(incl, jnp.exp(jnp.where(incl, gc_c - gc_r, 0.0)), 0.0)
        q = act[:, sl]
        q = q * lax.rsqrt(jnp.sum(q * q, axis=-1, keepdims=True) + EPS) * (B_DK ** -0.5)
        k = act[:, B_QK + hh * B_DK:B_QK + (hh + 1) * B_DK]
        k = k * lax.rsqrt(jnp.sum(k * k, axis=-1, keepdims=True) + EPS)
        v = act[:, 2 * B_QK + hh * B_DV:2 * B_QK + (hh + 1) * B_DV]
        beta = jax.nn.sigmoid(bac_ref[:, hh:hh + 1])
        kb = k * beta
        kbf = k.astype(BF16)
        lows.append(jnp.where(strict, dec * _dot_nt(kb, kbf), 0.0))
        attns.append(dec * _dot_nt(q, kbf))
        rhss.append(jnp.concatenate([v * beta, kb * jnp.exp(gc_c)], axis=1))
        qgs.append(q * jnp.exp(gc_c))
        kds.append(k * jnp.exp(rev[:, B_HEADS + hh:B_HEADS + hh + 1]))
    tinvs = [eye - low for low in lows]
    pws = lows
    sq = 2
    while sq < t:
        pws = [_dot(pw, pw) for pw in pws]
        tinvs = [tinv + _dot(tinv, pw) for tinv, pw in zip(tinvs, pws)]
        sq *= 2
    sols = [_dot(tinv, rhs) for tinv, rhs in zip(tinvs, rhss)]

    rgrp = lax.broadcasted_iota(jnp.int32, (rows, B_DV), 0) // t
    units = [(j, hh) for hh in range(B_HEADS) for j in range(nseq)]
    rws = lambda j: slice(j * t, (j + 1) * t)
    states = [s0_ref[j, hh] for j, hh in units]
    proj = [_dot(jnp.concatenate([sols[hh][rws(j), B_DV:], qgs[hh][rws(j)]], axis=0), s)
            for (j, hh), s in zip(units, states)]
    v_new = [sols[hh][rws(j), :B_DV] - pr[:t] for (j, hh), pr in zip(units, proj)]
    outs = []
    for (j, hh), pr, vn in zip(units, proj, v_new):
        v_full = jnp.where(rgrp == j, jnp.concatenate([vn] * nseq, axis=0), 0.0)
        outs.append(pr[t:] + _dot(attns[hh][rws(j)], v_full))
    for (j, hh), s, vn in zip(units, states, v_new):
        decay = egl[j * t:j * t + 1, B_HEADS + hh:B_HEADS + hh + 1]
        s_ref[j, hh] = s * decay + _dot_tn(kds[hh][rws(j)], vn)
    for hh in range(B_HEADS):
        sl = slice(hh * B_DV, (hh + 1) * B_DV)
        o = jnp.concatenate(outs[hh * nseq:(hh + 1) * nseq], axis=0)
        gate = _silu(z_ref[:, sl].astype(F32))
        o_ref[:, sl] = (_rms(o, gout_ref[...]) * gate).astype(o_ref.dtype)


def _b_short(p, st, bac, bar, cw, alog_r, dt_r, alog_c, dt_c, z, s0, gout, t):
    assert t == SUBLANES
    n = p.shape[0]
    nseq = ROW_BLOCK // t
    row = lambda i: (i, 0)
    fixed = lambda i: (0, 0)
    state = pl.BlockSpec((nseq, B_HEADS, B_DK, B_DV), lambda i: (i, 0, 0, 0))
    return pl.pallas_call(
        functools.partial(_b_short_kernel, t=t),
        grid=(n // ROW_BLOCK,),
        in_specs=[pl.BlockSpec((ROW_BLOCK, B_CONV_CH), row), pl.BlockSpec((ROW_BLOCK, B_CONV_CH), row),
                  pl.BlockSpec((ROW_BLOCK, LANES), row), pl.BlockSpec((2 * SUBLANES, ROW_BLOCK), lambda i: (0, i)),
                  pl.BlockSpec((B_CONV, B_CONV_CH), fixed),
                  pl.BlockSpec((1, LANES), fixed), pl.BlockSpec((1, LANES), fixed),
                  pl.BlockSpec((2 * SUBLANES, 1), fixed), pl.BlockSpec((2 * SUBLANES, 1), fixed),
                  pl.BlockSpec((ROW_BLOCK, B_V), row), state, pl.BlockSpec((1, B_DV), fixed)],
        out_specs=[pl.BlockSpec((ROW_BLOCK, B_V), row), state],
        out_shape=[jax.ShapeDtypeStruct((n, B_V), BF16), jax.ShapeDtypeStruct(s0.shape, F32)],
        scratch_shapes=[pltpu.VMEM((SUBLANES + ROW_BLOCK, B_CONV_CH), F32)] * 2,
        compiler_params=_cparams(("parallel",)),
        name="b_short",
    )(p, st, bac, bar, cw, alog_r, dt_r, alog_c, dt_c, z, s0, gout)


def _c_short_kernel(q_ref, k_ref, v_ref, z_ref, r0_ref, gout_ref, o_ref, rout_ref, *, t):
    rows = ROW_BLOCK
    nseq = rows // t
    ri = lax.broadcasted_iota(jnp.int32, (rows, rows), 0)
    ci = lax.broadcasted_iota(jnp.int32, (rows, rows), 1)
    causal = jnp.logical_and(ri // t == ci // t, ri >= ci)
    steps = jnp.where(causal, (ri - ci).astype(F32), 0.0)
    cnt = (lax.broadcasted_iota(jnp.int32, (rows, 1), 0) % t + 1).astype(F32)
    qk_lane = lax.broadcasted_iota(jnp.int32, (1, C_QK), 1) // C_DK
    lg_lane = jnp.zeros((1, C_QK), F32)
    qk_sub = lax.broadcasted_iota(jnp.int32, (C_QK, 1), 0) // C_DK
    lg_sub = jnp.zeros((C_QK, 1), F32)
    for hh in range(C_HEADS):
        lg_lane = jnp.where(qk_lane == hh, _log_gamma(hh), lg_lane)
        lg_sub = jnp.where(qk_sub == hh, _log_gamma(hh), lg_sub)
    q = q_ref[...]
    k = k_ref[...]
    v = v_ref[...].astype(F32)
    qd = q * jnp.exp(cnt * lg_lane)
    kd = k * jnp.exp((t - cnt) * lg_lane)
    r_scale = jnp.exp(t * lg_sub)
    head = lambda hh: slice(hh * C_DV, (hh + 1) * C_DV)
    rws = lambda j: slice(j * t, (j + 1) * t)

    intra = []
    for hh in range(C_HEADS):
        att = jnp.exp(steps * _log_gamma(hh)) * _dot_nt(q, jnp.where(qk_lane == hh, k, 0.0))
        intra.append(_dot(jnp.where(causal, att, 0.0), v[:, head(hh)]))
    stacks = [r0_ref[j].reshape(C_QK, C_DV) for j in range(nseq)]
    units = [(j, hh) for hh in range(C_HEADS) for j in range(nseq)]
    inter = [_dot(jnp.where(qk_lane == hh, qd[rws(j)], 0.0), stacks[j]) for j, hh in units]
    upd = [_dot_tn(jnp.where(qk_lane == hh, kd[rws(j)], 0.0), v[rws(j), head(hh)]) for j, hh in units]
    for j in range(nseq):
        new = stacks[j] * r_scale
        for hh in range(C_HEADS):
            new = new + upd[hh * nseq + j]
        rout_ref[j] = new.reshape(C_HEADS, C_DK, C_DV)
    for hh in range(C_HEADS):
        o = intra[hh] + jnp.concatenate(inter[hh * nseq:(hh + 1) * nseq], axis=0)
        gate = _silu(z_ref[:, head(hh)].astype(F32))
        o_ref[:, head(hh)] = (_rms(o, gout_ref[...]) * gate).astype(o_ref.dtype)


def _c_short(q, k, v, z, r0, gout, t):
    n = q.shape[0]
    nseq = ROW_BLOCK // t
    row = lambda i: (i, 0)
    state = pl.BlockSpec((nseq, C_HEADS, C_DK, C_DV), lambda i: (i, 0, 0, 0))
    return pl.pallas_call(
        functools.partial(_c_short_kernel, t=t),
        grid=(n // ROW_BLOCK,),
        in_specs=[pl.BlockSpec((ROW_BLOCK, C_QK), row), pl.BlockSpec((ROW_BLOCK, C_QK), row),
                  pl.BlockSpec((ROW_BLOCK, C_V), row), pl.BlockSpec((ROW_BLOCK, C_V), row),
                  state, pl.BlockSpec((1, C_DV), lambda i: (0, 0))],
        out_specs=[pl.BlockSpec((ROW_BLOCK, C_V), row), state],
        out_shape=[jax.ShapeDtypeStruct((n, C_V), BF16), jax.ShapeDtypeStruct(r0.shape, F32)],
        compiler_params=_cparams(("parallel",)),
        name="c_short",
    )(q, k, v, z, r0, gout)


def _merge_kernel(x_ref, g1_ref, g2_ref, wg_ref, o0_ref, o1_ref, o2_ref, l0_ref, l1_ref, l2_ref, ob_ref, oc_ref,
                  wa_ref, wb_ref, wc_ref, wo_ref, y_ref, h2_ref, *scr, residue_major):
    x = x_ref[...]
    tm = x.shape[0]
    h = _rms(x, g1_ref[...]).astype(BF16)
    lses = [r[...].reshape(tm, LANES) for r in (l0_ref, l1_ref, l2_ref)]
    outs = [r[...].reshape(tm, A_WIDTH) for r in (o0_ref, o1_ref, o2_ref)]
    heads = []
    for hh in range(A_HEADS):
        sl = slice(hh * A_HD, (hh + 1) * A_HD)
        ls = [l[:, 32 * hh:32 * hh + 1] for l in lses]
        m = jnp.maximum(jnp.maximum(ls[0], ls[1]), ls[2])
        es = [jnp.exp(l - m) for l in ls]
        tot = es[0] + es[1] + es[2]
        acc = (es[0] / tot) * outs[0][:, sl].astype(F32)
        acc = acc + (es[1] / tot) * outs[1][:, sl].astype(F32)
        acc = acc + (es[2] / tot) * outs[2][:, sl].astype(F32)
        heads.append(acc)
    o_a = jnp.concatenate(heads, axis=1)
    if residue_major:
        o_a = _swap_row_grid(scr[0], o_a)
    o_a = o_a.astype(BF16)
    merged = None
    for gi, (o_g, w_ref) in enumerate(((o_a, wa_ref), (ob_ref[...], wb_ref), (oc_ref[...], wc_ref))):
        gate = jax.nn.sigmoid(jnp.dot(h, wg_ref[:, gi * D_MODEL:(gi + 1) * D_MODEL], preferred_element_type=F32))
        term = gate * jnp.dot(o_g, w_ref[...], preferred_element_type=F32)
        merged = term if merged is None else merged + term
    y = x + jnp.dot(merged.astype(BF16), wo_ref[...], preferred_element_type=F32)
    y_ref[...] = y
    h2_ref[...] = _rms(y, g2_ref[...]).astype(h2_ref.dtype)


def _merge(x, g1, g2, wg, o_groups, lses, o_b, o_c, wa, wb, wc, wo, seq_len, residue_major):
    n = x.shape[0]
    tm = RES * RES
    row = lambda i: (i, 0)
    fixed = lambda i: (0, 0)
    half = pl.BlockSpec((tm, A_WIDTH), row)
    wbr = pl.BlockSpec((A_WIDTH, D_MODEL), fixed)
    if residue_major:
        tiles = seq_len // tm
        grp = lambda i: (i // tiles, 0, i % tiles, 0)
        o_spec = pl.BlockSpec((None, RES, tm // RES, A_WIDTH), grp)
        lse = pl.BlockSpec((None, RES, tm // RES, LANES), grp)
        scratch = [pltpu.VMEM((A_WIDTH // LANES, tm, LANES), F32)]
    else:
        o_spec = half
        lse = pl.BlockSpec((tm, LANES), row)
        scratch = []
    return pl.pallas_call(
        functools.partial(_merge_kernel, residue_major=residue_major),
        grid=(n // tm,),
        in_specs=[pl.BlockSpec((tm, D_MODEL), row), pl.BlockSpec((1, D_MODEL), fixed),
                  pl.BlockSpec((1, D_MODEL), fixed),
                  pl.BlockSpec((D_MODEL, 3 * D_MODEL), lambda i: (0, W_COLS["gates"] // (3 * D_MODEL))),
                  o_spec, o_spec, o_spec, lse, lse, lse, half, half, wbr, wbr, wbr,
                  pl.BlockSpec((D_MODEL, D_MODEL), fixed)],
        out_specs=[pl.BlockSpec((tm, D_MODEL), row)] * 2,
        out_shape=[jax.ShapeDtypeStruct((n, D_MODEL), F32), jax.ShapeDtypeStruct((n, D_MODEL), BF16)],
        scratch_shapes=scratch,
        compiler_params=_cparams(("parallel",)),
        name="merge",
    )(x, g1, g2, wg, *o_groups, *lses, o_b, o_c, wa, wb, wc, wo)


def _ffn_kernel(x_ref, h_ref, wg_ref, wu_ref, wo_ref, y_ref, acc_scr):
    j = pl.program_id(1)

    @pl.when(j == 0)
    def _():
        acc_scr[...] = jnp.zeros_like(acc_scr)

    h = h_ref[...]
    gate = jnp.dot(h, wg_ref[...], preferred_element_type=F32)
    up = jnp.dot(h, wu_ref[...], preferred_element_type=F32)
    acc_scr[...] += jnp.dot((_silu(gate) * up).astype(BF16), wo_ref[...], preferred_element_type=F32)

    @pl.when(j == pl.num_programs(1) - 1)
    def _():
        y_ref[...] = x_ref[...] + acc_scr[...]


def _ffn(x, h, w_in, w_out):
    n = x.shape[0]
    tm = min(n, 1024)
    tf = 256
    nf = D_FF // tf
    row = lambda i, j: (i, 0)
    return pl.pallas_call(
        _ffn_kernel,
        grid=(n // tm, nf),
        in_specs=[pl.BlockSpec((tm, D_MODEL), row), pl.BlockSpec((tm, D_MODEL), row),
                  pl.BlockSpec((D_MODEL, tf), lambda i, j: (0, j)),
                  pl.BlockSpec((D_MODEL, tf), lambda i, j: (0, nf + j)),
                  pl.BlockSpec((tf, D_MODEL), lambda i, j: (j, 0))],
        out_specs=pl.BlockSpec((tm, D_MODEL), row),
        out_shape=jax.ShapeDtypeStruct((n, D_MODEL), F32),
        scratch_shapes=[pltpu.VMEM((tm, D_MODEL), F32)],
        compiler_params=_cparams(("parallel", "arbitrary")),
        name="ffn",
    )(x, h, w_in, w_in, w_out)


def _rope_tables(pos, hd, reps):
    inv = ROPE_THETA ** (-jnp.arange(0, hd, 2, dtype=F32) / hd)
    ang = pos.astype(F32)[:, None] * inv[None, :]
    cos = jnp.cos(ang)
    sin = jnp.sin(ang)
    cos2 = jnp.concatenate([cos, cos], axis=1)
    sin2 = jnp.concatenate([-sin, sin], axis=1)
    return jnp.tile(cos2, (1, reps)), jnp.tile(sin2, (1, reps))


def _pad_rows(a, batch, t, t_pad):
    if t == t_pad:
        return a
    a = a.reshape(batch, t, a.shape[-1])
    a = jnp.pad(a, ((0, 0), (0, t_pad - t), (0, 0)))
    return a.reshape(batch * t_pad, a.shape[-1])


def _unpad_rows(a, batch, t, t_pad):
    if t == t_pad:
        return a
    return a.reshape(batch, t_pad, a.shape[-1])[:, :t].reshape(batch * t, a.shape[-1])


def _layer(x, pos, batch, t, lw, caches, layer, conv_state, s0, r0):
    n = batch * t
    prompt = caches is None
    reps = max(1, 256 // t)
    cos_a, sin_a = _rope_tables(pos, A_HD, 1)
    cos_c, sin_c = _rope_tables(pos, C_DK, LANES // C_DK)
    if reps > 1:
        cos_a, sin_a, cos_c, sin_c = (jnp.tile(a, (reps, 1)) for a in (cos_a, sin_a, cos_c, sin_c))

    if prompt:
        to_rm = lambda a: a.reshape(t // RES, RES, A_HD).transpose(1, 0, 2)
        q, k, v, kv_tail = _proj_a(x, lw["g1"], lw["w_all"], lw["qn"], lw["kn"], to_rm(cos_a), to_rm(sin_a),
                                   batch, t, True)
    else:
        q, k, v = _proj_a(x, lw["g1"], lw["w_all"], lw["qn"], lw["kn"], cos_a, sin_a, batch, t, False)
    outs, lses = [], []
    for gi in range(N_GROUPS):
        if prompt:
            o, lse = _attn_prompt(q, k, v, gi, batch, t)
        else:
            o, lse = _attn_sample(q, k, v, caches[gi], layer, gi, batch, t)
        outs.append(o)
        lses.append(lse)
    new_kv = []
    for gi, (win, _) in enumerate(A_GROUPS):
        if prompt:
            keep = min(win, t)
            first = kv_tail.shape[2] - keep // RES
            rows = kv_tail[:, :, first:, 2 * gi * A_WIDTH:2 * (gi + 1) * A_WIDTH]
            new_kv.append(rows.transpose(0, 2, 1, 3).reshape(batch, keep, 2, A_HEADS, A_HD))
        else:
            cols = slice(gi * A_WIDTH, (gi + 1) * A_WIDTH)
            tail = lambda a: a[:, cols].reshape(batch, t, A_HEADS, A_HD)
            new_kv.append(jnp.stack([tail(k), tail(v)], axis=2))

    t_pad = -(-t // ROW_BLOCK) * ROW_BLOCK
    cst = jnp.pad(conv_state, ((0, 0), (SUBLANES - (B_CONV - 1), 0), (0, 0)))
    decay = (lw["alog_r"], lw["dt_r"], lw["alog_c"], lw["dt_c"])
    short = t == SUBLANES and n % ROW_BLOCK == 0
    if t_pad == t:
        act, z_b, bac, bar, p_last = _proj_b(x, lw["g1"], lw["w_all"], lw["w_all"], lw["w_all"], lw["w_bat"],
                                             cst, lw["conv_w"], t)
        conv_new = p_last[:, -(B_CONV - 1):]
        qg, kd, u, w, attn, egl = _b_prep(act, bac, bar, *decay, t, t)
    else:
        p, z_b, bac, bar = _proj_b(x, lw["g1"], lw["w_all"], lw["w_all"], lw["w_all"], lw["w_bat"])
        conv_new = jnp.concatenate([conv_state, p.reshape(batch, t, B_CONV_CH)], axis=1)[:, -(B_CONV - 1):]
        if not short:
            qg, kd, u, w, attn, egl = _b_prep(
                _pad_rows(p, batch, t, t_pad), _pad_rows(bac, batch, t, t_pad),
                _pad_rows(bar.T, batch, t, t_pad).T, *decay, t_pad, t, cst, lw["conv_w"])
    if short:
        o_b, s_new = _b_short(p, cst.reshape(n, B_CONV_CH), bac, bar, lw["conv_w"], *decay, z_b, s0, lw["gb"], t)
    else:
        o_b, s_new = _b_scan(qg, kd, u, w, attn, egl, _pad_rows(z_b, batch, t, t_pad), s0, lw["gb"], batch, t_pad)
        o_b = _unpad_rows(o_b, batch, t, t_pad)

    cq, ck, cv, cz = _proj_c(x, lw["g1"], lw["w_all"], cos_c, sin_c)
    if short:
        o_c, r_new = _c_short(cq, ck, cv, cz, r0, lw["gc"], t)
    else:
        o_c, r_new = _c_scan(*(_pad_rows(a, batch, t, t_pad) for a in (cq, ck, cv, cz)), r0, lw["gc"], batch, t_pad,
                             t)
        o_c = _unpad_rows(o_c, batch, t, t_pad)

    x, h2 = _merge(x, lw["g1"], lw["g2"], lw["w_all"], outs, lses, o_b, o_c, lw["w_oa"], lw["w_ob"], lw["w_oc"],
                   lw["w_o"], t, prompt)
    x = _ffn(x, h2, lw["w_fi"], lw["w_fo"])
    return x, new_kv, conv_new, s_new, r_new


def _layer_weights(l, norm1_g, w_in, a_q_norm_g, a_k_norm_g, b_conv_w, b_a_log, b_dt_bias, b_out_norm_g,
                   c_out_norm_g, w_out_a, w_out_b, w_out_c, w_out, norm2_g, w_ffn_in, w_ffn_out):
    o = IN_OFFS
    wl = w_in[l]
    w_ba = wl[:, o[3]:o[5]]
    pad_r = lambda a: jnp.pad(a.reshape(1, B_HEADS), ((0, 0), (B_HEADS, LANES - 2 * B_HEADS)))
    pad_c = lambda a: jnp.pad(a.reshape(B_HEADS, 1), ((B_HEADS, 2 * SUBLANES - 2 * B_HEADS), (0, 0)))
    return dict(
        g1=norm1_g[l].reshape(1, D_MODEL), g2=norm2_g[l].reshape(1, D_MODEL),
        w_all=jnp.concatenate([wl[:, o[0]:o[1]], wl[:, o[1]:o[2]], wl[:, o[9]:o[10]], wl[:, o[5]:o[9]], wl[:, o[2]:o[3]],
                               jnp.pad(w_ba, ((0, 0), (0, LANES - 2 * B_HEADS)))], axis=1).astype(BF16),
        w_bat=jnp.pad(w_ba.T, ((0, 2 * SUBLANES - 2 * B_HEADS), (0, 0))).astype(BF16),
        qn=a_q_norm_g[l].reshape(1, A_HD), kn=a_k_norm_g[l].reshape(1, A_HD),
        conv_w=b_conv_w[l],
        alog_r=pad_r(b_a_log[l]), dt_r=pad_r(b_dt_bias[l]), alog_c=pad_c(b_a_log[l]), dt_c=pad_c(b_dt_bias[l]),
        gb=b_out_norm_g[l].reshape(1, B_DV), gc=c_out_norm_g[l].reshape(1, C_DV),
        w_oa=w_out_a[l].astype(BF16), w_ob=w_out_b[l].astype(BF16), w_oc=w_out_c[l].astype(BF16),
        w_o=w_out[l].astype(BF16), w_fi=w_ffn_in[l].astype(BF16), w_fo=w_ffn_out[l].astype(BF16),
    )


def kernel(x_prompt, x_sample, cache_a_kv0, cache_a_kv1, cache_a_kv2, state_b_conv, state_b_S, state_c_R, norm1_g, w_in, a_q_norm_g, a_k_norm_g, b_conv_w, b_a_log, b_dt_bias, b_out_norm_g, c_out_norm_g, w_out_a, w_out_b, w_out_c, w_out, norm2_g, w_ffn_in, w_ffn_out):
    bp, t = x_prompt.shape[:2]
    bs, s = x_sample.shape[:2]
    depth = w_in.shape[0]
    pos_p = jnp.arange(t)
    pos_s = PAST_LEN + jnp.arange(s)
    yp = x_prompt.reshape(bp * t, D_MODEL)
    ys = x_sample.reshape(bs * s, D_MODEL)
    caches = (cache_a_kv0, cache_a_kv1, cache_a_kv2)
    zeros_conv = jnp.zeros((bp, B_CONV - 1, B_CONV_CH), F32)
    zeros_s = jnp.zeros((bp, B_HEADS, B_DK, B_DV), F32)
    zeros_r = jnp.zeros((bp, C_HEADS, C_DK, C_DV), F32)
    acc = [[] for _ in range(12)]
    for l in range(depth):
        lw = _layer_weights(l, norm1_g, w_in, a_q_norm_g, a_k_norm_g, b_conv_w, b_a_log, b_dt_bias,
                            b_out_norm_g, c_out_norm_g, w_out_a, w_out_b, w_out_c, w_out, norm2_g,
                            w_ffn_in, w_ffn_out)
        yp, kv, cv, sn, rn = _layer(yp, pos_p, bp, t, lw, None, l, zeros_conv, zeros_s, zeros_r)
        for i, a in enumerate((kv[0], kv[1], kv[2], cv, sn, rn)):
            acc[i].append(a)
        ys, kv, cv, sn, rn = _layer(ys, pos_s, bs, s, lw, caches, l, state_b_conv[l], state_b_S[l], state_c_R[l])
        for i, a in enumerate((kv[0], kv[1], kv[2], cv, sn, rn)):
            acc[6 + i].append(a)
    return (yp.reshape(bp, t, D_MODEL), ys.reshape(bs, s, D_MODEL)) + tuple(jnp.stack(a) for a in acc)
```

```python
import functools
import math

import jax
import jax.numpy as jnp
import numpy as np
from jax import lax
from jax.experimental import pallas as pl
from jax.experimental.pallas import tpu as pltpu

F32 = jnp.float32
BF16 = jnp.bfloat16

D_MODEL = 1024
PAST_LEN = 8192
A_GROUPS = ((128, 1), (512, 4), (2048, 16))
N_GROUPS = 3
A_HEADS = 4
A_HD = 128
A_WIDTH = A_HEADS * A_HD
A_KEYS = 128
B_HEADS = 4
B_DK = 128
B_DV = 128
B_CONV = 4
B_QK = B_HEADS * B_DK
B_V = B_HEADS * B_DV
B_CONV_CH = 2 * B_QK + B_V
C_HEADS = 4
C_DK = 64
C_DV = 128
C_QK = C_HEADS * C_DK
C_V = C_HEADS * C_DV
CHUNK = 64
ROPE_THETA = 10000.0
EPS = 1e-6
D_FF = 2816
IN_SIZES = (3 * N_GROUPS * A_WIDTH, B_CONV_CH, B_V, B_HEADS, B_HEADS, C_QK, C_QK, C_V, C_V, 3 * D_MODEL)
IN_OFFS = tuple(int(v) for v in np.cumsum((0,) + IN_SIZES))

W_COLS = {"a": 0, "b_qkv": 4608, "gates": 6144, "c": 9216, "b_z": 10752, "b_ba": 11264}
W_ALL = 11392

ROW_BLOCK = 128
RES = 16
SUBLANES = 8
LANES = 128
VMEM_LIMIT = 48 * 1024 * 1024


def _cparams(sem):
    return pltpu.CompilerParams(dimension_semantics=sem, vmem_limit_bytes=VMEM_LIMIT)


def _rms(x, g):
    return x * lax.rsqrt(jnp.mean(x * x, axis=-1, keepdims=True) + EPS) * g


def _silu(x):
    return x * jax.nn.sigmoid(x)


def _softplus(x):
    return jnp.maximum(x, 0.0) + jnp.log(1.0 + jnp.exp(-jnp.abs(x)))


def _dot(a, b):
    return jnp.dot(a.astype(BF16), b.astype(BF16), preferred_element_type=F32)


def _dot_nt(a, b):
    return lax.dot_general(a.astype(BF16), b.astype(BF16), (((1,), (1,)), ((), ())), preferred_element_type=F32)


def _dot_tn(a, b):
    return lax.dot_general(a.astype(BF16), b.astype(BF16), (((0,), (0,)), ((), ())), preferred_element_type=F32)


def _swap_row_grid(scr, val):
    slabs = val.shape[1] // LANES
    for c in range(slabs):
        scr[c] = val[:, c * LANES:(c + 1) * LANES]
    cols = [jnp.concatenate([scr[c, pl.ds(r, RES, stride=RES), :] for r in range(RES)], axis=0)
            for c in range(slabs)]
    return jnp.concatenate(cols, axis=1)


def _proj_a_kernel(x_ref, g1_ref, w_ref, qg_ref, kg_ref, cos_ref, sin_ref, q_ref, k_ref, v_ref, *rest, residue_major):
    x = x_ref[...]
    tm = x.shape[0]
    if residue_major:
        tail_ref, scr = rest
        x = _swap_row_grid(scr, x)
    h = _rms(x, g1_ref[...]).astype(BF16)
    cos = cos_ref[...].reshape(tm, A_HD)
    sin = sin_ref[...].reshape(tm, A_HD)

    def norm_rope(seg, g):
        y = _rms(seg, g)
        return y * cos + pltpu.roll(y, A_HD // 2, 1) * sin

    def put(ref, col, val):
        if residue_major:
            ref[:, :, col:col + val.shape[1]] = val.reshape(RES, tm // RES, val.shape[1]).astype(ref.dtype)
        else:
            ref[:, col:col + val.shape[1]] = val

    for j in range(3 * N_GROUPS):
        acc = jnp.dot(h, w_ref[:, j * A_WIDTH:(j + 1) * A_WIDTH], preferred_element_type=F32)
        if j < N_GROUPS:
            for hh in range(A_HEADS):
                sl = slice(hh * A_HD, (hh + 1) * A_HD)
                put(q_ref, j * A_WIDTH + hh * A_HD, norm_rope(acc[:, sl], qg_ref[...]) * (A_HD ** -0.5))
        elif j < 2 * N_GROUPS:
            jj = j - N_GROUPS
            for hh in range(A_HEADS):
                sl = slice(hh * A_HD, (hh + 1) * A_HD)
                val = norm_rope(acc[:, sl], kg_ref[...])
                put(k_ref, jj * A_WIDTH + hh * A_HD, val)
                if residue_major:
                    put(tail_ref, 2 * jj * A_WIDTH + hh * A_HD, val)
        else:
            jj = j - 2 * N_GROUPS
            put(v_ref, jj * A_WIDTH, acc)
            if residue_major:
                put(tail_ref, (2 * jj + 1) * A_WIDTH, acc)


def _proj_a(x, g1, w, qg, kg, cos, sin, batch, seq_len, residue_major):
    n = x.shape[0]
    tm = RES * RES
    nw = N_GROUPS * A_WIDTH
    fixed = lambda i: (0, 0)
    common = [pl.BlockSpec((1, D_MODEL), fixed), pl.BlockSpec((D_MODEL, 3 * nw), fixed),
              pl.BlockSpec((1, A_HD), fixed), pl.BlockSpec((1, A_HD), fixed)]
    if residue_major:
        tiles = seq_len // tm
        tail_tiles = min(max(wd for wd, _ in A_GROUPS), seq_len) // tm
        blk = (None, RES, tm // RES, nw)
        tab = pl.BlockSpec((RES, tm // RES, A_HD), lambda i: (0, i % tiles, 0))
        main = pl.BlockSpec(blk, lambda i: (i // tiles, 0, i % tiles, 0))
        tail = pl.BlockSpec((None, RES, tm // RES, 2 * nw),
                            lambda i: (i // tiles, 0, jnp.maximum(i % tiles - (tiles - tail_tiles), 0), 0))
        out_specs = [main] * 3 + [tail]
        out_shape = ([jax.ShapeDtypeStruct((batch, RES, seq_len // RES, nw), BF16)] * 3
                     + [jax.ShapeDtypeStruct((batch, RES, tail_tiles * tm // RES, 2 * nw), F32)])
        scratch = [pltpu.VMEM((D_MODEL // LANES, tm, LANES), F32)]
    else:
        assert cos.shape[0] == tm
        tab = pl.BlockSpec((tm, A_HD), fixed)
        out_specs = [pl.BlockSpec((tm, nw), lambda i: (i, 0))] * 3
        out_shape = [jax.ShapeDtypeStruct((n, nw), F32)] * 3
        scratch = []
    return pl.pallas_call(
        functools.partial(_proj_a_kernel, residue_major=residue_major),
        grid=(n // tm,),
        in_specs=[pl.BlockSpec((tm, D_MODEL), lambda i: (i, 0))] + common + [tab, tab],
        out_specs=out_specs,
        out_shape=out_shape,
        scratch_shapes=scratch,
        compiler_params=_cparams(("arbitrary",)),
        name="proj_a",
    )(x, g1, w, qg, kg, cos, sin)


ATTN_SUBS = 2


def _attn_prompt_kernel(q_ref, kc_ref, kp_ref, vc_ref, vp_ref, o_ref, lse_ref, *, parts):
    n = pl.program_id(2)
    per = ROW_BLOCK // parts
    qi = lax.broadcasted_iota(jnp.int32, (ROW_BLOCK, ROW_BLOCK), 0)
    kj = lax.broadcasted_iota(jnp.int32, (ROW_BLOCK, ROW_BLOCK), 1)
    qi = parts * (qi % per) + qi // per
    kj = parts * (kj % per) + kj // per
    cur_ok = kj <= qi
    prev_ok = kj >= qi
    first_ok = jnp.logical_and(prev_ok, n > 0)
    lane = lax.broadcasted_iota(jnp.int32, (ROW_BLOCK, LANES), 1)
    neg = -jnp.inf
    packed_rows = 2 * SUBLANES

    def sub(ref, half, sl):
        full = ref[:, :, sl]
        if per % packed_rows == 0:
            return full[:, half * per:(half + 1) * per].reshape(ROW_BLOCK, A_HD)
        return full.astype(F32)[:, half * per:(half + 1) * per].reshape(ROW_BLOCK, A_HD).astype(BF16)

    units = [(half, hh) for half in range(ATTN_SUBS) for hh in range(A_HEADS)]
    head = lambda hh: slice(hh * A_HD, (hh + 1) * A_HD)

    keys = {(half, hh): sub(kc_ref, half, head(hh)) for half, hh in units}
    vals = {(half, hh): sub(vc_ref, half, head(hh)) for half, hh in units}
    for hh in range(A_HEADS):
        last = kp_ref.shape[1] // per - 1
        keys[(-1, hh)] = sub(kp_ref, last, head(hh))
        vals[(-1, hh)] = sub(vp_ref, last, head(hh))

    scores = []
    for half, hh in units:
        q = sub(q_ref, half, head(hh))
        s_cur = jnp.where(cur_ok, _dot_nt(q, keys[(half, hh)]), neg)
        s_prev = jnp.where(first_ok if half == 0 else prev_ok, _dot_nt(q, keys[(half - 1, hh)]), neg)
        scores.append((s_cur, s_prev))
    probs = []
    for s_cur, s_prev in scores:
        m = jnp.max(jnp.maximum(s_cur, s_prev), axis=-1, keepdims=True)
        p_cur = jnp.exp(s_cur - m)
        p_prev = jnp.exp(s_prev - m)
        den = jnp.sum(p_cur + p_prev, axis=-1, keepdims=True)
        probs.append((p_cur, p_prev, m, den))
    lse_blk = [jnp.zeros((ROW_BLOCK, LANES), F32) for _ in range(ATTN_SUBS)]
    outs = {}
    for (half, hh), (p_cur, p_prev, m, den) in zip(units, probs):
        o = (_dot(p_cur, vals[(half, hh)]) + _dot(p_prev, vals[(half - 1, hh)])) / den
        outs[(half, hh)] = o.reshape(parts, per, A_HD)
        lse_blk[half] = jnp.where(lane // 32 == hh, m + jnp.log(den), lse_blk[half])
    for hh in range(A_HEADS):
        both = jnp.concatenate([outs[(half, hh)] for half in range(ATTN_SUBS)], axis=1)
        o_ref[:, :, head(hh)] = both.astype(o_ref.dtype)
    for half in range(ATTN_SUBS):
        lse_ref[:, half * per:(half + 1) * per, :] = lse_blk[half].reshape(parts, per, LANES)


def _attn_prompt(q, k, v, gi, batch, seq_len):
    _, dil = A_GROUPS[gi]
    parts = RES // dil
    per = ROW_BLOCK // parts
    rows = seq_len // RES
    nblk = seq_len // dil // (ATTN_SUBS * ROW_BLOCK)
    split = lambda a: a.reshape(batch, parts, dil, rows, a.shape[-1])
    cur = lambda b, r, n: (b, 0, r, n, gi)
    out = lambda b, r, n: (b, 0, r, n, 0)
    blk = (None, parts, None, ATTN_SUBS * per, A_WIDTH)
    if per % (2 * SUBLANES) == 0:
        blk_prev = (None, parts, None, per, A_WIDTH)
        prev = lambda b, r, n: (b, 0, r, jnp.maximum(ATTN_SUBS * n - 1, 0), gi)
    else:
        blk_prev = blk
        prev = lambda b, r, n: (b, 0, r, jnp.maximum(n - 1, 0), gi)
    o, lse = pl.pallas_call(
        functools.partial(_attn_prompt_kernel, parts=parts),
        grid=(batch, dil, nblk),
        in_specs=[pl.BlockSpec(blk, cur), pl.BlockSpec(blk, cur), pl.BlockSpec(blk_prev, prev),
                  pl.BlockSpec(blk, cur), pl.BlockSpec(blk_prev, prev)],
        out_specs=[pl.BlockSpec(blk, out), pl.BlockSpec((None, parts, None, ATTN_SUBS * per, LANES), out)],
        out_shape=[jax.ShapeDtypeStruct((batch, parts, dil, rows, A_WIDTH), BF16),
                   jax.ShapeDtypeStruct((batch, parts, dil, rows, LANES), F32)],
        compiler_params=_cparams(("parallel", "parallel", "arbitrary")),
        name=f"attn_prompt_g{gi}",
    )(split(q), split(k), split(k), split(v), split(v))
    return o.reshape(batch, RES, rows, A_WIDTH), lse.reshape(batch, RES, rows, LANES)


def _attn_sample_kernel(q_ref, kn_ref, vn_ref, cache_ref, o_ref, lse_ref, *, dil, n_new):
    n_res = min(dil, n_new)
    neg = -jnp.inf
    srow = lax.broadcasted_iota(jnp.int32, (n_new, A_KEYS), 0)
    mcol = lax.broadcasted_iota(jnp.int32, (n_new, A_KEYS), 1)
    in_window = mcol >= srow // dil
    row_res = [srow % dil == res for res in range(n_res)]
    srow_n = lax.broadcasted_iota(jnp.int32, (n_new, n_new), 0)
    tcol_n = lax.broadcasted_iota(jnp.int32, (n_new, n_new), 1)
    new_ok = jnp.logical_and(tcol_n <= srow_n, (srow_n - tcol_n) % dil == 0)
    heads = range(A_HEADS)

    qs = [q_ref[hh].astype(BF16) for hh in heads]
    keys = {(res, hh): cache_ref[:, res, 0, hh, :].astype(BF16) for res in range(n_res) for hh in heads}
    vals = {(res, hh): cache_ref[:, res, 1, hh, :].astype(BF16) for res in range(n_res) for hh in heads}
    raw = {key: _dot_nt(qs[key[1]], kmat) for key, kmat in keys.items()}
    s_new = [jnp.where(new_ok, _dot_nt(qs[hh], kn_ref[hh]), neg) for hh in heads]
    probs = []
    for hh in heads:
        s_buf = raw[(0, hh)]
        for res in range(1, n_res):
            s_buf = jnp.where(row_res[res], raw[(res, hh)], s_buf)
        s_buf = jnp.where(in_window, s_buf, neg)
        m = jnp.maximum(jnp.max(s_buf, axis=-1, keepdims=True), jnp.max(s_new[hh], axis=-1, keepdims=True))
        p_buf = jnp.exp(s_buf - m)
        p_new = jnp.exp(s_new[hh] - m)
        den = jnp.sum(p_buf, axis=-1, keepdims=True) + jnp.sum(p_new, axis=-1, keepdims=True)
        probs.append((p_buf, p_new, m, den))
    for hh in heads:
        p_buf, p_new, m, den = probs[hh]
        acc = _dot(p_new, vn_ref[hh])
        for res in range(n_res):
            p_res = p_buf if n_res == 1 else jnp.where(row_res[res], p_buf, 0.0)
            acc = acc + _dot(p_res, vals[(res, hh)])
        o_ref[hh] = acc / den
        lse_ref[hh] = jnp.broadcast_to(m + jnp.log(den), (n_new, LANES))


def _attn_sample_rows_kernel(q_ref, kn_ref, vn_ref, cache_ref, o_ref, lse_ref, *, dil, n_new):
    row = lax.broadcasted_iota(jnp.int32, (A_KEYS, A_HEADS, 1), 0)
    trow = lax.broadcasted_iota(jnp.int32, (n_new, A_HEADS, 1), 0)
    neg = -jnp.inf
    kn = kn_ref[...]
    vn = vn_ref[...]
    for s in range(n_new):
        res = s % dil
        first = s // dil
        q = q_ref[s][None]
        kc = cache_ref[:, res, 0]
        vc = cache_ref[:, res, 1]
        sc = jnp.sum(kc * q, axis=-1, keepdims=True)
        if first > 0:
            sc = jnp.where(row >= first, sc, neg)
        new_ok = jnp.logical_and(trow <= s, (s - trow) % dil == 0)
        sn = jnp.where(new_ok, jnp.sum(kn * q, axis=-1, keepdims=True), neg)
        m = jnp.maximum(jnp.max(sc, axis=0, keepdims=True), jnp.max(sn, axis=0, keepdims=True))
        pc = jnp.exp(sc - m)
        pn = jnp.exp(sn - m)
        den = jnp.sum(pc, axis=0, keepdims=True) + jnp.sum(pn, axis=0, keepdims=True)
        o = (jnp.sum(pc * vc, axis=0, keepdims=True) + jnp.sum(pn * vn, axis=0, keepdims=True)) / den
        o_ref[s] = o[0]
        lse_ref[s] = jnp.broadcast_to((m + jnp.log(den))[0], (A_HEADS, A_HD))


def _attn_sample(q, k, v, cache, layer, gi, batch, n_new):
    win, dil = A_GROUPS[gi]
    depth = cache.shape[0]
    assert cache.shape[2] == win and win // dil == A_KEYS
    n_res = min(dil, n_new)
    cv = cache.reshape(depth, batch, A_KEYS, dil, 2, A_HEADS, A_HD)
    cache_spec = pl.BlockSpec((None, None, A_KEYS, n_res, 2, A_HEADS, A_HD), lambda b: (layer, b, 0, 0, 0, 0, 0))
    if n_res > 1:
        heads = lambda a: a.reshape(batch, n_new, N_GROUPS, A_HEADS, A_HD)
        new = pl.BlockSpec((None, n_new, None, A_HEADS, A_HD), lambda b: (b, 0, gi, 0, 0))
        out = pl.BlockSpec((None, n_new, A_HEADS, A_HD), lambda b: (b, 0, 0, 0))
        o, lse = pl.pallas_call(
            functools.partial(_attn_sample_rows_kernel, dil=dil, n_new=n_new),
            grid=(batch,),
            in_specs=[new, new, new, cache_spec],
            out_specs=[out, out],
            out_shape=[jax.ShapeDtypeStruct((batch, n_new, A_HEADS, A_HD), F32)] * 2,
            compiler_params=_cparams(("parallel",)),
            name=f"attn_sample_g{gi}",
        )(heads(q), heads(k), heads(v), cv)
        lse = jnp.repeat(lse[..., 0], LANES // A_HEADS, axis=-1)
        return o.reshape(batch * n_new, A_WIDTH), lse.reshape(batch * n_new, LANES)
    heads = lambda a: a.reshape(batch, n_new, N_GROUPS, A_HEADS, A_HD).transpose(0, 2, 3, 1, 4)
    new = pl.BlockSpec((None, None, A_HEADS, n_new, A_HD), lambda b: (b, gi, 0, 0, 0))
    out = pl.BlockSpec((None, A_HEADS, n_new, LANES), lambda b: (b, 0, 0, 0))
    o, lse = pl.pallas_call(
        functools.partial(_attn_sample_kernel, dil=dil, n_new=n_new),
        grid=(batch,),
        in_specs=[new, new, new,
                  pl.BlockSpec((None, None, A_KEYS, n_res, 2, A_HEADS, A_HD), lambda b: (layer, b, 0, 0, 0, 0, 0))],
        out_specs=[out, out],
        out_shape=[jax.ShapeDtypeStruct((batch, A_HEADS, n_new, A_HD), F32)] * 2,
        compiler_params=_cparams(("parallel",)),
        name=f"attn_sample_g{gi}",
    )(heads(q), heads(k), heads(v), cv)
    o = o.transpose(0, 2, 1, 3)
    lse = jnp.repeat(lse[..., 0].transpose(0, 2, 1), LANES // A_HEADS, axis=-1)
    return o.reshape(batch * n_new, A_WIDTH), lse.reshape(batch * n_new, LANES)


def _causal_conv(e_scr, cw_ref, rows):
    xc = e_scr[SUBLANES:SUBLANES + rows, :] * cw_ref[B_CONV - 1:B_CONV, :]
    for kk in range(1, B_CONV):
        xc = xc + e_scr[SUBLANES - kk:SUBLANES - kk + rows, :] * cw_ref[B_CONV - 1 - kk:B_CONV - kk, :]
    return xc


def _proj_c_body(h, w_ref, cos_ref, sin_ref, q_ref, k_ref, v_ref, z_ref):
    cos = cos_ref[...]
    sin = sin_ref[...]
    lane = lax.broadcasted_iota(jnp.int32, cos.shape, 1)
    first_half = (lane % C_DK) < (C_DK // 2)

    def rope(seg):
        swapped = jnp.where(first_half, pltpu.roll(seg, LANES - C_DK // 2, 1), pltpu.roll(seg, C_DK // 2, 1))
        return seg * cos + swapped * sin

    qk = jnp.dot(h, w_ref[:, 0:2 * C_QK], preferred_element_type=F32)
    for j in range(2 * C_QK // LANES):
        seg = rope(qk[:, j * LANES:(j + 1) * LANES])
        if j < C_QK // LANES:
            q_ref[:, j * LANES:(j + 1) * LANES] = seg
        else:
            jj = j - C_QK // LANES
            k_ref[:, jj * LANES:(jj + 1) * LANES] = seg * (C_DK ** -0.5)
    v_ref[...] = jnp.dot(h, w_ref[:, 2 * C_QK:2 * C_QK + C_V], preferred_element_type=F32).astype(v_ref.dtype)
    z_ref[...] = jnp.dot(h, w_ref[:, 2 * C_QK + C_V:2 * C_QK + 2 * C_V],
                         preferred_element_type=F32).astype(z_ref.dtype)


def _proj_bc_kernel(x_ref, g1_ref, wqkv_ref, wz_ref, wba_ref, wbat_ref, wc_ref, cosc_ref, sinc_ref, *rest,
                    fuse_conv, tiles_per_seq):
    h = _rms(x_ref[...], g1_ref[...]).astype(BF16)
    tm = h.shape[0]
    if fuse_conv:
        cst_ref, cw_ref, p_ref, z_ref, bac_ref, bar_ref, cq_ref, ck_ref, cv_ref, cz_ref, ptail_ref, e_scr = rest
        first = pl.program_id(0) % tiles_per_seq == 0

        @pl.when(first)
        def _():
            e_scr[0:SUBLANES, :] = cst_ref[...]

        @pl.when(jnp.logical_not(first))
        def _():
            e_scr[0:SUBLANES, :] = e_scr[tm:tm + SUBLANES, :]

        for j in range(3):
            sl = slice(j * B_QK, (j + 1) * B_QK)
            e_scr[SUBLANES:SUBLANES + tm, sl] = jnp.dot(h, wqkv_ref[:, sl], preferred_element_type=F32)
        p_ref[...] = _silu(_causal_conv(e_scr, cw_ref, tm)).astype(p_ref.dtype)
        ptail_ref[...] = e_scr[tm:tm + SUBLANES, :]
    else:
        p_ref, z_ref, bac_ref, bar_ref, cq_ref, ck_ref, cv_ref, cz_ref = rest
        for j in range(3):
            sl = slice(j * B_QK, (j + 1) * B_QK)
            p_ref[:, sl] = jnp.dot(h, wqkv_ref[:, sl], preferred_element_type=F32)
    z_ref[...] = jnp.dot(h, wz_ref[...], preferred_element_type=F32).astype(z_ref.dtype)
    bac_ref[...] = jnp.dot(h, wba_ref[...], preferred_element_type=F32)
    bar_ref[...] = lax.dot_general(wbat_ref[...], h, (((1,), (1,)), ((), ())), preferred_element_type=F32)
    _proj_c_body(h, wc_ref, cosc_ref, sinc_ref, cq_ref, ck_ref, cv_ref, cz_ref)


def _proj_bc(x, g1, w_all, wbat, cos_c, sin_c, cstate=None, conv_w=None, seq_len=None):
    n = x.shape[0]
    tm = 256
    tab_blocks = cos_c.shape[0] // tm
    row = lambda i: (i, 0)
    fixed = lambda i: (0, 0)
    tab = (lambda i: (i % tab_blocks, 0)) if tab_blocks > 1 else fixed
    fuse_conv = cstate is not None
    c_width = 2 * C_QK + 2 * C_V
    in_specs = [
        pl.BlockSpec((tm, D_MODEL), row),
        pl.BlockSpec((1, D_MODEL), fixed),
        pl.BlockSpec((D_MODEL, B_CONV_CH), lambda i: (0, W_COLS["b_qkv"] // B_CONV_CH)),
        pl.BlockSpec((D_MODEL, B_V), lambda i: (0, W_COLS["b_z"] // B_V)),
        pl.BlockSpec((D_MODEL, LANES), lambda i: (0, W_COLS["b_ba"] // LANES)),
        pl.BlockSpec((2 * SUBLANES, D_MODEL), fixed),
        pl.BlockSpec((D_MODEL, c_width), lambda i: (0, W_COLS["c"] // c_width)),
        pl.BlockSpec((tm, LANES), tab),
        pl.BlockSpec((tm, LANES), tab),
    ]
    out_specs = [pl.BlockSpec((tm, B_CONV_CH), row), pl.BlockSpec((tm, B_V), row),
                 pl.BlockSpec((tm, LANES), row), pl.BlockSpec((2 * SUBLANES, tm), lambda i: (0, i)),
                 pl.BlockSpec((tm, C_QK), row), pl.BlockSpec((tm, C_QK), row),
                 pl.BlockSpec((tm, C_V), row), pl.BlockSpec((tm, C_V), row)]
    out_shape = [jax.ShapeDtypeStruct((n, B_CONV_CH), BF16 if fuse_conv else F32),
                 jax.ShapeDtypeStruct((n, B_V), BF16),
                 jax.ShapeDtypeStruct((n, LANES), F32), jax.ShapeDtypeStruct((2 * SUBLANES, n), F32),
                 jax.ShapeDtypeStruct((n, C_QK), F32), jax.ShapeDtypeStruct((n, C_QK), F32),
                 jax.ShapeDtypeStruct((n, C_V), BF16), jax.ShapeDtypeStruct((n, C_V), BF16)]
    args = [x, g1, w_all, w_all, w_all, wbat, w_all, cos_c, sin_c]
    scratch = []
    tiles = None
    if fuse_conv:
        tiles = seq_len // tm
        per_seq = pl.BlockSpec((None, SUBLANES, B_CONV_CH), lambda i: (i // tiles, 0, 0))
        in_specs += [per_seq, pl.BlockSpec((B_CONV, B_CONV_CH), fixed)]
        out_specs.append(per_seq)
        out_shape.append(jax.ShapeDtypeStruct((n // seq_len, SUBLANES, B_CONV_CH), F32))
        args += [cstate, conv_w]
        scratch = [pltpu.VMEM((SUBLANES + tm, B_CONV_CH), F32)]
    return pl.pallas_call(
        functools.partial(_proj_bc_kernel, fuse_conv=fuse_conv, tiles_per_seq=tiles),
        grid=(n // tm,),
        in_specs=in_specs,
        out_specs=out_specs,
        out_shape=out_shape,
        scratch_shapes=scratch,
        compiler_params=_cparams(("arbitrary",) if fuse_conv else ("parallel",)),
        name="proj_bc",
    )(*args)


def _b_prep_kernel(*refs, rows, blocks_per_seq, t_valid, conv_done):
    i = pl.program_id(0)
    blk = i % blocks_per_seq
    if conv_done:
        (p_ref, bac_ref, bar_ref, alog_r_ref, dt_r_ref, alog_c_ref, dt_c_ref,
         qg_ref, kd_ref, u_ref, w_ref, attn_ref, egl_ref) = refs
        act = p_ref[...].astype(F32)
    else:
        (p_ref, halo_ref, cst_ref, cw_ref, bac_ref, bar_ref, alog_r_ref, dt_r_ref, alog_c_ref, dt_c_ref,
         qg_ref, kd_ref, u_ref, w_ref, attn_ref, egl_ref, e_scr) = refs
        e_scr[0:SUBLANES, :] = jnp.where(blk == 0, cst_ref[...], halo_ref[...])
        e_scr[SUBLANES:SUBLANES + rows, :] = p_ref[...]
        act = _silu(_causal_conv(e_scr, cw_ref, rows))

    ri = lax.broadcasted_iota(jnp.int32, (rows, LANES), 0)
    li16 = lax.broadcasted_iota(jnp.int32, (2 * SUBLANES, rows), 1)
    li1 = lax.broadcasted_iota(jnp.int32, (1, LANES), 1)
    masked = t_valid < blocks_per_seq * rows
    if masked:
        row_ok = (blk * rows + ri) < t_valid
        col_ok = (blk * rows + li16) < t_valid
        act = jnp.where(ri[:, 0:1] + blk * rows < t_valid, act, 0.0)

    head_lane = jnp.logical_and(li1 >= B_HEADS, li1 < 2 * B_HEADS)
    a_r = jnp.where(head_lane, -jnp.exp(alog_r_ref[...]), 0.0)
    g_col = a_r * _softplus(bac_ref[...] + dt_r_ref[...])
    si = lax.broadcasted_iota(jnp.int32, (2 * SUBLANES, 1), 0)
    head_sub = jnp.logical_and(si >= B_HEADS, si < 2 * B_HEADS)
    a_c = jnp.where(head_sub, -jnp.exp(alog_c_ref[...]), 0.0)
    g_row = a_c * _softplus(bar_ref[...] + dt_c_ref[...])
    if masked:
        g_col = jnp.where(row_ok, g_col, 0.0)
        g_row = jnp.where(col_ok, g_row, 0.0)

    rpos = ri % CHUNK
    lpos = li16 % CHUNK
    gc = g_col
    rev = g_col
    gcr = g_row
    step = 1
    while step < CHUNK:
        gc = gc + jnp.where(rpos >= step, pltpu.roll(gc, step, 0), 0.0)
        rev = rev + jnp.where(rpos < CHUNK - step, pltpu.roll(rev, rows - step, 0), 0.0)
        gcr = gcr + jnp.where(lpos >= step, pltpu.roll(gcr, step, 1), 0.0)
        step *= 2
    rev = rev - g_col
    egl_ref[...] = jnp.exp(gc + rev)

    bi = lax.broadcasted_iota(jnp.int32, (ROW_BLOCK, ROW_BLOCK), 0)
    bj = lax.broadcasted_iota(jnp.int32, (ROW_BLOCK, ROW_BLOCK), 1)
    same = (bi // CHUNK) == (bj // CHUNK)
    incl = jnp.logical_and(same, bi >= bj)
    strict = jnp.logical_and(same, bi > bj)
    eye = (bi == bj).astype(F32)

    units = [(sb, hh) for sb in range(rows // ROW_BLOCK) for hh in range(B_HEADS)]
    lows, rhss = [], []
    for sb, hh in units:
        rs = slice(sb * ROW_BLOCK, (sb + 1) * ROW_BLOCK)
        sl = slice(hh * B_DK, (hh + 1) * B_DK)
        gc_c = gc[rs, B_HEADS + hh:B_HEADS + hh + 1]
        gc_r = gcr[B_HEADS + hh:B_HEADS + hh + 1, rs]
        dec = jnp.where(incl, jnp.exp(jnp.where(incl, gc_c - gc_r, 0.0)), 0.0)
        q = act[rs, sl]
        q = q * lax.rsqrt(jnp.sum(q * q, axis=-1, keepdims=True) + EPS) * (B_DK ** -0.5)
        k = act[rs, B_QK + hh * B_DK:B_QK + (hh + 1) * B_DK]
        k = k * lax.rsqrt(jnp.sum(k * k, axis=-1, keepdims=True) + EPS)
        v = act[rs, 2 * B_QK + hh * B_DV:2 * B_QK + (hh + 1) * B_DV]
        beta = jax.nn.sigmoid(bac_ref[rs, hh:hh + 1])
        kb = k * beta
        kbf = k.astype(BF16)
        lows.append(jnp.where(strict, dec * _dot_nt(kb, kbf), 0.0))
        attn_ref[rs, sl] = (dec * _dot_nt(q, kbf)).astype(attn_ref.dtype)
        rhss.append(jnp.concatenate([v * beta, kb * jnp.exp(gc_c)], axis=1).astype(BF16))
        qg_ref[rs, sl] = (q * jnp.exp(gc_c)).astype(qg_ref.dtype)
        kd_ref[rs, sl] = (k * jnp.exp(rev[rs, B_HEADS + hh:B_HEADS + hh + 1])).astype(kd_ref.dtype)

    tinvs = [eye - low for low in lows]
    pws = lows
    sq = 2
    while sq < CHUNK:
        pws = [_dot(pw, pw) for pw in pws]
        tinvs = [tinv + _dot(tinv, pw) for tinv, pw in zip(tinvs, pws)]
        sq *= 2
    for (sb, hh), tinv, rhs in zip(units, tinvs, rhss):
        rs = slice(sb * ROW_BLOCK, (sb + 1) * ROW_BLOCK)
        sol = _dot(tinv, rhs)
        u_ref[rs, hh * B_DV:(hh + 1) * B_DV] = sol[:, :B_DV]
        w_ref[rs, hh * B_DK:(hh + 1) * B_DK] = sol[:, B_DV:].astype(w_ref.dtype)


def _b_prep(p, bac, bar, alog_r, dt_r, alog_c, dt_c, seq_len, t_valid, cstate=None, cw=None):
    n = p.shape[0]
    rows = min(seq_len, 2 * ROW_BLOCK)
    bps = seq_len // rows
    row = lambda i: (i, 0)
    fixed = lambda i: (0, 0)
    per_row = rows // SUBLANES
    conv_done = cstate is None
    wide = lambda dt: jax.ShapeDtypeStruct((n, B_V), dt)
    in_specs = [pl.BlockSpec((rows, B_CONV_CH), row)]
    args = [p]
    scratch = []
    if not conv_done:
        in_specs += [pl.BlockSpec((SUBLANES, B_CONV_CH), lambda i: (jnp.maximum(i * per_row - 1, 0), 0)),
                     pl.BlockSpec((None, SUBLANES, B_CONV_CH), lambda i: (i // bps, 0, 0)),
                     pl.BlockSpec((B_CONV, B_CONV_CH), fixed)]
        args += [p, cstate, cw]
        scratch = [pltpu.VMEM((SUBLANES + rows, B_CONV_CH), F32)]
    in_specs += [pl.BlockSpec((rows, LANES), row),
                 pl.BlockSpec((2 * SUBLANES, rows), lambda i: (0, i)),
                 pl.BlockSpec((1, LANES), fixed),
                 pl.BlockSpec((1, LANES), fixed),
                 pl.BlockSpec((2 * SUBLANES, 1), fixed),
                 pl.BlockSpec((2 * SUBLANES, 1), fixed)]
    args += [bac, bar, alog_r, dt_r, alog_c, dt_c]
    return pl.pallas_call(
        functools.partial(_b_prep_kernel, rows=rows, blocks_per_seq=bps, t_valid=t_valid, conv_done=conv_done),
        grid=(n // rows,),
        in_specs=in_specs,
        out_specs=[pl.BlockSpec((rows, B_V), row)] * 5 + [pl.BlockSpec((rows, LANES), row)],
        out_shape=[wide(BF16), wide(BF16), wide(F32), wide(BF16), wide(BF16), jax.ShapeDtypeStruct((n, LANES), F32)],
        scratch_shapes=scratch,
        compiler_params=_cparams(("parallel",)),
        name="b_prep",
    )(*args)


def _b_scan_kernel(qg_ref, kd_ref, u_ref, w_ref, attn_ref, egl_ref, z_ref, s0_ref, gout_ref, o_ref, s_ref, *, nb):
    c = pl.program_id(1)

    @pl.when(c == 0)
    def _():
        s_ref[...] = s0_ref[...]

    half = c % (ROW_BLOCK // CHUNK)
    rgrp = lax.broadcasted_iota(jnp.int32, (ROW_BLOCK, B_DV), 0) // CHUNK
    here = rgrp == half
    units = [(b, hh) for b in range(nb) for hh in range(B_HEADS)]
    head = lambda hh: slice(hh * B_DV, (hh + 1) * B_DV)
    states = [s_ref[b, hh] for b, hh in units]
    proj = [_dot(jnp.concatenate([w_ref[b, :, head(hh)], qg_ref[b, :, head(hh)]], axis=0), s)
            for (b, hh), s in zip(units, states)]
    v_new = [u_ref[b, :, head(hh)] - pr[:CHUNK] for (b, hh), pr in zip(units, proj)]
    outs = []
    for (b, hh), pr, vn in zip(units, proj, v_new):
        v_full = jnp.where(here, jnp.concatenate([vn] * (ROW_BLOCK // CHUNK), axis=0), 0.0)
        outs.append(pr[CHUNK:] + _dot(attn_ref[b, :, head(hh)], v_full))
    for (b, hh), s, vn in zip(units, states, v_new):
        decay = egl_ref[b, 0:1, B_HEADS + hh:B_HEADS + hh + 1]
        s_ref[b, hh] = s * decay + _dot_tn(kd_ref[b, :, head(hh)], vn)
    for (b, hh), o in zip(units, outs):
        gate = _silu(z_ref[b, :, head(hh)].astype(F32))
        o_ref[b, :, head(hh)] = (_rms(o, gout_ref[...]) * gate).astype(o_ref.dtype)


def _b_scan(qg, kd, u, w, attn, egl, z, s0, gout, batch, seq_len):
    nb = 4
    nchunk = seq_len // CHUNK
    v3 = lambda a: a.reshape(batch, seq_len, a.shape[-1])
    rows = lambda bi, c: (bi, c, 0)
    state = lambda bi, c: (bi, 0, 0, 0)
    wide = pl.BlockSpec((nb, CHUNK, B_V), rows)
    o, s_new = pl.pallas_call(
        functools.partial(_b_scan_kernel, nb=nb),
        grid=(batch // nb, nchunk),
        in_specs=[wide] * 5 + [pl.BlockSpec((nb, CHUNK, LANES), rows), wide,
                               pl.BlockSpec((nb, B_HEADS, B_DK, B_DV), state),
                               pl.BlockSpec((1, B_DV), lambda bi, c: (0, 0))],
        out_specs=[wide, pl.BlockSpec((nb, B_HEADS, B_DK, B_DV), state)],
        out_shape=[jax.ShapeDtypeStruct((batch, seq_len, B_V), BF16),
                   jax.ShapeDtypeStruct((batch, B_HEADS, B_DK, B_DV), F32)],
        compiler_params=_cparams(("parallel", "arbitrary")),
        name="b_scan",
    )(v3(qg), v3(kd), v3(u), v3(w), v3(attn), v3(egl), v3(z), s0, gout)
    return o.reshape(batch * seq_len, B_V), s_new


def _log_gamma(hh):
    return math.log1p(-(2.0 ** (-5.0 - hh)))


def _c_scan_kernel(q_ref, k_ref, v_ref, z_ref, r0_ref, gout_ref, o_ref, rout_ref, r_ref, *, nb, t_valid):
    c = pl.program_id(1)
    rows = ROW_BLOCK
    state_blocks = [(b, hh, slice(hh * C_DK, (hh + 1) * C_DK), slice(hh * C_DV, (hh + 1) * C_DV))
                    for b in range(nb) for hh in range(C_HEADS)]

    @pl.when(c == 0)
    def _():
        r_ref[...] = jnp.zeros_like(r_ref)
        for b, hh, rsl, csl in state_blocks:
            r_ref[b, rsl, csl] = r0_ref[b, hh]

    left = jnp.clip(t_valid - c * rows, 0, rows)
    ri = lax.broadcasted_iota(jnp.int32, (rows, rows), 0)
    ci = lax.broadcasted_iota(jnp.int32, (rows, rows), 1)
    cnt_i = jnp.minimum(ri + 1, left).astype(F32)
    cnt_j = jnp.minimum(ci + 1, left).astype(F32)
    incl = ri >= ci
    steps = jnp.where(incl, cnt_i - cnt_j, 0.0)
    cnt_col = cnt_i[:, 0:1]
    left_f = left.astype(F32)
    qk_lane = lax.broadcasted_iota(jnp.int32, (1, C_QK), 1) // C_DK
    lg_lane = jnp.zeros((1, C_QK), F32)
    for hh in range(C_HEADS):
        lg_lane = jnp.where(qk_lane == hh, _log_gamma(hh), lg_lane)
    qk_sub = lax.broadcasted_iota(jnp.int32, (C_QK, 1), 0) // C_DK
    lg_sub = jnp.zeros((C_QK, 1), F32)
    for hh in range(C_HEADS):
        lg_sub = jnp.where(qk_sub == hh, _log_gamma(hh), lg_sub)
    q_scale = jnp.exp(cnt_col * lg_lane)
    k_scale = jnp.exp((left_f - cnt_col) * lg_lane)
    r_scale = jnp.exp(left_f * lg_sub)
    row_ok = (lax.broadcasted_iota(jnp.int32, (rows, 1), 0) + c * rows) < t_valid
    diag = (lax.broadcasted_iota(jnp.int32, (C_QK, C_V), 0) // C_DK) == (
        lax.broadcasted_iota(jnp.int32, (C_QK, C_V), 1) // C_DV)

    head = lambda hh: slice(hh * C_DV, (hh + 1) * C_DV)
    decays = [jnp.where(incl, jnp.exp(steps * _log_gamma(hh)), 0.0) for hh in range(C_HEADS)]
    qs = [q_ref[b] for b in range(nb)]
    ks = [jnp.where(row_ok, k_ref[b], 0.0) for b in range(nb)]
    vs = [v_ref[b].astype(BF16) for b in range(nb)]
    rs = [r_ref[b] for b in range(nb)]
    inters = [_dot(q * q_scale, r) for q, r in zip(qs, rs)]
    units = [(b, hh) for b in range(nb) for hh in range(C_HEADS)]
    atts = [decays[hh] * _dot_nt(qs[b], jnp.where(qk_lane == hh, ks[b], 0.0)) for b, hh in units]
    outs = [inters[b][:, head(hh)] + _dot(att, vs[b][:, head(hh)]) for (b, hh), att in zip(units, atts)]
    for b in range(nb):
        r_ref[b] = rs[b] * r_scale + jnp.where(diag, _dot_tn(ks[b] * k_scale, vs[b]), 0.0)
    for (b, hh), o in zip(units, outs):
        gate = _silu(z_ref[b, :, head(hh)].astype(F32))
        o_ref[b, :, head(hh)] = (_rms(o, gout_ref[...]) * gate).astype(o_ref.dtype)

    @pl.when(c == pl.num_programs(1) - 1)
    def _():
        for b, hh, rsl, csl in state_blocks:
            rout_ref[b, hh] = r_ref[b, rsl, csl]


def _c_scan(q, k, v, z, r0, gout, batch, seq_len, t_valid):
    nb = 4
    nblk = seq_len // ROW_BLOCK
    v3 = lambda a: a.reshape(batch, seq_len, a.shape[-1])
    rows = lambda bi, c: (bi, c, 0)
    state = pl.BlockSpec((nb, C_HEADS, C_DK, C_DV), lambda bi, c: (bi, 0, 0, 0))
    o, r_new = pl.pallas_call(
        functools.partial(_c_scan_kernel, nb=nb, t_valid=t_valid),
        grid=(batch // nb, nblk),
        in_specs=[pl.BlockSpec((nb, ROW_BLOCK, C_QK), rows), pl.BlockSpec((nb, ROW_BLOCK, C_QK), rows),
                  pl.BlockSpec((nb, ROW_BLOCK, C_V), rows), pl.BlockSpec((nb, ROW_BLOCK, C_V), rows),
                  state, pl.BlockSpec((1, C_DV), lambda bi, c: (0, 0))],
        out_specs=[pl.BlockSpec((nb, ROW_BLOCK, C_V), rows), state],
        out_shape=[jax.ShapeDtypeStruct((batch, seq_len, C_V), BF16),
                   jax.ShapeDtypeStruct((batch, C_HEADS, C_DK, C_DV), F32)],
        scratch_shapes=[pltpu.VMEM((nb, C_QK, C_V), F32)],
        compiler_params=_cparams(("parallel", "arbitrary")),
        name="c_scan",
    )(v3(q), v3(k), v3(v), v3(z), r0, gout)
    return o.reshape(batch * seq_len, C_V), r_new


def _b_short_kernel(p_ref, st_ref, bac_ref, bar_ref, cw_ref, alog_r_ref, dt_r_ref, alog_c_ref, dt_c_ref,
                    z_ref, s0_ref, gout_ref, o_ref, s_ref, e_new, e_old, *, t):
    rows = ROW_BLOCK
    nseq = rows // t
    e_new[0:SUBLANES, :] = jnp.zeros((SUBLANES, B_CONV_CH), F32)
    e_new[SUBLANES:SUBLANES + rows, :] = p_ref[...]
    e_old[0:rows, :] = st_ref[...]
    e_old[rows:rows + SUBLANES, :] = jnp.zeros((SUBLANES, B_CONV_CH), F32)
    pos = lax.broadcasted_iota(jnp.int32, (rows, 1), 0) % t
    xc = e_new[SUBLANES:SUBLANES + rows, :] * cw_ref[B_CONV - 1:B_CONV, :]
    for kk in range(1, B_CONV):
        window = slice(SUBLANES - kk, SUBLANES - kk + rows)
        src = jnp.where(pos >= kk, e_new[window, :], e_old[window, :])
        xc = xc + src * cw_ref[B_CONV - 1 - kk:B_CONV - kk, :]
    act = _silu(xc)

    ri = lax.broadcasted_iota(jnp.int32, (rows, LANES), 0)
    li16 = lax.broadcasted_iota(jnp.int32, (2 * SUBLANES, rows), 1)
    li1 = lax.broadcasted_iota(jnp.int32, (1, LANES), 1)
    head_lane = jnp.logical_and(li1 >= B_HEADS, li1 < 2 * B_HEADS)
    a_r = jnp.where(head_lane, -jnp.exp(alog_r_ref[...]), 0.0)
    g_col = a_r * _softplus(bac_ref[...] + dt_r_ref[...])
    si = lax.broadcasted_iota(jnp.int32, (2 * SUBLANES, 1), 0)
    head_sub = jnp.logical_and(si >= B_HEADS, si < 2 * B_HEADS)
    a_c = jnp.where(head_sub, -jnp.exp(alog_c_ref[...]), 0.0)
    g_row = a_c * _softplus(bar_ref[...] + dt_c_ref[...])
    rpos = ri % t
    lpos = li16 % t
    gc, rev, gcr = g_col, g_col, g_row
    step = 1
    while step < t:
        gc = gc + jnp.where(rpos >= step, pltpu.roll(gc, step, 0), 0.0)
        rev = rev + jnp.where(rpos < t - step, pltpu.roll(rev, rows - step, 0), 0.0)
        gcr = gcr + jnp.where(lpos >= step, pltpu.roll(gcr, step, 1), 0.0)
        step *= 2
    rev = rev - g_col
    egl = jnp.exp(gc + rev)

    bi = lax.broadcasted_iota(jnp.int32, (rows, rows), 0)
    bj = lax.broadcasted_iota(jnp.int32, (rows, rows), 1)
    same = (bi // t) == (bj // t)
    incl = jnp.logical_and(same, bi >= bj)
    strict = jnp.logical_and(same, bi > bj)
    eye = (bi == bj).astype(F32)

    lows, rhss, attns, qgs, kds = [], [], [], [], []
    for hh in range(B_HEADS):
        sl = slice(hh * B_DK, (hh + 1) * B_DK)
        gc_c = gc[:, B_HEADS + hh:B_HEADS + hh + 1]
        gc_r = gcr[B_HEADS + hh:B_HEADS + hh + 1, :]
        dec = jnp.where(incl, jnp.exp(jnp.where(incl, gc_c - gc_r, 0.0)), 0.0)
        q = act[:, sl]
        q = q * lax.rsqrt(jnp.sum(q * q, axis=-1, keepdims=True) + EPS) * (B_DK ** -0.5)
        k = act[:, B_QK + hh * B_DK:B_QK + (hh + 1) * B_DK]
        k = k * lax.rsqrt(jnp.sum(k * k, axis=-1, keepdims=True) + EPS)
        v = act[:, 2 * B_QK + hh * B_DV:2 * B_QK + (hh + 1) * B_DV]
        beta = jax.nn.sigmoid(bac_ref[:, hh:hh + 1])
        kb = k * beta
        kbf = k.astype(BF16)
        lows.append(jnp.where(strict, dec * _dot_nt(kb, kbf), 0.0))
        attns.append(dec * _dot_nt(q, kbf))
        rhss.append(jnp.concatenate([v * beta, kb * jnp.exp(gc_c)], axis=1))
        qgs.append(q * jnp.exp(gc_c))
        kds.append(k * jnp.exp(rev[:, B_HEADS + hh:B_HEADS + hh + 1]))
    tinvs = [eye - low for low in lows]
    pws = lows
    sq = 2
    while sq < t:
        pws = [_dot(pw, pw) for pw in pws]
        tinvs = [tinv + _dot(tinv, pw) for tinv, pw in zip(tinvs, pws)]
        sq *= 2
    sols = [_dot(tinv, rhs) for tinv, rhs in zip(tinvs, rhss)]

    rgrp = lax.broadcasted_iota(jnp.int32, (rows, B_DV), 0) // t
    units = [(j, hh) for hh in range(B_HEADS) for j in range(nseq)]
    rws = lambda j: slice(j * t, (j + 1) * t)
    states = [s0_ref[j, hh] for j, hh in units]
    proj = [_dot(jnp.concatenate([sols[hh][rws(j), B_DV:], qgs[hh][rws(j)]], axis=0), s)
            for (j, hh), s in zip(units, states)]
    v_new = [sols[hh][rws(j), :B_DV] - pr[:t] for (j, hh), pr in zip(units, proj)]
    outs = []
    for (j, hh), pr, vn in zip(units, proj, v_new):
        v_full = jnp.where(rgrp == j, jnp.concatenate([vn] * nseq, axis=0), 0.0)
        outs.append(pr[t:] + _dot(attns[hh][rws(j)], v_full))
    for (j, hh), s, vn in zip(units, states, v_new):
        decay = egl[j * t:j * t + 1, B_HEADS + hh:B_HEADS + hh + 1]
        s_ref[j, hh] = s * decay + _dot_tn(kds[hh][rws(j)], vn)
    for hh in range(B_HEADS):
        sl = slice(hh * B_DV, (hh + 1) * B_DV)
        o = jnp.concatenate(outs[hh * nseq:(hh + 1) * nseq], axis=0)
        gate = _silu(z_ref[:, sl].astype(F32))
        o_ref[:, sl] = (_rms(o, gout_ref[...]) * gate).astype(o_ref.dtype)


def _b_short(p, st, bac, bar, cw, alog_r, dt_r, alog_c, dt_c, z, s0, gout, t):
    assert t == SUBLANES
    n = p.shape[0]
    nseq = ROW_BLOCK // t
    row = lambda i: (i, 0)
    fixed = lambda i: (0, 0)
    state = pl.BlockSpec((nseq, B_HEADS, B_DK, B_DV), lambda i: (i, 0, 0, 0))
    return pl.pallas_call(
        functools.partial(_b_short_kernel, t=t),
        grid=(n // ROW_BLOCK,),
        in_specs=[pl.BlockSpec((ROW_BLOCK, B_CONV_CH), row), pl.BlockSpec((ROW_BLOCK, B_CONV_CH), row),
                  pl.BlockSpec((ROW_BLOCK, LANES), row), pl.BlockSpec((2 * SUBLANES, ROW_BLOCK), lambda i: (0, i)),
                  pl.BlockSpec((B_CONV, B_CONV_CH), fixed),
                  pl.BlockSpec((1, LANES), fixed), pl.BlockSpec((1, LANES), fixed),
                  pl.BlockSpec((2 * SUBLANES, 1), fixed), pl.BlockSpec((2 * SUBLANES, 1), fixed),
                  pl.BlockSpec((ROW_BLOCK, B_V), row), state, pl.BlockSpec((1, B_DV), fixed)],
        out_specs=[pl.BlockSpec((ROW_BLOCK, B_V), row), state],
        out_shape=[jax.ShapeDtypeStruct((n, B_V), BF16), jax.ShapeDtypeStruct(s0.shape, F32)],
        scratch_shapes=[pltpu.VMEM((SUBLANES + ROW_BLOCK, B_CONV_CH), F32)] * 2,
        compiler_params=_cparams(("parallel",)),
        name="b_short",
    )(p, st, bac, bar, cw, alog_r, dt_r, alog_c, dt_c, z, s0, gout)


def _c_short_kernel(q_ref, k_ref, v_ref, z_ref, r0_ref, gout_ref, o_ref, rout_ref, *, t):
    rows = ROW_BLOCK
    nseq = rows // t
    ri = lax.broadcasted_iota(jnp.int32, (rows, rows), 0)
    ci = lax.broadcasted_iota(jnp.int32, (rows, rows), 1)
    causal = jnp.logical_and(ri // t == ci // t, ri >= ci)
    steps = jnp.where(causal, (ri - ci).astype(F32), 0.0)
    cnt = (lax.broadcasted_iota(jnp.int32, (rows, 1), 0) % t + 1).astype(F32)
    qk_lane = lax.broadcasted_iota(jnp.int32, (1, C_QK), 1) // C_DK
    lg_lane = jnp.zeros((1, C_QK), F32)
    qk_sub = lax.broadcasted_iota(jnp.int32, (C_QK, 1), 0) // C_DK
    lg_sub = jnp.zeros((C_QK, 1), F32)
    for hh in range(C_HEADS):
        lg_lane = jnp.where(qk_lane == hh, _log_gamma(hh), lg_lane)
        lg_sub = jnp.where(qk_sub == hh, _log_gamma(hh), lg_sub)
    q = q_ref[...]
    k = k_ref[...]
    v = v_ref[...].astype(F32)
    qd = q * jnp.exp(cnt * lg_lane)
    kd = k * jnp.exp((t - cnt) * lg_lane)
    r_scale = jnp.exp(t * lg_sub)
    head = lambda hh: slice(hh * C_DV, (hh + 1) * C_DV)
    rws = lambda j: slice(j * t, (j + 1) * t)

    intra = []
    for hh in range(C_HEADS):
        att = jnp.exp(steps * _log_gamma(hh)) * _dot_nt(q, jnp.where(qk_lane == hh, k, 0.0))
        intra.append(_dot(jnp.where(causal, att, 0.0), v[:, head(hh)]))
    stacks = [r0_ref[j].reshape(C_QK, C_DV) for j in range(nseq)]
    units = [(j, hh) for hh in range(C_HEADS) for j in range(nseq)]
    inter = [_dot(jnp.where(qk_lane == hh, qd[rws(j)], 0.0), stacks[j]) for j, hh in units]
    upd = [_dot_tn(jnp.where(qk_lane == hh, kd[rws(j)], 0.0), v[rws(j), head(hh)]) for j, hh in units]
    for j in range(nseq):
        new = stacks[j] * r_scale
        for hh in range(C_HEADS):
            new = new + upd[hh * nseq + j]
        rout_ref[j] = new.reshape(C_HEADS, C_DK, C_DV)
    for hh in range(C_HEADS):
        o = intra[hh] + jnp.concatenate(inter[hh * nseq:(hh + 1) * nseq], axis=0)
        gate = _silu(z_ref[:, head(hh)].astype(F32))
        o_ref[:, head(hh)] = (_rms(o, gout_ref[...]) * gate).astype(o_ref.dtype)


def _c_short(q, k, v, z, r0, gout, t):
    n = q.shape[0]
    nseq = ROW_BLOCK // t
    row = lambda i: (i, 0)
    state = pl.BlockSpec((nseq, C_HEADS, C_DK, C_DV), lambda i: (i, 0, 0, 0))
    return pl.pallas_call(
        functools.partial(_c_short_kernel, t=t),
        grid=(n // ROW_BLOCK,),
        in_specs=[pl.BlockSpec((ROW_BLOCK, C_QK), row), pl.BlockSpec((ROW_BLOCK, C_QK), row),
                  pl.BlockSpec((ROW_BLOCK, C_V), row), pl.BlockSpec((ROW_BLOCK, C_V), row),
                  state, pl.BlockSpec((1, C_DV), lambda i: (0, 0))],
        out_specs=[pl.BlockSpec((ROW_BLOCK, C_V), row), state],
        out_shape=[jax.ShapeDtypeStruct((n, C_V), BF16), jax.ShapeDtypeStruct(r0.shape, F32)],
        compiler_params=_cparams(("parallel",)),
        name="c_short",
    )(q, k, v, z, r0, gout)


def _merge_kernel(x_ref, g1_ref, g2_ref, wg_ref, o0_ref, o1_ref, o2_ref, l0_ref, l1_ref, l2_ref, ob_ref, oc_ref,
                  wa_ref, wb_ref, wc_ref, wo_ref, y_ref, h2_ref, *scr, residue_major):
    x = x_ref[...]
    tm = x.shape[0]
    h = _rms(x, g1_ref[...]).astype(BF16)
    lses = [r[...].reshape(tm, LANES) for r in (l0_ref, l1_ref, l2_ref)]
    outs = [r[...].reshape(tm, A_WIDTH) for r in (o0_ref, o1_ref, o2_ref)]
    heads = []
    for hh in range(A_HEADS):
        sl = slice(hh * A_HD, (hh + 1) * A_HD)
        ls = [l[:, 32 * hh:32 * hh + 1] for l in lses]
        m = jnp.maximum(jnp.maximum(ls[0], ls[1]), ls[2])
        es = [jnp.exp(l - m) for l in ls]
        tot = es[0] + es[1] + es[2]
        acc = (es[0] / tot) * outs[0][:, sl].astype(F32)
        acc = acc + (es[1] / tot) * outs[1][:, sl].astype(F32)
        acc = acc + (es[2] / tot) * outs[2][:, sl].astype(F32)
        heads.append(acc)
    o_a = jnp.concatenate(heads, axis=1)
    if residue_major:
        o_a = _swap_row_grid(scr[0], o_a)
    o_a = o_a.astype(BF16)
    merged = None
    for gi, (o_g, w_ref) in enumerate(((o_a, wa_ref), (ob_ref[...], wb_ref), (oc_ref[...], wc_ref))):
        gate = jax.nn.sigmoid(jnp.dot(h, wg_ref[:, gi * D_MODEL:(gi + 1) * D_MODEL], preferred_element_type=F32))
        term = gate * jnp.dot(o_g, w_ref[...], preferred_element_type=F32)
        merged = term if merged is None else merged + term
    y = x + jnp.dot(merged.astype(BF16), wo_ref[...], preferred_element_type=F32)
    y_ref[...] = y
    h2_ref[...] = _rms(y, g2_ref[...]).astype(h2_ref.dtype)


def _merge(x, g1, g2, wg, o_groups, lses, o_b, o_c, wa, wb, wc, wo, seq_len, residue_major):
    n = x.shape[0]
    tm = RES * RES
    row = lambda i: (i, 0)
    fixed = lambda i: (0, 0)
    half = pl.BlockSpec((tm, A_WIDTH), row)
    wbr = pl.BlockSpec((A_WIDTH, D_MODEL), fixed)
    if residue_major:
        tiles = seq_len // tm
        grp = lambda i: (i // tiles, 0, i % tiles, 0)
        o_spec = pl.BlockSpec((None, RES, tm // RES, A_WIDTH), grp)
        lse = pl.BlockSpec((None, RES, tm // RES, LANES), grp)
        scratch = [pltpu.VMEM((A_WIDTH // LANES, tm, LANES), F32)]
    else:
        o_spec = half
        lse = pl.BlockSpec((tm, LANES), row)
        scratch = []
    return pl.pallas_call(
        functools.partial(_merge_kernel, residue_major=residue_major),
        grid=(n // tm,),
        in_specs=[pl.BlockSpec((tm, D_MODEL), row), pl.BlockSpec((1, D_MODEL), fixed),
                  pl.BlockSpec((1, D_MODEL), fixed),
                  pl.BlockSpec((D_MODEL, 3 * D_MODEL), lambda i: (0, W_COLS["gates"] // (3 * D_MODEL))),
                  o_spec, o_spec, o_spec, lse, lse, lse, half, half, wbr, wbr, wbr,
                  pl.BlockSpec((D_MODEL, D_MODEL), fixed)],
        out_specs=[pl.BlockSpec((tm, D_MODEL), row)] * 2,
        out_shape=[jax.ShapeDtypeStruct((n, D_MODEL), F32), jax.ShapeDtypeStruct((n, D_MODEL), BF16)],
        scratch_shapes=scratch,
        compiler_params=_cparams(("parallel",)),
        name="merge",
    )(x, g1, g2, wg, *o_groups, *lses, o_b, o_c, wa, wb, wc, wo)


def _ffn_kernel(x_ref, h_ref, wg_ref, wu_ref, wo_ref, y_ref, acc_scr):
    j = pl.program_id(1)

    @pl.when(j == 0)
    def _():
        acc_scr[...] = jnp.zeros_like(acc_scr)

    h = h_ref[...]
    gate = jnp.dot(h, wg_ref[...], preferred_element_type=F32)
    up = jnp.dot(h, wu_ref[...], preferred_element_type=F32)
    acc_scr[...] += jnp.dot((_silu(gate) * up).astype(BF16), wo_ref[...], preferred_element_type=F32)

    @pl.when(j == pl.num_programs(1) - 1)
    def _():
        y_ref[...] = x_ref[...] + acc_scr[...]


def _ffn(x, h, w_in, w_out):
    n = x.shape[0]
    tm = min(n, 1024)
    tf = 256
    nf = D_FF // tf
    row = lambda i, j: (i, 0)
    return pl.pallas_call(
        _ffn_kernel,
        grid=(n // tm, nf),
        in_specs=[pl.BlockSpec((tm, D_MODEL), row), pl.BlockSpec((tm, D_MODEL), row),
                  pl.BlockSpec((D_MODEL, tf), lambda i, j: (0, j)),
                  pl.BlockSpec((D_MODEL, tf), lambda i, j: (0, nf + j)),
                  pl.BlockSpec((tf, D_MODEL), lambda i, j: (j, 0))],
        out_specs=pl.BlockSpec((tm, D_MODEL), row),
        out_shape=jax.ShapeDtypeStruct((n, D_MODEL), F32),
        scratch_shapes=[pltpu.VMEM((tm, D_MODEL), F32)],
        compiler_params=_cparams(("parallel", "arbitrary")),
        name="ffn",
    )(x, h, w_in, w_in, w_out)


def _rope_tables(pos, hd, reps):
    inv = ROPE_THETA ** (-jnp.arange(0, hd, 2, dtype=F32) / hd)
    ang = pos.astype(F32)[:, None] * inv[None, :]
    cos = jnp.cos(ang)
    sin = jnp.sin(ang)
    cos2 = jnp.concatenate([cos, cos], axis=1)
    sin2 = jnp.concatenate([-sin, sin], axis=1)
    return jnp.tile(cos2, (1, reps)), jnp.tile(sin2, (1, reps))


def _pad_rows(a, batch, t, t_pad):
    if t == t_pad:
        return a
    a = a.reshape(batch, t, a.shape[-1])
    a = jnp.pad(a, ((0, 0), (0, t_pad - t), (0, 0)))
    return a.reshape(batch * t_pad, a.shape[-1])


def _unpad_rows(a, batch, t, t_pad):
    if t == t_pad:
        return a
    return a.reshape(batch, t_pad, a.shape[-1])[:, :t].reshape(batch * t, a.shape[-1])


def _layer(x, pos, batch, t, lw, caches, layer, conv_state, s0, r0):
    n = batch * t
    prompt = caches is None
    reps = max(1, 256 // t)
    cos_a, sin_a = _rope_tables(pos, A_HD, 1)
    cos_c, sin_c = _rope_tables(pos, C_DK, LANES // C_DK)
    if reps > 1:
        cos_a, sin_a, cos_c, sin_c = (jnp.tile(a, (reps, 1)) for a in (cos_a, sin_a, cos_c, sin_c))

    if prompt:
        to_rm = lambda a: a.reshape(t // RES, RES, A_HD).transpose(1, 0, 2)
        q, k, v, kv_tail = _proj_a(x, lw["g1"], lw["w_all"], lw["qn"], lw["kn"], to_rm(cos_a), to_rm(sin_a),
                                   batch, t, True)
    else:
        q, k, v = _proj_a(x, lw["g1"], lw["w_all"], lw["qn"], lw["kn"], cos_a, sin_a, batch, t, False)
    outs, lses = [], []
    for gi in range(N_GROUPS):
        if prompt:
            o, lse = _attn_prompt(q, k, v, gi, batch, t)
        else:
            o, lse = _attn_sample(q, k, v, caches[gi], layer, gi, batch, t)
        outs.append(o)
        lses.append(lse)
    new_kv = []
    for gi, (win, _) in enumerate(A_GROUPS):
        if prompt:
            keep = min(win, t)
            first = kv_tail.shape[2] - keep // RES
            rows = kv_tail[:, :, first:, 2 * gi * A_WIDTH:2 * (gi + 1) * A_WIDTH]
            new_kv.append(rows.transpose(0, 2, 1, 3).reshape(batch, keep, 2, A_HEADS, A_HD))
        else:
            cols = slice(gi * A_WIDTH, (gi + 1) * A_WIDTH)
            tail = lambda a: a[:, cols].reshape(batch, t, A_HEADS, A_HD)
            new_kv.append(jnp.stack([tail(k), tail(v)], axis=2))

    t_pad = -(-t // ROW_BLOCK) * ROW_BLOCK
    cst = jnp.pad(conv_state, ((0, 0), (SUBLANES - (B_CONV - 1), 0), (0, 0)))
    decay = (lw["alog_r"], lw["dt_r"], lw["alog_c"], lw["dt_c"])
    short = t == SUBLANES and n % ROW_BLOCK == 0
    if t_pad == t:
        act, z_b, bac, bar, cq, ck, cv, cz, p_last = _proj_bc(x, lw["g1"], lw["w_all"], lw["w_bat"], cos_c, sin_c,
                                                              cst, lw["conv_w"], t)
        conv_new = p_last[:, -(B_CONV - 1):]
        qg, kd, u, w, attn, egl = _b_prep(act, bac, bar, *decay, t, t)
    else:
        p, z_b, bac, bar, cq, ck, cv, cz = _proj_bc(x, lw["g1"], lw["w_all"], lw["w_bat"], cos_c, sin_c)
        conv_new = jnp.concatenate([conv_state, p.reshape(batch, t, B_CONV_CH)], axis=1)[:, -(B_CONV - 1):]
        if not short:
            qg, kd, u, w, attn, egl = _b_prep(
                _pad_rows(p, batch, t, t_pad), _pad_rows(bac, batch, t, t_pad),
                _pad_rows(bar.T, batch, t, t_pad).T, *decay, t_pad, t, cst, lw["conv_w"])
    if short:
        o_b, s_new = _b_short(p, cst.reshape(n, B_CONV_CH), bac, bar, lw["conv_w"], *decay, z_b, s0, lw["gb"], t)
    else:
        o_b, s_new = _b_scan(qg, kd, u, w, attn, egl, _pad_rows(z_b, batch, t, t_pad), s0, lw["gb"], batch, t_pad)
        o_b = _unpad_rows(o_b, batch, t, t_pad)

    if short:
        o_c, r_new = _c_short(cq, ck, cv, cz, r0, lw["gc"], t)
    else:
        o_c, r_new = _c_scan(*(_pad_rows(a, batch, t, t_pad) for a in (cq, ck, cv, cz)), r0, lw["gc"], batch, t_pad,
                             t)
        o_c = _unpad_rows(o_c, batch, t, t_pad)

    x, h2 = _merge(x, lw["g1"], lw["g2"], lw["w_all"], outs, lses, o_b, o_c, lw["w_oa"], lw["w_ob"], lw["w_oc"],
                   lw["w_o"], t, prompt)
    x = _ffn(x, h2, lw["w_fi"], lw["w_fo"])
    return x, new_kv, conv_new, s_new, r_new


def _layer_weights(l, norm1_g, w_in, a_q_norm_g, a_k_norm_g, b_conv_w, b_a_log, b_dt_bias, b_out_norm_g,
                   c_out_norm_g, w_out_a, w_out_b, w_out_c, w_out, norm2_g, w_ffn_in, w_ffn_out):
    o = IN_OFFS
    wl = w_in[l]
    w_ba = wl[:, o[3]:o[5]]
    pad_r = lambda a: jnp.pad(a.reshape(1, B_HEADS), ((0, 0), (B_HEADS, LANES - 2 * B_HEADS)))
    pad_c = lambda a: jnp.pad(a.reshape(B_HEADS, 1), ((B_HEADS, 2 * SUBLANES - 2 * B_HEADS), (0, 0)))
    return dict(
        g1=norm1_g[l].reshape(1, D_MODEL), g2=norm2_g[l].reshape(1, D_MODEL),
        w_all=jnp.concatenate([wl[:, o[0]:o[1]], wl[:, o[1]:o[2]], wl[:, o[9]:o[10]], wl[:, o[5]:o[9]], wl[:, o[2]:o[3]],
                               jnp.pad(w_ba, ((0, 0), (0, LANES - 2 * B_HEADS)))], axis=1).astype(BF16),
        w_bat=jnp.pad(w_ba.T, ((0, 2 * SUBLANES - 2 * B_HEADS), (0, 0))).astype(BF16),
        qn=a_q_norm_g[l].reshape(1, A_HD), kn=a_k_norm_g[l].reshape(1, A_HD),
        conv_w=b_conv_w[l],
        alog_r=pad_r(b_a_log[l]), dt_r=pad_r(b_dt_bias[l]), alog_c=pad_c(b_a_log[l]), dt_c=pad_c(b_dt_bias[l]),
        gb=b_out_norm_g[l].reshape(1, B_DV), gc=c_out_norm_g[l].reshape(1, C_DV),
        w_oa=w_out_a[l].astype(BF16), w_ob=w_out_b[l].astype(BF16), w_oc=w_out_c[l].astype(BF16),
        w_o=w_out[l].astype(BF16), w_fi=w_ffn_in[l].astype(BF16), w_fo=w_ffn_out[l].astype(BF16),
    )


def kernel(x_prompt, x_sample, cache_a_kv0, cache_a_kv1, cache_a_kv2, state_b_conv, state_b_S, state_c_R, norm1_g, w_in, a_q_norm_g, a_k_norm_g, b_conv_w, b_a_log, b_dt_bias, b_out_norm_g, c_out_norm_g, w_out_a, w_out_b, w_out_c, w_out, norm2_g, w_ffn_in, w_ffn_out):
    bp, t = x_prompt.shape[:2]
    bs, s = x_sample.shape[:2]
    depth = w_in.shape[0]
    pos_p = jnp.arange(t)
    pos_s = PAST_LEN + jnp.arange(s)
    yp = x_prompt.reshape(bp * t, D_MODEL)
    ys = x_sample.reshape(bs * s, D_MODEL)
    caches = (cache_a_kv0, cache_a_kv1, cache_a_kv2)
    zeros_conv = jnp.zeros((bp, B_CONV - 1, B_CONV_CH), F32)
    zeros_s = jnp.zeros((bp, B_HEADS, B_DK, B_DV), F32)
    zeros_r = jnp.zeros((bp, C_HEADS, C_DK, C_DV), F32)
    acc = [[] for _ in range(12)]
    for l in range(depth):
        lw = _layer_weights(l, norm1_g, w_in, a_q_norm_g, a_k_norm_g, b_conv_w, b_a_log, b_dt_bias,
                            b_out_norm_g, c_out_norm_g, w_out_a, w_out_b, w_out_c, w_out, norm2_g,
                            w_ffn_in, w_ffn_out)
        yp, kv, cv, sn, rn = _layer(yp, pos_p, bp, t, lw, None, l, zeros_conv, zeros_s, zeros_r)
        for i, a in enumerate((kv[0], kv[1], kv[2], cv, sn, rn)):
            acc[i].append(a)
        ys, kv, cv, sn, rn = _layer(ys, pos_s, bs, s, lw, caches, l, state_b_conv[l], state_b_S[l], state_c_R[l])
        for i, a in enumerate((kv[0], kv[1], kv[2], cv, sn, rn)):
            acc[6 + i].append(a)
    return (yp.reshape(bp, t, D_MODEL), ys.reshape(bs, s, D_MODEL)) + tuple(jnp.stack(a) for a in acc)
```

```python
import functools
import math

import jax
import jax.numpy as jnp
import numpy as np
from jax import lax
from jax.experimental import pallas as pl
from jax.experimental.pallas import tpu as pltpu

F32 = jnp.float32
BF16 = jnp.bfloat16

D_MODEL = 1024
PAST_LEN = 8192
A_GROUPS = ((128, 1), (512, 4), (2048, 16))
N_GROUPS = 3
A_HEADS = 4
A_HD = 128
A_WIDTH = A_HEADS * A_HD
A_KEYS = 128
B_HEADS = 4
B_DK = 128
B_DV = 128
B_CONV = 4
B_QK = B_HEADS * B_DK
B_V = B_HEADS * B_DV
B_CONV_CH = 2 * B_QK + B_V
C_HEADS = 4
C_DK = 64
C_DV = 128
C_QK = C_HEADS * C_DK
C_V = C_HEADS * C_DV
CHUNK = 64
ROPE_THETA = 10000.0
EPS = 1e-6
D_FF = 2816
IN_SIZES = (3 * N_GROUPS * A_WIDTH, B_CONV_CH, B_V, B_HEADS, B_HEADS, C_QK, C_QK, C_V, C_V, 3 * D_MODEL)
IN_OFFS = tuple(int(v) for v in np.cumsum((0,) + IN_SIZES))

W_COLS = {"a": 0, "b_qkv": 4608, "gates": 6144, "c": 9216, "b_z": 10752, "b_ba": 11264}
W_ALL = 11392

ROW_BLOCK = 128
RES = 16
SUBLANES = 8
LANES = 128
VMEM_LIMIT = 48 * 1024 * 1024


def _cparams(sem):
    return pltpu.CompilerParams(dimension_semantics=sem, vmem_limit_bytes=VMEM_LIMIT)


def _rms(x, g):
    return x * lax.rsqrt(jnp.mean(x * x, axis=-1, keepdims=True) + EPS) * g


def _silu(x):
    return x * jax.nn.sigmoid(x)


def _softplus(x):
    return jnp.maximum(x, 0.0) + jnp.log(1.0 + jnp.exp(-jnp.abs(x)))


def _dot(a, b):
    return jnp.dot(a.astype(BF16), b.astype(BF16), preferred_element_type=F32)


def _dot_nt(a, b):
    return lax.dot_general(a.astype(BF16), b.astype(BF16), (((1,), (1,)), ((), ())), preferred_element_type=F32)


def _dot_tn(a, b):
    return lax.dot_general(a.astype(BF16), b.astype(BF16), (((0,), (0,)), ((), ())), preferred_element_type=F32)


def _swap_row_grid(scr, val):
    slabs = val.shape[1] // LANES
    for c in range(slabs):
        scr[c] = val[:, c * LANES:(c + 1) * LANES]
    cols = [jnp.concatenate([scr[c, pl.ds(r, RES, stride=RES), :] for r in range(RES)], axis=0)
            for c in range(slabs)]
    return jnp.concatenate(cols, axis=1)


def _proj_a_kernel(x_ref, g1_ref, w_ref, qg_ref, kg_ref, cos_ref, sin_ref, q_ref, k_ref, v_ref, *rest, residue_major):
    x = x_ref[...]
    tm = x.shape[0]
    if residue_major:
        tail_ref, scr = rest
        x = _swap_row_grid(scr, x)
    h = _rms(x, g1_ref[...]).astype(BF16)
    cos = cos_ref[...].reshape(tm, A_HD)
    sin = sin_ref[...].reshape(tm, A_HD)

    def norm_rope(seg, g):
        y = _rms(seg, g)
        return y * cos + pltpu.roll(y, A_HD // 2, 1) * sin

    def put(ref, col, val):
        if residue_major:
            ref[:, :, col:col + val.shape[1]] = val.reshape(RES, tm // RES, val.shape[1]).astype(ref.dtype)
        else:
            ref[:, col:col + val.shape[1]] = val

    for j in range(3 * N_GROUPS):
        acc = jnp.dot(h, w_ref[:, j * A_WIDTH:(j + 1) * A_WIDTH], preferred_element_type=F32)
        if j < N_GROUPS:
            for hh in range(A_HEADS):
                sl = slice(hh * A_HD, (hh + 1) * A_HD)
                put(q_ref, j * A_WIDTH + hh * A_HD, norm_rope(acc[:, sl], qg_ref[...]) * (A_HD ** -0.5))
        elif j < 2 * N_GROUPS:
            jj = j - N_GROUPS
            for hh in range(A_HEADS):
                sl = slice(hh * A_HD, (hh + 1) * A_HD)
                val = norm_rope(acc[:, sl], kg_ref[...])
                put(k_ref, jj * A_WIDTH + hh * A_HD, val)
                if residue_major:
                    put(tail_ref, 2 * jj * A_WIDTH + hh * A_HD, val)
        else:
            jj = j - 2 * N_GROUPS
            put(v_ref, jj * A_WIDTH, acc)
            if residue_major:
                put(tail_ref, (2 * jj + 1) * A_WIDTH, acc)


def _proj_a(x, g1, w, qg, kg, cos, sin, batch, seq_len, residue_major):
    n = x.shape[0]
    tm = RES * RES
    nw = N_GROUPS * A_WIDTH
    fixed = lambda i: (0, 0)
    common = [pl.BlockSpec((1, D_MODEL), fixed), pl.BlockSpec((D_MODEL, 3 * nw), fixed),
              pl.BlockSpec((1, A_HD), fixed), pl.BlockSpec((1, A_HD), fixed)]
    if residue_major:
        tiles = seq_len // tm
        tail_tiles = min(max(wd for wd, _ in A_GROUPS), seq_len) // tm
        blk = (None, RES, tm // RES, nw)
        tab = pl.BlockSpec((RES, tm // RES, A_HD), lambda i: (0, i % tiles, 0))
        main = pl.BlockSpec(blk, lambda i: (i // tiles, 0, i % tiles, 0))
        tail = pl.BlockSpec((None, RES, tm // RES, 2 * nw),
                            lambda i: (i // tiles, 0, jnp.maximum(i % tiles - (tiles - tail_tiles), 0), 0))
        out_specs = [main] * 3 + [tail]
        out_shape = ([jax.ShapeDtypeStruct((batch, RES, seq_len // RES, nw), BF16)] * 3
                     + [jax.ShapeDtypeStruct((batch, RES, tail_tiles * tm // RES, 2 * nw), F32)])
        scratch = [pltpu.VMEM((D_MODEL // LANES, tm, LANES), F32)]
    else:
        assert cos.shape[0] == tm
        tab = pl.BlockSpec((tm, A_HD), fixed)
        out_specs = [pl.BlockSpec((tm, nw), lambda i: (i, 0))] * 3
        out_shape = [jax.ShapeDtypeStruct((n, nw), F32)] * 3
        scratch = []
    return pl.pallas_call(
        functools.partial(_proj_a_kernel, residue_major=residue_major),
        grid=(n // tm,),
        in_specs=[pl.BlockSpec((tm, D_MODEL), lambda i: (i, 0))] + common + [tab, tab],
        out_specs=out_specs,
        out_shape=out_shape,
        scratch_shapes=scratch,
        compiler_params=_cparams(("arbitrary",)),
        name="proj_a",
    )(x, g1, w, qg, kg, cos, sin)


ATTN_SUBS = 4


def _attn_prompt_kernel(q_ref, kc_ref, kp_ref, vc_ref, vp_ref, o_ref, lse_ref, *, parts):
    n = pl.program_id(2)
    per = ROW_BLOCK // parts
    subs = q_ref.shape[1] // per
    qi = lax.broadcasted_iota(jnp.int32, (ROW_BLOCK, ROW_BLOCK), 0)
    kj = lax.broadcasted_iota(jnp.int32, (ROW_BLOCK, ROW_BLOCK), 1)
    qi = parts * (qi % per) + qi // per
    kj = parts * (kj % per) + kj // per
    cur_ok = kj <= qi
    prev_ok = kj >= qi
    first_ok = jnp.logical_and(prev_ok, n > 0)
    lane = lax.broadcasted_iota(jnp.int32, (ROW_BLOCK, LANES), 1)
    neg = -jnp.inf
    packed_rows = 2 * SUBLANES

    def sub(ref, half, sl):
        full = ref[:, :, sl]
        if per % packed_rows == 0:
            return full[:, half * per:(half + 1) * per].reshape(ROW_BLOCK, A_HD)
        return full.astype(F32)[:, half * per:(half + 1) * per].reshape(ROW_BLOCK, A_HD).astype(BF16)

    units = [(half, hh) for half in range(subs) for hh in range(A_HEADS)]
    head = lambda hh: slice(hh * A_HD, (hh + 1) * A_HD)

    keys = {(half, hh): sub(kc_ref, half, head(hh)) for half, hh in units}
    vals = {(half, hh): sub(vc_ref, half, head(hh)) for half, hh in units}
    for hh in range(A_HEADS):
        last = kp_ref.shape[1] // per - 1
        keys[(-1, hh)] = sub(kp_ref, last, head(hh))
        vals[(-1, hh)] = sub(vp_ref, last, head(hh))

    scores = []
    for half, hh in units:
        q = sub(q_ref, half, head(hh))
        s_cur = jnp.where(cur_ok, _dot_nt(q, keys[(half, hh)]), neg)
        s_prev = jnp.where(first_ok if half == 0 else prev_ok, _dot_nt(q, keys[(half - 1, hh)]), neg)
        scores.append((s_cur, s_prev))
    probs = []
    for s_cur, s_prev in scores:
        m = jnp.max(jnp.maximum(s_cur, s_prev), axis=-1, keepdims=True)
        p_cur = jnp.exp(s_cur - m)
        p_prev = jnp.exp(s_prev - m)
        den = jnp.sum(p_cur + p_prev, axis=-1, keepdims=True)
        probs.append((p_cur, p_prev, m, den))
    lse_blk = [jnp.zeros((ROW_BLOCK, LANES), F32) for _ in range(subs)]
    outs = {}
    for (half, hh), (p_cur, p_prev, m, den) in zip(units, probs):
        o = (_dot(p_cur, vals[(half, hh)]) + _dot(p_prev, vals[(half - 1, hh)])) / den
        outs[(half, hh)] = o.reshape(parts, per, A_HD)
        lse_blk[half] = jnp.where(lane // 32 == hh, m + jnp.log(den), lse_blk[half])
    for hh in range(A_HEADS):
        both = jnp.concatenate([outs[(half, hh)] for half in range(subs)], axis=1)
        o_ref[:, :, head(hh)] = both.astype(o_ref.dtype)
    for half in range(subs):
        lse_ref[:, half * per:(half + 1) * per, :] = lse_blk[half].reshape(parts, per, LANES)


def _attn_prompt(q, k, v, gi, batch, seq_len):
    _, dil = A_GROUPS[gi]
    parts = RES // dil
    per = ROW_BLOCK // parts
    rows = seq_len // RES
    subs = min(ATTN_SUBS, seq_len // dil // ROW_BLOCK)
    nblk = seq_len // dil // (subs * ROW_BLOCK)
    split = lambda a: a.reshape(batch, parts, dil, rows, a.shape[-1])
    cur = lambda b, r, n: (b, 0, r, n, gi)
    out = lambda b, r, n: (b, 0, r, n, 0)
    blk = (None, parts, None, subs * per, A_WIDTH)
    if per % (2 * SUBLANES) == 0:
        blk_prev = (None, parts, None, per, A_WIDTH)
        prev = lambda b, r, n: (b, 0, r, jnp.maximum(subs * n - 1, 0), gi)
    else:
        blk_prev = blk
        prev = lambda b, r, n: (b, 0, r, jnp.maximum(n - 1, 0), gi)
    o, lse = pl.pallas_call(
        functools.partial(_attn_prompt_kernel, parts=parts),
        grid=(batch, dil, nblk),
        in_specs=[pl.BlockSpec(blk, cur), pl.BlockSpec(blk, cur), pl.BlockSpec(blk_prev, prev),
                  pl.BlockSpec(blk, cur), pl.BlockSpec(blk_prev, prev)],
        out_specs=[pl.BlockSpec(blk, out), pl.BlockSpec((None, parts, None, subs * per, LANES), out)],
        out_shape=[jax.ShapeDtypeStruct((batch, parts, dil, rows, A_WIDTH), BF16),
                   jax.ShapeDtypeStruct((batch, parts, dil, rows, LANES), F32)],
        compiler_params=_cparams(("parallel", "parallel", "arbitrary")),
        name=f"attn_prompt_g{gi}",
    )(split(q), split(k), split(k), split(v), split(v))
    return o.reshape(batch, RES, rows, A_WIDTH), lse.reshape(batch, RES, rows, LANES)


def _attn_sample_kernel(q_ref, kn_ref, vn_ref, cache_ref, o_ref, lse_ref, *, dil, n_new):
    n_res = min(dil, n_new)
    neg = -jnp.inf
    srow = lax.broadcasted_iota(jnp.int32, (n_new, A_KEYS), 0)
    mcol = lax.broadcasted_iota(jnp.int32, (n_new, A_KEYS), 1)
    in_window = mcol >= srow // dil
    row_res = [srow % dil == res for res in range(n_res)]
    srow_n = lax.broadcasted_iota(jnp.int32, (n_new, n_new), 0)
    tcol_n = lax.broadcasted_iota(jnp.int32, (n_new, n_new), 1)
    new_ok = jnp.logical_and(tcol_n <= srow_n, (srow_n - tcol_n) % dil == 0)
    heads = range(A_HEADS)

    qs = [q_ref[hh].astype(BF16) for hh in heads]
    keys = {(res, hh): cache_ref[:, res, 0, hh, :].astype(BF16) for res in range(n_res) for hh in heads}
    vals = {(res, hh): cache_ref[:, res, 1, hh, :].astype(BF16) for res in range(n_res) for hh in heads}
    raw = {key: _dot_nt(qs[key[1]], kmat) for key, kmat in keys.items()}
    s_new = [jnp.where(new_ok, _dot_nt(qs[hh], kn_ref[hh]), neg) for hh in heads]
    probs = []
    for hh in heads:
        s_buf = raw[(0, hh)]
        for res in range(1, n_res):
            s_buf = jnp.where(row_res[res], raw[(res, hh)], s_buf)
        s_buf = jnp.where(in_window, s_buf, neg)
        m = jnp.maximum(jnp.max(s_buf, axis=-1, keepdims=True), jnp.max(s_new[hh], axis=-1, keepdims=True))
        p_buf = jnp.exp(s_buf - m)
        p_new = jnp.exp(s_new[hh] - m)
        den = jnp.sum(p_buf, axis=-1, keepdims=True) + jnp.sum(p_new, axis=-1, keepdims=True)
        probs.append((p_buf, p_new, m, den))
    for hh in heads:
        p_buf, p_new, m, den = probs[hh]
        acc = _dot(p_new, vn_ref[hh])
        for res in range(n_res):
            p_res = p_buf if n_res == 1 else jnp.where(row_res[res], p_buf, 0.0)
            acc = acc + _dot(p_res, vals[(res, hh)])
        o_ref[hh] = acc / den
        lse_ref[hh] = jnp.broadcast_to(m + jnp.log(den), (n_new, LANES))


def _attn_sample_rows_kernel(q_ref, kn_ref, vn_ref, cache_ref, o_ref, lse_ref, *, dil, n_new):
    row = lax.broadcasted_iota(jnp.int32, (A_KEYS, A_HEADS, 1), 0)
    trow = lax.broadcasted_iota(jnp.int32, (n_new, A_HEADS, 1), 0)
    neg = -jnp.inf
    kn = kn_ref[...]
    vn = vn_ref[...]
    for s in range(n_new):
        res = s % dil
        first = s // dil
        q = q_ref[s][None]
        kc = cache_ref[:, res, 0]
        vc = cache_ref[:, res, 1]
        sc = jnp.sum(kc * q, axis=-1, keepdims=True)
        if first > 0:
            sc = jnp.where(row >= first, sc, neg)
        new_ok = jnp.logical_and(trow <= s, (s - trow) % dil == 0)
        sn = jnp.where(new_ok, jnp.sum(kn * q, axis=-1, keepdims=True), neg)
        m = jnp.maximum(jnp.max(sc, axis=0, keepdims=True), jnp.max(sn, axis=0, keepdims=True))
        pc = jnp.exp(sc - m)
        pn = jnp.exp(sn - m)
        den = jnp.sum(pc, axis=0, keepdims=True) + jnp.sum(pn, axis=0, keepdims=True)
        o = (jnp.sum(pc * vc, axis=0, keepdims=True) + jnp.sum(pn * vn, axis=0, keepdims=True)) / den
        o_ref[s] = o[0]
        lse_ref[s] = jnp.broadcast_to((m + jnp.log(den))[0], (A_HEADS, A_HD))


def _attn_sample(q, k, v, cache, layer, gi, batch, n_new):
    win, dil = A_GROUPS[gi]
    depth = cache.shape[0]
    assert cache.shape[2] == win and win // dil == A_KEYS
    n_res = min(dil, n_new)
    cv = cache.reshape(depth, batch, A_KEYS, dil, 2, A_HEADS, A_HD)
    cache_spec = pl.BlockSpec((None, None, A_KEYS, n_res, 2, A_HEADS, A_HD), lambda b: (layer, b, 0, 0, 0, 0, 0))
    if n_res > 1:
        heads = lambda a: a.reshape(batch, n_new, N_GROUPS, A_HEADS, A_HD)
        new = pl.BlockSpec((None, n_new, None, A_HEADS, A_HD), lambda b: (b, 0, gi, 0, 0))
        out = pl.BlockSpec((None, n_new, A_HEADS, A_HD), lambda b: (b, 0, 0, 0))
        o, lse = pl.pallas_call(
            functools.partial(_attn_sample_rows_kernel, dil=dil, n_new=n_new),
            grid=(batch,),
            in_specs=[new, new, new, cache_spec],
            out_specs=[out, out],
            out_shape=[jax.ShapeDtypeStruct((batch, n_new, A_HEADS, A_HD), F32)] * 2,
            compiler_params=_cparams(("parallel",)),
            name=f"attn_sample_g{gi}",
        )(heads(q), heads(k), heads(v), cv)
        lse = jnp.repeat(lse[..., 0], LANES // A_HEADS, axis=-1)
        return o.reshape(batch * n_new, A_WIDTH), lse.reshape(batch * n_new, LANES)
    heads = lambda a: a.reshape(batch, n_new, N_GROUPS, A_HEADS, A_HD).transpose(0, 2, 3, 1, 4)
    new = pl.BlockSpec((None, None, A_HEADS, n_new, A_HD), lambda b: (b, gi, 0, 0, 0))
    out = pl.BlockSpec((None, A_HEADS, n_new, LANES), lambda b: (b, 0, 0, 0))
    o, lse = pl.pallas_call(
        functools.partial(_attn_sample_kernel, dil=dil, n_new=n_new),
        grid=(batch,),
        in_specs=[new, new, new,
                  pl.BlockSpec((None, None, A_KEYS, n_res, 2, A_HEADS, A_HD), lambda b: (layer, b, 0, 0, 0, 0, 0))],
        out_specs=[out, out],
        out_shape=[jax.ShapeDtypeStruct((batch, A_HEADS, n_new, A_HD), F32)] * 2,
        compiler_params=_cparams(("parallel",)),
        name=f"attn_sample_g{gi}",
    )(heads(q), heads(k), heads(v), cv)
    o = o.transpose(0, 2, 1, 3)
    lse = jnp.repeat(lse[..., 0].transpose(0, 2, 1), LANES // A_HEADS, axis=-1)
    return o.reshape(batch * n_new, A_WIDTH), lse.reshape(batch * n_new, LANES)


def _causal_conv(e_scr, cw_ref, rows):
    xc = e_scr[SUBLANES:SUBLANES + rows, :] * cw_ref[B_CONV - 1:B_CONV, :]
    for kk in range(1, B_CONV):
        xc = xc + e_scr[SUBLANES - kk:SUBLANES - kk + rows, :] * cw_ref[B_CONV - 1 - kk:B_CONV - kk, :]
    return xc


def _proj_c_body(h, w_ref, cos_ref, sin_ref, q_ref, k_ref, v_ref, z_ref):
    cos = cos_ref[...]
    sin = sin_ref[...]
    lane = lax.broadcasted_iota(jnp.int32, cos.shape, 1)
    first_half = (lane % C_DK) < (C_DK // 2)

    def rope(seg):
        swapped = jnp.where(first_half, pltpu.roll(seg, LANES - C_DK // 2, 1), pltpu.roll(seg, C_DK // 2, 1))
        return seg * cos + swapped * sin

    qk = jnp.dot(h, w_ref[:, 0:2 * C_QK], preferred_element_type=F32)
    for j in range(2 * C_QK // LANES):
        seg = rope(qk[:, j * LANES:(j + 1) * LANES])
        if j < C_QK // LANES:
            q_ref[:, j * LANES:(j + 1) * LANES] = seg
        else:
            jj = j - C_QK // LANES
            k_ref[:, jj * LANES:(jj + 1) * LANES] = seg * (C_DK ** -0.5)
    v_ref[...] = jnp.dot(h, w_ref[:, 2 * C_QK:2 * C_QK + C_V], preferred_element_type=F32).astype(v_ref.dtype)
    z_ref[...] = jnp.dot(h, w_ref[:, 2 * C_QK + C_V:2 * C_QK + 2 * C_V],
                         preferred_element_type=F32).astype(z_ref.dtype)


def _proj_bc_kernel(x_ref, g1_ref, wqkv_ref, wz_ref, wba_ref, wbat_ref, wc_ref, cosc_ref, sinc_ref, *rest,
                    fuse_conv, tiles_per_seq):
    h = _rms(x_ref[...], g1_ref[...]).astype(BF16)
    tm = h.shape[0]
    if fuse_conv:
        cst_ref, cw_ref, p_ref, z_ref, bac_ref, bar_ref, cq_ref, ck_ref, cv_ref, cz_ref, ptail_ref, e_scr = rest
        first = pl.program_id(0) % tiles_per_seq == 0

        @pl.when(first)
        def _():
            e_scr[0:SUBLANES, :] = cst_ref[...]

        @pl.when(jnp.logical_not(first))
        def _():
            e_scr[0:SUBLANES, :] = e_scr[tm:tm + SUBLANES, :]

        for j in range(3):
            sl = slice(j * B_QK, (j + 1) * B_QK)
            e_scr[SUBLANES:SUBLANES + tm, sl] = jnp.dot(h, wqkv_ref[:, sl], preferred_element_type=F32)
        p_ref[...] = _silu(_causal_conv(e_scr, cw_ref, tm)).astype(p_ref.dtype)
        ptail_ref[...] = e_scr[tm:tm + SUBLANES, :]
    else:
        p_ref, z_ref, bac_ref, bar_ref, cq_ref, ck_ref, cv_ref, cz_ref = rest
        for j in range(3):
            sl = slice(j * B_QK, (j + 1) * B_QK)
            p_ref[:, sl] = jnp.dot(h, wqkv_ref[:, sl], preferred_element_type=F32)
    z_ref[...] = jnp.dot(h, wz_ref[...], preferred_element_type=F32).astype(z_ref.dtype)
    bac_ref[...] = jnp.dot(h, wba_ref[...], preferred_element_type=F32)
    bar_ref[...] = lax.dot_general(wbat_ref[...], h, (((1,), (1,)), ((), ())), preferred_element_type=F32)
    _proj_c_body(h, wc_ref, cosc_ref, sinc_ref, cq_ref, ck_ref, cv_ref, cz_ref)


def _proj_bc(x, g1, w_all, wbat, cos_c, sin_c, cstate=None, conv_w=None, seq_len=None):
    n = x.shape[0]
    tm = 256
    tab_blocks = cos_c.shape[0] // tm
    row = lambda i: (i, 0)
    fixed = lambda i: (0, 0)
    tab = (lambda i: (i % tab_blocks, 0)) if tab_blocks > 1 else fixed
    fuse_conv = cstate is not None
    c_width = 2 * C_QK + 2 * C_V
    in_specs = [
        pl.BlockSpec((tm, D_MODEL), row),
        pl.BlockSpec((1, D_MODEL), fixed),
        pl.BlockSpec((D_MODEL, B_CONV_CH), lambda i: (0, W_COLS["b_qkv"] // B_CONV_CH)),
        pl.BlockSpec((D_MODEL, B_V), lambda i: (0, W_COLS["b_z"] // B_V)),
        pl.BlockSpec((D_MODEL, LANES), lambda i: (0, W_COLS["b_ba"] // LANES)),
        pl.BlockSpec((2 * SUBLANES, D_MODEL), fixed),
        pl.BlockSpec((D_MODEL, c_width), lambda i: (0, W_COLS["c"] // c_width)),
        pl.BlockSpec((tm, LANES), tab),
        pl.BlockSpec((tm, LANES), tab),
    ]
    out_specs = [pl.BlockSpec((tm, B_CONV_CH), row), pl.BlockSpec((tm, B_V), row),
                 pl.BlockSpec((tm, LANES), row), pl.BlockSpec((2 * SUBLANES, tm), lambda i: (0, i)),
                 pl.BlockSpec((tm, C_QK), row), pl.BlockSpec((tm, C_QK), row),
                 pl.BlockSpec((tm, C_V), row), pl.BlockSpec((tm, C_V), row)]
    out_shape = [jax.ShapeDtypeStruct((n, B_CONV_CH), BF16 if fuse_conv else F32),
                 jax.ShapeDtypeStruct((n, B_V), BF16),
                 jax.ShapeDtypeStruct((n, LANES), F32), jax.ShapeDtypeStruct((2 * SUBLANES, n), F32),
                 jax.ShapeDtypeStruct((n, C_QK), F32), jax.ShapeDtypeStruct((n, C_QK), F32),
                 jax.ShapeDtypeStruct((n, C_V), BF16), jax.ShapeDtypeStruct((n, C_V), BF16)]
    args = [x, g1, w_all, w_all, w_all, wbat, w_all, cos_c, sin_c]
    scratch = []
    tiles = None
    if fuse_conv:
        tiles = seq_len // tm
        per_seq = pl.BlockSpec((None, SUBLANES, B_CONV_CH), lambda i: (i // tiles, 0, 0))
        in_specs += [per_seq, pl.BlockSpec((B_CONV, B_CONV_CH), fixed)]
        out_specs.append(per_seq)
        out_shape.append(jax.ShapeDtypeStruct((n // seq_len, SUBLANES, B_CONV_CH), F32))
        args += [cstate, conv_w]
        scratch = [pltpu.VMEM((SUBLANES + tm, B_CONV_CH), F32)]
    return pl.pallas_call(
        functools.partial(_proj_bc_kernel, fuse_conv=fuse_conv, tiles_per_seq=tiles),
        grid=(n // tm,),
        in_specs=in_specs,
        out_specs=out_specs,
        out_shape=out_shape,
        scratch_shapes=scratch,
        compiler_params=_cparams(("arbitrary",) if fuse_conv else ("parallel",)),
        name="proj_bc",
    )(*args)


def _b_prep_kernel(*refs, rows, blocks_per_seq, t_valid, conv_done):
    i = pl.program_id(0)
    blk = i % blocks_per_seq
    if conv_done:
        (p_ref, bac_ref, bar_ref, alog_r_ref, dt_r_ref, alog_c_ref, dt_c_ref,
         qg_ref, kd_ref, u_ref, w_ref, attn_ref, egl_ref) = refs
        act = p_ref[...].astype(F32)
    else:
        (p_ref, halo_ref, cst_ref, cw_ref, bac_ref, bar_ref, alog_r_ref, dt_r_ref, alog_c_ref, dt_c_ref,
         qg_ref, kd_ref, u_ref, w_ref, attn_ref, egl_ref, e_scr) = refs
        e_scr[0:SUBLANES, :] = jnp.where(blk == 0, cst_ref[...], halo_ref[...])
        e_scr[SUBLANES:SUBLANES + rows, :] = p_ref[...]
        act = _silu(_causal_conv(e_scr, cw_ref, rows))

    ri = lax.broadcasted_iota(jnp.int32, (rows, LANES), 0)
    li16 = lax.broadcasted_iota(jnp.int32, (2 * SUBLANES, rows), 1)
    li1 = lax.broadcasted_iota(jnp.int32, (1, LANES), 1)
    masked = t_valid < blocks_per_seq * rows
    if masked:
        row_ok = (blk * rows + ri) < t_valid
        col_ok = (blk * rows + li16) < t_valid
        act = jnp.where(ri[:, 0:1] + blk * rows < t_valid, act, 0.0)

    head_lane = jnp.logical_and(li1 >= B_HEADS, li1 < 2 * B_HEADS)
    a_r = jnp.where(head_lane, -jnp.exp(alog_r_ref[...]), 0.0)
    g_col = a_r * _softplus(bac_ref[...] + dt_r_ref[...])
    si = lax.broadcasted_iota(jnp.int32, (2 * SUBLANES, 1), 0)
    head_sub = jnp.logical_and(si >= B_HEADS, si < 2 * B_HEADS)
    a_c = jnp.where(head_sub, -jnp.exp(alog_c_ref[...]), 0.0)
    g_row = a_c * _softplus(bar_ref[...] + dt_c_ref[...])
    if masked:
        g_col = jnp.where(row_ok, g_col, 0.0)
        g_row = jnp.where(col_ok, g_row, 0.0)

    rpos = ri % CHUNK
    lpos = li16 % CHUNK
    gc = g_col
    rev = g_col
    gcr = g_row
    step = 1
    while step < CHUNK:
        gc = gc + jnp.where(rpos >= step, pltpu.roll(gc, step, 0), 0.0)
        rev = rev + jnp.where(rpos < CHUNK - step, pltpu.roll(rev, rows - step, 0), 0.0)
        gcr = gcr + jnp.where(lpos >= step, pltpu.roll(gcr, step, 1), 0.0)
        step *= 2
    rev = rev - g_col
    egl_ref[...] = jnp.exp(gc + rev)

    bi = lax.broadcasted_iota(jnp.int32, (ROW_BLOCK, ROW_BLOCK), 0)
    bj = lax.broadcasted_iota(jnp.int32, (ROW_BLOCK, ROW_BLOCK), 1)
    same = (bi // CHUNK) == (bj // CHUNK)
    incl = jnp.logical_and(same, bi >= bj)
    strict = jnp.logical_and(same, bi > bj)
    eye = (bi == bj).astype(F32)

    units = [(sb, hh) for sb in range(rows // ROW_BLOCK) for hh in range(B_HEADS)]
    lows, rhss = [], []
    for sb, hh in units:
        rs = slice(sb * ROW_BLOCK, (sb + 1) * ROW_BLOCK)
        sl = slice(hh * B_DK, (hh + 1) * B_DK)
        gc_c = gc[rs, B_HEADS + hh:B_HEADS + hh + 1]
        gc_r = gcr[B_HEADS + hh:B_HEADS + hh + 1, rs]
        dec = jnp.where(incl, jnp.exp(jnp.where(incl, gc_c - gc_r, 0.0)), 0.0)
        q = act[rs, sl]
        q = q * lax.rsqrt(jnp.sum(q * q, axis=-1, keepdims=True) + EPS) * (B_DK ** -0.5)
        k = act[rs, B_QK + hh * B_DK:B_QK + (hh + 1) * B_DK]
        k = k * lax.rsqrt(jnp.sum(k * k, axis=-1, keepdims=True) + EPS)
        v = act[rs, 2 * B_QK + hh * B_DV:2 * B_QK + (hh + 1) * B_DV]
        beta = jax.nn.sigmoid(bac_ref[rs, hh:hh + 1])
        kb = k * beta
        kbf = k.astype(BF16)
        lows.append(jnp.where(strict, dec * _dot_nt(kb, kbf), 0.0))
        attn_ref[rs, sl] = (dec * _dot_nt(q, kbf)).astype(attn_ref.dtype)
        rhss.append(jnp.concatenate([v * beta, kb * jnp.exp(gc_c)], axis=1).astype(BF16))
        qg_ref[rs, sl] = (q * jnp.exp(gc_c)).astype(qg_ref.dtype)
        kd_ref[rs, sl] = (k * jnp.exp(rev[rs, B_HEADS + hh:B_HEADS + hh + 1])).astype(kd_ref.dtype)

    tinvs = [eye - low for low in lows]
    pws = lows
    sq = 2
    while sq < CHUNK:
        pws = [_dot(pw, pw) for pw in pws]
        tinvs = [tinv + _dot(tinv, pw) for tinv, pw in zip(tinvs, pws)]
        sq *= 2
    for (sb, hh), tinv, rhs in zip(units, tinvs, rhss):
        rs = slice(sb * ROW_BLOCK, (sb + 1) * ROW_BLOCK)
        sol = _dot(tinv, rhs)
        u_ref[rs, hh * B_DV:(hh + 1) * B_DV] = sol[:, :B_DV]
        w_ref[rs, hh * B_DK:(hh + 1) * B_DK] = sol[:, B_DV:].astype(w_ref.dtype)


def _b_prep(p, bac, bar, alog_r, dt_r, alog_c, dt_c, seq_len, t_valid, cstate=None, cw=None):
    n = p.shape[0]
    rows = min(seq_len, 4 * ROW_BLOCK)
    bps = seq_len // rows
    row = lambda i: (i, 0)
    fixed = lambda i: (0, 0)
    per_row = rows // SUBLANES
    conv_done = cstate is None
    wide = lambda dt: jax.ShapeDtypeStruct((n, B_V), dt)
    in_specs = [pl.BlockSpec((rows, B_CONV_CH), row)]
    args = [p]
    scratch = []
    if not conv_done:
        in_specs += [pl.BlockSpec((SUBLANES, B_CONV_CH), lambda i: (jnp.maximum(i * per_row - 1, 0), 0)),
                     pl.BlockSpec((None, SUBLANES, B_CONV_CH), lambda i: (i // bps, 0, 0)),
                     pl.BlockSpec((B_CONV, B_CONV_CH), fixed)]
        args += [p, cstate, cw]
        scratch = [pltpu.VMEM((SUBLANES + rows, B_CONV_CH), F32)]
    in_specs += [pl.BlockSpec((rows, LANES), row),
                 pl.BlockSpec((2 * SUBLANES, rows), lambda i: (0, i)),
                 pl.BlockSpec((1, LANES), fixed),
                 pl.BlockSpec((1, LANES), fixed),
                 pl.BlockSpec((2 * SUBLANES, 1), fixed),
                 pl.BlockSpec((2 * SUBLANES, 1), fixed)]
    args += [bac, bar, alog_r, dt_r, alog_c, dt_c]
    return pl.pallas_call(
        functools.partial(_b_prep_kernel, rows=rows, blocks_per_seq=bps, t_valid=t_valid, conv_done=conv_done),
        grid=(n // rows,),
        in_specs=in_specs,
        out_specs=[pl.BlockSpec((rows, B_V), row)] * 5 + [pl.BlockSpec((rows, LANES), row)],
        out_shape=[wide(BF16), wide(BF16), wide(F32), wide(BF16), wide(BF16), jax.ShapeDtypeStruct((n, LANES), F32)],
        scratch_shapes=scratch,
        compiler_params=_cparams(("parallel",)),
        name="b_prep",
    )(*args)


def _b_scan_kernel(qg_ref, kd_ref, u_ref, w_ref, attn_ref, egl_ref, z_ref, s0_ref, gout_ref, o_ref, s_ref, *, nb):
    c = pl.program_id(1)

    @pl.when(c == 0)
    def _():
        s_ref[...] = s0_ref[...]

    half = c % (ROW_BLOCK // CHUNK)
    rgrp = lax.broadcasted_iota(jnp.int32, (ROW_BLOCK, B_DV), 0) // CHUNK
    here = rgrp == half
    units = [(b, hh) for b in range(nb) for hh in range(B_HEADS)]
    head = lambda hh: slice(hh * B_DV, (hh + 1) * B_DV)
    states = [s_ref[b, hh] for b, hh in units]
    proj = [_dot(jnp.concatenate([w_ref[b, :, head(hh)], qg_ref[b, :, head(hh)]], axis=0), s)
            for (b, hh), s in zip(units, states)]
    v_new = [u_ref[b, :, head(hh)] - pr[:CHUNK] for (b, hh), pr in zip(units, proj)]
    outs = []
    for (b, hh), pr, vn in zip(units, proj, v_new):
        v_full = jnp.where(here, jnp.concatenate([vn] * (ROW_BLOCK // CHUNK), axis=0), 0.0)
        outs.append(pr[CHUNK:] + _dot(attn_ref[b, :, head(hh)], v_full))
    for (b, hh), s, vn in zip(units, states, v_new):
        decay = egl_ref[b, 0:1, B_HEADS + hh:B_HEADS + hh + 1]
        s_ref[b, hh] = s * decay + _dot_tn(kd_ref[b, :, head(hh)], vn)
    for (b, hh), o in zip(units, outs):
        gate = _silu(z_ref[b, :, head(hh)].astype(F32))
        o_ref[b, :, head(hh)] = (_rms(o, gout_ref[...]) * gate).astype(o_ref.dtype)


def _b_scan(qg, kd, u, w, attn, egl, z, s0, gout, batch, seq_len):
    nb = 4
    nchunk = seq_len // CHUNK
    v3 = lambda a: a.reshape(batch, seq_len, a.shape[-1])
    rows = lambda bi, c: (bi, c, 0)
    state = lambda bi, c: (bi, 0, 0, 0)
    wide = pl.BlockSpec((nb, CHUNK, B_V), rows)
    o, s_new = pl.pallas_call(
        functools.partial(_b_scan_kernel, nb=nb),
        grid=(batch // nb, nchunk),
        in_specs=[wide] * 5 + [pl.BlockSpec((nb, CHUNK, LANES), rows), wide,
                               pl.BlockSpec((nb, B_HEADS, B_DK, B_DV), state),
                               pl.BlockSpec((1, B_DV), lambda bi, c: (0, 0))],
        out_specs=[wide, pl.BlockSpec((nb, B_HEADS, B_DK, B_DV), state)],
        out_shape=[jax.ShapeDtypeStruct((batch, seq_len, B_V), BF16),
                   jax.ShapeDtypeStruct((batch, B_HEADS, B_DK, B_DV), F32)],
        compiler_params=_cparams(("parallel", "arbitrary")),
        name="b_scan",
    )(v3(qg), v3(kd), v3(u), v3(w), v3(attn), v3(egl), v3(z), s0, gout)
    return o.reshape(batch * seq_len, B_V), s_new


def _log_gamma(hh):
    return math.log1p(-(2.0 ** (-5.0 - hh)))


def _c_scan_kernel(q_ref, k_ref, v_ref, z_ref, r0_ref, gout_ref, o_ref, rout_ref, r_ref, *, nb, t_valid):
    c = pl.program_id(1)
    rows = ROW_BLOCK
    state_blocks = [(b, hh, slice(hh * C_DK, (hh + 1) * C_DK), slice(hh * C_DV, (hh + 1) * C_DV))
                    for b in range(nb) for hh in range(C_HEADS)]

    @pl.when(c == 0)
    def _():
        r_ref[...] = jnp.zeros_like(r_ref)
        for b, hh, rsl, csl in state_blocks:
            r_ref[b, rsl, csl] = r0_ref[b, hh]

    left = jnp.clip(t_valid - c * rows, 0, rows)
    ri = lax.broadcasted_iota(jnp.int32, (rows, rows), 0)
    ci = lax.broadcasted_iota(jnp.int32, (rows, rows), 1)
    cnt_i = jnp.minimum(ri + 1, left).astype(F32)
    cnt_j = jnp.minimum(ci + 1, left).astype(F32)
    incl = ri >= ci
    steps = jnp.where(incl, cnt_i - cnt_j, 0.0)
    cnt_col = cnt_i[:, 0:1]
    left_f = left.astype(F32)
    qk_lane = lax.broadcasted_iota(jnp.int32, (1, C_QK), 1) // C_DK
    lg_lane = jnp.zeros((1, C_QK), F32)
    for hh in range(C_HEADS):
        lg_lane = jnp.where(qk_lane == hh, _log_gamma(hh), lg_lane)
    qk_sub = lax.broadcasted_iota(jnp.int32, (C_QK, 1), 0) // C_DK
    lg_sub = jnp.zeros((C_QK, 1), F32)
    for hh in range(C_HEADS):
        lg_sub = jnp.where(qk_sub == hh, _log_gamma(hh), lg_sub)
    q_scale = jnp.exp(cnt_col * lg_lane)
    k_scale = jnp.exp((left_f - cnt_col) * lg_lane)
    r_scale = jnp.exp(left_f * lg_sub)
    row_ok = (lax.broadcasted_iota(jnp.int32, (rows, 1), 0) + c * rows) < t_valid
    diag = (lax.broadcasted_iota(jnp.int32, (C_QK, C_V), 0) // C_DK) == (
        lax.broadcasted_iota(jnp.int32, (C_QK, C_V), 1) // C_DV)

    head = lambda hh: slice(hh * C_DV, (hh + 1) * C_DV)
    decays = [jnp.where(incl, jnp.exp(steps * _log_gamma(hh)), 0.0) for hh in range(C_HEADS)]
    qs = [q_ref[b] for b in range(nb)]
    ks = [jnp.where(row_ok, k_ref[b], 0.0) for b in range(nb)]
    vs = [v_ref[b].astype(BF16) for b in range(nb)]
    rs = [r_ref[b] for b in range(nb)]
    inters = [_dot(q * q_scale, r) for q, r in zip(qs, rs)]
    units = [(b, hh) for b in range(nb) for hh in range(C_HEADS)]
    atts = [decays[hh] * _dot_nt(qs[b], jnp.where(qk_lane == hh, ks[b], 0.0)) for b, hh in units]
    outs = [inters[b][:, head(hh)] + _dot(att, vs[b][:, head(hh)]) for (b, hh), att in zip(units, atts)]
    for b in range(nb):
        r_ref[b] = rs[b] * r_scale + jnp.where(diag, _dot_tn(ks[b] * k_scale, vs[b]), 0.0)
    for (b, hh), o in zip(units, outs):
        gate = _silu(z_ref[b, :, head(hh)].astype(F32))
        o_ref[b, :, head(hh)] = (_rms(o, gout_ref[...]) * gate).astype(o_ref.dtype)

    @pl.when(c == pl.num_programs(1) - 1)
    def _():
        for b, hh, rsl, csl in state_blocks:
            rout_ref[b, hh] = r_ref[b, rsl, csl]


def _c_scan(q, k, v, z, r0, gout, batch, seq_len, t_valid):
    nb = 4
    nblk = seq_len // ROW_BLOCK
    v3 = lambda a: a.reshape(batch, seq_len, a.shape[-1])
    rows = lambda bi, c: (bi, c, 0)
    state = pl.BlockSpec((nb, C_HEADS, C_DK, C_DV), lambda bi, c: (bi, 0, 0, 0))
    o, r_new = pl.pallas_call(
        functools.partial(_c_scan_kernel, nb=nb, t_valid=t_valid),
        grid=(batch // nb, nblk),
        in_specs=[pl.BlockSpec((nb, ROW_BLOCK, C_QK), rows), pl.BlockSpec((nb, ROW_BLOCK, C_QK), rows),
                  pl.BlockSpec((nb, ROW_BLOCK, C_V), rows), pl.BlockSpec((nb, ROW_BLOCK, C_V), rows),
                  state, pl.BlockSpec((1, C_DV), lambda bi, c: (0, 0))],
        out_specs=[pl.BlockSpec((nb, ROW_BLOCK, C_V), rows), state],
        out_shape=[jax.ShapeDtypeStruct((batch, seq_len, C_V), BF16),
                   jax.ShapeDtypeStruct((batch, C_HEADS, C_DK, C_DV), F32)],
        scratch_shapes=[pltpu.VMEM((nb, C_QK, C_V), F32)],
        compiler_params=_cparams(("parallel", "arbitrary")),
        name="c_scan",
    )(v3(q), v3(k), v3(v), v3(z), r0, gout)
    return o.reshape(batch * seq_len, C_V), r_new


def _b_short_kernel(p_ref, st_ref, bac_ref, bar_ref, cw_ref, alog_r_ref, dt_r_ref, alog_c_ref, dt_c_ref,
                    z_ref, s0_ref, gout_ref, o_ref, s_ref, e_new, e_old, *, t):
    rows = ROW_BLOCK
    nseq = rows // t
    e_new[0:SUBLANES, :] = jnp.zeros((SUBLANES, B_CONV_CH), F32)
    e_new[SUBLANES:SUBLANES + rows, :] = p_ref[...]
    e_old[0:rows, :] = st_ref[...]
    e_old[rows:rows + SUBLANES, :] = jnp.zeros((SUBLANES, B_CONV_CH), F32)
    pos = lax.broadcasted_iota(jnp.int32, (rows, 1), 0) % t
    xc = e_new[SUBLANES:SUBLANES + rows, :] * cw_ref[B_CONV - 1:B_CONV, :]
    for kk in range(1, B_CONV):
        window = slice(SUBLANES - kk, SUBLANES - kk + rows)
        src = jnp.where(pos >= kk, e_new[window, :], e_old[window, :])
        xc = xc + src * cw_ref[B_CONV - 1 - kk:B_CONV - kk, :]
    act = _silu(xc)

    ri = lax.broadcasted_iota(jnp.int32, (rows, LANES), 0)
    li16 = lax.broadcasted_iota(jnp.int32, (2 * SUBLANES, rows), 1)
    li1 = lax.broadcasted_iota(jnp.int32, (1, LANES), 1)
    head_lane = jnp.logical_and(li1 >= B_HEADS, li1 < 2 * B_HEADS)
    a_r = jnp.where(head_lane, -jnp.exp(alog_r_ref[...]), 0.0)
    g_col = a_r * _softplus(bac_ref[...] + dt_r_ref[...])
    si = lax.broadcasted_iota(jnp.int32, (2 * SUBLANES, 1), 0)
    head_sub = jnp.logical_and(si >= B_HEADS, si < 2 * B_HEADS)
    a_c = jnp.where(head_sub, -jnp.exp(alog_c_ref[...]), 0.0)
    g_row = a_c * _softplus(bar_ref[...] + dt_c_ref[...])
    rpos = ri % t
    lpos = li16 % t
    gc, rev, gcr = g_col, g_col, g_row
    step = 1
    while step < t:
        gc = gc + jnp.where(rpos >= step, pltpu.roll(gc, step, 0), 0.0)
        rev = rev + jnp.where(rpos < t - step, pltpu.roll(rev, rows - step, 0), 0.0)
        gcr = gcr + jnp.where(lpos >= step, pltpu.roll(gcr, step, 1), 0.0)
        step *= 2
    rev = rev - g_col
    egl = jnp.exp(gc + rev)

    bi = lax.broadcasted_iota(jnp.int32, (rows, rows), 0)
    bj = lax.broadcasted_iota(jnp.int32, (rows, rows), 1)
    same = (bi // t) == (bj // t)
    incl = jnp.logical_and(same, bi >= bj)
    strict = jnp.logical_and(same, bi > bj)
    eye = (bi == bj).astype(F32)

    lows, rhss, attns, qgs, kds = [], [], [], [], []
    for hh in range(B_HEADS):
        sl = slice(hh * B_DK, (hh + 1) * B_DK)
        gc_c = gc[:, B_HEADS + hh:B_HEADS + hh + 1]
        gc_r = gcr[B_HEADS + hh:B_HEADS + hh + 1, :]
        dec = jnp.where(incl, jnp.exp(jnp.where(incl, gc_c - gc_r, 0.0)), 0.0)
        q = act[:, sl]
        q = q * lax.rsqrt(jnp.sum(q * q, axis=-1, keepdims=True) + EPS) * (B_DK ** -0.5)
        k = act[:, B_QK + hh * B_DK:B_QK + (hh + 1) * B_DK]
        k = k * lax.rsqrt(jnp.sum(k * k, axis=-1, keepdims=True) + EPS)
        v = act[:, 2 * B_QK + hh * B_DV:2 * B_QK + (hh + 1) * B_DV]
        beta = jax.nn.sigmoid(bac_ref[:, hh:hh + 1])
        kb = k * beta
        kbf = k.astype(BF16)
        lows.append(jnp.where(strict, dec * _dot_nt(kb, kbf), 0.0))
        attns.append(dec * _dot_nt(q, kbf))
        rhss.append(jnp.concatenate([v * beta, kb * jnp.exp(gc_c)], axis=1))
        qgs.append(q * jnp.exp(gc_c))
        kds.append(k * jnp.exp(rev[:, B_HEADS + hh:B_HEADS + hh + 1]))
    tinvs = [eye - low for low in lows]
    pws = lows
    sq = 2
    while sq < t:
        pws = [_dot(pw, pw) for pw in pws]
        tinvs = [tinv + _dot(tinv, pw) for tinv, pw in zip(tinvs, pws)]
        sq *= 2
    sols = [_dot(tinv, rhs) for tinv, rhs in zip(tinvs, rhss)]

    rgrp = lax.broadcasted_iota(jnp.int32, (rows, B_DV), 0) // t
    units = [(j, hh) for hh in range(B_HEADS) for j in range(nseq)]
    rws = lambda j: slice(j * t, (j + 1) * t)
    states = [s0_ref[j, hh] for j, hh in units]
    proj = [_dot(jnp.concatenate([sols[hh][rws(j), B_DV:], qgs[hh][rws(j)]], axis=0), s)
            for (j, hh), s in zip(units, states)]
    v_new = [sols[hh][rws(j), :B_DV] - pr[:t] for (j, hh), pr in zip(units, proj)]
    outs = []
    for (j, hh), pr, vn in zip(units, proj, v_new):
        v_full = jnp.where(rgrp == j, jnp.concatenate([vn] * nseq, axis=0), 0.0)
        outs.append(pr[t:] + _dot(attns[hh][rws(j)], v_full))
    for (j, hh), s, vn in zip(units, states, v_new):
        decay = egl[j * t:j * t + 1, B_HEADS + hh:B_HEADS + hh + 1]
        s_ref[j, hh] = s * decay + _dot_tn(kds[hh][rws(j)], vn)
    for hh in range(B_HEADS):
        sl = slice(hh * B_DV, (hh + 1) * B_DV)
        o = jnp.concatenate(outs[hh * nseq:(hh + 1) * nseq], axis=0)
        gate = _silu(z_ref[:, sl].astype(F32))
        o_ref[:, sl] = (_rms(o, gout_ref[...]) * gate).astype(o_ref.dtype)


def _b_short(p, st, bac, bar, cw, alog_r, dt_r, alog_c, dt_c, z, s0, gout, t):
    assert t == SUBLANES
    n = p.shape[0]
    nseq = ROW_BLOCK // t
    row = lambda i: (i, 0)
    fixed = lambda i: (0, 0)
    state = pl.BlockSpec((nseq, B_HEADS, B_DK, B_DV), lambda i: (i, 0, 0, 0))
    return pl.pallas_call(
        functools.partial(_b_short_kernel, t=t),
        grid=(n // ROW_BLOCK,),
        in_specs=[pl.BlockSpec((ROW_BLOCK, B_CONV_CH), row), pl.BlockSpec((ROW_BLOCK, B_CONV_CH), row),
                  pl.BlockSpec((ROW_BLOCK, LANES), row), pl.BlockSpec((2 * SUBLANES, ROW_BLOCK), lambda i: (0, i)),
                  pl.BlockSpec((B_CONV, B_CONV_CH), fixed),
                  pl.BlockSpec((1, LANES), fixed), pl.BlockSpec((1, LANES), fixed),
                  pl.BlockSpec((2 * SUBLANES, 1), fixed), pl.BlockSpec((2 * SUBLANES, 1), fixed),
                  pl.BlockSpec((ROW_BLOCK, B_V), row), state, pl.BlockSpec((1, B_DV), fixed)],
        out_specs=[pl.BlockSpec((ROW_BLOCK, B_V), row), state],
        out_shape=[jax.ShapeDtypeStruct((n, B_V), BF16), jax.ShapeDtypeStruct(s0.shape, F32)],
        scratch_shapes=[pltpu.VMEM((SUBLANES + ROW_BLOCK, B_CONV_CH), F32)] * 2,
        compiler_params=_cparams(("parallel",)),
        name="b_short",
    )(p, st, bac, bar, cw, alog_r, dt_r, alog_c, dt_c, z, s0, gout)


def _c_short_kernel(q_ref, k_ref, v_ref, z_ref, r0_ref, gout_ref, o_ref, rout_ref, *, t):
    rows = ROW_BLOCK
    nseq = rows // t
    ri = lax.broadcasted_iota(jnp.int32, (rows, rows), 0)
    ci = lax.broadcasted_iota(jnp.int32, (rows, rows), 1)
    causal = jnp.logical_and(ri // t == ci // t, ri >= ci)
    steps = jnp.where(causal, (ri - ci).astype(F32), 0.0)
    cnt = (lax.broadcasted_iota(jnp.int32, (rows, 1), 0) % t + 1).astype(F32)
    qk_lane = lax.broadcasted_iota(jnp.int32, (1, C_QK), 1) // C_DK
    lg_lane = jnp.zeros((1, C_QK), F32)
    qk_sub = lax.broadcasted_iota(jnp.int32, (C_QK, 1), 0) // C_DK
    lg_sub = jnp.zeros((C_QK, 1), F32)
    for hh in range(C_HEADS):
        lg_lane = jnp.where(qk_lane == hh, _log_gamma(hh), lg_lane)
        lg_sub = jnp.where(qk_sub == hh, _log_gamma(hh), lg_sub)
    q = q_ref[...]
    k = k_ref[...]
    v = v_ref[...].astype(F32)
    qd = q * jnp.exp(cnt * lg_lane)
    kd = k * jnp.exp((t - cnt) * lg_lane)
    r_scale = jnp.exp(t * lg_sub)
    head = lambda hh: slice(hh * C_DV, (hh + 1) * C_DV)
    rws = lambda j: slice(j * t, (j + 1) * t)

    intra = []
    for hh in range(C_HEADS):
        att = jnp.exp(steps * _log_gamma(hh)) * _dot_nt(q, jnp.where(qk_lane == hh, k, 0.0))
        intra.append(_dot(jnp.where(causal, att, 0.0), v[:, head(hh)]))
    stacks = [r0_ref[j].reshape(C_QK, C_DV) for j in range(nseq)]
    units = [(j, hh) for hh in range(C_HEADS) for j in range(nseq)]
    inter = [_dot(jnp.where(qk_lane == hh, qd[rws(j)], 0.0), stacks[j]) for j, hh in units]
    upd = [_dot_tn(jnp.where(qk_lane == hh, kd[rws(j)], 0.0), v[rws(j), head(hh)]) for j, hh in units]
    for j in range(nseq):
        new = stacks[j] * r_scale
        for hh in range(C_HEADS):
            new = new + upd[hh * nseq + j]
        rout_ref[j] = new.reshape(C_HEADS, C_DK, C_DV)
    for hh in range(C_HEADS):
        o = intra[hh] + jnp.concatenate(inter[hh * nseq:(hh + 1) * nseq], axis=0)
        gate = _silu(z_ref[:, head(hh)].astype(F32))
        o_ref[:, head(hh)] = (_rms(o, gout_ref[...]) * gate).astype(o_ref.dtype)


def _c_short(q, k, v, z, r0, gout, t):
    n = q.shape[0]
    nseq = ROW_BLOCK // t
    row = lambda i: (i, 0)
    state = pl.BlockSpec((nseq, C_HEADS, C_DK, C_DV), lambda i: (i, 0, 0, 0))
    return pl.pallas_call(
        functools.partial(_c_short_kernel, t=t),
        grid=(n // ROW_BLOCK,),
        in_specs=[pl.BlockSpec((ROW_BLOCK, C_QK), row), pl.BlockSpec((ROW_BLOCK, C_QK), row),
                  pl.BlockSpec((ROW_BLOCK, C_V), row), pl.BlockSpec((ROW_BLOCK, C_V), row),
                  state, pl.BlockSpec((1, C_DV), lambda i: (0, 0))],
        out_specs=[pl.BlockSpec((ROW_BLOCK, C_V), row), state],
        out_shape=[jax.ShapeDtypeStruct((n, C_V), BF16), jax.ShapeDtypeStruct(r0.shape, F32)],
        compiler_params=_cparams(("parallel",)),
        name="c_short",
    )(q, k, v, z, r0, gout)


def _merge_kernel(x_ref, g1_ref, g2_ref, wg_ref, o0_ref, o1_ref, o2_ref, l0_ref, l1_ref, l2_ref, ob_ref, oc_ref,
                  wa_ref, wb_ref, wc_ref, wo_ref, y_ref, h2_ref, *scr, residue_major):
    x = x_ref[...]
    tm = x.shape[0]
    h = _rms(x, g1_ref[...]).astype(BF16)
    lses = [r[...].reshape(tm, LANES) for r in (l0_ref, l1_ref, l2_ref)]
    outs = [r[...].reshape(tm, A_WIDTH) for r in (o0_ref, o1_ref, o2_ref)]
    heads = []
    for hh in range(A_HEADS):
        sl = slice(hh * A_HD, (hh + 1) * A_HD)
        ls = [l[:, 32 * hh:32 * hh + 1] for l in lses]
        m = jnp.maximum(jnp.maximum(ls[0], ls[1]), ls[2])
        es = [jnp.exp(l - m) for l in ls]
        tot = es[0] + es[1] + es[2]
        acc = (es[0] / tot) * outs[0][:, sl].astype(F32)
        acc = acc + (es[1] / tot) * outs[1][:, sl].astype(F32)
        acc = acc + (es[2] / tot) * outs[2][:, sl].astype(F32)
        heads.append(acc)
    o_a = jnp.concatenate(heads, axis=1)
    if residue_major:
        o_a = _swap_row_grid(scr[0], o_a)
    o_a = o_a.astype(BF16)
    merged = None
    for gi, (o_g, w_ref) in enumerate(((o_a, wa_ref), (ob_ref[...], wb_ref), (oc_ref[...], wc_ref))):
        gate = jax.nn.sigmoid(jnp.dot(h, wg_ref[:, gi * D_MODEL:(gi + 1) * D_MODEL], preferred_element_type=F32))
        term = gate * jnp.dot(o_g, w_ref[...], preferred_element_type=F32)
        merged = term if merged is None else merged + term
    y = x + jnp.dot(merged.astype(BF16), wo_ref[...], preferred_element_type=F32)
    y_ref[...] = y
    h2_ref[...] = _rms(y, g2_ref[...]).astype(h2_ref.dtype)


def _merge(x, g1, g2, wg, o_groups, lses, o_b, o_c, wa, wb, wc, wo, seq_len, residue_major):
    n = x.shape[0]
    tm = RES * RES
    row = lambda i: (i, 0)
    fixed = lambda i: (0, 0)
    half = pl.BlockSpec((tm, A_WIDTH), row)
    wbr = pl.BlockSpec((A_WIDTH, D_MODEL), fixed)
    if residue_major:
        tiles = seq_len // tm
        grp = lambda i: (i // tiles, 0, i % tiles, 0)
        o_spec = pl.BlockSpec((None, RES, tm // RES, A_WIDTH), grp)
        lse = pl.BlockSpec((None, RES, tm // RES, LANES), grp)
        scratch = [pltpu.VMEM((A_WIDTH // LANES, tm, LANES), F32)]
    else:
        o_spec = half
        lse = pl.BlockSpec((tm, LANES), row)
        scratch = []
    return pl.pallas_call(
        functools.partial(_merge_kernel, residue_major=residue_major),
        grid=(n // tm,),
        in_specs=[pl.BlockSpec((tm, D_MODEL), row), pl.BlockSpec((1, D_MODEL), fixed),
                  pl.BlockSpec((1, D_MODEL), fixed),
                  pl.BlockSpec((D_MODEL, 3 * D_MODEL), lambda i: (0, W_COLS["gates"] // (3 * D_MODEL))),
                  o_spec, o_spec, o_spec, lse, lse, lse, half, half, wbr, wbr, wbr,
                  pl.BlockSpec((D_MODEL, D_MODEL), fixed)],
        out_specs=[pl.BlockSpec((tm, D_MODEL), row)] * 2,
        out_shape=[jax.ShapeDtypeStruct((n, D_MODEL), F32), jax.ShapeDtypeStruct((n, D_MODEL), BF16)],
        scratch_shapes=scratch,
        compiler_params=_cparams(("parallel",)),
        name="merge",
    )(x, g1, g2, wg, *o_groups, *lses, o_b, o_c, wa, wb, wc, wo)


def _ffn_kernel(x_ref, h_ref, wg_ref, wu_ref, wo_ref, y_ref, acc_scr):
    j = pl.program_id(1)

    @pl.when(j == 0)
    def _():
        acc_scr[...] = jnp.zeros_like(acc_scr)

    h = h_ref[...]
    gate = jnp.dot(h, wg_ref[...], preferred_element_type=F32)
    up = jnp.dot(h, wu_ref[...], preferred_element_type=F32)
    acc_scr[...] += jnp.dot((_silu(gate) * up).astype(BF16), wo_ref[...], preferred_element_type=F32)

    @pl.when(j == pl.num_programs(1) - 1)
    def _():
        y_ref[...] = x_ref[...] + acc_scr[...]


def _ffn(x, h, w_in, w_out):
    n = x.shape[0]
    tm = min(n, 1024)
    tf = 256
    nf = D_FF // tf
    row = lambda i, j: (i, 0)
    return pl.pallas_call(
        _ffn_kernel,
        grid=(n // tm, nf),
        in_specs=[pl.BlockSpec((tm, D_MODEL), row), pl.BlockSpec((tm, D_MODEL), row),
                  pl.BlockSpec((D_MODEL, tf), lambda i, j: (0, j)),
                  pl.BlockSpec((D_MODEL, tf), lambda i, j: (0, nf + j)),
                  pl.BlockSpec((tf, D_MODEL), lambda i, j: (j, 0))],
        out_specs=pl.BlockSpec((tm, D_MODEL), row),
        out_shape=jax.ShapeDtypeStruct((n, D_MODEL), F32),
        scratch_shapes=[pltpu.VMEM((tm, D_MODEL), F32)],
        compiler_params=_cparams(("parallel", "arbitrary")),
        name="ffn",
    )(x, h, w_in, w_in, w_out)


def _rope_tables(pos, hd, reps):
    inv = ROPE_THETA ** (-jnp.arange(0, hd, 2, dtype=F32) / hd)
    ang = pos.astype(F32)[:, None] * inv[None, :]
    cos = jnp.cos(ang)
    sin = jnp.sin(ang)
    cos2 = jnp.concatenate([cos, cos], axis=1)
    sin2 = jnp.concatenate([-sin, sin], axis=1)
    return jnp.tile(cos2, (1, reps)), jnp.tile(sin2, (1, reps))


def _pad_rows(a, batch, t, t_pad):
    if t == t_pad:
        return a
    a = a.reshape(batch, t, a.shape[-1])
    a = jnp.pad(a, ((0, 0), (0, t_pad - t), (0, 0)))
    return a.reshape(batch * t_pad, a.shape[-1])


def _unpad_rows(a, batch, t, t_pad):
    if t == t_pad:
        return a
    return a.reshape(batch, t_pad, a.shape[-1])[:, :t].reshape(batch * t, a.shape[-1])


def _layer(x, pos, batch, t, lw, caches, layer, conv_state, s0, r0):
    n = batch * t
    prompt = caches is None
    reps = max(1, 256 // t)
    cos_a, sin_a = _rope_tables(pos, A_HD, 1)
    cos_c, sin_c = _rope_tables(pos, C_DK, LANES // C_DK)
    if reps > 1:
        cos_a, sin_a, cos_c, sin_c = (jnp.tile(a, (reps, 1)) for a in (cos_a, sin_a, cos_c, sin_c))

    if prompt:
        to_rm = lambda a: a.reshape(t // RES, RES, A_HD).transpose(1, 0, 2)
        q, k, v, kv_tail = _proj_a(x, lw["g1"], lw["w_all"], lw["qn"], lw["kn"], to_rm(cos_a), to_rm(sin_a),
                                   batch, t, True)
    else:
        q, k, v = _proj_a(x, lw["g1"], lw["w_all"], lw["qn"], lw["kn"], cos_a, sin_a, batch, t, False)
    outs, lses = [], []
    for gi in range(N_GROUPS):
        if prompt:
            o, lse = _attn_prompt(q, k, v, gi, batch, t)
        else:
            o, lse = _attn_sample(q, k, v, caches[gi], layer, gi, batch, t)
        outs.append(o)
        lses.append(lse)
    new_kv = []
    for gi, (win, _) in enumerate(A_GROUPS):
        if prompt:
            keep = min(win, t)
            first = kv_tail.shape[2] - keep // RES
            rows = kv_tail[:, :, first:, 2 * gi * A_WIDTH:2 * (gi + 1) * A_WIDTH]
            new_kv.append(rows.transpose(0, 2, 1, 3).reshape(batch, keep, 2, A_HEADS, A_HD))
        else:
            cols = slice(gi * A_WIDTH, (gi + 1) * A_WIDTH)
            tail = lambda a: a[:, cols].reshape(batch, t, A_HEADS, A_HD)
            new_kv.append(jnp.stack([tail(k), tail(v)], axis=2))

    t_pad = -(-t // ROW_BLOCK) * ROW_BLOCK
    cst = jnp.pad(conv_state, ((0, 0), (SUBLANES - (B_CONV - 1), 0), (0, 0)))
    decay = (lw["alog_r"], lw["dt_r"], lw["alog_c"], lw["dt_c"])
    short = t == SUBLANES and n % ROW_BLOCK == 0
    if t_pad == t:
        act, z_b, bac, bar, cq, ck, cv, cz, p_last = _proj_bc(x, lw["g1"], lw["w_all"], lw["w_bat"], cos_c, sin_c,
                                                              cst, lw["conv_w"], t)
        conv_new = p_last[:, -(B_CONV - 1):]
        qg, kd, u, w, attn, egl = _b_prep(act, bac, bar, *decay, t, t)
    else:
        p, z_b, bac, bar, cq, ck, cv, cz = _proj_bc(x, lw["g1"], lw["w_all"], lw["w_bat"], cos_c, sin_c)
        conv_new = jnp.concatenate([conv_state, p.reshape(batch, t, B_CONV_CH)], axis=1)[:, -(B_CONV - 1):]
        if not short:
            qg, kd, u, w, attn, egl = _b_prep(
                _pad_rows(p, batch, t, t_pad), _pad_rows(bac, batch, t, t_pad),
                _pad_rows(bar.T, batch, t, t_pad).T, *decay, t_pad, t, cst, lw["conv_w"])
    if short:
        o_b, s_new = _b_short(p, cst.reshape(n, B_CONV_CH), bac, bar, lw["conv_w"], *decay, z_b, s0, lw["gb"], t)
    else:
        o_b, s_new = _b_scan(qg, kd, u, w, attn, egl, _pad_rows(z_b, batch, t, t_pad), s0, lw["gb"], batch, t_pad)
        o_b = _unpad_rows(o_b, batch, t, t_pad)

    if short:
        o_c, r_new = _c_short(cq, ck, cv, cz, r0, lw["gc"], t)
    else:
        o_c, r_new = _c_scan(*(_pad_rows(a, batch, t, t_pad) for a in (cq, ck, cv, cz)), r0, lw["gc"], batch, t_pad,
                             t)
        o_c = _unpad_rows(o_c, batch, t, t_pad)

    x, h2 = _merge(x, lw["g1"], lw["g2"], lw["w_all"], outs, lses, o_b, o_c, lw["w_oa"], lw["w_ob"], lw["w_oc"],
                   lw["w_o"], t, prompt)
    x = _ffn(x, h2, lw["w_fi"], lw["w_fo"])
    return x, new_kv, conv_new, s_new, r_new


def _layer_weights(l, norm1_g, w_in, a_q_norm_g, a_k_norm_g, b_conv_w, b_a_log, b_dt_bias, b_out_norm_g,
                   c_out_norm_g, w_out_a, w_out_b, w_out_c, w_out, norm2_g, w_ffn_in, w_ffn_out):
    o = IN_OFFS
    wl = w_in[l]
    w_ba = wl[:, o[3]:o[5]]
    pad_r = lambda a: jnp.pad(a.reshape(1, B_HEADS), ((0, 0), (B_HEADS, LANES - 2 * B_HEADS)))
    pad_c = lambda a: jnp.pad(a.reshape(B_HEADS, 1), ((B_HEADS, 2 * SUBLANES - 2 * B_HEADS), (0, 0)))
    return dict(
        g1=norm1_g[l].reshape(1, D_MODEL), g2=norm2_g[l].reshape(1, D_MODEL),
        w_all=jnp.concatenate([wl[:, o[0]:o[1]], wl[:, o[1]:o[2]], wl[:, o[9]:o[10]], wl[:, o[5]:o[9]], wl[:, o[2]:o[3]],
                               jnp.pad(w_ba, ((0, 0), (0, LANES - 2 * B_HEADS)))], axis=1).astype(BF16),
        w_bat=jnp.pad(w_ba.T, ((0, 2 * SUBLANES - 2 * B_HEADS), (0, 0))).astype(BF16),
        qn=a_q_norm_g[l].reshape(1, A_HD), kn=a_k_norm_g[l].reshape(1, A_HD),
        conv_w=b_conv_w[l],
        alog_r=pad_r(b_a_log[l]), dt_r=pad_r(b_dt_bias[l]), alog_c=pad_c(b_a_log[l]), dt_c=pad_c(b_dt_bias[l]),
        gb=b_out_norm_g[l].reshape(1, B_DV), gc=c_out_norm_g[l].reshape(1, C_DV),
        w_oa=w_out_a[l].astype(BF16), w_ob=w_out_b[l].astype(BF16), w_oc=w_out_c[l].astype(BF16),
        w_o=w_out[l].astype(BF16), w_fi=w_ffn_in[l].astype(BF16), w_fo=w_ffn_out[l].astype(BF16),
    )


def kernel(x_prompt, x_sample, cache_a_kv0, cache_a_kv1, cache_a_kv2, state_b_conv, state_b_S, state_c_R, norm1_g, w_in, a_q_norm_g, a_k_norm_g, b_conv_w, b_a_log, b_dt_bias, b_out_norm_g, c_out_norm_g, w_out_a, w_out_b, w_out_c, w_out, norm2_g, w_ffn_in, w_ffn_out):
    bp, t = x_prompt.shape[:2]
    bs, s = x_sample.shape[:2]
    depth = w_in.shape[0]
    pos_p = jnp.arange(t)
    pos_s = PAST_LEN + jnp.arange(s)
    yp = x_prompt.reshape(bp * t, D_MODEL)
    ys = x_sample.reshape(bs * s, D_MODEL)
    caches = (cache_a_kv0, cache_a_kv1, cache_a_kv2)
    zeros_conv = jnp.zeros((bp, B_CONV - 1, B_CONV_CH), F32)
    zeros_s = jnp.zeros((bp, B_HEADS, B_DK, B_DV), F32)
    zeros_r = jnp.zeros((bp, C_HEADS, C_DK, C_DV), F32)
    acc = [[] for _ in range(12)]
    for l in range(depth):
        lw = _layer_weights(l, norm1_g, w_in, a_q_norm_g, a_k_norm_g, b_conv_w, b_a_log, b_dt_bias,
                            b_out_norm_g, c_out_norm_g, w_out_a, w_out_b, w_out_c, w_out, norm2_g,
                            w_ffn_in, w_ffn_out)
        yp, kv, cv, sn, rn = _layer(yp, pos_p, bp, t, lw, None, l, zeros_conv, zeros_s, zeros_r)
        for i, a in enumerate((kv[0], kv[1], kv[2], cv, sn, rn)):
            acc[i].append(a)
        ys, kv, cv, sn, rn = _layer(ys, pos_s, bs, s, lw, caches, l, state_b_conv[l], state_b_S[l], state_c_R[l])
        for i, a in enumerate((kv[0], kv[1], kv[2], cv, sn, rn)):
            acc[6 + i].append(a)
    return (yp.reshape(bp, t, D_MODEL), ys.reshape(bs, s, D_MODEL)) + tuple(jnp.stack(a) for a in acc)
```

```python
import functools
import math

import jax
import jax.numpy as jnp
import numpy as np
from jax import lax
from jax.experimental import pallas as pl
from jax.experimental.pallas import tpu as pltpu

F32 = jnp.float32
BF16 = jnp.bfloat16

D_MODEL = 1024
PAST_LEN = 8192
A_GROUPS = ((128, 1), (512, 4), (2048, 16))
N_GROUPS = 3
A_HEADS = 4
A_HD = 128
A_WIDTH = A_HEADS * A_HD
A_KEYS = 128
B_HEADS = 4
B_DK = 128
B_DV = 128
B_CONV = 4
B_QK = B_HEADS * B_DK
B_V = B_HEADS * B_DV
B_CONV_CH = 2 * B_QK + B_V
C_HEADS = 4
C_DK = 64
C_DV = 128
C_QK = C_HEADS * C_DK
C_V = C_HEADS * C_DV
CHUNK = 64
ROPE_THETA = 10000.0
EPS = 1e-6
D_FF = 2816
IN_SIZES = (3 * N_GROUPS * A_WIDTH, B_CONV_CH, B_V, B_HEADS, B_HEADS, C_QK, C_QK, C_V, C_V, 3 * D_MODEL)
IN_OFFS = tuple(int(v) for v in np.cumsum((0,) + IN_SIZES))

W_COLS = {"a": 0, "b_qkv": 4608, "gates": 6144, "c": 9216, "b_z": 10752, "b_ba": 11264}
W_ALL = 11392

ROW_BLOCK = 128
RES = 16
SUBLANES = 8
LANES = 128
VMEM_LIMIT = 48 * 1024 * 1024


def _cparams(sem):
    return pltpu.CompilerParams(dimension_semantics=sem, vmem_limit_bytes=VMEM_LIMIT)


def _rms(x, g):
    return x * lax.rsqrt(jnp.mean(x * x, axis=-1, keepdims=True) + EPS) * g


def _silu(x):
    return x * jax.nn.sigmoid(x)


def _softplus(x):
    return jnp.maximum(x, 0.0) + jnp.log(1.0 + jnp.exp(-jnp.abs(x)))


def _dot(a, b):
    return jnp.dot(a.astype(BF16), b.astype(BF16), preferred_element_type=F32)


def _dot_nt(a, b):
    return lax.dot_general(a.astype(BF16), b.astype(BF16), (((1,), (1,)), ((), ())), preferred_element_type=F32)


def _dot_tn(a, b):
    return lax.dot_general(a.astype(BF16), b.astype(BF16), (((0,), (0,)), ((), ())), preferred_element_type=F32)


def _swap_row_grid(scr, val):
    slabs = val.shape[1] // LANES
    for c in range(slabs):
        scr[c] = val[:, c * LANES:(c + 1) * LANES]
    cols = [jnp.concatenate([scr[c, pl.ds(r, RES, stride=RES), :] for r in range(RES)], axis=0)
            for c in range(slabs)]
    return jnp.concatenate(cols, axis=1)


def _proj_a_kernel(x_ref, g1_ref, w_ref, qg_ref, kg_ref, cos_ref, sin_ref, q_ref, k_ref, v_ref, *rest, residue_major):
    x = x_ref[...]
    tm = x.shape[0]
    if residue_major:
        tail_ref, scr = rest
        x = _swap_row_grid(scr, x)
    h = _rms(x, g1_ref[...]).astype(BF16)
    cos = cos_ref[...].reshape(tm, A_HD)
    sin = sin_ref[...].reshape(tm, A_HD)

    def norm_rope(seg, g):
        y = _rms(seg, g)
        return y * cos + pltpu.roll(y, A_HD // 2, 1) * sin

    def put(ref, col, val):
        if residue_major:
            ref[:, :, col:col + val.shape[1]] = val.reshape(RES, tm // RES, val.shape[1]).astype(ref.dtype)
        else:
            ref[:, col:col + val.shape[1]] = val

    for j in range(3 * N_GROUPS):
        acc = jnp.dot(h, w_ref[:, j * A_WIDTH:(j + 1) * A_WIDTH], preferred_element_type=F32)
        if j < N_GROUPS:
            for hh in range(A_HEADS):
                sl = slice(hh * A_HD, (hh + 1) * A_HD)
                put(q_ref, j * A_WIDTH + hh * A_HD, norm_rope(acc[:, sl], qg_ref[...]) * (A_HD ** -0.5))
        elif j < 2 * N_GROUPS:
            jj = j - N_GROUPS
            for hh in range(A_HEADS):
                sl = slice(hh * A_HD, (hh + 1) * A_HD)
                val = norm_rope(acc[:, sl], kg_ref[...])
                put(k_ref, jj * A_WIDTH + hh * A_HD, val)
                if residue_major:
                    put(tail_ref, 2 * jj * A_WIDTH + hh * A_HD, val)
        else:
            jj = j - 2 * N_GROUPS
            put(v_ref, jj * A_WIDTH, acc)
            if residue_major:
                put(tail_ref, (2 * jj + 1) * A_WIDTH, acc)


def _proj_a(x, g1, w, qg, kg, cos, sin, batch, seq_len, residue_major):
    n = x.shape[0]
    tm = RES * RES
    nw = N_GROUPS * A_WIDTH
    fixed = lambda i: (0, 0)
    common = [pl.BlockSpec((1, D_MODEL), fixed), pl.BlockSpec((D_MODEL, 3 * nw), fixed),
              pl.BlockSpec((1, A_HD), fixed), pl.BlockSpec((1, A_HD), fixed)]
    if residue_major:
        tiles = seq_len // tm
        tail_tiles = min(max(wd for wd, _ in A_GROUPS), seq_len) // tm
        blk = (None, RES, tm // RES, nw)
        tab = pl.BlockSpec((RES, tm // RES, A_HD), lambda i: (0, i % tiles, 0))
        main = pl.BlockSpec(blk, lambda i: (i // tiles, 0, i % tiles, 0))
        tail = pl.BlockSpec((None, RES, tm // RES, 2 * nw),
                            lambda i: (i // tiles, 0, jnp.maximum(i % tiles - (tiles - tail_tiles), 0), 0))
        out_specs = [main] * 3 + [tail]
        out_shape = ([jax.ShapeDtypeStruct((batch, RES, seq_len // RES, nw), BF16)] * 3
                     + [jax.ShapeDtypeStruct((batch, RES, tail_tiles * tm // RES, 2 * nw), F32)])
        scratch = [pltpu.VMEM((D_MODEL // LANES, tm, LANES), F32)]
    else:
        assert cos.shape[0] == tm
        tab = pl.BlockSpec((tm, A_HD), fixed)
        out_specs = [pl.BlockSpec((tm, nw), lambda i: (i, 0))] * 3
        out_shape = [jax.ShapeDtypeStruct((n, nw), F32)] * 3
        scratch = []
    return pl.pallas_call(
        functools.partial(_proj_a_kernel, residue_major=residue_major),
        grid=(n // tm,),
        in_specs=[pl.BlockSpec((tm, D_MODEL), lambda i: (i, 0))] + common + [tab, tab],
        out_specs=out_specs,
        out_shape=out_shape,
        scratch_shapes=scratch,
        compiler_params=_cparams(("arbitrary",)),
        name="proj_a",
    )(x, g1, w, qg, kg, cos, sin)


ATTN_SUBS = 4


def _attn_prompt_kernel(q_ref, kc_ref, kp_ref, vc_ref, vp_ref, o_ref, lse_ref, *, parts):
    n = pl.program_id(2)
    per = ROW_BLOCK // parts
    multi = len(q_ref.shape) == 4
    n_seq = q_ref.shape[1] if multi else 1
    view = lambda ref, r: ref.at[:, r] if multi else ref
    subs = q_ref.shape[-2] // per
    qi = lax.broadcasted_iota(jnp.int32, (ROW_BLOCK, ROW_BLOCK), 0)
    kj = lax.broadcasted_iota(jnp.int32, (ROW_BLOCK, ROW_BLOCK), 1)
    qi = parts * (qi % per) + qi // per
    kj = parts * (kj % per) + kj // per
    cur_ok = kj <= qi
    prev_ok = kj >= qi
    first_ok = jnp.logical_and(prev_ok, n > 0)
    lane = lax.broadcasted_iota(jnp.int32, (ROW_BLOCK, LANES), 1)
    neg = -jnp.inf
    packed_rows = 2 * SUBLANES

    def sub(ref, half, sl):
        full = ref[:, :, sl]
        if per % packed_rows == 0:
            return full[:, half * per:(half + 1) * per].reshape(ROW_BLOCK, A_HD)
        return full.astype(F32)[:, half * per:(half + 1) * per].reshape(ROW_BLOCK, A_HD).astype(BF16)

    units = [(r, half, hh) for r in range(n_seq) for half in range(subs) for hh in range(A_HEADS)]
    head = lambda hh: slice(hh * A_HD, (hh + 1) * A_HD)

    keys = {(r, half, hh): sub(view(kc_ref, r), half, head(hh)) for r, half, hh in units}
    vals = {(r, half, hh): sub(view(vc_ref, r), half, head(hh)) for r, half, hh in units}
    for r in range(n_seq):
        for hh in range(A_HEADS):
            last = kp_ref.shape[-2] // per - 1
            keys[(r, -1, hh)] = sub(view(kp_ref, r), last, head(hh))
            vals[(r, -1, hh)] = sub(view(vp_ref, r), last, head(hh))

    scores = []
    for r, half, hh in units:
        q = sub(view(q_ref, r), half, head(hh))
        s_cur = jnp.where(cur_ok, _dot_nt(q, keys[(r, half, hh)]), neg)
        s_prev = jnp.where(first_ok if half == 0 else prev_ok, _dot_nt(q, keys[(r, half - 1, hh)]), neg)
        scores.append((s_cur, s_prev))
    probs = []
    for s_cur, s_prev in scores:
        m = jnp.max(jnp.maximum(s_cur, s_prev), axis=-1, keepdims=True)
        p_cur = jnp.exp(s_cur - m)
        p_prev = jnp.exp(s_prev - m)
        den = jnp.sum(p_cur + p_prev, axis=-1, keepdims=True)
        probs.append((p_cur, p_prev, m, den))
    lse_blk = {(r, half): jnp.zeros((ROW_BLOCK, LANES), F32) for r in range(n_seq) for half in range(subs)}
    outs = {}
    for (r, half, hh), (p_cur, p_prev, m, den) in zip(units, probs):
        o = (_dot(p_cur, vals[(r, half, hh)]) + _dot(p_prev, vals[(r, half - 1, hh)])) / den
        outs[(r, half, hh)] = o.reshape(parts, per, A_HD)
        lse_blk[(r, half)] = jnp.where(lane // 32 == hh, m + jnp.log(den), lse_blk[(r, half)])
    for r in range(n_seq):
        for hh in range(A_HEADS):
            both = jnp.concatenate([outs[(r, half, hh)] for half in range(subs)], axis=1)
            view(o_ref, r)[:, :, head(hh)] = both.astype(o_ref.dtype)
        for half in range(subs):
            view(lse_ref, r)[:, half * per:(half + 1) * per, :] = lse_blk[(r, half)].reshape(parts, per, LANES)


def _attn_prompt(q, k, v, gi, batch, seq_len):
    _, dil = A_GROUPS[gi]
    parts = RES // dil
    per = ROW_BLOCK // parts
    rows = seq_len // RES
    subs = min(ATTN_SUBS, seq_len // dil // ROW_BLOCK)
    nblk = seq_len // dil // (subs * ROW_BLOCK)
    split = lambda a: a.reshape(batch, parts, dil, rows, a.shape[-1])
    cur = lambda b, r, n: (b, 0, r, n, gi)
    out = lambda b, r, n: (b, 0, r, n, 0)
    n_seq = min(ATTN_SUBS // subs, dil)
    seq_dim = None if n_seq == 1 else n_seq
    blk = (None, parts, seq_dim, subs * per, A_WIDTH)
    if per % (2 * SUBLANES) == 0:
        blk_prev = (None, parts, seq_dim, per, A_WIDTH)
        prev = lambda b, r, n: (b, 0, r, jnp.maximum(subs * n - 1, 0), gi)
    else:
        blk_prev = blk
        prev = lambda b, r, n: (b, 0, r, jnp.maximum(n - 1, 0), gi)
    o, lse = pl.pallas_call(
        functools.partial(_attn_prompt_kernel, parts=parts),
        grid=(batch, dil // n_seq, nblk),
        in_specs=[pl.BlockSpec(blk, cur), pl.BlockSpec(blk, cur), pl.BlockSpec(blk_prev, prev),
                  pl.BlockSpec(blk, cur), pl.BlockSpec(blk_prev, prev)],
        out_specs=[pl.BlockSpec(blk, out), pl.BlockSpec((None, parts, seq_dim, subs * per, LANES), out)],
        out_shape=[jax.ShapeDtypeStruct((batch, parts, dil, rows, A_WIDTH), BF16),
                   jax.ShapeDtypeStruct((batch, parts, dil, rows, LANES), F32)],
        compiler_params=_cparams(("parallel", "parallel", "arbitrary")),
        name=f"attn_prompt_g{gi}",
    )(split(q), split(k), split(k), split(v), split(v))
    return o.reshape(batch, RES, rows, A_WIDTH), lse.reshape(batch, RES, rows, LANES)


def _attn_sample_kernel(q_ref, kn_ref, vn_ref, cache_ref, o_ref, lse_ref, *, dil, n_new):
    n_res = min(dil, n_new)
    neg = -jnp.inf
    srow = lax.broadcasted_iota(jnp.int32, (n_new, A_KEYS), 0)
    mcol = lax.broadcasted_iota(jnp.int32, (n_new, A_KEYS), 1)
    in_window = mcol >= srow // dil
    row_res = [srow % dil == res for res in range(n_res)]
    srow_n = lax.broadcasted_iota(jnp.int32, (n_new, n_new), 0)
    tcol_n = lax.broadcasted_iota(jnp.int32, (n_new, n_new), 1)
    new_ok = jnp.logical_and(tcol_n <= srow_n, (srow_n - tcol_n) % dil == 0)
    heads = range(A_HEADS)

    qs = [q_ref[hh].astype(BF16) for hh in heads]
    keys = {(res, hh): cache_ref[:, res, 0, hh, :].astype(BF16) for res in range(n_res) for hh in heads}
    vals = {(res, hh): cache_ref[:, res, 1, hh, :].astype(BF16) for res in range(n_res) for hh in heads}
    raw = {key: _dot_nt(qs[key[1]], kmat) for key, kmat in keys.items()}
    s_new = [jnp.where(new_ok, _dot_nt(qs[hh], kn_ref[hh]), neg) for hh in heads]
    probs = []
    for hh in heads:
        s_buf = raw[(0, hh)]
        for res in range(1, n_res):
            s_buf = jnp.where(row_res[res], raw[(res, hh)], s_buf)
        s_buf = jnp.where(in_window, s_buf, neg)
        m = jnp.maximum(jnp.max(s_buf, axis=-1, keepdims=True), jnp.max(s_new[hh], axis=-1, keepdims=True))
        p_buf = jnp.exp(s_buf - m)
        p_new = jnp.exp(s_new[hh] - m)
        den = jnp.sum(p_buf, axis=-1, keepdims=True) + jnp.sum(p_new, axis=-1, keepdims=True)
        probs.append((p_buf, p_new, m, den))
    for hh in heads:
        p_buf, p_new, m, den = probs[hh]
        acc = _dot(p_new, vn_ref[hh])
        for res in range(n_res):
            p_res = p_buf if n_res == 1 else jnp.where(row_res[res], p_buf, 0.0)
            acc = acc + _dot(p_res, vals[(res, hh)])
        o_ref[hh] = acc / den
        lse_ref[hh] = jnp.broadcast_to(m + jnp.log(den), (n_new, LANES))


def _attn_sample_rows_kernel(q_ref, kn_ref, vn_ref, cache_ref, o_ref, lse_ref, *, dil, n_new):
    row = lax.broadcasted_iota(jnp.int32, (A_KEYS, A_HEADS, 1), 0)
    trow = lax.broadcasted_iota(jnp.int32, (n_new, A_HEADS, 1), 0)
    neg = -jnp.inf
    kn = kn_ref[...]
    vn = vn_ref[...]
    for s in range(n_new):
        res = s % dil
        first = s // dil
        q = q_ref[s][None]
        kc = cache_ref[:, res, 0]
        vc = cache_ref[:, res, 1]
        sc = jnp.sum(kc * q, axis=-1, keepdims=True)
        if first > 0:
            sc = jnp.where(row >= first, sc, neg)
        new_ok = jnp.logical_and(trow <= s, (s - trow) % dil == 0)
        sn = jnp.where(new_ok, jnp.sum(kn * q, axis=-1, keepdims=True), neg)
        m = jnp.maximum(jnp.max(sc, axis=0, keepdims=True), jnp.max(sn, axis=0, keepdims=True))
        pc = jnp.exp(sc - m)
        pn = jnp.exp(sn - m)
        den = jnp.sum(pc, axis=0, keepdims=True) + jnp.sum(pn, axis=0, keepdims=True)
        o = (jnp.sum(pc * vc, axis=0, keepdims=True) + jnp.sum(pn * vn, axis=0, keepdims=True)) / den
        o_ref[s] = o[0]
        lse_ref[s] = jnp.broadcast_to((m + jnp.log(den))[0], (A_HEADS, A_HD))


def _attn_sample(q, k, v, cache, layer, gi, batch, n_new):
    win, dil = A_GROUPS[gi]
    depth = cache.shape[0]
    assert cache.shape[2] == win and win // dil == A_KEYS
    n_res = min(dil, n_new)
    cv = cache.reshape(depth, batch, A_KEYS, dil, 2, A_HEADS, A_HD)
    cache_spec = pl.BlockSpec((None, None, A_KEYS, n_res, 2, A_HEADS, A_HD), lambda b: (layer, b, 0, 0, 0, 0, 0))
    if n_res > 1:
        heads = lambda a: a.reshape(batch, n_new, N_GROUPS, A_HEADS, A_HD)
        new = pl.BlockSpec((None, n_new, None, A_HEADS, A_HD), lambda b: (b, 0, gi, 0, 0))
        out = pl.BlockSpec((None, n_new, A_HEADS, A_HD), lambda b: (b, 0, 0, 0))
        o, lse = pl.pallas_call(
            functools.partial(_attn_sample_rows_kernel, dil=dil, n_new=n_new),
            grid=(batch,),
            in_specs=[new, new, new, cache_spec],
            out_specs=[out, out],
            out_shape=[jax.ShapeDtypeStruct((batch, n_new, A_HEADS, A_HD), F32)] * 2,
            compiler_params=_cparams(("parallel",)),
            name=f"attn_sample_g{gi}",
        )(heads(q), heads(k), heads(v), cv)
        lse = jnp.repeat(lse[..., 0], LANES // A_HEADS, axis=-1)
        return o.reshape(batch * n_new, A_WIDTH), lse.reshape(batch * n_new, LANES)
    heads = lambda a: a.reshape(batch, n_new, N_GROUPS, A_HEADS, A_HD).transpose(0, 2, 3, 1, 4)
    new = pl.BlockSpec((None, None, A_HEADS, n_new, A_HD), lambda b: (b, gi, 0, 0, 0))
    out = pl.BlockSpec((None, A_HEADS, n_new, LANES), lambda b: (b, 0, 0, 0))
    o, lse = pl.pallas_call(
        functools.partial(_attn_sample_kernel, dil=dil, n_new=n_new),
        grid=(batch,),
        in_specs=[new, new, new,
                  pl.BlockSpec((None, None, A_KEYS, n_res, 2, A_HEADS, A_HD), lambda b: (layer, b, 0, 0, 0, 0, 0))],
        out_specs=[out, out],
        out_shape=[jax.ShapeDtypeStruct((batch, A_HEADS, n_new, A_HD), F32)] * 2,
        compiler_params=_cparams(("parallel",)),
        name=f"attn_sample_g{gi}",
    )(heads(q), heads(k), heads(v), cv)
    o = o.transpose(0, 2, 1, 3)
    lse = jnp.repeat(lse[..., 0].transpose(0, 2, 1), LANES // A_HEADS, axis=-1)
    return o.reshape(batch * n_new, A_WIDTH), lse.reshape(batch * n_new, LANES)


def _causal_conv(e_scr, cw_ref, rows):
    xc = e_scr[SUBLANES:SUBLANES + rows, :] * cw_ref[B_CONV - 1:B_CONV, :]
    for kk in range(1, B_CONV):
        xc = xc + e_scr[SUBLANES - kk:SUBLANES - kk + rows, :] * cw_ref[B_CONV - 1 - kk:B_CONV - kk, :]
    return xc


def _proj_c_body(h, w_ref, cos_ref, sin_ref, q_ref, k_ref, v_ref, z_ref):
    cos = cos_ref[...]
    sin = sin_ref[...]
    lane = lax.broadcasted_iota(jnp.int32, cos.shape, 1)
    first_half = (lane % C_DK) < (C_DK // 2)

    def rope(seg):
        swapped = jnp.where(first_half, pltpu.roll(seg, LANES - C_DK // 2, 1), pltpu.roll(seg, C_DK // 2, 1))
        return seg * cos + swapped * sin

    qk = jnp.dot(h, w_ref[:, 0:2 * C_QK], preferred_element_type=F32)
    for j in range(2 * C_QK // LANES):
        seg = rope(qk[:, j * LANES:(j + 1) * LANES])
        if j < C_QK // LANES:
            q_ref[:, j * LANES:(j + 1) * LANES] = seg
        else:
            jj = j - C_QK // LANES
            k_ref[:, jj * LANES:(jj + 1) * LANES] = seg * (C_DK ** -0.5)
    v_ref[...] = jnp.dot(h, w_ref[:, 2 * C_QK:2 * C_QK + C_V], preferred_element_type=F32).astype(v_ref.dtype)
    z_ref[...] = jnp.dot(h, w_ref[:, 2 * C_QK + C_V:2 * C_QK + 2 * C_V],
                         preferred_element_type=F32).astype(z_ref.dtype)


def _proj_bc_kernel(x_ref, g1_ref, wqkv_ref, wz_ref, wba_ref, wbat_ref, wc_ref, cosc_ref, sinc_ref, *rest,
                    fuse_conv, tiles_per_seq):
    h = _rms(x_ref[...], g1_ref[...]).astype(BF16)
    tm = h.shape[0]
    if fuse_conv:
        cst_ref, cw_ref, p_ref, z_ref, bac_ref, bar_ref, cq_ref, ck_ref, cv_ref, cz_ref, ptail_ref, e_scr = rest
        first = pl.program_id(0) % tiles_per_seq == 0

        @pl.when(first)
        def _():
            e_scr[0:SUBLANES, :] = cst_ref[...]

        @pl.when(jnp.logical_not(first))
        def _():
            e_scr[0:SUBLANES, :] = e_scr[tm:tm + SUBLANES, :]

        for j in range(3):
            sl = slice(j * B_QK, (j + 1) * B_QK)
            e_scr[SUBLANES:SUBLANES + tm, sl] = jnp.dot(h, wqkv_ref[:, sl], preferred_element_type=F32)
        p_ref[...] = _silu(_causal_conv(e_scr, cw_ref, tm)).astype(p_ref.dtype)
        ptail_ref[...] = e_scr[tm:tm + SUBLANES, :]
    else:
        p_ref, z_ref, bac_ref, bar_ref, cq_ref, ck_ref, cv_ref, cz_ref = rest
        for j in range(3):
            sl = slice(j * B_QK, (j + 1) * B_QK)
            p_ref[:, sl] = jnp.dot(h, wqkv_ref[:, sl], preferred_element_type=F32)
    z_ref[...] = jnp.dot(h, wz_ref[...], preferred_element_type=F32).astype(z_ref.dtype)
    bac_ref[...] = jnp.dot(h, wba_ref[...], preferred_element_type=F32)
    bar_ref[...] = lax.dot_general(wbat_ref[...], h, (((1,), (1,)), ((), ())), preferred_element_type=F32)
    _proj_c_body(h, wc_ref, cosc_ref, sinc_ref, cq_ref, ck_ref, cv_ref, cz_ref)


def _proj_bc(x, g1, w_all, wbat, cos_c, sin_c, cstate=None, conv_w=None, seq_len=None):
    n = x.shape[0]
    tm = 256
    tab_blocks = cos_c.shape[0] // tm
    row = lambda i: (i, 0)
    fixed = lambda i: (0, 0)
    tab = (lambda i: (i % tab_blocks, 0)) if tab_blocks > 1 else fixed
    fuse_conv = cstate is not None
    c_width = 2 * C_QK + 2 * C_V
    in_specs = [
        pl.BlockSpec((tm, D_MODEL), row),
        pl.BlockSpec((1, D_MODEL), fixed),
        pl.BlockSpec((D_MODEL, B_CONV_CH), lambda i: (0, W_COLS["b_qkv"] // B_CONV_CH)),
        pl.BlockSpec((D_MODEL, B_V), lambda i: (0, W_COLS["b_z"] // B_V)),
        pl.BlockSpec((D_MODEL, LANES), lambda i: (0, W_COLS["b_ba"] // LANES)),
        pl.BlockSpec((2 * SUBLANES, D_MODEL), fixed),
        pl.BlockSpec((D_MODEL, c_width), lambda i: (0, W_COLS["c"] // c_width)),
        pl.BlockSpec((tm, LANES), tab),
        pl.BlockSpec((tm, LANES), tab),
    ]
    out_specs = [pl.BlockSpec((tm, B_CONV_CH), row), pl.BlockSpec((tm, B_V), row),
                 pl.BlockSpec((tm, LANES), row), pl.BlockSpec((2 * SUBLANES, tm), lambda i: (0, i)),
                 pl.BlockSpec((tm, C_QK), row), pl.BlockSpec((tm, C_QK), row),
                 pl.BlockSpec((tm, C_V), row), pl.BlockSpec((tm, C_V), row)]
    out_shape = [jax.ShapeDtypeStruct((n, B_CONV_CH), BF16 if fuse_conv else F32),
                 jax.ShapeDtypeStruct((n, B_V), BF16),
                 jax.ShapeDtypeStruct((n, LANES), F32), jax.ShapeDtypeStruct((2 * SUBLANES, n), F32),
                 jax.ShapeDtypeStruct((n, C_QK), F32), jax.ShapeDtypeStruct((n, C_QK), F32),
                 jax.ShapeDtypeStruct((n, C_V), BF16), jax.ShapeDtypeStruct((n, C_V), BF16)]
    args = [x, g1, w_all, w_all, w_all, wbat, w_all, cos_c, sin_c]
    scratch = []
    tiles = None
    if fuse_conv:
        tiles = seq_len // tm
        per_seq = pl.BlockSpec((None, SUBLANES, B_CONV_CH), lambda i: (i // tiles, 0, 0))
        in_specs += [per_seq, pl.BlockSpec((B_CONV, B_CONV_CH), fixed)]
        out_specs.append(per_seq)
        out_shape.append(jax.ShapeDtypeStruct((n // seq_len, SUBLANES, B_CONV_CH), F32))
        args += [cstate, conv_w]
        scratch = [pltpu.VMEM((SUBLANES + tm, B_CONV_CH), F32)]
    return pl.pallas_call(
        functools.partial(_proj_bc_kernel, fuse_conv=fuse_conv, tiles_per_seq=tiles),
        grid=(n // tm,),
        in_specs=in_specs,
        out_specs=out_specs,
        out_shape=out_shape,
        scratch_shapes=scratch,
        compiler_params=_cparams(("arbitrary",) if fuse_conv else ("parallel",)),
        name="proj_bc",
    )(*args)


def _b_prep_kernel(*refs, rows, blocks_per_seq, t_valid, conv_done):
    i = pl.program_id(0)
    blk = i % blocks_per_seq
    if conv_done:
        (p_ref, bac_ref, bar_ref, alog_r_ref, dt_r_ref, alog_c_ref, dt_c_ref,
         qg_ref, kd_ref, u_ref, w_ref, attn_ref, egl_ref) = refs
        act = p_ref[...].astype(F32)
    else:
        (p_ref, halo_ref, cst_ref, cw_ref, bac_ref, bar_ref, alog_r_ref, dt_r_ref, alog_c_ref, dt_c_ref,
         qg_ref, kd_ref, u_ref, w_ref, attn_ref, egl_ref, e_scr) = refs
        e_scr[0:SUBLANES, :] = jnp.where(blk == 0, cst_ref[...], halo_ref[...])
        e_scr[SUBLANES:SUBLANES + rows, :] = p_ref[...]
        act = _silu(_causal_conv(e_scr, cw_ref, rows))

    ri = lax.broadcasted_iota(jnp.int32, (rows, LANES), 0)
    li16 = lax.broadcasted_iota(jnp.int32, (2 * SUBLANES, rows), 1)
    li1 = lax.broadcasted_iota(jnp.int32, (1, LANES), 1)
    masked = t_valid < blocks_per_seq * rows
    if masked:
        row_ok = (blk * rows + ri) < t_valid
        col_ok = (blk * rows + li16) < t_valid
        act = jnp.where(ri[:, 0:1] + blk * rows < t_valid, act, 0.0)

    head_lane = jnp.logical_and(li1 >= B_HEADS, li1 < 2 * B_HEADS)
    a_r = jnp.where(head_lane, -jnp.exp(alog_r_ref[...]), 0.0)
    g_col = a_r * _softplus(bac_ref[...] + dt_r_ref[...])
    si = lax.broadcasted_iota(jnp.int32, (2 * SUBLANES, 1), 0)
    head_sub = jnp.logical_and(si >= B_HEADS, si < 2 * B_HEADS)
    a_c = jnp.where(head_sub, -jnp.exp(alog_c_ref[...]), 0.0)
    g_row = a_c * _softplus(bar_ref[...] + dt_c_ref[...])
    if masked:
        g_col = jnp.where(row_ok, g_col, 0.0)
        g_row = jnp.where(col_ok, g_row, 0.0)

    rpos = ri % CHUNK
    lpos = li16 % CHUNK
    gc = g_col
    rev = g_col
    gcr = g_row
    step = 1
    while step < CHUNK:
        gc = gc + jnp.where(rpos >= step, pltpu.roll(gc, step, 0), 0.0)
        rev = rev + jnp.where(rpos < CHUNK - step, pltpu.roll(rev, rows - step, 0), 0.0)
        gcr = gcr + jnp.where(lpos >= step, pltpu.roll(gcr, step, 1), 0.0)
        step *= 2
    rev = rev - g_col
    egl_ref[...] = jnp.exp(gc + rev)

    bi = lax.broadcasted_iota(jnp.int32, (ROW_BLOCK, ROW_BLOCK), 0)
    bj = lax.broadcasted_iota(jnp.int32, (ROW_BLOCK, ROW_BLOCK), 1)
    same = (bi // CHUNK) == (bj // CHUNK)
    incl = jnp.logical_and(same, bi >= bj)
    strict = jnp.logical_and(same, bi > bj)
    eye = (bi == bj).astype(F32)

    units = [(sb, hh) for sb in range(rows // ROW_BLOCK) for hh in range(B_HEADS)]
    lows, rhss = [], []
    for sb, hh in units:
        rs = slice(sb * ROW_BLOCK, (sb + 1) * ROW_BLOCK)
        sl = slice(hh * B_DK, (hh + 1) * B_DK)
        gc_c = gc[rs, B_HEADS + hh:B_HEADS + hh + 1]
        gc_r = gcr[B_HEADS + hh:B_HEADS + hh + 1, rs]
        dec = jnp.where(incl, jnp.exp(jnp.where(incl, gc_c - gc_r, 0.0)), 0.0)
        q = act[rs, sl]
        q = q * lax.rsqrt(jnp.sum(q * q, axis=-1, keepdims=True) + EPS) * (B_DK ** -0.5)
        k = act[rs, B_QK + hh * B_DK:B_QK + (hh + 1) * B_DK]
        k = k * lax.rsqrt(jnp.sum(k * k, axis=-1, keepdims=True) + EPS)
        v = act[rs, 2 * B_QK + hh * B_DV:2 * B_QK + (hh + 1) * B_DV]
        beta = jax.nn.sigmoid(bac_ref[rs, hh:hh + 1])
        kb = k * beta
        kbf = k.astype(BF16)
        lows.append(jnp.where(strict, dec * _dot_nt(kb, kbf), 0.0))
        attn_ref[rs, sl] = (dec * _dot_nt(q, kbf)).astype(attn_ref.dtype)
        rhss.append(jnp.concatenate([v * beta, kb * jnp.exp(gc_c)], axis=1).astype(BF16))
        qg_ref[rs, sl] = (q * jnp.exp(gc_c)).astype(qg_ref.dtype)
        kd_ref[rs, sl] = (k * jnp.exp(rev[rs, B_HEADS + hh:B_HEADS + hh + 1])).astype(kd_ref.dtype)

    tinvs = [eye - low for low in lows]
    pws = lows
    sq = 2
    while sq < CHUNK:
        pws = [_dot(pw, pw) for pw in pws]
        tinvs = [tinv + _dot(tinv, pw) for tinv, pw in zip(tinvs, pws)]
        sq *= 2
    for (sb, hh), tinv, rhs in zip(units, tinvs, rhss):
        rs = slice(sb * ROW_BLOCK, (sb + 1) * ROW_BLOCK)
        sol = _dot(tinv, rhs)
        u_ref[rs, hh * B_DV:(hh + 1) * B_DV] = sol[:, :B_DV]
        w_ref[rs, hh * B_DK:(hh + 1) * B_DK] = sol[:, B_DV:].astype(w_ref.dtype)


def _b_prep(p, bac, bar, alog_r, dt_r, alog_c, dt_c, seq_len, t_valid, cstate=None, cw=None):
    n = p.shape[0]
    rows = min(seq_len, 4 * ROW_BLOCK)
    bps = seq_len // rows
    row = lambda i: (i, 0)
    fixed = lambda i: (0, 0)
    per_row = rows // SUBLANES
    conv_done = cstate is None
    wide = lambda dt: jax.ShapeDtypeStruct((n, B_V), dt)
    in_specs = [pl.BlockSpec((rows, B_CONV_CH), row)]
    args = [p]
    scratch = []
    if not conv_done:
        in_specs += [pl.BlockSpec((SUBLANES, B_CONV_CH), lambda i: (jnp.maximum(i * per_row - 1, 0), 0)),
                     pl.BlockSpec((None, SUBLANES, B_CONV_CH), lambda i: (i // bps, 0, 0)),
                     pl.BlockSpec((B_CONV, B_CONV_CH), fixed)]
        args += [p, cstate, cw]
        scratch = [pltpu.VMEM((SUBLANES + rows, B_CONV_CH), F32)]
    in_specs += [pl.BlockSpec((rows, LANES), row),
                 pl.BlockSpec((2 * SUBLANES, rows), lambda i: (0, i)),
                 pl.BlockSpec((1, LANES), fixed),
                 pl.BlockSpec((1, LANES), fixed),
                 pl.BlockSpec((2 * SUBLANES, 1), fixed),
                 pl.BlockSpec((2 * SUBLANES, 1), fixed)]
    args += [bac, bar, alog_r, dt_r, alog_c, dt_c]
    return pl.pallas_call(
        functools.partial(_b_prep_kernel, rows=rows, blocks_per_seq=bps, t_valid=t_valid, conv_done=conv_done),
        grid=(n // rows,),
        in_specs=in_specs,
        out_specs=[pl.BlockSpec((rows, B_V), row)] * 5 + [pl.BlockSpec((rows, LANES), row)],
        out_shape=[wide(BF16), wide(BF16), wide(F32), wide(BF16), wide(BF16), jax.ShapeDtypeStruct((n, LANES), F32)],
        scratch_shapes=scratch,
        compiler_params=_cparams(("parallel",)),
        name="b_prep",
    )(*args)


def _b_scan_kernel(qg_ref, kd_ref, u_ref, w_ref, attn_ref, egl_ref, z_ref, s0_ref, gout_ref, o_ref, s_ref, *, nb):
    c = pl.program_id(1)

    @pl.when(c == 0)
    def _():
        s_ref[...] = s0_ref[...]

    half = c % (ROW_BLOCK // CHUNK)
    rgrp = lax.broadcasted_iota(jnp.int32, (ROW_BLOCK, B_DV), 0) // CHUNK
    here = rgrp == half
    units = [(b, hh) for b in range(nb) for hh in range(B_HEADS)]
    head = lambda hh: slice(hh * B_DV, (hh + 1) * B_DV)
    states = [s_ref[b, hh] for b, hh in units]
    proj = [_dot(jnp.concatenate([w_ref[b, :, head(hh)], qg_ref[b, :, head(hh)]], axis=0), s)
            for (b, hh), s in zip(units, states)]
    v_new = [u_ref[b, :, head(hh)] - pr[:CHUNK] for (b, hh), pr in zip(units, proj)]
    outs = []
    for (b, hh), pr, vn in zip(units, proj, v_new):
        v_full = jnp.where(here, jnp.concatenate([vn] * (ROW_BLOCK // CHUNK), axis=0), 0.0)
        outs.append(pr[CHUNK:] + _dot(attn_ref[b, :, head(hh)], v_full))
    for (b, hh), s, vn in zip(units, states, v_new):
        decay = egl_ref[b, 0:1, B_HEADS + hh:B_HEADS + hh + 1]
        s_ref[b, hh] = s * decay + _dot_tn(kd_ref[b, :, head(hh)], vn)
    for (b, hh), o in zip(units, outs):
        gate = _silu(z_ref[b, :, head(hh)].astype(F32))
        o_ref[b, :, head(hh)] = (_rms(o, gout_ref[...]) * gate).astype(o_ref.dtype)


def _b_scan(qg, kd, u, w, attn, egl, z, s0, gout, batch, seq_len):
    nb = 4
    nchunk = seq_len // CHUNK
    v3 = lambda a: a.reshape(batch, seq_len, a.shape[-1])
    rows = lambda bi, c: (bi, c, 0)
    state = lambda bi, c: (bi, 0, 0, 0)
    wide = pl.BlockSpec((nb, CHUNK, B_V), rows)
    o, s_new = pl.pallas_call(
        functools.partial(_b_scan_kernel, nb=nb),
        grid=(batch // nb, nchunk),
        in_specs=[wide] * 5 + [pl.BlockSpec((nb, CHUNK, LANES), rows), wide,
                               pl.BlockSpec((nb, B_HEADS, B_DK, B_DV), state),
                               pl.BlockSpec((1, B_DV), lambda bi, c: (0, 0))],
        out_specs=[wide, pl.BlockSpec((nb, B_HEADS, B_DK, B_DV), state)],
        out_shape=[jax.ShapeDtypeStruct((batch, seq_len, B_V), BF16),
                   jax.ShapeDtypeStruct((batch, B_HEADS, B_DK, B_DV), F32)],
        compiler_params=_cparams(("parallel", "arbitrary")),
        name="b_scan",
    )(v3(qg), v3(kd), v3(u), v3(w), v3(attn), v3(egl), v3(z), s0, gout)
    return o.reshape(batch * seq_len, B_V), s_new


def _log_gamma(hh):
    return math.log1p(-(2.0 ** (-5.0 - hh)))


def _c_scan_kernel(q_ref, k_ref, v_ref, z_ref, r0_ref, gout_ref, o_ref, rout_ref, r_ref, *, nb, t_valid):
    c = pl.program_id(1)
    rows = ROW_BLOCK
    state_blocks = [(b, hh, slice(hh * C_DK, (hh + 1) * C_DK), slice(hh * C_DV, (hh + 1) * C_DV))
                    for b in range(nb) for hh in range(C_HEADS)]

    @pl.when(c == 0)
    def _():
        r_ref[...] = jnp.zeros_like(r_ref)
        for b, hh, rsl, csl in state_blocks:
            r_ref[b, rsl, csl] = r0_ref[b, hh]

    left = jnp.clip(t_valid - c * rows, 0, rows)
    ri = lax.broadcasted_iota(jnp.int32, (rows, rows), 0)
    ci = lax.broadcasted_iota(jnp.int32, (rows, rows), 1)
    cnt_i = jnp.minimum(ri + 1, left).astype(F32)
    cnt_j = jnp.minimum(ci + 1, left).astype(F32)
    incl = ri >= ci
    steps = jnp.where(incl, cnt_i - cnt_j, 0.0)
    cnt_col = cnt_i[:, 0:1]
    left_f = left.astype(F32)
    qk_lane = lax.broadcasted_iota(jnp.int32, (1, C_QK), 1) // C_DK
    lg_lane = jnp.zeros((1, C_QK), F32)
    for hh in range(C_HEADS):
        lg_lane = jnp.where(qk_lane == hh, _log_gamma(hh), lg_lane)
    qk_sub = lax.broadcasted_iota(jnp.int32, (C_QK, 1), 0) // C_DK
    lg_sub = jnp.zeros((C_QK, 1), F32)
    for hh in range(C_HEADS):
        lg_sub = jnp.where(qk_sub == hh, _log_gamma(hh), lg_sub)
    q_scale = jnp.exp(cnt_col * lg_lane)
    k_scale = jnp.exp((left_f - cnt_col) * lg_lane)
    r_scale = jnp.exp(left_f * lg_sub)
    row_ok = (lax.broadcasted_iota(jnp.int32, (rows, 1), 0) + c * rows) < t_valid
    diag = (lax.broadcasted_iota(jnp.int32, (C_QK, C_V), 0) // C_DK) == (
        lax.broadcasted_iota(jnp.int32, (C_QK, C_V), 1) // C_DV)

    head = lambda hh: slice(hh * C_DV, (hh + 1) * C_DV)
    decays = [jnp.where(incl, jnp.exp(steps * _log_gamma(hh)), 0.0) for hh in range(C_HEADS)]
    qs = [q_ref[b] for b in range(nb)]
    ks = [jnp.where(row_ok, k_ref[b], 0.0) for b in range(nb)]
    vs = [v_ref[b].astype(BF16) for b in range(nb)]
    rs = [r_ref[b] for b in range(nb)]
    inters = [_dot(q * q_scale, r) for q, r in zip(qs, rs)]
    units = [(b, hh) for b in range(nb) for hh in range(C_HEADS)]
    atts = [decays[hh] * _dot_nt(qs[b], jnp.where(qk_lane == hh, ks[b], 0.0)) for b, hh in units]
    outs = [inters[b][:, head(hh)] + _dot(att, vs[b][:, head(hh)]) for (b, hh), att in zip(units, atts)]
    for b in range(nb):
        r_ref[b] = rs[b] * r_scale + jnp.where(diag, _dot_tn(ks[b] * k_scale, vs[b]), 0.0)
    for (b, hh), o in zip(units, outs):
        gate = _silu(z_ref[b, :, head(hh)].astype(F32))
        o_ref[b, :, head(hh)] = (_rms(o, gout_ref[...]) * gate).astype(o_ref.dtype)

    @pl.when(c == pl.num_programs(1) - 1)
    def _():
        for b, hh, rsl, csl in state_blocks:
            rout_ref[b, hh] = r_ref[b, rsl, csl]


def _c_scan(q, k, v, z, r0, gout, batch, seq_len, t_valid):
    nb = 4
    nblk = seq_len // ROW_BLOCK
    v3 = lambda a: a.reshape(batch, seq_len, a.shape[-1])
    rows = lambda bi, c: (bi, c, 0)
    state = pl.BlockSpec((nb, C_HEADS, C_DK, C_DV), lambda bi, c: (bi, 0, 0, 0))
    o, r_new = pl.pallas_call(
        functools.partial(_c_scan_kernel, nb=nb, t_valid=t_valid),
        grid=(batch // nb, nblk),
        in_specs=[pl.BlockSpec((nb, ROW_BLOCK, C_QK), rows), pl.BlockSpec((nb, ROW_BLOCK, C_QK), rows),
                  pl.BlockSpec((nb, ROW_BLOCK, C_V), rows), pl.BlockSpec((nb, ROW_BLOCK, C_V), rows),
                  state, pl.BlockSpec((1, C_DV), lambda bi, c: (0, 0))],
        out_specs=[pl.BlockSpec((nb, ROW_BLOCK, C_V), rows), state],
        out_shape=[jax.ShapeDtypeStruct((batch, seq_len, C_V), BF16),
                   jax.ShapeDtypeStruct((batch, C_HEADS, C_DK, C_DV), F32)],
        scratch_shapes=[pltpu.VMEM((nb, C_QK, C_V), F32)],
        compiler_params=_cparams(("parallel", "arbitrary")),
        name="c_scan",
    )(v3(q), v3(k), v3(v), v3(z), r0, gout)
    return o.reshape(batch * seq_len, C_V), r_new


def _b_short_kernel(p_ref, st_ref, bac_ref, bar_ref, cw_ref, alog_r_ref, dt_r_ref, alog_c_ref, dt_c_ref,
                    z_ref, s0_ref, gout_ref, o_ref, s_ref, e_new, e_old, *, t):
    rows = ROW_BLOCK
    nseq = rows // t
    e_new[0:SUBLANES, :] = jnp.zeros((SUBLANES, B_CONV_CH), F32)
    e_new[SUBLANES:SUBLANES + rows, :] = p_ref[...]
    e_old[0:rows, :] = st_ref[...]
    e_old[rows:rows + SUBLANES, :] = jnp.zeros((SUBLANES, B_CONV_CH), F32)
    pos = lax.broadcasted_iota(jnp.int32, (rows, 1), 0) % t
    xc = e_new[SUBLANES:SUBLANES + rows, :] * cw_ref[B_CONV - 1:B_CONV, :]
    for kk in range(1, B_CONV):
        window = slice(SUBLANES - kk, SUBLANES - kk + rows)
        src = jnp.where(pos >= kk, e_new[window, :], e_old[window, :])
        xc = xc + src * cw_ref[B_CONV - 1 - kk:B_CONV - kk, :]
    act = _silu(xc)

    ri = lax.broadcasted_iota(jnp.int32, (rows, LANES), 0)
    li16 = lax.broadcasted_iota(jnp.int32, (2 * SUBLANES, rows), 1)
    li1 = lax.broadcasted_iota(jnp.int32, (1, LANES), 1)
    head_lane = jnp.logical_and(li1 >= B_HEADS, li1 < 2 * B_HEADS)
    a_r = jnp.where(head_lane, -jnp.exp(alog_r_ref[...]), 0.0)
    g_col = a_r * _softplus(bac_ref[...] + dt_r_ref[...])
    si = lax.broadcasted_iota(jnp.int32, (2 * SUBLANES, 1), 0)
    head_sub = jnp.logical_and(si >= B_HEADS, si < 2 * B_HEADS)
    a_c = jnp.where(head_sub, -jnp.exp(alog_c_ref[...]), 0.0)
    g_row = a_c * _softplus(bar_ref[...] + dt_c_ref[...])
    rpos = ri % t
    lpos = li16 % t
    gc, rev, gcr = g_col, g_col, g_row
    step = 1
    while step < t:
        gc = gc + jnp.where(rpos >= step, pltpu.roll(gc, step, 0), 0.0)
        rev = rev + jnp.where(rpos < t - step, pltpu.roll(rev, rows - step, 0), 0.0)
        gcr = gcr + jnp.where(lpos >= step, pltpu.roll(gcr, step, 1), 0.0)
        step *= 2
    rev = rev - g_col
    egl = jnp.exp(gc + rev)

    bi = lax.broadcasted_iota(jnp.int32, (rows, rows), 0)
    bj = lax.broadcasted_iota(jnp.int32, (rows, rows), 1)
    same = (bi // t) == (bj // t)
    incl = jnp.logical_and(same, bi >= bj)
    strict = jnp.logical_and(same, bi > bj)
    eye = (bi == bj).astype(F32)

    lows, rhss, attns, qgs, kds = [], [], [], [], []
    for hh in range(B_HEADS):
        sl = slice(hh * B_DK, (hh + 1) * B_DK)
        gc_c = gc[:, B_HEADS + hh:B_HEADS + hh + 1]
        gc_r = gcr[B_HEADS + hh:B_HEADS + hh + 1, :]
        dec = jnp.where(incl, jnp.exp(jnp.where(incl, gc_c - gc_r, 0.0)), 0.0)
        q = act[:, sl]
        q = q * lax.rsqrt(jnp.sum(q * q, axis=-1, keepdims=True) + EPS) * (B_DK ** -0.5)
        k = act[:, B_QK + hh * B_DK:B_QK + (hh + 1) * B_DK]
        k = k * lax.rsqrt(jnp.sum(k * k, axis=-1, keepdims=True) + EPS)
        v = act[:, 2 * B_QK + hh * B_DV:2 * B_QK + (hh + 1) * B_DV]
        beta = jax.nn.sigmoid(bac_ref[:, hh:hh + 1])
        kb = k * beta
        kbf = k.astype(BF16)
        lows.append(jnp.where(strict, dec * _dot_nt(kb, kbf), 0.0))
        attns.append(dec * _dot_nt(q, kbf))
        rhss.append(jnp.concatenate([v * beta, kb * jnp.exp(gc_c)], axis=1))
        qgs.append(q * jnp.exp(gc_c))
        kds.append(k * jnp.exp(rev[:, B_HEADS + hh:B_HEADS + hh + 1]))
    tinvs = [eye - low for low in lows]
    pws = lows
    sq = 2
    while sq < t:
        pws = [_dot(pw, pw) for pw in pws]
        tinvs = [tinv + _dot(tinv, pw) for tinv, pw in zip(tinvs, pws)]
        sq *= 2
    sols = [_dot(tinv, rhs) for tinv, rhs in zip(tinvs, rhss)]

    rgrp = lax.broadcasted_iota(jnp.int32, (rows, B_DV), 0) // t
    units = [(j, hh) for hh in range(B_HEADS) for j in range(nseq)]
    rws = lambda j: slice(j * t, (j + 1) * t)
    states = [s0_ref[j, hh] for j, hh in units]
    proj = [_dot(jnp.concatenate([sols[hh][rws(j), B_DV:], qgs[hh][rws(j)]], axis=0), s)
            for (j, hh), s in zip(units, states)]
    v_new = [sols[hh][rws(j), :B_DV] - pr[:t] for (j, hh), pr in zip(units, proj)]
    outs = []
    for (j, hh), pr, vn in zip(units, proj, v_new):
        v_full = jnp.where(rgrp == j, jnp.concatenate([vn] * nseq, axis=0), 0.0)
        outs.append(pr[t:] + _dot(attns[hh][rws(j)], v_full))
    for (j, hh), s, vn in zip(units, states, v_new):
        decay = egl[j * t:j * t + 1, B_HEADS + hh:B_HEADS + hh + 1]
        s_ref[j, hh] = s * decay + _dot_tn(kds[hh][rws(j)], vn)
    for hh in range(B_HEADS):
        sl = slice(hh * B_DV, (hh + 1) * B_DV)
        o = jnp.concatenate(outs[hh * nseq:(hh + 1) * nseq], axis=0)
        gate = _silu(z_ref[:, sl].astype(F32))
        o_ref[:, sl] = (_rms(o, gout_ref[...]) * gate).astype(o_ref.dtype)


def _b_short(p, st, bac, bar, cw, alog_r, dt_r, alog_c, dt_c, z, s0, gout, t):
    assert t == SUBLANES
    n = p.shape[0]
    nseq = ROW_BLOCK // t
    row = lambda i: (i, 0)
    fixed = lambda i: (0, 0)
    state = pl.BlockSpec((nseq, B_HEADS, B_DK, B_DV), lambda i: (i, 0, 0, 0))
    return pl.pallas_call(
        functools.partial(_b_short_kernel, t=t),
        grid=(n // ROW_BLOCK,),
        in_specs=[pl.BlockSpec((ROW_BLOCK, B_CONV_CH), row), pl.BlockSpec((ROW_BLOCK, B_CONV_CH), row),
                  pl.BlockSpec((ROW_BLOCK, LANES), row), pl.BlockSpec((2 * SUBLANES, ROW_BLOCK), lambda i: (0, i)),
                  pl.BlockSpec((B_CONV, B_CONV_CH), fixed),
                  pl.BlockSpec((1, LANES), fixed), pl.BlockSpec((1, LANES), fixed),
                  pl.BlockSpec((2 * SUBLANES, 1), fixed), pl.BlockSpec((2 * SUBLANES, 1), fixed),
                  pl.BlockSpec((ROW_BLOCK, B_V), row), state, pl.BlockSpec((1, B_DV), fixed)],
        out_specs=[pl.BlockSpec((ROW_BLOCK, B_V), row), state],
        out_shape=[jax.ShapeDtypeStruct((n, B_V), BF16), jax.ShapeDtypeStruct(s0.shape, F32)],
        scratch_shapes=[pltpu.VMEM((SUBLANES + ROW_BLOCK, B_CONV_CH), F32)] * 2,
        compiler_params=_cparams(("parallel",)),
        name="b_short",
    )(p, st, bac, bar, cw, alog_r, dt_r, alog_c, dt_c, z, s0, gout)


def _c_short_kernel(q_ref, k_ref, v_ref, z_ref, r0_ref, gout_ref, o_ref, rout_ref, *, t):
    rows = ROW_BLOCK
    nseq = rows // t
    ri = lax.broadcasted_iota(jnp.int32, (rows, rows), 0)
    ci = lax.broadcasted_iota(jnp.int32, (rows, rows), 1)
    causal = jnp.logical_and(ri // t == ci // t, ri >= ci)
    steps = jnp.where(causal, (ri - ci).astype(F32), 0.0)
    cnt = (lax.broadcasted_iota(jnp.int32, (rows, 1), 0) % t + 1).astype(F32)
    qk_lane = lax.broadcasted_iota(jnp.int32, (1, C_QK), 1) // C_DK
    lg_lane = jnp.zeros((1, C_QK), F32)
    qk_sub = lax.broadcasted_iota(jnp.int32, (C_QK, 1), 0) // C_DK
    lg_sub = jnp.zeros((C_QK, 1), F32)
    for hh in range(C_HEADS):
        lg_lane = jnp.where(qk_lane == hh, _log_gamma(hh), lg_lane)
        lg_sub = jnp.where(qk_sub == hh, _log_gamma(hh), lg_sub)
    q = q_ref[...]
    k = k_ref[...]
    v = v_ref[...].astype(F32)
    qd = q * jnp.exp(cnt * lg_lane)
    kd = k * jnp.exp((t - cnt) * lg_lane)
    r_scale = jnp.exp(t * lg_sub)
    head = lambda hh: slice(hh * C_DV, (hh + 1) * C_DV)
    rws = lambda j: slice(j * t, (j + 1) * t)

    intra = []
    for hh in range(C_HEADS):
        att = jnp.exp(steps * _log_gamma(hh)) * _dot_nt(q, jnp.where(qk_lane == hh, k, 0.0))
        intra.append(_dot(jnp.where(causal, att, 0.0), v[:, head(hh)]))
    stacks = [r0_ref[j].reshape(C_QK, C_DV) for j in range(nseq)]
    units = [(j, hh) for hh in range(C_HEADS) for j in range(nseq)]
    inter = [_dot(jnp.where(qk_lane == hh, qd[rws(j)], 0.0), stacks[j]) for j, hh in units]
    upd = [_dot_tn(jnp.where(qk_lane == hh, kd[rws(j)], 0.0), v[rws(j), head(hh)]) for j, hh in units]
    for j in range(nseq):
        new = stacks[j] * r_scale
        for hh in range(C_HEADS):
            new = new + upd[hh * nseq + j]
        rout_ref[j] = new.reshape(C_HEADS, C_DK, C_DV)
    for hh in range(C_HEADS):
        o = intra[hh] + jnp.concatenate(inter[hh * nseq:(hh + 1) * nseq], axis=0)
        gate = _silu(z_ref[:, head(hh)].astype(F32))
        o_ref[:, head(hh)] = (_rms(o, gout_ref[...]) * gate).astype(o_ref.dtype)


def _c_short(q, k, v, z, r0, gout, t):
    n = q.shape[0]
    nseq = ROW_BLOCK // t
    row = lambda i: (i, 0)
    state = pl.BlockSpec((nseq, C_HEADS, C_DK, C_DV), lambda i: (i, 0, 0, 0))
    return pl.pallas_call(
        functools.partial(_c_short_kernel, t=t),
        grid=(n // ROW_BLOCK,),
        in_specs=[pl.BlockSpec((ROW_BLOCK, C_QK), row), pl.BlockSpec((ROW_BLOCK, C_QK), row),
                  pl.BlockSpec((ROW_BLOCK, C_V), row), pl.BlockSpec((ROW_BLOCK, C_V), row),
                  state, pl.BlockSpec((1, C_DV), lambda i: (0, 0))],
        out_specs=[pl.BlockSpec((ROW_BLOCK, C_V), row), state],
        out_shape=[jax.ShapeDtypeStruct((n, C_V), BF16), jax.ShapeDtypeStruct(r0.shape, F32)],
        compiler_params=_cparams(("parallel",)),
        name="c_short",
    )(q, k, v, z, r0, gout)


def _merge_kernel(x_ref, g1_ref, g2_ref, wg_ref, o0_ref, o1_ref, o2_ref, l0_ref, l1_ref, l2_ref, ob_ref, oc_ref,
                  wa_ref, wb_ref, wc_ref, wo_ref, y_ref, h2_ref, *scr, residue_major):
    x = x_ref[...]
    tm = x.shape[0]
    h = _rms(x, g1_ref[...]).astype(BF16)
    lses = [r[...].reshape(tm, LANES) for r in (l0_ref, l1_ref, l2_ref)]
    outs = [r[...].reshape(tm, A_WIDTH) for r in (o0_ref, o1_ref, o2_ref)]
    heads = []
    for hh in range(A_HEADS):
        sl = slice(hh * A_HD, (hh + 1) * A_HD)
        ls = [l[:, 32 * hh:32 * hh + 1] for l in lses]
        m = jnp.maximum(jnp.maximum(ls[0], ls[1]), ls[2])
        es = [jnp.exp(l - m) for l in ls]
        tot = es[0] + es[1] + es[2]
        acc = (es[0] / tot) * outs[0][:, sl].astype(F32)
        acc = acc + (es[1] / tot) * outs[1][:, sl].astype(F32)
        acc = acc + (es[2] / tot) * outs[2][:, sl].astype(F32)
        heads.append(acc)
    o_a = jnp.concatenate(heads, axis=1)
    if residue_major:
        o_a = _swap_row_grid(scr[0], o_a)
    o_a = o_a.astype(BF16)
    merged = None
    for gi, (o_g, w_ref) in enumerate(((o_a, wa_ref), (ob_ref[...], wb_ref), (oc_ref[...], wc_ref))):
        gate = jax.nn.sigmoid(jnp.dot(h, wg_ref[:, gi * D_MODEL:(gi + 1) * D_MODEL], preferred_element_type=F32))
        term = gate * jnp.dot(o_g, w_ref[...], preferred_element_type=F32)
        merged = term if merged is None else merged + term
    y = x + jnp.dot(merged.astype(BF16), wo_ref[...], preferred_element_type=F32)
    y_ref[...] = y
    h2_ref[...] = _rms(y, g2_ref[...]).astype(h2_ref.dtype)


def _merge(x, g1, g2, wg, o_groups, lses, o_b, o_c, wa, wb, wc, wo, seq_len, residue_major):
    n = x.shape[0]
    tm = RES * RES
    row = lambda i: (i, 0)
    fixed = lambda i: (0, 0)
    half = pl.BlockSpec((tm, A_WIDTH), row)
    wbr = pl.BlockSpec((A_WIDTH, D_MODEL), fixed)
    if residue_major:
        tiles = seq_len // tm
        grp = lambda i: (i // tiles, 0, i % tiles, 0)
        o_spec = pl.BlockSpec((None, RES, tm // RES, A_WIDTH), grp)
        lse = pl.BlockSpec((None, RES, tm // RES, LANES), grp)
        scratch = [pltpu.VMEM((A_WIDTH // LANES, tm, LANES), F32)]
    else:
        o_spec = half
        lse = pl.BlockSpec((tm, LANES), row)
        scratch = []
    return pl.pallas_call(
        functools.partial(_merge_kernel, residue_major=residue_major),
        grid=(n // tm,),
        in_specs=[pl.BlockSpec((tm, D_MODEL), row), pl.BlockSpec((1, D_MODEL), fixed),
                  pl.BlockSpec((1, D_MODEL), fixed),
                  pl.BlockSpec((D_MODEL, 3 * D_MODEL), lambda i: (0, W_COLS["gates"] // (3 * D_MODEL))),
                  o_spec, o_spec, o_spec, lse, lse, lse, half, half, wbr, wbr, wbr,
                  pl.BlockSpec((D_MODEL, D_MODEL), fixed)],
        out_specs=[pl.BlockSpec((tm, D_MODEL), row)] * 2,
        out_shape=[jax.ShapeDtypeStruct((n, D_MODEL), F32), jax.ShapeDtypeStruct((n, D_MODEL), BF16)],
        scratch_shapes=scratch,
        compiler_params=_cparams(("parallel",)),
        name="merge",
    )(x, g1, g2, wg, *o_groups, *lses, o_b, o_c, wa, wb, wc, wo)


def _ffn_kernel(x_ref, h_ref, wg_ref, wu_ref, wo_ref, y_ref, acc_scr):
    j = pl.program_id(1)

    @pl.when(j == 0)
    def _():
        acc_scr[...] = jnp.zeros_like(acc_scr)

    h = h_ref[...]
    gate = jnp.dot(h, wg_ref[...], preferred_element_type=F32)
    up = jnp.dot(h, wu_ref[...], preferred_element_type=F32)
    acc_scr[...] += jnp.dot((_silu(gate) * up).astype(BF16), wo_ref[...], preferred_element_type=F32)

    @pl.when(j == pl.num_programs(1) - 1)
    def _():
        y_ref[...] = x_ref[...] + acc_scr[...]


def _ffn(x, h, w_in, w_out):
    n = x.shape[0]
    tm = min(n, 1024)
    tf = 256
    nf = D_FF // tf
    row = lambda i, j: (i, 0)
    return pl.pallas_call(
        _ffn_kernel,
        grid=(n // tm, nf),
        in_specs=[pl.BlockSpec((tm, D_MODEL), row), pl.BlockSpec((tm, D_MODEL), row),
                  pl.BlockSpec((D_MODEL, tf), lambda i, j: (0, j)),
                  pl.BlockSpec((D_MODEL, tf), lambda i, j: (0, nf + j)),
                  pl.BlockSpec((tf, D_MODEL), lambda i, j: (j, 0))],
        out_specs=pl.BlockSpec((tm, D_MODEL), row),
        out_shape=jax.ShapeDtypeStruct((n, D_MODEL), F32),
        scratch_shapes=[pltpu.VMEM((tm, D_MODEL), F32)],
        compiler_params=_cparams(("parallel", "arbitrary")),
        name="ffn",
    )(x, h, w_in, w_in, w_out)


def _rope_tables(pos, hd, reps):
    inv = ROPE_THETA ** (-jnp.arange(0, hd, 2, dtype=F32) / hd)
    ang = pos.astype(F32)[:, None] * inv[None, :]
    cos = jnp.cos(ang)
    sin = jnp.sin(ang)
    cos2 = jnp.concatenate([cos, cos], axis=1)
    sin2 = jnp.concatenate([-sin, sin], axis=1)
    return jnp.tile(cos2, (1, reps)), jnp.tile(sin2, (1, reps))


def _pad_rows(a, batch, t, t_pad):
    if t == t_pad:
        return a
    a = a.reshape(batch, t, a.shape[-1])
    a = jnp.pad(a, ((0, 0), (0, t_pad - t), (0, 0)))
    return a.reshape(batch * t_pad, a.shape[-1])


def _unpad_rows(a, batch, t, t_pad):
    if t == t_pad:
        return a
    return a.reshape(batch, t_pad, a.shape[-1])[:, :t].reshape(batch * t, a.shape[-1])


def _layer(x, pos, batch, t, lw, caches, layer, conv_state, s0, r0):
    n = batch * t
    prompt = caches is None
    reps = max(1, 256 // t)
    cos_a, sin_a = _rope_tables(pos, A_HD, 1)
    cos_c, sin_c = _rope_tables(pos, C_DK, LANES // C_DK)
    if reps > 1:
        cos_a, sin_a, cos_c, sin_c = (jnp.tile(a, (reps, 1)) for a in (cos_a, sin_a, cos_c, sin_c))

    if prompt:
        to_rm = lambda a: a.reshape(t // RES, RES, A_HD).transpose(1, 0, 2)
        q, k, v, kv_tail = _proj_a(x, lw["g1"], lw["w_all"], lw["qn"], lw["kn"], to_rm(cos_a), to_rm(sin_a),
                                   batch, t, True)
    else:
        q, k, v = _proj_a(x, lw["g1"], lw["w_all"], lw["qn"], lw["kn"], cos_a, sin_a, batch, t, False)
    outs, lses = [], []
    for gi in range(N_GROUPS):
        if prompt:
            o, lse = _attn_prompt(q, k, v, gi, batch, t)
        else:
            o, lse = _attn_sample(q, k, v, caches[gi], layer, gi, batch, t)
        outs.append(o)
        lses.append(lse)
    new_kv = []
    for gi, (win, _) in enumerate(A_GROUPS):
        if prompt:
            keep = min(win, t)
            first = kv_tail.shape[2] - keep // RES
            rows = kv_tail[:, :, first:, 2 * gi * A_WIDTH:2 * (gi + 1) * A_WIDTH]
            new_kv.append(rows.transpose(0, 2, 1, 3).reshape(batch, keep, 2, A_HEADS, A_HD))
        else:
            cols = slice(gi * A_WIDTH, (gi + 1) * A_WIDTH)
            tail = lambda a: a[:, cols].reshape(batch, t, A_HEADS, A_HD)
            new_kv.append(jnp.stack([tail(k), tail(v)], axis=2))

    t_pad = -(-t // ROW_BLOCK) * ROW_BLOCK
    cst = jnp.pad(conv_state, ((0, 0), (SUBLANES - (B_CONV - 1), 0), (0, 0)))
    decay = (lw["alog_r"], lw["dt_r"], lw["alog_c"], lw["dt_c"])
    short = t == SUBLANES and n % ROW_BLOCK == 0
    if t_pad == t:
        act, z_b, bac, bar, cq, ck, cv, cz, p_last = _proj_bc(x, lw["g1"], lw["w_all"], lw["w_bat"], cos_c, sin_c,
                                                              cst, lw["conv_w"], t)
        conv_new = p_last[:, -(B_CONV - 1):]
        qg, kd, u, w, attn, egl = _b_prep(act, bac, bar, *decay, t, t)
    else:
        p, z_b, bac, bar, cq, ck, cv, cz = _proj_bc(x, lw["g1"], lw["w_all"], lw["w_bat"], cos_c, sin_c)
        conv_new = jnp.concatenate([conv_state, p.reshape(batch, t, B_CONV_CH)], axis=1)[:, -(B_CONV - 1):]
        if not short:
            qg, kd, u, w, attn, egl = _b_prep(
                _pad_rows(p, batch, t, t_pad), _pad_rows(bac, batch, t, t_pad),
                _pad_rows(bar.T, batch, t, t_pad).T, *decay, t_pad, t, cst, lw["conv_w"])
    if short:
        o_b, s_new = _b_short(p, cst.reshape(n, B_CONV_CH), bac, bar, lw["conv_w"], *decay, z_b, s0, lw["gb"], t)
    else:
        o_b, s_new = _b_scan(qg, kd, u, w, attn, egl, _pad_rows(z_b, batch, t, t_pad), s0, lw["gb"], batch, t_pad)
        o_b = _unpad_rows(o_b, batch, t, t_pad)

    if short:
        o_c, r_new = _c_short(cq, ck, cv, cz, r0, lw["gc"], t)
    else:
        o_c, r_new = _c_scan(*(_pad_rows(a, batch, t, t_pad) for a in (cq, ck, cv, cz)), r0, lw["gc"], batch, t_pad,
                             t)
        o_c = _unpad_rows(o_c, batch, t, t_pad)

    x, h2 = _merge(x, lw["g1"], lw["g2"], lw["w_all"], outs, lses, o_b, o_c, lw["w_oa"], lw["w_ob"], lw["w_oc"],
                   lw["w_o"], t, prompt)
    x = _ffn(x, h2, lw["w_fi"], lw["w_fo"])
    return x, new_kv, conv_new, s_new, r_new


def _layer_weights(l, norm1_g, w_in, a_q_norm_g, a_k_norm_g, b_conv_w, b_a_log, b_dt_bias, b_out_norm_g,
                   c_out_norm_g, w_out_a, w_out_b, w_out_c, w_out, norm2_g, w_ffn_in, w_ffn_out):
    o = IN_OFFS
    wl = w_in[l]
    w_ba = wl[:, o[3]:o[5]]
    pad_r = lambda a: jnp.pad(a.reshape(1, B_HEADS), ((0, 0), (B_HEADS, LANES - 2 * B_HEADS)))
    pad_c = lambda a: jnp.pad(a.reshape(B_HEADS, 1), ((B_HEADS, 2 * SUBLANES - 2 * B_HEADS), (0, 0)))
    return dict(
        g1=norm1_g[l].reshape(1, D_MODEL), g2=norm2_g[l].reshape(1, D_MODEL),
        w_all=jnp.concatenate([wl[:, o[0]:o[1]], wl[:, o[1]:o[2]], wl[:, o[9]:o[10]], wl[:, o[5]:o[9]], wl[:, o[2]:o[3]],
                               jnp.pad(w_ba, ((0, 0), (0, LANES - 2 * B_HEADS)))], axis=1).astype(BF16),
        w_bat=jnp.pad(w_ba.T, ((0, 2 * SUBLANES - 2 * B_HEADS), (0, 0))).astype(BF16),
        qn=a_q_norm_g[l].reshape(1, A_HD), kn=a_k_norm_g[l].reshape(1, A_HD),
        conv_w=b_conv_w[l],
        alog_r=pad_r(b_a_log[l]), dt_r=pad_r(b_dt_bias[l]), alog_c=pad_c(b_a_log[l]), dt_c=pad_c(b_dt_bias[l]),
        gb=b_out_norm_g[l].reshape(1, B_DV), gc=c_out_norm_g[l].reshape(1, C_DV),
        w_oa=w_out_a[l].astype(BF16), w_ob=w_out_b[l].astype(BF16), w_oc=w_out_c[l].astype(BF16),
        w_o=w_out[l].astype(BF16), w_fi=w_ffn_in[l].astype(BF16), w_fo=w_ffn_out[l].astype(BF16),
    )


def kernel(x_prompt, x_sample, cache_a_kv0, cache_a_kv1, cache_a_kv2, state_b_conv, state_b_S, state_c_R, norm1_g, w_in, a_q_norm_g, a_k_norm_g, b_conv_w, b_a_log, b_dt_bias, b_out_norm_g, c_out_norm_g, w_out_a, w_out_b, w_out_c, w_out, norm2_g, w_ffn_in, w_ffn_out):
    bp, t = x_prompt.shape[:2]
    bs, s = x_sample.shape[:2]
    depth = w_in.shape[0]
    pos_p = jnp.arange(t)
    pos_s = PAST_LEN + jnp.arange(s)
    yp = x_prompt.reshape(bp * t, D_MODEL)
    ys = x_sample.reshape(bs * s, D_MODEL)
    caches = (cache_a_kv0, cache_a_kv1, cache_a_kv2)
    zeros_conv = jnp.zeros((bp, B_CONV - 1, B_CONV_CH), F32)
    zeros_s = jnp.zeros((bp, B_HEADS, B_DK, B_DV), F32)
    zeros_r = jnp.zeros((bp, C_HEADS, C_DK, C_DV), F32)
    acc = [[] for _ in range(12)]
    for l in range(depth):
        lw = _layer_weights(l, norm1_g, w_in, a_q_norm_g, a_k_norm_g, b_conv_w, b_a_log, b_dt_bias,
                            b_out_norm_g, c_out_norm_g, w_out_a, w_out_b, w_out_c, w_out, norm2_g,
                            w_ffn_in, w_ffn_out)
        yp, kv, cv, sn, rn = _layer(yp, pos_p, bp, t, lw, None, l, zeros_conv, zeros_s, zeros_r)
        for i, a in enumerate((kv[0], kv[1], kv[2], cv, sn, rn)):
            acc[i].append(a)
        ys, kv, cv, sn, rn = _layer(ys, pos_s, bs, s, lw, caches, l, state_b_conv[l], state_b_S[l], state_c_R[l])
        for i, a in enumerate((kv[0], kv[1], kv[2], cv, sn, rn)):
            acc[6 + i].append(a)
    return (yp.reshape(bp, t, D_MODEL), ys.reshape(bs, s, D_MODEL)) + tuple(jnp.stack(a) for a in acc)
```
